```python
import math
import jax, jax.numpy as jnp
from jax import lax
import numpy as np

D_MODEL = 2048
BATCH = 4
SEQ = 4096
DEPTH = 2

D_FF = 5632
SSM_HEADS = 16
SSM_HEAD_DIM = 64
SSM_D_INNER = SSM_HEADS * SSM_HEAD_DIM
SSM_GROUPS = 2
SSM_STATE = 128
SSM_CONV = 4
SSM_CONV_DIM = SSM_D_INNER + 2 * SSM_GROUPS * SSM_STATE
SSD_CHUNK = 256
MLA_HEADS = 8
MLA_Q_LORA = 768
MLA_KV_LORA = 512
MLA_NOPE = 128
MLA_ROPE = 64
MLA_QK_DIM = MLA_NOPE + MLA_ROPE
MLA_V = 128
MLA_WIDTH = MLA_HEADS * MLA_V
ATTN_Q_BLOCK = 128
RET_HEADS = 4
RET_QK_HEAD = 256
RET_V_HEAD = 256
RET_QK = RET_HEADS * RET_QK_HEAD
RET_V = RET_HEADS * RET_V_HEAD
RET_CHUNK = 256
N_BRANCH = 3
ROPE_THETA = 10000.0
NORM_EPS = 1e-6
IN_SPLITS = (SSM_D_INNER, SSM_CONV_DIM, SSM_HEADS,
             MLA_Q_LORA, MLA_KV_LORA + MLA_ROPE,
             RET_QK, RET_QK, RET_V, RET_V,
             N_BRANCH * D_MODEL)
D_IN = sum(IN_SPLITS)

kernel_name = 'hybrid_ssd_mla_retention_macaron'


def rmsnorm(x, w):
    xf = x.astype(jnp.float32)
    y = xf * lax.rsqrt(jnp.mean(xf * xf, axis=-1, keepdims=True) + NORM_EPS)
    return (y * w.astype(jnp.float32)).astype(x.dtype)


def swiglu(x, w_gate, w_up, w_down):
    return (jax.nn.silu(x @ w_gate) * (x @ w_up)) @ w_down


def rope_tables(positions, dim):
    inv = 1.0 / (ROPE_THETA ** (jnp.arange(0, dim, 2, dtype=jnp.float32) / dim))
    ang = positions.astype(jnp.float32)[..., None] * inv
    return jnp.cos(ang), jnp.sin(ang)


def apply_rope(x, cos, sin):
    x1, x2 = jnp.split(x, 2, axis=-1)
    c = cos[:, :, None, :]
    s = sin[:, :, None, :]
    return jnp.concatenate([x1 * c - x2 * s, x1 * s + x2 * c], axis=-1).astype(x.dtype)


def causal_depthwise_conv(x, w, b):
    k = w.shape[0]
    out = lax.conv_general_dilated(
        x, w[:, None, :].astype(x.dtype), window_strides=(1,), padding=[(k - 1, 0)],
        dimension_numbers=('NWC', 'WIO', 'NWC'), feature_group_count=x.shape[-1])
    return out + b


def ssd_chunked(xs, dt, a, bm, cm):
    b, s, h, p = xs.shape
    g, n = bm.shape[2], bm.shape[3]
    r = h // g
    L = math.gcd(s, SSD_CHUNK)
    nc = s // L
    xd = (xs * dt[..., None]).reshape(b, nc, L, g, r, p)
    adt = (dt * a).reshape(b, nc, L, g, r)
    bm = bm.reshape(b, nc, L, g, n)
    cm = cm.reshape(b, nc, L, g, n)
    acs = jnp.cumsum(adt, axis=2)
    causal = jnp.tril(jnp.ones((L, L), dtype=bool))[None, None, :, :, None, None]
    seg = acs[:, :, :, None] - acs[:, :, None, :]
    decay = jnp.exp(jnp.where(causal, seg, -jnp.inf))
    cb = jnp.einsum('bctgn,bcsgn->bctsg', cm, bm)
    y_diag = jnp.einsum('bctsgr,bcsgrp->bctgrp', cb[..., None] * decay, xd)
    xdd = xd * jnp.exp(acs[:, :, -1:] - acs)[..., None]
    states = jnp.einsum('bcsgn,bcsgrp->bcgrpn', bm, xdd)
    chunk_decay = jnp.exp(acs[:, :, -1])

    def carry_state(state, inp):
        st, dec = inp
        return state * dec[..., None, None] + st, state

    init = jnp.zeros((b, g, r, p, n), states.dtype)
    _, prev = lax.scan(carry_state, init,
                       (jnp.swapaxes(states, 0, 1), jnp.swapaxes(chunk_decay, 0, 1)))
    prev = jnp.swapaxes(prev, 0, 1)
    y_off = jnp.einsum('bctgn,bcgrpn->bctgrp', cm, prev) * jnp.exp(acs)[..., None]
    return (y_diag + y_off).reshape(b, s, h, p)


def retention_chunked(q, k, v):
    b, s, h, dk = q.shape
    dv = v.shape[-1]
    L = math.gcd(s, RET_CHUNK)
    nc = s // L
    expo = 5.0 + 7.0 * jnp.arange(h, dtype=jnp.float32) / (h - 1)
    log_gamma = jnp.log1p(-jnp.exp2(-expo))
    pos = jnp.arange(L, dtype=jnp.float32)
    rel = pos[:, None] - pos[None, :]
    dmask = jnp.where(rel[None] >= 0, jnp.exp(rel[None] * log_gamma[:, None, None]), 0.0).astype(q.dtype)
    q = q.reshape(b, nc, L, h, dk)
    k = k.reshape(b, nc, L, h, dk)
    v = v.reshape(b, nc, L, h, dv)
    scores = jnp.einsum('bcthd,bcshd->bchts', q, k) * dmask
    y_in = jnp.einsum('bchts,bcshe->bcthe', scores, v)
    k_dec = jnp.exp((L - 1 - pos)[:, None] * log_gamma).astype(q.dtype)
    states = jnp.einsum('bcshd,bcshe->bchde', k * k_dec[None, None, :, :, None], v)
    chunk_decay = jnp.exp(L * log_gamma).astype(q.dtype)

    def carry_state(state, st):
        return state * chunk_decay[None, :, None, None] + st, state

    init = jnp.zeros((b, h, dk, dv), states.dtype)
    _, prev = lax.scan(carry_state, init, jnp.swapaxes(states, 0, 1))
    prev = jnp.swapaxes(prev, 0, 1)
    q_dec = jnp.exp((pos + 1)[:, None] * log_gamma).astype(q.dtype)
    y_cross = jnp.einsum('bcthd,bchde->bcthe', q, prev) * q_dec[None, None, :, :, None]
    return (y_in + y_cross).reshape(b, s, h, dv)


def causal_attention_blocks(q, k, v, scale):
    b, s, h, d = q.shape
    nb = s // ATTN_Q_BLOCK
    qb = jnp.swapaxes(q.reshape(b, nb, ATTN_Q_BLOCK, h, d), 0, 1)
    kpos = jnp.arange(s)

    def one_block(args):
        qi, i = args
        sc = jnp.einsum('bqhd,bkhd->bhqk', qi, k).astype(jnp.float32) * scale
        qpos = i * ATTN_Q_BLOCK + jnp.arange(ATTN_Q_BLOCK)
        sc = jnp.where(kpos[None, :] <= qpos[:, None], sc, -jnp.inf)
        pr = jax.nn.softmax(sc, axis=-1).astype(v.dtype)
        return jnp.einsum('bhqk,bkhe->bqhe', pr, v)

    out = lax.map(one_block, (qb, jnp.arange(nb)))
    return jnp.swapaxes(out, 0, 1).reshape(b, s, h, v.shape[-1])


def ssm_branch(z, xbc, dt_raw, conv_w, conv_b, dt_bias, a_log, d_skip, ssm_norm):
    b, s, _ = z.shape
    xbc = jax.nn.silu(causal_depthwise_conv(xbc, conv_w, conv_b))
    xs, bm, cm = jnp.split(xbc, [SSM_D_INNER, SSM_D_INNER + SSM_GROUPS * SSM_STATE], axis=-1)
    xs = xs.reshape(b, s, SSM_HEADS, SSM_HEAD_DIM)
    bm = bm.reshape(b, s, SSM_GROUPS, SSM_STATE)
    cm = cm.reshape(b, s, SSM_GROUPS, SSM_STATE)
    dt = jax.nn.softplus(dt_raw.astype(jnp.float32) + dt_bias.astype(jnp.float32))
    a = -jnp.exp(a_log.astype(jnp.float32))
    y = ssd_chunked(xs, dt, a, bm, cm) + d_skip[:, None] * xs
    y = y.reshape(b, s, SSM_D_INNER) * jax.nn.silu(z)
    y = rmsnorm(y.reshape(b, s, SSM_GROUPS, SSM_D_INNER // SSM_GROUPS),
                ssm_norm.reshape(SSM_GROUPS, SSM_D_INNER // SSM_GROUPS))
    return y.reshape(b, s, SSM_D_INNER).astype(z.dtype)


def mla_branch(q_lat, kv_lat, cos, sin, q_a_norm, w_q_b, kv_a_norm, w_kv_b, q_norm, k_norm):
    b, s, _ = q_lat.shape
    q = (rmsnorm(q_lat, q_a_norm) @ w_q_b).reshape(b, s, MLA_HEADS, MLA_QK_DIM)
    c_kv, k_pe = jnp.split(kv_lat, [MLA_KV_LORA], axis=-1)
    kv = (rmsnorm(c_kv, kv_a_norm) @ w_kv_b).reshape(b, s, MLA_HEADS, MLA_NOPE + MLA_V)
    k_nope, v = jnp.split(kv, [MLA_NOPE], axis=-1)
    k = jnp.concatenate([k_nope, jnp.broadcast_to(k_pe[:, :, None, :], (b, s, MLA_HEADS, MLA_ROPE))], axis=-1)
    q = rmsnorm(q, q_norm)
    k = rmsnorm(k, k_norm)
    q = jnp.concatenate([q[..., :MLA_NOPE], apply_rope(q[..., MLA_NOPE:], cos, sin)], axis=-1)
    k = jnp.concatenate([k[..., :MLA_NOPE], apply_rope(k[..., MLA_NOPE:], cos, sin)], axis=-1)
    o = causal_attention_blocks(q, k, v, MLA_QK_DIM ** -0.5)
    return o.reshape(b, s, MLA_WIDTH)


def retention_branch(rq, rk, rv, rg, cos, sin, ret_norm):
    b, s, _ = rq.shape
    q = apply_rope(rq.reshape(b, s, RET_HEADS, RET_QK_HEAD), cos, sin)
    k = apply_rope(rk.reshape(b, s, RET_HEADS, RET_QK_HEAD), cos, sin) * (RET_QK_HEAD ** -0.5)
    v = rv.reshape(b, s, RET_HEADS, RET_V_HEAD)
    y = retention_chunked(q, k, v)
    y = rmsnorm(y, ret_norm.reshape(RET_HEADS, RET_V_HEAD)).reshape(b, s, RET_V)
    return jax.nn.silu(rg) * y


def hybrid_mixer(h, cos_mla, sin_mla, cos_ret, sin_ret, w_in, gate_b, conv_w, conv_b,
                 dt_bias, a_log, d_skip, ssm_norm, q_a_norm, w_q_b, kv_a_norm, w_kv_b,
                 q_norm, k_norm, ret_norm, w_br_ssm, w_br_mla, w_br_ret, w_out):
    b, s, d = h.shape
    proj = h @ w_in
    idx = np.cumsum(IN_SPLITS)[:-1].tolist()
    z, xbc, dt_raw, q_lat, kv_lat, rq, rk, rv, rg, gates = jnp.split(proj, idx, axis=-1)
    y_ssm = ssm_branch(z, xbc, dt_raw, conv_w, conv_b, dt_bias, a_log, d_skip, ssm_norm)
    y_mla = mla_branch(q_lat, kv_lat, cos_mla, sin_mla, q_a_norm, w_q_b, kv_a_norm, w_kv_b, q_norm, k_norm)
    y_ret = retention_branch(rq, rk, rv, rg, cos_ret, sin_ret, ret_norm)
    g = jax.nn.sigmoid((gates + gate_b).astype(jnp.float32)).astype(h.dtype).reshape(b, s, N_BRANCH, d)
    merged = (g[:, :, 0] * (y_ssm @ w_br_ssm)
              + g[:, :, 1] * (y_mla @ w_br_mla)
              + g[:, :, 2] * (y_ret @ w_br_ret))
    return merged @ w_out


def setup_inputs(seed: int = 0) -> dict:
    key = jax.random.key(seed)
    ks = iter(jax.random.split(key, 48))

    def normal(shape, scale):
        return scale * jax.random.normal(next(ks), shape, jnp.float32)

    def gain(shape):
        return 1.0 + normal(shape, 0.05)

    L_ = DEPTH
    x = normal((BATCH, SEQ, D_MODEL), 1.0)
    offset = jax.random.randint(next(ks), (BATCH, 1), 0, 1024, dtype=jnp.int32)
    positions = (offset + jnp.arange(SEQ, dtype=jnp.int32)[None, :]).astype(jnp.int32)
    dt0 = jnp.exp(jax.random.uniform(next(ks), (L_, SSM_HEADS), jnp.float32,
                                     math.log(1e-3), math.log(1e-1)))
    dt_bias = dt0 + jnp.log(-jnp.expm1(-dt0))
    a_log = jnp.log(jax.random.uniform(next(ks), (L_, SSM_HEADS), jnp.float32, 1.0, 16.0))
    return {
        'x': x,
        'positions': positions,
        'ffn1_norm': gain((L_, D_MODEL)),
        'ffn1_w_gate': normal((L_, D_MODEL, D_FF), D_MODEL ** -0.5),
        'ffn1_w_up': normal((L_, D_MODEL, D_FF), D_MODEL ** -0.5),
        'ffn1_w_down': normal((L_, D_FF, D_MODEL), D_FF ** -0.5),
        'mix_norm': gain((L_, D_MODEL)),
        'w_in': normal((L_, D_MODEL, D_IN), D_MODEL ** -0.5),
        'gate_b': normal((L_, N_BRANCH * D_MODEL), 0.01),
        'conv_w': normal((L_, SSM_CONV, SSM_CONV_DIM), SSM_CONV ** -0.5),
        'conv_b': normal((L_, SSM_CONV_DIM), 0.01),
        'dt_bias': dt_bias,
        'a_log': a_log,
        'd_skip': gain((L_, SSM_HEADS)),
        'ssm_norm': gain((L_, SSM_D_INNER)),
        'q_a_norm': gain((L_, MLA_Q_LORA)),
        'w_q_b': normal((L_, MLA_Q_LORA, MLA_HEADS * MLA_QK_DIM), MLA_Q_LORA ** -0.5),
        'kv_a_norm': gain((L_, MLA_KV_LORA)),
        'w_kv_b': normal((L_, MLA_KV_LORA, MLA_HEADS * (MLA_NOPE + MLA_V)), MLA_KV_LORA ** -0.5),
        'q_norm': gain((L_, MLA_QK_DIM)),
        'k_norm': gain((L_, MLA_QK_DIM)),
        'ret_norm': gain((L_, RET_V)),
        'w_br_ssm': normal((L_, SSM_D_INNER, D_MODEL), SSM_D_INNER ** -0.5),
        'w_br_mla': normal((L_, MLA_WIDTH, D_MODEL), MLA_WIDTH ** -0.5),
        'w_br_ret': normal((L_, RET_V, D_MODEL), RET_V ** -0.5),
        'w_out': normal((L_, D_MODEL, D_MODEL), D_MODEL ** -0.5),
        'ffn2_norm': gain((L_, D_MODEL)),
        'ffn2_w_gate': normal((L_, D_MODEL, D_FF), D_MODEL ** -0.5),
        'ffn2_w_up': normal((L_, D_MODEL, D_FF), D_MODEL ** -0.5),
        'ffn2_w_down': normal((L_, D_FF, D_MODEL), D_FF ** -0.5),
    }


def reference(x, positions, ffn1_norm, ffn1_w_gate, ffn1_w_up, ffn1_w_down, mix_norm, w_in,
              gate_b, conv_w, conv_b, dt_bias, a_log, d_skip, ssm_norm, q_a_norm, w_q_b,
              kv_a_norm, w_kv_b, q_norm, k_norm, ret_norm, w_br_ssm, w_br_mla, w_br_ret,
              w_out, ffn2_norm, ffn2_w_gate, ffn2_w_up, ffn2_w_down):
    cos_mla, sin_mla = rope_tables(positions, MLA_ROPE)
    cos_ret, sin_ret = rope_tables(positions, RET_QK_HEAD)
    for l in range(DEPTH):
        x = x + 0.5 * swiglu(rmsnorm(x, ffn1_norm[l]), ffn1_w_gate[l], ffn1_w_up[l], ffn1_w_down[l])
        x = x + hybrid_mixer(rmsnorm(x, mix_norm[l]), cos_mla, sin_mla, cos_ret, sin_ret,
                             w_in[l], gate_b[l], conv_w[l], conv_b[l], dt_bias[l], a_log[l],
                             d_skip[l], ssm_norm[l], q_a_norm[l], w_q_b[l], kv_a_norm[l],
                             w_kv_b[l], q_norm[l], k_norm[l], ret_norm[l], w_br_ssm[l],
                             w_br_mla[l], w_br_ret[l], w_out[l])
        x = x + 0.5 * swiglu(rmsnorm(x, ffn2_norm[l]), ffn2_w_gate[l], ffn2_w_up[l], ffn2_w_down[l])
    return x
```

```python
import functools
import math

import jax
import jax.numpy as jnp
import numpy as np
from jax import lax
from jax.experimental import pallas as pl
from jax.experimental.pallas import tpu as pltpu

F32 = jnp.float32
BF16 = jnp.bfloat16

NORM_EPS = 1e-6
ROPE_THETA = 10000.0

SSM_HEADS = 16
SSM_HEAD_DIM = 64
SSM_D_INNER = SSM_HEADS * SSM_HEAD_DIM
SSM_GROUPS = 2
SSM_STATE = 128
SSM_CONV = 4
SSD_CHUNK = 256
MLA_HEADS = 8
MLA_Q_LORA = 768
MLA_KV_LORA = 512
MLA_NOPE = 128
MLA_ROPE = 64
MLA_QK_DIM = MLA_NOPE + MLA_ROPE
MLA_V = 128
RET_HEADS = 4
RET_HEAD = 256
RET_CHUNK = 256
N_BRANCH = 3

LANES = 128
MLA_QK_PAD = MLA_NOPE + LANES
VMEM_LIMIT = 56 * 1024 * 1024


def _cparams(sem):
    return pltpu.CompilerParams(dimension_semantics=sem, vmem_limit_bytes=VMEM_LIMIT)


def _resident(shape):
    nd = len(shape)
    return pl.BlockSpec(shape, lambda *_: (0,) * nd, pipeline_mode=pl.Buffered(1))


def _rms(x, n):
    return lax.rsqrt(jnp.sum(x * x, axis=-1, keepdims=True) * (1.0 / n) + NORM_EPS)


def _silu(x):
    return x * jax.nn.sigmoid(x)


def _dot(a, b):
    return jnp.dot(a, b, preferred_element_type=F32)


def _dot_nt(a, b):
    return lax.dot_general(a, b, (((1,), (1,)), ((), ())), preferred_element_type=F32)


def _ffn_kernel(x_ref, nw_ref, wg_ref, wu_ref, wd_ref, o_ref, xn_ref, acc_ref):
    j = pl.program_id(1)
    d = x_ref.shape[-1]

    @pl.when(j == 0)
    def _():
        x = x_ref[...]
        xn_ref[...] = (x * _rms(x, d) * nw_ref[...]).astype(BF16)

    xn = xn_ref[...]
    g = _dot(xn, wg_ref[...])
    u = _dot(xn, wu_ref[...])
    part = _dot((_silu(g) * u).astype(BF16), wd_ref[...])

    @pl.when(j == 0)
    def _():
        acc_ref[...] = part

    @pl.when(j > 0)
    def _():
        acc_ref[...] += part

    @pl.when(j == pl.num_programs(1) - 1)
    def _():
        o_ref[...] = x_ref[...] + 0.5 * acc_ref[...]


def _ffn(x, nw, wg, wu, wd, *, tm=512, tf=512):
    m, d = x.shape
    f = wg.shape[1]
    tm = min(tm, m)
    return pl.pallas_call(
        _ffn_kernel,
        out_shape=jax.ShapeDtypeStruct((m, d), F32),
        grid=(m // tm, f // tf),
        in_specs=[
            pl.BlockSpec((tm, d), lambda i, j: (i, 0)),
            pl.BlockSpec((1, d), lambda i, j: (0, 0)),
            pl.BlockSpec((d, tf), lambda i, j: (0, j)),
            pl.BlockSpec((d, tf), lambda i, j: (0, j)),
            pl.BlockSpec((tf, d), lambda i, j: (j, 0)),
        ],
        out_specs=pl.BlockSpec((tm, d), lambda i, j: (i, 0)),
        scratch_shapes=[pltpu.VMEM((tm, d), BF16), pltpu.VMEM((tm, d), F32)],
        compiler_params=_cparams(("parallel", "arbitrary")),
        name="ffn",
    )(x, nw, wg, wu, wd)


P_GATES = 0
P_RQ = 6144
P_RK = 7168
P_RV = 8192
P_RG = 9216
P_Z = 10240
P_XS = 11264
P_BC = 12288
P_KPE = 12800
P_QLAT = 13056
P_CKV = 13824
P_TOTAL = 14336


def _inproj_kernel(x_ref, nw_ref, w_ref, wdt_ref, o_ref, dt_ref, xn_ref):
    j = pl.program_id(1)
    d = x_ref.shape[-1]

    @pl.when(j == 0)
    def _():
        x = x_ref[...]
        xn = (x * _rms(x, d) * nw_ref[...]).astype(BF16)
        xn_ref[...] = xn
        dt_ref[...] = _dot(xn, wdt_ref[...])

    o_ref[...] = _dot(xn_ref[...], w_ref[...]).astype(BF16)


def _inproj(x, nw, w, wdt, *, tm=1024, tn=512):
    m, d = x.shape
    n = w.shape[1]
    tm = min(tm, m)
    return pl.pallas_call(
        _inproj_kernel,
        out_shape=(jax.ShapeDtypeStruct((m, n), BF16), jax.ShapeDtypeStruct((m, LANES), F32)),
        grid=(m // tm, n // tn),
        in_specs=[
            pl.BlockSpec((tm, d), lambda i, j: (i, 0)),
            pl.BlockSpec((1, d), lambda i, j: (0, 0)),
            pl.BlockSpec((d, tn), lambda i, j: (0, j)),
            pl.BlockSpec((d, LANES), lambda i, j: (0, 0)),
        ],
        out_specs=(
            pl.BlockSpec((tm, tn), lambda i, j: (i, j)),
            pl.BlockSpec((tm, LANES), lambda i, j: (i, 0)),
        ),
        scratch_shapes=[pltpu.VMEM((tm, d), BF16)],
        compiler_params=_cparams(("parallel", "arbitrary")),
        name="inproj",
    )(x, nw, w, wdt)


def _split3(x):
    hi = x.astype(BF16)
    r1 = x - hi.astype(F32)
    mid = r1.astype(BF16)
    lo = (r1 - mid.astype(F32)).astype(BF16)
    return hi, mid, lo


def _ssd_kernel(xs_ref, bc_ref, z_ref, dt_ref, cwx_ref, cbx_ref, cwb_ref, cbb_ref,
                dtb_ref, alog_ref, dskip_ref, nw_ref, o_ref,
                xext_ref, bext_ref, state_ref, dtx_ref, ax_ref, y_ref):
    L = xs_ref.shape[0]
    P = SSM_HEAD_DIM
    N = SSM_STATE
    HG = SSM_HEADS // SSM_GROUPS
    GW = HG * P
    T = 8

    @pl.when(pl.program_id(1) == 0)
    def _():
        xext_ref[0:T, :] = jnp.zeros((T, xext_ref.shape[1]), F32)
        bext_ref[0:T, :] = jnp.zeros((T, bext_ref.shape[1]), F32)
        state_ref[...] = jnp.zeros(state_ref.shape, F32)

    def conv_silu(ext_ref, in_ref, w_ref, b_ref):
        ext_ref[T:T + L, :] = in_ref[...].astype(F32)
        acc = b_ref[...] + w_ref[0:1, :] * ext_ref[T - 3:T - 3 + L, :]
        for j in range(1, SSM_CONV):
            acc = acc + w_ref[j:j + 1, :] * ext_ref[T - 3 + j:T - 3 + j + L, :]
        ext_ref[0:T, :] = ext_ref[L:L + T, :]
        return _silu(acc)

    xc = conv_silu(xext_ref, xs_ref, cwx_ref, cbx_ref)
    bcc = conv_silu(bext_ref, bc_ref, cwb_ref, cbb_ref)

    dtr = dt_ref[...] + dtb_ref[...]
    dt = jnp.maximum(dtr, 0.0) + jnp.log1p(jnp.exp(-jnp.abs(dtr)))
    adt = dt * (-jnp.exp(alog_ref[...]))
    row = lax.broadcasted_iota(jnp.int32, (L, L), 0)
    col = lax.broadcasted_iota(jnp.int32, (L, L), 1)
    causal = row >= col
    tril = jnp.where(causal, 1.0, 0.0).astype(BF16)
    hi, mid, lo = _split3(adt)
    acs = _dot(tril, hi) + _dot(tril, mid) + _dot(tril, lo)
    acs_t = acs.T

    for h in range(SSM_HEADS):
        dtx_ref[:, h * P:(h + 1) * P] = jnp.broadcast_to(dt[:, h:h + 1], (L, P))
        ax_ref[:, h * P:(h + 1) * P] = jnp.broadcast_to(acs[:, h:h + 1], (L, P))
    ax = ax_ref[...]
    a_last = ax[L - 1:L, :]
    xd = xc * dtx_ref[...]
    xd16 = xd.astype(BF16)
    xdd16 = (xd * jnp.exp(a_last - ax)).astype(BF16)
    e_ax = jnp.exp(ax)
    e_last = jnp.exp(a_last)

    for g in range(SSM_GROUPS):
        bm = bcc[:, g * N:(g + 1) * N]
        cm16 = bcc[:, (SSM_GROUPS + g) * N:(SSM_GROUPS + g + 1) * N].astype(BF16)
        bm16 = bm.astype(BF16)
        bmt16 = bm.T.astype(BF16)
        cb = _dot_nt(cm16, bm16)
        gs = slice(g * GW, (g + 1) * GW)
        prev = state_ref[g]
        y_ref[:, gs] = _dot(cm16, prev.astype(BF16)) * e_ax[:, gs]
        state_ref[g] = prev * e_last[:, gs] + _dot(bmt16, xdd16[:, gs])
        for r in range(HG):
            h = g * HG + r
            seg = acs[:, h:h + 1] - acs_t[h:h + 1, :]
            decay = jnp.exp(jnp.where(causal, seg, -jnp.inf))
            hs = slice(h * P, (h + 1) * P)
            y_ref[:, hs] += _dot((cb * decay).astype(BF16), xd16[:, hs])

    y = (y_ref[...] + dskip_ref[...] * xc) * _silu(z_ref[...].astype(F32))
    for g in range(SSM_GROUPS):
        gs = slice(g * GW, (g + 1) * GW)
        yg = y[:, gs]
        o_ref[:, gs] = (yg * _rms(yg, GW) * nw_ref[:, gs]).astype(BF16)


def _ssd(proj, dt_raw, cwx, cbx, cwb, cbb, dtb, alog, dskip, nw, *, batch, seq):
    L = math.gcd(seq, SSD_CHUNK)
    nc = seq // L
    di = SSM_D_INNER
    bcw = 2 * SSM_GROUPS * SSM_STATE
    tok = lambda b, c: b * nc + c
    return pl.pallas_call(
        _ssd_kernel,
        out_shape=jax.ShapeDtypeStruct((batch * seq, di), BF16),
        grid=(batch, nc),
        in_specs=[
            pl.BlockSpec((L, di), lambda b, c: (tok(b, c), P_XS // di)),
            pl.BlockSpec((L, bcw), lambda b, c: (tok(b, c), P_BC // bcw)),
            pl.BlockSpec((L, di), lambda b, c: (tok(b, c), P_Z // di)),
            pl.BlockSpec((L, LANES), lambda b, c: (tok(b, c), 0)),
            _resident(cwx.shape), _resident(cbx.shape), _resident(cwb.shape),
            _resident(cbb.shape), _resident(dtb.shape), _resident(alog.shape),
            _resident(dskip.shape), _resident(nw.shape),
        ],
        out_specs=pl.BlockSpec((L, di), lambda b, c: (tok(b, c), 0)),
        scratch_shapes=[
            pltpu.VMEM((L + 8, di), F32),
            pltpu.VMEM((L + 8, bcw), F32),
            pltpu.VMEM((SSM_GROUPS, SSM_STATE, di // SSM_GROUPS), F32),
            pltpu.VMEM((L, di), F32),
            pltpu.VMEM((L, di), F32),
            pltpu.VMEM((L, di), F32),
        ],
        compiler_params=_cparams(("parallel", "arbitrary")),
        name="ssd",
    )(proj, proj, proj, dt_raw, cwx, cbx, cwb, cbb, dtb, alog, dskip, nw)


def _rope_tile(x, c, s1, s2):
    q = MLA_ROPE // 2
    return x * c + pltpu.roll(x, LANES - q, 1) * s1 + pltpu.roll(x, q, 1) * s2


def _mla_prep_kernel(ql_ref, ckv_ref, kpe_ref, c_ref, s1_ref, s2_ref, qan_ref, kvan_ref,
                     qn_ref, kn_ref, wq_ref, wkv_ref, q_out, k_out, v_out, *, q_scale):
    ql = ql_ref[...].astype(F32)
    qa = (ql * _rms(ql, MLA_Q_LORA) * qan_ref[...]).astype(BF16)
    ckv = ckv_ref[...].astype(F32)
    kva = (ckv * _rms(ckv, MLA_KV_LORA) * kvan_ref[...]).astype(BF16)
    kpe = kpe_ref[...].astype(F32)
    kpe_ss = jnp.sum(kpe * kpe, axis=-1, keepdims=True)
    c, s1, s2 = c_ref[...], s1_ref[...], s2_ref[...]
    qn, kn = qn_ref[...], kn_ref[...]
    for h in range(MLA_HEADS):
        qh = _dot(qa, wq_ref[h])
        qh = qh * _rms(qh, MLA_QK_DIM) * qn
        q_out[h, :, 0:MLA_NOPE] = (qh[:, 0:MLA_NOPE] * q_scale).astype(BF16)
        q_out[h, :, MLA_NOPE:] = (_rope_tile(qh[:, MLA_NOPE:], c, s1, s2) * q_scale).astype(BF16)
        kvh = _dot(kva, wkv_ref[h])
        kno = kvh[:, 0:MLA_NOPE]
        r = lax.rsqrt((jnp.sum(kno * kno, axis=-1, keepdims=True) + kpe_ss)
                      * (1.0 / MLA_QK_DIM) + NORM_EPS)
        k_out[h, :, 0:MLA_NOPE] = (kno * r * kn[:, 0:MLA_NOPE]).astype(BF16)
        k_out[h, :, MLA_NOPE:] = _rope_tile(kpe * r * kn[:, MLA_NOPE:], c, s1, s2).astype(BF16)
        v_out[h] = kvh[:, MLA_NOPE:].astype(BF16)


def _mla_prep(proj, c, s1, s2, qan, kvan, qn, kn, wq, wkv, *, tm=512):
    m = proj.shape[0]
    tm = min(tm, m)
    row = lambda i: (i, 0)
    return pl.pallas_call(
        functools.partial(_mla_prep_kernel, q_scale=MLA_QK_DIM ** -0.5),
        out_shape=(
            jax.ShapeDtypeStruct((MLA_HEADS, m, MLA_QK_PAD), BF16),
            jax.ShapeDtypeStruct((MLA_HEADS, m, MLA_QK_PAD), BF16),
            jax.ShapeDtypeStruct((MLA_HEADS, m, MLA_V), BF16),
        ),
        grid=(m // tm,),
        in_specs=[
            pl.BlockSpec((tm, MLA_Q_LORA), lambda i: (i, P_QLAT // MLA_Q_LORA)),
            pl.BlockSpec((tm, MLA_KV_LORA), lambda i: (i, P_CKV // MLA_KV_LORA)),
            pl.BlockSpec((tm, LANES), lambda i: (i, P_KPE // LANES)),
            pl.BlockSpec((tm, LANES), row), pl.BlockSpec((tm, LANES), row),
            pl.BlockSpec((tm, LANES), row),
            _resident(qan.shape), _resident(kvan.shape), _resident(qn.shape),
            _resident(kn.shape), _resident(wq.shape), _resident(wkv.shape),
        ],
        out_specs=(
            pl.BlockSpec((MLA_HEADS, tm, MLA_QK_PAD), lambda i: (0, i, 0)),
            pl.BlockSpec((MLA_HEADS, tm, MLA_QK_PAD), lambda i: (0, i, 0)),
            pl.BlockSpec((MLA_HEADS, tm, MLA_V), lambda i: (0, i, 0)),
        ),
        compiler_params=_cparams(("parallel",)),
        name="mla_prep",
    )(proj, proj, proj, c, s1, s2, qan, kvan, qn, kn, wq, wkv)


def _attn_kernel(q_ref, k_ref, v_ref, o_ref, *, blk):
    i = pl.program_id(2)
    q = q_ref[...]

    def step(j, carry, masked):
        m, l, acc = carry
        start = pl.multiple_of(j * blk, blk)
        k = k_ref[pl.ds(start, blk), :]
        v = v_ref[pl.ds(start, blk), :]
        s = _dot_nt(q, k)
        if masked:
            row = lax.broadcasted_iota(jnp.int32, (blk, blk), 0)
            col = lax.broadcasted_iota(jnp.int32, (blk, blk), 1)
            s = jnp.where(row >= col, s, -jnp.inf)
        m_new = jnp.maximum(m, jnp.max(s, axis=-1, keepdims=True))
        p = jnp.exp(s - m_new)
        alpha = jnp.exp(m - m_new)
        l = alpha * l + jnp.sum(p, axis=-1, keepdims=True)
        acc = alpha * acc + _dot(p.astype(BF16), v)
        return m_new, l, acc

    init = (jnp.full((blk, 1), -jnp.inf, F32), jnp.zeros((blk, 1), F32),
            jnp.zeros((blk, MLA_V), F32))
    carry = lax.fori_loop(0, i, lambda j, c: step(j, c, False), init)
    _, l, acc = step(i, carry, True)
    o_ref[...] = (acc / l).astype(BF16)


def _attention(q, k, v, *, batch, seq, blk=512):
    blk = min(blk, seq)
    nq = seq // blk
    return pl.pallas_call(
        functools.partial(_attn_kernel, blk=blk),
        out_shape=jax.ShapeDtypeStruct((batch * seq, MLA_HEADS * MLA_V), BF16),
        grid=(batch, MLA_HEADS, nq),
        in_specs=[
            pl.BlockSpec((None, blk, MLA_QK_PAD), lambda b, h, i: (h, b * nq + i, 0)),
            pl.BlockSpec((None, seq, MLA_QK_PAD), lambda b, h, i: (h, b, 0)),
            pl.BlockSpec((None, seq, MLA_V), lambda b, h, i: (h, b, 0)),
        ],
        out_specs=pl.BlockSpec((blk, MLA_V), lambda b, h, i: (b * nq + i, h)),
        compiler_params=_cparams(("parallel", "parallel", "arbitrary")),
        name="mla_attention",
    )(q, k, v)


def _ret_log_gamma():
    expo = 5.0 + 7.0 * np.arange(RET_HEADS, dtype=np.float32) / np.float32(RET_HEADS - 1)
    return [float(v) for v in np.log1p(-np.exp2(-expo).astype(np.float32)).astype(np.float32)]


def _ret_kernel(rq_ref, rk_ref, rv_ref, rg_ref, cos_ref, sin_ref, nw_ref, o_ref, state_ref,
                *, log_gamma):
    L = rq_ref.shape[0]
    D = RET_HEAD
    half = D // 2

    @pl.when(pl.program_id(1) == 0)
    def _():
        state_ref[...] = jnp.zeros(state_ref.shape, F32)

    cos, sin = cos_ref[...], sin_ref[...]
    row = lax.broadcasted_iota(jnp.int32, (L, L), 0)
    col = lax.broadcasted_iota(jnp.int32, (L, L), 1)
    rel = (row - col).astype(F32)
    pos = lax.broadcasted_iota(jnp.int32, (L, D), 0).astype(F32)

    def rope(ref, hs):
        x1 = ref[:, hs.start:hs.start + half].astype(F32)
        x2 = ref[:, hs.start + half:hs.stop].astype(F32)
        return jnp.concatenate([x1 * cos - x2 * sin, x1 * sin + x2 * cos], axis=-1)

    for h in range(RET_HEADS):
        lg = log_gamma[h]
        hs = slice(h * D, (h + 1) * D)
        q = rope(rq_ref, hs)
        k = rope(rk_ref, hs) * (RET_HEAD ** -0.5)
        v16 = rv_ref[:, hs]
        q16 = q.astype(BF16)
        dmask = jnp.where(rel >= 0, jnp.exp(rel * lg), 0.0)
        scores = _dot_nt(q16, k.astype(BF16)) * dmask
        y = _dot(scores.astype(BF16), v16)
        prev = state_ref[h]
        y = y + _dot(q16, prev.astype(BF16)) * jnp.exp((pos + 1.0) * lg)
        kd = k * jnp.exp((L - 1.0 - pos) * lg)
        state_ref[h] = prev * math.exp(L * lg) + _dot(kd.T.astype(BF16), v16)
        y = y * _rms(y, D) * nw_ref[:, hs]
        o_ref[:, hs] = (_silu(rg_ref[:, hs].astype(F32)) * y).astype(BF16)


def _retention(proj, cos, sin, nw, *, batch, seq):
    L = math.gcd(seq, RET_CHUNK)
    nc = seq // L
    w = RET_HEADS * RET_HEAD
    tok = lambda b, c: b * nc + c
    return pl.pallas_call(
        functools.partial(_ret_kernel, log_gamma=_ret_log_gamma()),
        out_shape=jax.ShapeDtypeStruct((batch * seq, w), BF16),
        grid=(batch, nc),
        in_specs=[
            pl.BlockSpec((L, w), lambda b, c: (tok(b, c), P_RQ // w)),
            pl.BlockSpec((L, w), lambda b, c: (tok(b, c), P_RK // w)),
            pl.BlockSpec((L, w), lambda b, c: (tok(b, c), P_RV // w)),
            pl.BlockSpec((L, w), lambda b, c: (tok(b, c), P_RG // w)),
            pl.BlockSpec((L, RET_HEAD // 2), lambda b, c: (tok(b, c), 0)),
            pl.BlockSpec((L, RET_HEAD // 2), lambda b, c: (tok(b, c), 0)),
            _resident(nw.shape),
        ],
        out_specs=pl.BlockSpec((L, w), lambda b, c: (tok(b, c), 0)),
        scratch_shapes=[pltpu.VMEM((RET_HEADS, RET_HEAD, RET_HEAD), F32)],
        compiler_params=_cparams(("parallel", "arbitrary")),
        name="retention",
    )(proj, proj, proj, proj, cos, sin, nw)


def _merge_kernel(ys_ref, ym_ref, yr_ref, g0_ref, g1_ref, g2_ref, gb_ref, x_ref,
                  w0_ref, w1_ref, w2_ref, wo_ref, o_ref):
    d = x_ref.shape[-1]

    def branch(k, y_ref, g_ref, w_ref):
        gate = jax.nn.sigmoid(g_ref[...].astype(F32) + gb_ref[:, k * d:(k + 1) * d])
        return gate * _dot(y_ref[...], w_ref[...])

    merged = (branch(0, ys_ref, g0_ref, w0_ref) + branch(1, ym_ref, g1_ref, w1_ref)
              + branch(2, yr_ref, g2_ref, w2_ref))
    o_ref[...] = x_ref[...] + _dot(merged.astype(BF16), wo_ref[...])


def _merge(ys, ym, yr, proj, gb, x, w0, w1, w2, wo, *, tm=256):
    m, d = x.shape
    tm = min(tm, m)
    bw = ys.shape[1]
    row = lambda i: (i, 0)
    return pl.pallas_call(
        _merge_kernel,
        out_shape=jax.ShapeDtypeStruct((m, d), F32),
        grid=(m // tm,),
        in_specs=[
            pl.BlockSpec((tm, bw), row), pl.BlockSpec((tm, bw), row), pl.BlockSpec((tm, bw), row),
            pl.BlockSpec((tm, d), lambda i: (i, P_GATES // d)),
            pl.BlockSpec((tm, d), lambda i: (i, P_GATES // d + 1)),
            pl.BlockSpec((tm, d), lambda i: (i, P_GATES // d + 2)),
            _resident(gb.shape),
            pl.BlockSpec((tm, d), row),
            _resident(w0.shape), _resident(w1.shape), _resident(w2.shape), _resident(wo.shape),
        ],
        out_specs=pl.BlockSpec((tm, d), row),
        compiler_params=_cparams(("parallel",)),
        name="merge",
    )(ys, ym, yr, proj, proj, proj, gb, x, w0, w1, w2, wo)


def _rope_angles(positions, dim):
    inv = 1.0 / (ROPE_THETA ** (jnp.arange(0, dim, 2, dtype=F32) / dim))
    return positions.astype(F32).reshape(-1, 1) * inv


def _relayout_w_in(w):
    d = w.shape[0]
    o = np.cumsum([0, 1024, 1536, 16, 768, 576, 1024, 1024, 1024, 1024, 6144]).tolist()
    z, xbc, dt, qlat, kvlat, rq, rk, rv, rg, gates = (w[:, o[i]:o[i + 1]] for i in range(10))
    zeros = lambda n: jnp.zeros((d, n), w.dtype)
    main = jnp.concatenate([
        gates, rq, rk, rv, rg, z, xbc,
        kvlat[:, MLA_KV_LORA:], zeros(LANES - MLA_ROPE), zeros(P_QLAT - P_KPE - LANES),
        qlat, kvlat[:, :MLA_KV_LORA]], axis=1)
    assert main.shape[1] == P_TOTAL
    wdt = jnp.concatenate([dt, zeros(LANES - dt.shape[1])], axis=1)
    return main.astype(BF16), wdt.astype(BF16)


def _pad_lanes(v, n):
    return jnp.concatenate([v, jnp.zeros((n - v.shape[0],), v.dtype)]).reshape(1, n)


def kernel(x, positions, ffn1_norm, ffn1_w_gate, ffn1_w_up, ffn1_w_down, mix_norm, w_in, gate_b, conv_w, conv_b, dt_bias, a_log, d_skip, ssm_norm, q_a_norm, w_q_b, kv_a_norm, w_kv_b, q_norm, k_norm, ret_norm, w_br_ssm, w_br_mla, w_br_ret, w_out, ffn2_norm, ffn2_w_gate, ffn2_w_up, ffn2_w_down):
    batch, seq, d = x.shape
    depth = w_in.shape[0]
    x = x.reshape(batch * seq, d)

    ang = _rope_angles(positions, MLA_ROPE)
    cm, sm = jnp.cos(ang), jnp.sin(ang)
    zq = jnp.zeros_like(cm)
    rope_c = jnp.concatenate([cm, cm, zq, zq], axis=1)
    rope_s1 = jnp.concatenate([-sm, zq, zq, zq], axis=1)
    rope_s2 = jnp.concatenate([zq, sm, zq, zq], axis=1)
    ang = _rope_angles(positions, RET_HEAD)
    cos_ret, sin_ret = jnp.cos(ang), jnp.sin(ang)

    row = lambda v: v.reshape(1, -1)
    for l in range(depth):
        x = _ffn(x, row(ffn1_norm[l]), ffn1_w_gate[l].astype(BF16), ffn1_w_up[l].astype(BF16),
                 ffn1_w_down[l].astype(BF16))

        w_main, w_dt = _relayout_w_in(w_in[l])
        proj, dt_raw = _inproj(x, row(mix_norm[l]), w_main, w_dt)

        y_ssm = _ssd(
            proj, dt_raw,
            conv_w[l][:, :SSM_D_INNER], row(conv_b[l][:SSM_D_INNER]),
            conv_w[l][:, SSM_D_INNER:], row(conv_b[l][SSM_D_INNER:]),
            _pad_lanes(dt_bias[l], LANES), _pad_lanes(a_log[l], LANES),
            row(jnp.repeat(d_skip[l], SSM_HEAD_DIM)), row(ssm_norm[l]),
            batch=batch, seq=seq)

        wq = w_q_b[l].reshape(MLA_Q_LORA, MLA_HEADS, MLA_QK_DIM)
        wq = jnp.pad(wq, ((0, 0), (0, 0), (0, MLA_QK_PAD - MLA_QK_DIM)))
        wq = wq.transpose(1, 0, 2).astype(BF16)
        wkv = w_kv_b[l].reshape(MLA_KV_LORA, MLA_HEADS, MLA_NOPE + MLA_V)
        wkv = wkv.transpose(1, 0, 2).astype(BF16)
        q, k, v = _mla_prep(
            proj, rope_c, rope_s1, rope_s2, row(q_a_norm[l]), row(kv_a_norm[l]),
            _pad_lanes(q_norm[l], MLA_QK_PAD), _pad_lanes(k_norm[l], MLA_QK_PAD), wq, wkv)
        y_mla = _attention(q, k, v, batch=batch, seq=seq)

        y_ret = _retention(proj, cos_ret, sin_ret, row(ret_norm[l]), batch=batch, seq=seq)

        x = _merge(y_ssm, y_mla, y_ret, proj, row(gate_b[l]), x,
                   w_br_ssm[l].astype(BF16), w_br_mla[l].astype(BF16),
                   w_br_ret[l].astype(BF16), w_out[l].astype(BF16))

        x = _ffn(x, row(ffn2_norm[l]), ffn2_w_gate[l].astype(BF16), ffn2_w_up[l].astype(BF16),
                 ffn2_w_down[l].astype(BF16))
    return x.reshape(batch, seq, d)
```

```python
import functools
import math

import jax
import jax.numpy as jnp
import numpy as np
from jax import lax
from jax.experimental import pallas as pl
from jax.experimental.pallas import tpu as pltpu

F32 = jnp.float32
BF16 = jnp.bfloat16

NORM_EPS = 1e-6
ROPE_THETA = 10000.0

SSM_HEADS = 16
SSM_HEAD_DIM = 64
SSM_D_INNER = SSM_HEADS * SSM_HEAD_DIM
SSM_GROUPS = 2
SSM_STATE = 128
SSM_CONV = 4
SSD_CHUNK = 256
MLA_HEADS = 8
MLA_Q_LORA = 768
MLA_KV_LORA = 512
MLA_NOPE = 128
MLA_ROPE = 64
MLA_QK_DIM = MLA_NOPE + MLA_ROPE
MLA_V = 128
RET_HEADS = 4
RET_HEAD = 256
RET_CHUNK = 256
N_BRANCH = 3

LANES = 128
MLA_QK_PAD = MLA_NOPE + LANES
VMEM_LIMIT = 56 * 1024 * 1024


def _cparams(sem):
    return pltpu.CompilerParams(dimension_semantics=sem, vmem_limit_bytes=VMEM_LIMIT)


def _resident(shape):
    nd = len(shape)
    return pl.BlockSpec(shape, lambda *_: (0,) * nd, pipeline_mode=pl.Buffered(1))


def _rms(x, n):
    return lax.rsqrt(jnp.sum(x * x, axis=-1, keepdims=True) * (1.0 / n) + NORM_EPS)


def _silu(x):
    return x * jax.nn.sigmoid(x)


def _dot(a, b):
    return jnp.dot(a, b, preferred_element_type=F32)


def _dot_nt(a, b):
    return lax.dot_general(a, b, (((1,), (1,)), ((), ())), preferred_element_type=F32)


def _ffn_kernel(x_ref, nw_ref, wg_ref, wu_ref, wd_ref, o_ref, xn_ref):
    d = x_ref.shape[-1]

    @pl.when(pl.program_id(1) == 0)
    def _():
        x = x_ref[...]
        xn_ref[...] = (x * _rms(x, d) * nw_ref[...]).astype(BF16)
        o_ref[...] = x

    xn = xn_ref[...]
    g = _dot(xn, wg_ref[...])
    u = _dot(xn, wu_ref[...])
    o_ref[...] += _dot((_silu(g) * (0.5 * u)).astype(BF16), wd_ref[...])


def _ffn(x, nw, wg, wu, wd, *, tm=1024, tf=512):
    m, d = x.shape
    f = wg.shape[1]
    tm = min(tm, m)
    return pl.pallas_call(
        _ffn_kernel,
        out_shape=jax.ShapeDtypeStruct((m, d), F32),
        grid=(m // tm, f // tf),
        in_specs=[
            pl.BlockSpec((tm, d), lambda i, j: (i, 0)),
            pl.BlockSpec((1, d), lambda i, j: (0, 0)),
            pl.BlockSpec((d, tf), lambda i, j: (0, j)),
            pl.BlockSpec((d, tf), lambda i, j: (0, j)),
            pl.BlockSpec((tf, d), lambda i, j: (j, 0)),
        ],
        out_specs=pl.BlockSpec((tm, d), lambda i, j: (i, 0)),
        scratch_shapes=[pltpu.VMEM((tm, d), BF16)],
        compiler_params=_cparams(("parallel", "arbitrary")),
        name="ffn",
    )(x, nw, wg, wu, wd)


P_GATES = 0
P_RQ = 6144
P_RK = 7168
P_RV = 8192
P_RG = 9216
P_Z = 10240
P_XS = 11264
P_BC = 12288
P_KPE = 12800
P_QLAT = 13056
P_CKV = 13824
P_TOTAL = 14336


def _inproj_kernel(x_ref, nw_ref, w_ref, wdt_ref, o_ref, dt_ref, xn_ref):
    j = pl.program_id(1)
    d = x_ref.shape[-1]

    @pl.when(j == 0)
    def _():
        x = x_ref[...]
        xn = (x * _rms(x, d) * nw_ref[...]).astype(BF16)
        xn_ref[...] = xn
        dt_ref[...] = _dot(xn, wdt_ref[...])

    o_ref[...] = _dot(xn_ref[...], w_ref[...]).astype(BF16)


def _inproj(x, nw, w, wdt, *, tm=1024, tn=1024):
    m, d = x.shape
    n = w.shape[1]
    tm = min(tm, m)
    return pl.pallas_call(
        _inproj_kernel,
        out_shape=(jax.ShapeDtypeStruct((m, n), BF16), jax.ShapeDtypeStruct((m, LANES), F32)),
        grid=(m // tm, n // tn),
        in_specs=[
            pl.BlockSpec((tm, d), lambda i, j: (i, 0)),
            pl.BlockSpec((1, d), lambda i, j: (0, 0)),
            pl.BlockSpec((d, tn), lambda i, j: (0, j)),
            pl.BlockSpec((d, LANES), lambda i, j: (0, 0)),
        ],
        out_specs=(
            pl.BlockSpec((tm, tn), lambda i, j: (i, j)),
            pl.BlockSpec((tm, LANES), lambda i, j: (i, 0)),
        ),
        scratch_shapes=[pltpu.VMEM((tm, d), BF16)],
        compiler_params=_cparams(("parallel", "arbitrary")),
        name="inproj",
    )(x, nw, w, wdt)


def _split3(x):
    hi = x.astype(BF16)
    r1 = x - hi.astype(F32)
    mid = r1.astype(BF16)
    lo = (r1 - mid.astype(F32)).astype(BF16)
    return hi, mid, lo


def _ssd_kernel(xs_ref, bc_ref, z_ref, dt_ref, cwx_ref, cbx_ref, cwb_ref, cbb_ref,
                dtb_ref, alog_ref, dskip_ref, nw_ref, o_ref,
                xext_ref, bext_ref, state_ref, dtx_ref, ax_ref, y_ref):
    L = xs_ref.shape[0]
    P = SSM_HEAD_DIM
    N = SSM_STATE
    HG = SSM_HEADS // SSM_GROUPS
    GW = HG * P
    T = 8

    @pl.when(pl.program_id(1) == 0)
    def _():
        xext_ref[0:T, :] = jnp.zeros((T, xext_ref.shape[1]), F32)
        bext_ref[0:T, :] = jnp.zeros((T, bext_ref.shape[1]), F32)
        state_ref[...] = jnp.zeros(state_ref.shape, F32)

    def conv_silu(ext_ref, in_ref, w_ref, b_ref):
        ext_ref[T:T + L, :] = in_ref[...].astype(F32)
        acc = b_ref[...] + w_ref[0:1, :] * ext_ref[T - 3:T - 3 + L, :]
        for j in range(1, SSM_CONV):
            acc = acc + w_ref[j:j + 1, :] * ext_ref[T - 3 + j:T - 3 + j + L, :]
        ext_ref[0:T, :] = ext_ref[L:L + T, :]
        return _silu(acc)

    xc = conv_silu(xext_ref, xs_ref, cwx_ref, cbx_ref)
    bcc = conv_silu(bext_ref, bc_ref, cwb_ref, cbb_ref)

    dtr = dt_ref[...] + dtb_ref[...]
    dt = jnp.maximum(dtr, 0.0) + jnp.log1p(jnp.exp(-jnp.abs(dtr)))
    adt = dt * (-jnp.exp(alog_ref[...]))
    row = lax.broadcasted_iota(jnp.int32, (L, L), 0)
    col = lax.broadcasted_iota(jnp.int32, (L, L), 1)
    causal = row >= col
    tril = jnp.where(causal, 1.0, 0.0).astype(BF16)
    hi, mid, lo = _split3(adt)
    acs = _dot(tril, hi) + _dot(tril, mid) + _dot(tril, lo)
    acs_t = acs.T

    for h in range(SSM_HEADS):
        dtx_ref[:, h * P:(h + 1) * P] = jnp.broadcast_to(dt[:, h:h + 1], (L, P))
        ax_ref[:, h * P:(h + 1) * P] = jnp.broadcast_to(acs[:, h:h + 1], (L, P))
    ax = ax_ref[...]
    a_last = ax[L - 1:L, :]
    xd = xc * dtx_ref[...]
    xd16 = xd.astype(BF16)
    xdd16 = (xd * jnp.exp(a_last - ax)).astype(BF16)
    e_ax = jnp.exp(ax)
    e_last = jnp.exp(a_last)

    for g in range(SSM_GROUPS):
        bm = bcc[:, g * N:(g + 1) * N]
        cm16 = bcc[:, (SSM_GROUPS + g) * N:(SSM_GROUPS + g + 1) * N].astype(BF16)
        bm16 = bm.astype(BF16)
        bmt16 = bm.T.astype(BF16)
        cb = _dot_nt(cm16, bm16)
        gs = slice(g * GW, (g + 1) * GW)
        prev = state_ref[g]
        y_ref[:, gs] = _dot(cm16, prev.astype(BF16)) * e_ax[:, gs]
        state_ref[g] = prev * e_last[:, gs] + _dot(bmt16, xdd16[:, gs])
        for r in range(HG):
            h = g * HG + r
            seg = acs[:, h:h + 1] - acs_t[h:h + 1, :]
            decay = jnp.exp(jnp.where(causal, seg, -jnp.inf))
            hs = slice(h * P, (h + 1) * P)
            y_ref[:, hs] += _dot((cb * decay).astype(BF16), xd16[:, hs])

    y = (y_ref[...] + dskip_ref[...] * xc) * _silu(z_ref[...].astype(F32))
    for g in range(SSM_GROUPS):
        gs = slice(g * GW, (g + 1) * GW)
        yg = y[:, gs]
        o_ref[:, gs] = (yg * _rms(yg, GW) * nw_ref[:, gs]).astype(BF16)


def _ssd(proj, dt_raw, cwx, cbx, cwb, cbb, dtb, alog, dskip, nw, *, batch, seq):
    L = math.gcd(seq, SSD_CHUNK)
    nc = seq // L
    di = SSM_D_INNER
    bcw = 2 * SSM_GROUPS * SSM_STATE
    tok = lambda b, c: b * nc + c
    return pl.pallas_call(
        _ssd_kernel,
        out_shape=jax.ShapeDtypeStruct((batch * seq, di), BF16),
        grid=(batch, nc),
        in_specs=[
            pl.BlockSpec((L, di), lambda b, c: (tok(b, c), P_XS // di)),
            pl.BlockSpec((L, bcw), lambda b, c: (tok(b, c), P_BC // bcw)),
            pl.BlockSpec((L, di), lambda b, c: (tok(b, c), P_Z // di)),
            pl.BlockSpec((L, LANES), lambda b, c: (tok(b, c), 0)),
            _resident(cwx.shape), _resident(cbx.shape), _resident(cwb.shape),
            _resident(cbb.shape), _resident(dtb.shape), _resident(alog.shape),
            _resident(dskip.shape), _resident(nw.shape),
        ],
        out_specs=pl.BlockSpec((L, di), lambda b, c: (tok(b, c), 0)),
        scratch_shapes=[
            pltpu.VMEM((L + 8, di), F32),
            pltpu.VMEM((L + 8, bcw), F32),
            pltpu.VMEM((SSM_GROUPS, SSM_STATE, di // SSM_GROUPS), F32),
            pltpu.VMEM((L, di), F32),
            pltpu.VMEM((L, di), F32),
            pltpu.VMEM((L, di), F32),
        ],
        compiler_params=_cparams(("parallel", "arbitrary")),
        name="ssd",
    )(proj, proj, proj, dt_raw, cwx, cbx, cwb, cbb, dtb, alog, dskip, nw)


def _rope_tile(x, c, s1, s2):
    q = MLA_ROPE // 2
    return x * c + pltpu.roll(x, LANES - q, 1) * s1 + pltpu.roll(x, q, 1) * s2


def _mla_prep_kernel(ql_ref, ckv_ref, kpe_ref, c_ref, s1_ref, s2_ref, qan_ref, kvan_ref,
                     qn_ref, kn_ref, wq_ref, wkv_ref, q_out, k_out, v_out, *, q_scale):
    ql = ql_ref[...].astype(F32)
    qa = (ql * _rms(ql, MLA_Q_LORA) * qan_ref[...]).astype(BF16)
    ckv = ckv_ref[...].astype(F32)
    kva = (ckv * _rms(ckv, MLA_KV_LORA) * kvan_ref[...]).astype(BF16)
    kpe = kpe_ref[...].astype(F32)
    kpe_ss = jnp.sum(kpe * kpe, axis=-1, keepdims=True)
    c, s1, s2 = c_ref[...], s1_ref[...], s2_ref[...]
    qn, kn = qn_ref[...], kn_ref[...]
    for h in range(MLA_HEADS):
        qh = _dot(qa, wq_ref[h])
        qh = qh * _rms(qh, MLA_QK_DIM) * qn
        q_out[h, :, 0:MLA_NOPE] = (qh[:, 0:MLA_NOPE] * q_scale).astype(BF16)
        q_out[h, :, MLA_NOPE:] = (_rope_tile(qh[:, MLA_NOPE:], c, s1, s2) * q_scale).astype(BF16)
        kvh = _dot(kva, wkv_ref[h])
        kno = kvh[:, 0:MLA_NOPE]
        r = lax.rsqrt((jnp.sum(kno * kno, axis=-1, keepdims=True) + kpe_ss)
                      * (1.0 / MLA_QK_DIM) + NORM_EPS)
        k_out[h, :, 0:MLA_NOPE] = (kno * r * kn[:, 0:MLA_NOPE]).astype(BF16)
        k_out[h, :, MLA_NOPE:] = _rope_tile(kpe * r * kn[:, MLA_NOPE:], c, s1, s2).astype(BF16)
        v_out[h, :, 0:MLA_V] = kvh[:, MLA_NOPE:].astype(BF16)
        v_out[h, :, MLA_V:] = jnp.ones((kvh.shape[0], MLA_V), BF16)


def _mla_prep(proj, c, s1, s2, qan, kvan, qn, kn, wq, wkv, *, tm=512):
    m = proj.shape[0]
    tm = min(tm, m)
    row = lambda i: (i, 0)
    return pl.pallas_call(
        functools.partial(_mla_prep_kernel, q_scale=MLA_QK_DIM ** -0.5 * math.log2(math.e)),
        out_shape=(
            jax.ShapeDtypeStruct((MLA_HEADS, m, MLA_QK_PAD), BF16),
            jax.ShapeDtypeStruct((MLA_HEADS, m, MLA_QK_PAD), BF16),
            jax.ShapeDtypeStruct((MLA_HEADS, m, 2 * MLA_V), BF16),
        ),
        grid=(m // tm,),
        in_specs=[
            pl.BlockSpec((tm, MLA_Q_LORA), lambda i: (i, P_QLAT // MLA_Q_LORA)),
            pl.BlockSpec((tm, MLA_KV_LORA), lambda i: (i, P_CKV // MLA_KV_LORA)),
            pl.BlockSpec((tm, LANES), lambda i: (i, P_KPE // LANES)),
            pl.BlockSpec((tm, LANES), row), pl.BlockSpec((tm, LANES), row),
            pl.BlockSpec((tm, LANES), row),
            _resident(qan.shape), _resident(kvan.shape), _resident(qn.shape),
            _resident(kn.shape), _resident(wq.shape), _resident(wkv.shape),
        ],
        out_specs=(
            pl.BlockSpec((MLA_HEADS, tm, MLA_QK_PAD), lambda i: (0, i, 0)),
            pl.BlockSpec((MLA_HEADS, tm, MLA_QK_PAD), lambda i: (0, i, 0)),
            pl.BlockSpec((MLA_HEADS, tm, 2 * MLA_V), lambda i: (0, i, 0)),
        ),
        compiler_params=_cparams(("parallel",)),
        name="mla_prep",
    )(proj, proj, proj, c, s1, s2, qan, kvan, qn, kn, wq, wkv)


def _attn_kernel(q_ref, k_ref, v_ref, o_ref, *, blk, heads, rows):
    i = pl.program_id(2)
    chains = [(h, r) for h in range(heads) for r in range(blk // rows)]

    def step(j, carry, masked):
        start = pl.multiple_of(j * blk, blk)
        scores = []
        for h, r in chains:
            s = _dot_nt(q_ref[h, r * rows:(r + 1) * rows, :], k_ref[h, pl.ds(start, blk), :])
            if masked:
                row = lax.broadcasted_iota(jnp.int32, (rows, blk), 0) + r * rows
                col = lax.broadcasted_iota(jnp.int32, (rows, blk), 1)
                s = jnp.where(row >= col, s, -jnp.inf)
            scores.append(s)
        probs = []
        for (m, acc), s in zip(carry, scores):
            m_new = jnp.maximum(m, jnp.max(s, axis=-1, keepdims=True))
            probs.append((m_new, jnp.exp2(m - m_new) * acc, jnp.exp2(s - m_new).astype(BF16)))
        out = []
        for (h, r), (m_new, acc, p) in zip(chains, probs):
            out.append((m_new, acc + _dot(p, v_ref[h, pl.ds(start, blk), :])))
        return tuple(out)

    init = tuple((jnp.full((rows, 1), -jnp.inf, F32), jnp.zeros((rows, 2 * MLA_V), F32))
                 for _ in chains)
    carry = lax.fori_loop(0, i, lambda j, c: step(j, c, False), init)
    carry = step(i, carry, True)
    for (h, r), (_, acc) in zip(chains, carry):
        o_ref[r * rows:(r + 1) * rows, h * MLA_V:(h + 1) * MLA_V] = (
            acc[:, :MLA_V] / acc[:, MLA_V:]).astype(BF16)


def _attention(q, k, v, *, batch, seq, blk=512, heads=4, rows=512):
    blk = min(blk, seq)
    nq = seq // blk
    return pl.pallas_call(
        functools.partial(_attn_kernel, blk=blk, heads=heads, rows=min(rows, blk)),
        out_shape=jax.ShapeDtypeStruct((batch * seq, MLA_HEADS * MLA_V), BF16),
        grid=(batch, MLA_HEADS // heads, nq),
        in_specs=[
            pl.BlockSpec((heads, blk, MLA_QK_PAD), lambda b, h, i: (h, b * nq + i, 0)),
            pl.BlockSpec((heads, seq, MLA_QK_PAD), lambda b, h, i: (h, b, 0)),
            pl.BlockSpec((heads, seq, 2 * MLA_V), lambda b, h, i: (h, b, 0)),
        ],
        out_specs=pl.BlockSpec((blk, heads * MLA_V), lambda b, h, i: (b * nq + i, h)),
        compiler_params=_cparams(("parallel", "parallel", "arbitrary")),
        name="mla_attention",
    )(q, k, v)


def _ret_log_gamma():
    expo = 5.0 + 7.0 * np.arange(RET_HEADS, dtype=np.float32) / np.float32(RET_HEADS - 1)
    return [float(v) for v in np.log1p(-np.exp2(-expo).astype(np.float32)).astype(np.float32)]


def _ret_kernel(rq_ref, rk_ref, rv_ref, rg_ref, cos_ref, sin_ref, nw_ref, o_ref, state_ref,
                *, log_gamma):
    L = rq_ref.shape[0]
    D = RET_HEAD
    half = D // 2

    @pl.when(pl.program_id(1) == 0)
    def _():
        state_ref[...] = jnp.zeros(state_ref.shape, F32)

    cos, sin = cos_ref[...], sin_ref[...]
    row = lax.broadcasted_iota(jnp.int32, (L, L), 0)
    col = lax.broadcasted_iota(jnp.int32, (L, L), 1)
    rel = (row - col).astype(F32)
    pos = lax.broadcasted_iota(jnp.int32, (L, D), 0).astype(F32)

    def rope(ref, hs):
        x1 = ref[:, hs.start:hs.start + half].astype(F32)
        x2 = ref[:, hs.start + half:hs.stop].astype(F32)
        return jnp.concatenate([x1 * cos - x2 * sin, x1 * sin + x2 * cos], axis=-1)

    for h in range(RET_HEADS):
        lg = log_gamma[h]
        hs = slice(h * D, (h + 1) * D)
        q = rope(rq_ref, hs)
        k = rope(rk_ref, hs) * (RET_HEAD ** -0.5)
        v16 = rv_ref[:, hs]
        q16 = q.astype(BF16)
        dmask = jnp.where(rel >= 0, jnp.exp(rel * lg), 0.0)
        scores = _dot_nt(q16, k.astype(BF16)) * dmask
        y = _dot(scores.astype(BF16), v16)
        prev = state_ref[h]
        y = y + _dot(q16, prev.astype(BF16)) * jnp.exp((pos + 1.0) * lg)
        kd = k * jnp.exp((L - 1.0 - pos) * lg)
        state_ref[h] = prev * math.exp(L * lg) + _dot(kd.T.astype(BF16), v16)
        y = y * _rms(y, D) * nw_ref[:, hs]
        o_ref[:, hs] = (_silu(rg_ref[:, hs].astype(F32)) * y).astype(BF16)


def _retention(proj, cos, sin, nw, *, batch, seq):
    L = math.gcd(seq, RET_CHUNK)
    nc = seq // L
    w = RET_HEADS * RET_HEAD
    tok = lambda b, c: b * nc + c
    return pl.pallas_call(
        functools.partial(_ret_kernel, log_gamma=_ret_log_gamma()),
        out_shape=jax.ShapeDtypeStruct((batch * seq, w), BF16),
        grid=(batch, nc),
        in_specs=[
            pl.BlockSpec((L, w), lambda b, c: (tok(b, c), P_RQ // w)),
            pl.BlockSpec((L, w), lambda b, c: (tok(b, c), P_RK // w)),
            pl.BlockSpec((L, w), lambda b, c: (tok(b, c), P_RV // w)),
            pl.BlockSpec((L, w), lambda b, c: (tok(b, c), P_RG // w)),
            pl.BlockSpec((L, RET_HEAD // 2), lambda b, c: (tok(b, c), 0)),
            pl.BlockSpec((L, RET_HEAD // 2), lambda b, c: (tok(b, c), 0)),
            _resident(nw.shape),
        ],
        out_specs=pl.BlockSpec((L, w), lambda b, c: (tok(b, c), 0)),
        scratch_shapes=[pltpu.VMEM((RET_HEADS, RET_HEAD, RET_HEAD), F32)],
        compiler_params=_cparams(("parallel", "arbitrary")),
        name="retention",
    )(proj, proj, proj, proj, cos, sin, nw)


def _merge_kernel(ys_ref, ym_ref, yr_ref, g0_ref, g1_ref, g2_ref, gb_ref, x_ref,
                  w0_ref, w1_ref, w2_ref, wo_ref, o_ref):
    d = x_ref.shape[-1]

    def branch(k, y_ref, g_ref, w_ref):
        gate = jax.nn.sigmoid(g_ref[...].astype(F32) + gb_ref[:, k * d:(k + 1) * d])
        return gate * _dot(y_ref[...], w_ref[...])

    merged = (branch(0, ys_ref, g0_ref, w0_ref) + branch(1, ym_ref, g1_ref, w1_ref)
              + branch(2, yr_ref, g2_ref, w2_ref))
    o_ref[...] = x_ref[...] + _dot(merged.astype(BF16), wo_ref[...])


def _merge(ys, ym, yr, proj, gb, x, w0, w1, w2, wo, *, tm=256):
    m, d = x.shape
    tm = min(tm, m)
    bw = ys.shape[1]
    row = lambda i: (i, 0)
    return pl.pallas_call(
        _merge_kernel,
        out_shape=jax.ShapeDtypeStruct((m, d), F32),
        grid=(m // tm,),
        in_specs=[
            pl.BlockSpec((tm, bw), row), pl.BlockSpec((tm, bw), row), pl.BlockSpec((tm, bw), row),
            pl.BlockSpec((tm, d), lambda i: (i, P_GATES // d)),
            pl.BlockSpec((tm, d), lambda i: (i, P_GATES // d + 1)),
            pl.BlockSpec((tm, d), lambda i: (i, P_GATES // d + 2)),
            _resident(gb.shape),
            pl.BlockSpec((tm, d), row),
            _resident(w0.shape), _resident(w1.shape), _resident(w2.shape), _resident(wo.shape),
        ],
        out_specs=pl.BlockSpec((tm, d), row),
        compiler_params=_cparams(("parallel",)),
        name="merge",
    )(ys, ym, yr, proj, proj, proj, gb, x, w0, w1, w2, wo)


def _rope_angles(positions, dim):
    inv = 1.0 / (ROPE_THETA ** (jnp.arange(0, dim, 2, dtype=F32) / dim))
    return positions.astype(F32).reshape(-1, 1) * inv


def _relayout_w_in(w):
    d = w.shape[0]
    o = np.cumsum([0, 1024, 1536, 16, 768, 576, 1024, 1024, 1024, 1024, 6144]).tolist()
    z, xbc, dt, qlat, kvlat, rq, rk, rv, rg, gates = (w[:, o[i]:o[i + 1]] for i in range(10))
    zeros = lambda n: jnp.zeros((d, n), w.dtype)
    main = jnp.concatenate([
        gates, rq, rk, rv, rg, z, xbc,
        kvlat[:, MLA_KV_LORA:], zeros(LANES - MLA_ROPE), zeros(P_QLAT - P_KPE - LANES),
        qlat, kvlat[:, :MLA_KV_LORA]], axis=1)
    assert main.shape[1] == P_TOTAL
    wdt = jnp.concatenate([dt, zeros(LANES - dt.shape[1])], axis=1)
    return main.astype(BF16), wdt.astype(BF16)


def _pad_lanes(v, n):
    return jnp.concatenate([v, jnp.zeros((n - v.shape[0],), v.dtype)]).reshape(1, n)


def kernel(x, positions, ffn1_norm, ffn1_w_gate, ffn1_w_up, ffn1_w_down, mix_norm, w_in, gate_b, conv_w, conv_b, dt_bias, a_log, d_skip, ssm_norm, q_a_norm, w_q_b, kv_a_norm, w_kv_b, q_norm, k_norm, ret_norm, w_br_ssm, w_br_mla, w_br_ret, w_out, ffn2_norm, ffn2_w_gate, ffn2_w_up, ffn2_w_down):
    batch, seq, d = x.shape
    depth = w_in.shape[0]
    x = x.reshape(batch * seq, d)

    ang = _rope_angles(positions, MLA_ROPE)
    cm, sm = jnp.cos(ang), jnp.sin(ang)
    zq = jnp.zeros_like(cm)
    rope_c = jnp.concatenate([cm, cm, zq, zq], axis=1)
    rope_s1 = jnp.concatenate([-sm, zq, zq, zq], axis=1)
    rope_s2 = jnp.concatenate([zq, sm, zq, zq], axis=1)
    ang = _rope_angles(positions, RET_HEAD)
    cos_ret, sin_ret = jnp.cos(ang), jnp.sin(ang)

    row = lambda v: v.reshape(1, -1)
    for l in range(depth):
        x = _ffn(x, row(ffn1_norm[l]), ffn1_w_gate[l].astype(BF16), ffn1_w_up[l].astype(BF16),
                 ffn1_w_down[l].astype(BF16))

        w_main, w_dt = _relayout_w_in(w_in[l])
        proj, dt_raw = _inproj(x, row(mix_norm[l]), w_main, w_dt)

        y_ssm = _ssd(
            proj, dt_raw,
            conv_w[l][:, :SSM_D_INNER], row(conv_b[l][:SSM_D_INNER]),
            conv_w[l][:, SSM_D_INNER:], row(conv_b[l][SSM_D_INNER:]),
            _pad_lanes(dt_bias[l], LANES), _pad_lanes(a_log[l], LANES),
            row(jnp.repeat(d_skip[l], SSM_HEAD_DIM)), row(ssm_norm[l]),
            batch=batch, seq=seq)

        wq = w_q_b[l].reshape(MLA_Q_LORA, MLA_HEADS, MLA_QK_DIM)
        wq = jnp.pad(wq, ((0, 0), (0, 0), (0, MLA_QK_PAD - MLA_QK_DIM)))
        wq = wq.transpose(1, 0, 2).astype(BF16)
        wkv = w_kv_b[l].reshape(MLA_KV_LORA, MLA_HEADS, MLA_NOPE + MLA_V)
        wkv = wkv.transpose(1, 0, 2).astype(BF16)
        q, k, v = _mla_prep(
            proj, rope_c, rope_s1, rope_s2, row(q_a_norm[l]), row(kv_a_norm[l]),
            _pad_lanes(q_norm[l], MLA_QK_PAD), _pad_lanes(k_norm[l], MLA_QK_PAD), wq, wkv)
        y_mla = _attention(q, k, v, batch=batch, seq=seq)

        y_ret = _retention(proj, cos_ret, sin_ret, row(ret_norm[l]), batch=batch, seq=seq)

        x = _merge(y_ssm, y_mla, y_ret, proj, row(gate_b[l]), x,
                   w_br_ssm[l].astype(BF16), w_br_mla[l].astype(BF16),
                   w_br_ret[l].astype(BF16), w_out[l].astype(BF16))

        x = _ffn(x, row(ffn2_norm[l]), ffn2_w_gate[l].astype(BF16), ffn2_w_up[l].astype(BF16),
                 ffn2_w_down[l].astype(BF16))
    return x.reshape(batch, seq, d)
```

```python
import functools
import math

import jax
import jax.numpy as jnp
import numpy as np
from jax import lax
from jax.experimental import pallas as pl
from jax.experimental.pallas import tpu as pltpu

F32 = jnp.float32
BF16 = jnp.bfloat16

NORM_EPS = 1e-6
ROPE_THETA = 10000.0

SSM_HEADS = 16
SSM_HEAD_DIM = 64
SSM_D_INNER = SSM_HEADS * SSM_HEAD_DIM
SSM_GROUPS = 2
SSM_STATE = 128
SSM_CONV = 4
SSD_CHUNK = 256
MLA_HEADS = 8
MLA_Q_LORA = 768
MLA_KV_LORA = 512
MLA_NOPE = 128
MLA_ROPE = 64
MLA_QK_DIM = MLA_NOPE + MLA_ROPE
MLA_V = 128
RET_HEADS = 4
RET_HEAD = 256
RET_CHUNK = 256
N_BRANCH = 3

LANES = 128
MLA_QK_PAD = MLA_NOPE + LANES
VMEM_LIMIT = 56 * 1024 * 1024


def _cparams(sem):
    return pltpu.CompilerParams(dimension_semantics=sem, vmem_limit_bytes=VMEM_LIMIT)


def _resident(shape):
    nd = len(shape)
    return pl.BlockSpec(shape, lambda *_: (0,) * nd, pipeline_mode=pl.Buffered(1))


def _rms(x, n):
    return lax.rsqrt(jnp.sum(x * x, axis=-1, keepdims=True) * (1.0 / n) + NORM_EPS)


def _sigmoid(x):
    return 0.5 * jnp.tanh(0.5 * x) + 0.5


def _silu(x):
    h = 0.5 * x
    return h * jnp.tanh(h) + h


def _dot(a, b):
    return jnp.dot(a, b, preferred_element_type=F32)


def _dot_nt(a, b):
    return lax.dot_general(a, b, (((1,), (1,)), ((), ())), preferred_element_type=F32)


def _ffn_kernel(x_ref, nw_ref, wg_ref, wu_ref, wd_ref, o_ref, xn_ref):
    d = x_ref.shape[-1]

    @pl.when(pl.program_id(1) == 0)
    def _():
        x = x_ref[...]
        xn_ref[...] = (x * _rms(x, d) * nw_ref[...]).astype(BF16)
        o_ref[...] = x

    xn = xn_ref[...]
    g = _dot(xn, wg_ref[...])
    u = _dot(xn, wu_ref[...])
    o_ref[...] += _dot((_silu(g) * (0.5 * u)).astype(BF16), wd_ref[...])


def _ffn(x, nw, wg, wu, wd, *, tm=1024, tf=512):
    m, d = x.shape
    f = wg.shape[1]
    tm = min(tm, m)
    return pl.pallas_call(
        _ffn_kernel,
        out_shape=jax.ShapeDtypeStruct((m, d), F32),
        grid=(m // tm, f // tf),
        in_specs=[
            pl.BlockSpec((tm, d), lambda i, j: (i, 0)),
            pl.BlockSpec((1, d), lambda i, j: (0, 0)),
            pl.BlockSpec((d, tf), lambda i, j: (0, j)),
            pl.BlockSpec((d, tf), lambda i, j: (0, j)),
            pl.BlockSpec((tf, d), lambda i, j: (j, 0)),
        ],
        out_specs=pl.BlockSpec((tm, d), lambda i, j: (i, 0)),
        scratch_shapes=[pltpu.VMEM((tm, d), BF16)],
        compiler_params=_cparams(("parallel", "arbitrary")),
        name="ffn",
    )(x, nw, wg, wu, wd)


P_GATES = 0
P_RQ = 6144
P_RK = 7168
P_RV = 8192
P_RG = 9216
P_Z = 10240
P_XS = 11264
P_BC = 12288
P_KPE = 12800
P_QLAT = 13056
P_CKV = 13824
P_TOTAL = 14336


def _inproj_kernel(x_ref, nw_ref, w_ref, wdt_ref, o_ref, dt_ref, xn_ref):
    j = pl.program_id(1)
    d = x_ref.shape[-1]

    @pl.when(j == 0)
    def _():
        x = x_ref[...]
        xn = (x * _rms(x, d) * nw_ref[...]).astype(BF16)
        xn_ref[...] = xn
        dt_ref[...] = _dot(xn, wdt_ref[...])

    o_ref[...] = _dot(xn_ref[...], w_ref[...]).astype(BF16)


def _inproj(x, nw, w, wdt, *, tm=1024, tn=1024):
    m, d = x.shape
    n = w.shape[1]
    tm = min(tm, m)
    return pl.pallas_call(
        _inproj_kernel,
        out_shape=(jax.ShapeDtypeStruct((m, n), BF16), jax.ShapeDtypeStruct((m, LANES), F32)),
        grid=(m // tm, n // tn),
        in_specs=[
            pl.BlockSpec((tm, d), lambda i, j: (i, 0)),
            pl.BlockSpec((1, d), lambda i, j: (0, 0)),
            pl.BlockSpec((d, tn), lambda i, j: (0, j)),
            pl.BlockSpec((d, LANES), lambda i, j: (0, 0)),
        ],
        out_specs=(
            pl.BlockSpec((tm, tn), lambda i, j: (i, j)),
            pl.BlockSpec((tm, LANES), lambda i, j: (i, 0)),
        ),
        scratch_shapes=[pltpu.VMEM((tm, d), BF16)],
        compiler_params=_cparams(("parallel", "arbitrary")),
        name="inproj",
    )(x, nw, w, wdt)


def _split3(x):
    hi = x.astype(BF16)
    r1 = x - hi.astype(F32)
    mid = r1.astype(BF16)
    lo = (r1 - mid.astype(F32)).astype(BF16)
    return hi, mid, lo


def _ssd_constants(L):
    t = np.arange(L)
    shifts = np.stack([(t[:, None] - t[None, :] == j) for j in range(1, SSM_CONV)])
    tril = t[:, None] >= t[None, :]
    expand = np.zeros((LANES, SSM_D_INNER), bool)
    for h in range(SSM_HEADS):
        expand[h, h * SSM_HEAD_DIM:(h + 1) * SSM_HEAD_DIM] = True
    expand2 = np.concatenate([expand, expand])
    expand4 = np.concatenate([expand, expand, expand, np.zeros_like(expand)])
    return tuple(jnp.asarray(m, BF16) for m in (shifts, tril, expand2, expand4))


def _ssd_kernel(xs_ref, bc_ref, z_ref, dt_ref, cwx_ref, cbx_ref, cwb_ref, cbb_ref,
                dtb_ref, alog_ref, dskip_ref, nw_ref, shift_ref, tril_ref, e2_ref, e4_ref,
                o_ref, xtail_ref, btail_ref, state_ref):
    L = xs_ref.shape[0]
    P = SSM_HEAD_DIM
    N = SSM_STATE
    HG = SSM_HEADS // SSM_GROUPS
    GW = HG * P
    T = 8

    @pl.when(pl.program_id(1) == 0)
    def _():
        xtail_ref[0:T, :] = jnp.zeros((T, xtail_ref.shape[1]), F32)
        btail_ref[0:T, :] = jnp.zeros((T, btail_ref.shape[1]), F32)
        state_ref[...] = jnp.zeros(state_ref.shape, F32)

    def conv_silu(tail_ref, in_ref, w_ref, b_ref):
        x16 = in_ref[...]
        xf = x16.astype(F32)
        acc = b_ref[...] + w_ref[SSM_CONV - 1:SSM_CONV, :] * xf
        for j in range(1, SSM_CONV):
            acc = acc + w_ref[SSM_CONV - 1 - j:SSM_CONV - j, :] * _dot(shift_ref[j - 1], x16)
        tail_ref[T:2 * T, :] = xf[0:T, :]
        head = b_ref[...] + w_ref[0:1, :] * tail_ref[T - 3:2 * T - 3, :]
        for j in range(1, SSM_CONV):
            head = head + w_ref[j:j + 1, :] * tail_ref[T - 3 + j:2 * T - 3 + j, :]
        tail_ref[0:T, :] = xf[L - T:L, :]
        return _silu(jnp.concatenate([head, acc[T:, :]], axis=0))

    xc = conv_silu(xtail_ref, xs_ref, cwx_ref, cbx_ref)
    bcc = conv_silu(btail_ref, bc_ref, cwb_ref, cbb_ref)

    dtr = dt_ref[...] + dtb_ref[...]
    dt = jnp.maximum(dtr, 0.0) + jnp.log1p(jnp.exp(-jnp.abs(dtr)))
    adt = dt * (-jnp.exp(alog_ref[...]))
    tril = tril_ref[...]
    hi, mid, lo = _split3(adt)
    acs = _dot(tril, hi) + _dot(tril, mid) + _dot(tril, lo)
    acs_t = acs.T

    hi, mid, _ = _split3(dt)
    dtx = _dot(jnp.concatenate([hi, mid], axis=1), e2_ref[...])
    hi, mid, lo = _split3(acs)
    ax = _dot(jnp.concatenate([hi, mid, lo, jnp.zeros_like(lo)], axis=1), e4_ref[...])
    a_last = ax[L - 1:L, :]
    xd = xc * dtx
    xd16 = xd.astype(BF16)
    xdd16 = (xd * jnp.exp(a_last - ax)).astype(BF16)
    e_ax = jnp.exp(ax)
    e_last = jnp.exp(a_last)

    causal = (lax.broadcasted_iota(jnp.int32, (L, L), 0)
              >= lax.broadcasted_iota(jnp.int32, (L, L), 1))
    first_half = lax.broadcasted_iota(jnp.int32, (L, LANES), 1) < P
    tiles = []
    for g in range(SSM_GROUPS):
        bm = bcc[:, g * N:(g + 1) * N]
        cm16 = bcc[:, (SSM_GROUPS + g) * N:(SSM_GROUPS + g + 1) * N].astype(BF16)
        cb = _dot_nt(cm16, bm.astype(BF16))
        gs = slice(g * GW, (g + 1) * GW)
        prev = state_ref[g]
        y_off = _dot(cm16, prev.astype(BF16)) * e_ax[:, gs]
        state_ref[g] = prev * e_last[:, gs] + _dot(bm.T.astype(BF16), xdd16[:, gs])
        for t in range(GW // LANES):
            ts = slice(g * GW + t * LANES, g * GW + (t + 1) * LANES)
            pair = []
            for k in range(LANES // P):
                h = (g * GW + t * LANES) // P + k
                seg = acs[:, h:h + 1] - acs_t[h:h + 1, :]
                decay = jnp.exp(jnp.where(causal, seg, -jnp.inf))
                pair.append(_dot((cb * decay).astype(BF16), xd16[:, ts]))
            tiles.append(jnp.where(first_half, pair[0], pair[1])
                         + y_off[:, t * LANES:(t + 1) * LANES])

    y = jnp.concatenate(tiles, axis=1)
    y = (y + dskip_ref[...] * xc) * _silu(z_ref[...].astype(F32))
    for g in range(SSM_GROUPS):
        gs = slice(g * GW, (g + 1) * GW)
        yg = y[:, gs]
        o_ref[:, gs] = (yg * _rms(yg, GW) * nw_ref[:, gs]).astype(BF16)


def _ssd(proj, dt_raw, cwx, cbx, cwb, cbb, dtb, alog, dskip, nw, *, batch, seq):
    L = math.gcd(seq, SSD_CHUNK)
    nc = seq // L
    di = SSM_D_INNER
    bcw = 2 * SSM_GROUPS * SSM_STATE
    tok = lambda b, c: b * nc + c
    consts = _ssd_constants(L)
    params = (cwx, cbx, cwb, cbb, dtb, alog, dskip, nw) + consts
    return pl.pallas_call(
        _ssd_kernel,
        out_shape=jax.ShapeDtypeStruct((batch * seq, di), BF16),
        grid=(batch, nc),
        in_specs=[
            pl.BlockSpec((L, di), lambda b, c: (tok(b, c), P_XS // di)),
            pl.BlockSpec((L, bcw), lambda b, c: (tok(b, c), P_BC // bcw)),
            pl.BlockSpec((L, di), lambda b, c: (tok(b, c), P_Z // di)),
            pl.BlockSpec((L, LANES), lambda b, c: (tok(b, c), 0)),
        ] + [_resident(p.shape) for p in params],
        out_specs=pl.BlockSpec((L, di), lambda b, c: (tok(b, c), 0)),
        scratch_shapes=[
            pltpu.VMEM((16, di), F32),
            pltpu.VMEM((16, bcw), F32),
            pltpu.VMEM((SSM_GROUPS, SSM_STATE, di // SSM_GROUPS), F32),
        ],
        compiler_params=_cparams(("parallel", "arbitrary")),
        name="ssd",
    )(proj, proj, proj, dt_raw, *params)


def _rope_tile(x, c, s1, s2):
    q = MLA_ROPE // 2
    return x * c + pltpu.roll(x, LANES - q, 1) * s1 + pltpu.roll(x, q, 1) * s2


def _mla_prep_kernel(ql_ref, ckv_ref, kpe_ref, c_ref, s1_ref, s2_ref, qan_ref, kvan_ref,
                     qn_ref, kn_ref, wq_ref, wkv_ref, q_out, k_out, v_out, *, q_scale):
    ql = ql_ref[...].astype(F32)
    qa = (ql * _rms(ql, MLA_Q_LORA) * qan_ref[...]).astype(BF16)
    ckv = ckv_ref[...].astype(F32)
    kva = (ckv * _rms(ckv, MLA_KV_LORA) * kvan_ref[...]).astype(BF16)
    kpe = kpe_ref[...].astype(F32)
    kpe_ss = jnp.sum(kpe * kpe, axis=-1, keepdims=True)
    c, s1, s2 = c_ref[...], s1_ref[...], s2_ref[...]
    qn, kn = qn_ref[...], kn_ref[...]
    for h in range(MLA_HEADS):
        qh = _dot(qa, wq_ref[h])
        qh = qh * _rms(qh, MLA_QK_DIM) * qn
        q_out[h, :, 0:MLA_NOPE] = (qh[:, 0:MLA_NOPE] * q_scale).astype(BF16)
        q_out[h, :, MLA_NOPE:] = (_rope_tile(qh[:, MLA_NOPE:], c, s1, s2) * q_scale).astype(BF16)
        kvh = _dot(kva, wkv_ref[h])
        kno = kvh[:, 0:MLA_NOPE]
        r = lax.rsqrt((jnp.sum(kno * kno, axis=-1, keepdims=True) + kpe_ss)
                      * (1.0 / MLA_QK_DIM) + NORM_EPS)
        k_out[h, :, 0:MLA_NOPE] = (kno * r * kn[:, 0:MLA_NOPE]).astype(BF16)
        k_out[h, :, MLA_NOPE:] = _rope_tile(kpe * r * kn[:, MLA_NOPE:], c, s1, s2).astype(BF16)
        v_out[h, :, 0:MLA_V] = kvh[:, MLA_NOPE:].astype(BF16)
        v_out[h, :, MLA_V:] = jnp.ones((kvh.shape[0], MLA_V), BF16)


def _mla_prep(proj, c, s1, s2, qan, kvan, qn, kn, wq, wkv, *, tm=512):
    m = proj.shape[0]
    tm = min(tm, m)
    row = lambda i: (i, 0)
    return pl.pallas_call(
        functools.partial(_mla_prep_kernel, q_scale=MLA_QK_DIM ** -0.5 * math.log2(math.e)),
        out_shape=(
            jax.ShapeDtypeStruct((MLA_HEADS, m, MLA_QK_PAD), BF16),
            jax.ShapeDtypeStruct((MLA_HEADS, m, MLA_QK_PAD), BF16),
            jax.ShapeDtypeStruct((MLA_HEADS, m, 2 * MLA_V), BF16),
        ),
        grid=(m // tm,),
        in_specs=[
            pl.BlockSpec((tm, MLA_Q_LORA), lambda i: (i, P_QLAT // MLA_Q_LORA)),
            pl.BlockSpec((tm, MLA_KV_LORA), lambda i: (i, P_CKV // MLA_KV_LORA)),
            pl.BlockSpec((tm, LANES), lambda i: (i, P_KPE // LANES)),
            pl.BlockSpec((tm, LANES), row), pl.BlockSpec((tm, LANES), row),
            pl.BlockSpec((tm, LANES), row),
            _resident(qan.shape), _resident(kvan.shape), _resident(qn.shape),
            _resident(kn.shape), _resident(wq.shape), _resident(wkv.shape),
        ],
        out_specs=(
            pl.BlockSpec((MLA_HEADS, tm, MLA_QK_PAD), lambda i: (0, i, 0)),
            pl.BlockSpec((MLA_HEADS, tm, MLA_QK_PAD), lambda i: (0, i, 0)),
            pl.BlockSpec((MLA_HEADS, tm, 2 * MLA_V), lambda i: (0, i, 0)),
        ),
        compiler_params=_cparams(("parallel",)),
        name="mla_prep",
    )(proj, proj, proj, c, s1, s2, qan, kvan, qn, kn, wq, wkv)


def _attn_kernel(q_ref, k_ref, v_ref, o_ref, *, blk, heads, rows):
    i = pl.program_id(2)
    chains = [(h, r) for h in range(heads) for r in range(blk // rows)]

    def step(j, carry, masked):
        start = pl.multiple_of(j * blk, blk)
        scores = []
        for h, r in chains:
            s = _dot_nt(q_ref[h, r * rows:(r + 1) * rows, :], k_ref[h, pl.ds(start, blk), :])
            if masked:
                row = lax.broadcasted_iota(jnp.int32, (rows, blk), 0) + r * rows
                col = lax.broadcasted_iota(jnp.int32, (rows, blk), 1)
                s = jnp.where(row >= col, s, -jnp.inf)
            scores.append(s)
        probs = []
        for (m, acc), s in zip(carry, scores):
            m_new = jnp.maximum(m, jnp.max(s, axis=-1, keepdims=True))
            probs.append((m_new, jnp.exp2(m - m_new) * acc, jnp.exp2(s - m_new).astype(BF16)))
        out = []
        for (h, r), (m_new, acc, p) in zip(chains, probs):
            out.append((m_new, acc + _dot(p, v_ref[h, pl.ds(start, blk), :])))
        return tuple(out)

    init = tuple((jnp.full((rows, 1), -jnp.inf, F32), jnp.zeros((rows, 2 * MLA_V), F32))
                 for _ in chains)
    carry = lax.fori_loop(0, i, lambda j, c: step(j, c, False), init)
    carry = step(i, carry, True)
    for (h, r), (_, acc) in zip(chains, carry):
        o_ref[r * rows:(r + 1) * rows, h * MLA_V:(h + 1) * MLA_V] = (
            acc[:, :MLA_V] / acc[:, MLA_V:]).astype(BF16)


def _attention(q, k, v, *, batch, seq, blk=512, heads=4, rows=512):
    blk = min(blk, seq)
    nq = seq // blk
    return pl.pallas_call(
        functools.partial(_attn_kernel, blk=blk, heads=heads, rows=min(rows, blk)),
        out_shape=jax.ShapeDtypeStruct((batch * seq, MLA_HEADS * MLA_V), BF16),
        grid=(batch, MLA_HEADS // heads, nq),
        in_specs=[
            pl.BlockSpec((heads, blk, MLA_QK_PAD), lambda b, h, i: (h, b * nq + i, 0)),
            pl.BlockSpec((heads, seq, MLA_QK_PAD), lambda b, h, i: (h, b, 0)),
            pl.BlockSpec((heads, seq, 2 * MLA_V), lambda b, h, i: (h, b, 0)),
        ],
        out_specs=pl.BlockSpec((blk, heads * MLA_V), lambda b, h, i: (b * nq + i, h)),
        compiler_params=_cparams(("parallel", "parallel", "arbitrary")),
        name="mla_attention",
    )(q, k, v)


def _ret_log_gamma():
    expo = 5.0 + 7.0 * np.arange(RET_HEADS, dtype=np.float32) / np.float32(RET_HEADS - 1)
    return [float(v) for v in np.log1p(-np.exp2(-expo).astype(np.float32)).astype(np.float32)]


def _ret_kernel(rq_ref, rk_ref, rv_ref, rg_ref, cos_ref, sin_ref, nw_ref, o_ref, state_ref,
                *, log_gamma):
    L = rq_ref.shape[0]
    D = RET_HEAD
    half = D // 2

    @pl.when(pl.program_id(1) == 0)
    def _():
        state_ref[...] = jnp.zeros(state_ref.shape, F32)

    cos, sin = cos_ref[...], sin_ref[...]
    row = lax.broadcasted_iota(jnp.int32, (L, L), 0)
    col = lax.broadcasted_iota(jnp.int32, (L, L), 1)
    rel = (row - col).astype(F32)
    pos = lax.broadcasted_iota(jnp.int32, (L, D), 0).astype(F32)

    def rope(ref, hs):
        x1 = ref[:, hs.start:hs.start + half].astype(F32)
        x2 = ref[:, hs.start + half:hs.stop].astype(F32)
        return jnp.concatenate([x1 * cos - x2 * sin, x1 * sin + x2 * cos], axis=-1)

    for h in range(RET_HEADS):
        lg = log_gamma[h]
        hs = slice(h * D, (h + 1) * D)
        q = rope(rq_ref, hs)
        k = rope(rk_ref, hs) * (RET_HEAD ** -0.5)
        v16 = rv_ref[:, hs]
        q16 = q.astype(BF16)
        dmask = jnp.where(rel >= 0, jnp.exp(rel * lg), 0.0)
        scores = _dot_nt(q16, k.astype(BF16)) * dmask
        y = _dot(scores.astype(BF16), v16)
        prev = state_ref[h]
        y = y + _dot(q16, prev.astype(BF16)) * jnp.exp((pos + 1.0) * lg)
        kd = k * jnp.exp((L - 1.0 - pos) * lg)
        state_ref[h] = prev * math.exp(L * lg) + _dot(kd.T.astype(BF16), v16)
        y = y * _rms(y, D) * nw_ref[:, hs]
        o_ref[:, hs] = (_silu(rg_ref[:, hs].astype(F32)) * y).astype(BF16)


def _retention(proj, cos, sin, nw, *, batch, seq):
    L = math.gcd(seq, RET_CHUNK)
    nc = seq // L
    w = RET_HEADS * RET_HEAD
    tok = lambda b, c: b * nc + c
    return pl.pallas_call(
        functools.partial(_ret_kernel, log_gamma=_ret_log_gamma()),
        out_shape=jax.ShapeDtypeStruct((batch * seq, w), BF16),
        grid=(batch, nc),
        in_specs=[
            pl.BlockSpec((L, w), lambda b, c: (tok(b, c), P_RQ // w)),
            pl.BlockSpec((L, w), lambda b, c: (tok(b, c), P_RK // w)),
            pl.BlockSpec((L, w), lambda b, c: (tok(b, c), P_RV // w)),
            pl.BlockSpec((L, w), lambda b, c: (tok(b, c), P_RG // w)),
            pl.BlockSpec((L, RET_HEAD // 2), lambda b, c: (tok(b, c), 0)),
            pl.BlockSpec((L, RET_HEAD // 2), lambda b, c: (tok(b, c), 0)),
            _resident(nw.shape),
        ],
        out_specs=pl.BlockSpec((L, w), lambda b, c: (tok(b, c), 0)),
        scratch_shapes=[pltpu.VMEM((RET_HEADS, RET_HEAD, RET_HEAD), F32)],
        compiler_params=_cparams(("parallel", "arbitrary")),
        name="retention",
    )(proj, proj, proj, proj, cos, sin, nw)


def _merge_kernel(ys_ref, ym_ref, yr_ref, g0_ref, g1_ref, g2_ref, gb_ref, x_ref,
                  w0_ref, w1_ref, w2_ref, wo_ref, o_ref):
    d = x_ref.shape[-1]

    def branch(k, y_ref, g_ref, w_ref):
        gate = _sigmoid(g_ref[...].astype(F32) + gb_ref[:, k * d:(k + 1) * d])
        return gate * _dot(y_ref[...], w_ref[...])

    merged = (branch(0, ys_ref, g0_ref, w0_ref) + branch(1, ym_ref, g1_ref, w1_ref)
              + branch(2, yr_ref, g2_ref, w2_ref))
    o_ref[...] = x_ref[...] + _dot(merged.astype(BF16), wo_ref[...])


def _merge(ys, ym, yr, proj, gb, x, w0, w1, w2, wo, *, tm=256):
    m, d = x.shape
    tm = min(tm, m)
    bw = ys.shape[1]
    row = lambda i: (i, 0)
    return pl.pallas_call(
        _merge_kernel,
        out_shape=jax.ShapeDtypeStruct((m, d), F32),
        grid=(m // tm,),
        in_specs=[
            pl.BlockSpec((tm, bw), row), pl.BlockSpec((tm, bw), row), pl.BlockSpec((tm, bw), row),
            pl.BlockSpec((tm, d), lambda i: (i, P_GATES // d)),
            pl.BlockSpec((tm, d), lambda i: (i, P_GATES // d + 1)),
            pl.BlockSpec((tm, d), lambda i: (i, P_GATES // d + 2)),
            _resident(gb.shape),
            pl.BlockSpec((tm, d), row),
            _resident(w0.shape), _resident(w1.shape), _resident(w2.shape), _resident(wo.shape),
        ],
        out_specs=pl.BlockSpec((tm, d), row),
        compiler_params=_cparams(("parallel",)),
        name="merge",
    )(ys, ym, yr, proj, proj, proj, gb, x, w0, w1, w2, wo)


def _rope_angles(positions, dim):
    inv = 1.0 / (ROPE_THETA ** (jnp.arange(0, dim, 2, dtype=F32) / dim))
    return positions.astype(F32).reshape(-1, 1) * inv


def _relayout_w_in(w):
    d = w.shape[0]
    w = w.astype(BF16)
    o = np.cumsum([0, 1024, 1536, 16, 768, 576, 1024, 1024, 1024, 1024, 6144]).tolist()
    z, xbc, dt, qlat, kvlat, rq, rk, rv, rg, gates = (w[:, o[i]:o[i + 1]] for i in range(10))
    zeros = lambda n: jnp.zeros((d, n), w.dtype)
    main = jnp.concatenate([
        gates, rq, rk, rv, rg, z, xbc,
        kvlat[:, MLA_KV_LORA:], zeros(LANES - MLA_ROPE), zeros(P_QLAT - P_KPE - LANES),
        qlat, kvlat[:, :MLA_KV_LORA]], axis=1)
    assert main.shape[1] == P_TOTAL
    wdt = jnp.concatenate([dt, zeros(LANES - dt.shape[1])], axis=1)
    return main, wdt


def _pad_lanes(v, n):
    return jnp.concatenate([v, jnp.zeros((n - v.shape[0],), v.dtype)]).reshape(1, n)


def kernel(x, positions, ffn1_norm, ffn1_w_gate, ffn1_w_up, ffn1_w_down, mix_norm, w_in, gate_b, conv_w, conv_b, dt_bias, a_log, d_skip, ssm_norm, q_a_norm, w_q_b, kv_a_norm, w_kv_b, q_norm, k_norm, ret_norm, w_br_ssm, w_br_mla, w_br_ret, w_out, ffn2_norm, ffn2_w_gate, ffn2_w_up, ffn2_w_down):
    batch, seq, d = x.shape
    depth = w_in.shape[0]
    x = x.reshape(batch * seq, d)

    ang = _rope_angles(positions, MLA_ROPE)
    cm, sm = jnp.cos(ang), jnp.sin(ang)
    zq = jnp.zeros_like(cm)
    rope_c = jnp.concatenate([cm, cm, zq, zq], axis=1)
    rope_s1 = jnp.concatenate([-sm, zq, zq, zq], axis=1)
    rope_s2 = jnp.concatenate([zq, sm, zq, zq], axis=1)
    ang = _rope_angles(positions, RET_HEAD)
    cos_ret, sin_ret = jnp.cos(ang), jnp.sin(ang)

    row = lambda v: v.reshape(1, -1)
    for l in range(depth):
        x = _ffn(x, row(ffn1_norm[l]), ffn1_w_gate[l].astype(BF16), ffn1_w_up[l].astype(BF16),
                 ffn1_w_down[l].astype(BF16))

        w_main, w_dt = _relayout_w_in(w_in[l])
        proj, dt_raw = _inproj(x, row(mix_norm[l]), w_main, w_dt)

        y_ssm = _ssd(
            proj, dt_raw,
            conv_w[l][:, :SSM_D_INNER], row(conv_b[l][:SSM_D_INNER]),
            conv_w[l][:, SSM_D_INNER:], row(conv_b[l][SSM_D_INNER:]),
            _pad_lanes(dt_bias[l], LANES), _pad_lanes(a_log[l], LANES),
            row(jnp.repeat(d_skip[l], SSM_HEAD_DIM)), row(ssm_norm[l]),
            batch=batch, seq=seq)

        wq = w_q_b[l].reshape(MLA_Q_LORA, MLA_HEADS, MLA_QK_DIM)
        wq = jnp.pad(wq, ((0, 0), (0, 0), (0, MLA_QK_PAD - MLA_QK_DIM)))
        wq = wq.transpose(1, 0, 2).astype(BF16)
        wkv = w_kv_b[l].reshape(MLA_KV_LORA, MLA_HEADS, MLA_NOPE + MLA_V)
        wkv = wkv.transpose(1, 0, 2).astype(BF16)
        q, k, v = _mla_prep(
            proj, rope_c, rope_s1, rope_s2, row(q_a_norm[l]), row(kv_a_norm[l]),
            _pad_lanes(q_norm[l], MLA_QK_PAD), _pad_lanes(k_norm[l], MLA_QK_PAD), wq, wkv)
        y_mla = _attention(q, k, v, batch=batch, seq=seq)

        y_ret = _retention(proj, cos_ret, sin_ret, row(ret_norm[l]), batch=batch, seq=seq)

        x = _merge(y_ssm, y_mla, y_ret, proj, row(gate_b[l]), x,
                   w_br_ssm[l].astype(BF16), w_br_mla[l].astype(BF16),
                   w_br_ret[l].astype(BF16), w_out[l].astype(BF16))

        x = _ffn(x, row(ffn2_norm[l]), ffn2_w_gate[l].astype(BF16), ffn2_w_up[l].astype(BF16),
                 ffn2_w_down[l].astype(BF16))
    return x.reshape(batch, seq, d)
```

```python
import functools
import math

import jax
import jax.numpy as jnp
import numpy as np
from jax import lax
from jax.experimental import pallas as pl
from jax.experimental.pallas import tpu as pltpu

F32 = jnp.float32
BF16 = jnp.bfloat16

NORM_EPS = 1e-6
ROPE_THETA = 10000.0

SSM_HEADS = 16
SSM_HEAD_DIM = 64
SSM_D_INNER = SSM_HEADS * SSM_HEAD_DIM
SSM_GROUPS = 2
SSM_STATE = 128
SSM_CONV = 4
SSD_CHUNK = 256
MLA_HEADS = 8
MLA_Q_LORA = 768
MLA_KV_LORA = 512
MLA_NOPE = 128
MLA_ROPE = 64
MLA_QK_DIM = MLA_NOPE + MLA_ROPE
MLA_V = 128
RET_HEADS = 4
RET_HEAD = 256
RET_CHUNK = 256
N_BRANCH = 3

LANES = 128
MLA_QK_PAD = MLA_NOPE + LANES
VMEM_LIMIT = 56 * 1024 * 1024


def _cparams(sem):
    return pltpu.CompilerParams(dimension_semantics=sem, vmem_limit_bytes=VMEM_LIMIT)


def _resident(shape):
    nd = len(shape)
    return pl.BlockSpec(shape, lambda *_: (0,) * nd, pipeline_mode=pl.Buffered(1))


def _rms(x, n):
    return lax.rsqrt(jnp.sum(x * x, axis=-1, keepdims=True) * (1.0 / n) + NORM_EPS)


def _sigmoid(x):
    return 0.5 * jnp.tanh(0.5 * x) + 0.5


def _silu(x):
    h = 0.5 * x
    return h * jnp.tanh(h) + h


def _dot(a, b):
    return jnp.dot(a, b, preferred_element_type=F32)


def _dot_nt(a, b):
    return lax.dot_general(a, b, (((1,), (1,)), ((), ())), preferred_element_type=F32)


def _ffn_kernel(x_ref, nw_ref, wg_ref, wu_ref, wd_ref, o_ref, xn_ref):
    d = x_ref.shape[-1]

    @pl.when(pl.program_id(1) == 0)
    def _():
        x = x_ref[...]
        xn_ref[...] = (x * _rms(x, d) * nw_ref[...]).astype(BF16)
        o_ref[...] = x

    xn = xn_ref[...]
    g = _dot(xn, wg_ref[...])
    u = _dot(xn, wu_ref[...])
    o_ref[...] += _dot((_silu(g) * (0.5 * u)).astype(BF16), wd_ref[...])


def _ffn(x, nw, wg, wu, wd, *, tm=1024, tf=512):
    m, d = x.shape
    f = wg.shape[1]
    tm = min(tm, m)
    return pl.pallas_call(
        _ffn_kernel,
        out_shape=jax.ShapeDtypeStruct((m, d), F32),
        grid=(m // tm, f // tf),
        in_specs=[
            pl.BlockSpec((tm, d), lambda i, j: (i, 0)),
            pl.BlockSpec((1, d), lambda i, j: (0, 0)),
            pl.BlockSpec((d, tf), lambda i, j: (0, j)),
            pl.BlockSpec((d, tf), lambda i, j: (0, j)),
            pl.BlockSpec((tf, d), lambda i, j: (j, 0)),
        ],
        out_specs=pl.BlockSpec((tm, d), lambda i, j: (i, 0)),
        scratch_shapes=[pltpu.VMEM((tm, d), BF16)],
        compiler_params=_cparams(("parallel", "arbitrary")),
        name="ffn",
    )(x, nw, wg, wu, wd)


P_GATES = 0
P_RQ = 6144
P_RK = 7168
P_RV = 8192
P_RG = 9216
P_Z = 10240
P_XS = 11264
P_BC = 12288
P_KPE = 12800
P_QLAT = 13056
P_CKV = 13824
P_TOTAL = 14336


def _inproj_kernel(x_ref, nw_ref, w_ref, wdt_ref, o_ref, dt_ref, xn_ref):
    j = pl.program_id(1)
    d = x_ref.shape[-1]

    @pl.when(j == 0)
    def _():
        x = x_ref[...]
        xn = (x * _rms(x, d) * nw_ref[...]).astype(BF16)
        xn_ref[...] = xn
        dt_ref[...] = _dot(xn, wdt_ref[...])

    o_ref[...] = _dot(xn_ref[...], w_ref[...]).astype(BF16)


def _inproj(x, nw, w_all, layer, *, tm=1024, tn=1024):
    m, d = x.shape
    n = P_TOTAL
    tm = min(tm, m)
    return pl.pallas_call(
        _inproj_kernel,
        out_shape=(jax.ShapeDtypeStruct((m, n), BF16), jax.ShapeDtypeStruct((m, LANES), F32)),
        grid=(m // tm, n // tn),
        in_specs=[
            pl.BlockSpec((tm, d), lambda i, j: (i, 0)),
            pl.BlockSpec((1, d), lambda i, j: (0, 0)),
            pl.BlockSpec((None, d, tn), lambda i, j: (layer, 0, j)),
            pl.BlockSpec((None, d, LANES), lambda i, j: (layer, 0, P_TOTAL // LANES)),
        ],
        out_specs=(
            pl.BlockSpec((tm, tn), lambda i, j: (i, j)),
            pl.BlockSpec((tm, LANES), lambda i, j: (i, 0)),
        ),
        scratch_shapes=[pltpu.VMEM((tm, d), BF16)],
        compiler_params=_cparams(("parallel", "arbitrary")),
        name="inproj",
    )(x, nw, w_all, w_all)


_W_IN_SEGMENTS = (
    (P_GATES, 8016, 6144), (P_RQ, 3920, 1024), (P_RK, 4944, 1024), (P_RV, 5968, 1024),
    (P_RG, 6992, 1024), (P_Z, 0, 1024), (P_XS, 1024, 1536), (P_KPE, 3856, MLA_ROPE),
    (P_QLAT, 2576, MLA_Q_LORA), (P_CKV, 3344, MLA_KV_LORA), (P_TOTAL, 2560, SSM_HEADS))
_W_IN_MAX_SHIFT = 80


def _w_in_tables():
    n_tiles = P_TOTAL // LANES + 1
    src = np.zeros(n_tiles, np.int32)
    mode = np.zeros(n_tiles, np.int32)
    sels = [np.zeros((2 * LANES, LANES), bool)]
    keys = {}
    for dest, source, width in _W_IN_SEGMENTS:
        for t in range(-(-width // LANES)):
            shift, lanes = source % LANES, min(LANES, width - t * LANES)
            assert shift <= _W_IN_MAX_SHIFT
            if (shift, lanes) not in keys:
                sel = np.zeros((2 * LANES, LANES), bool)
                sel[np.arange(lanes) + shift, np.arange(lanes)] = True
                keys[(shift, lanes)] = len(sels)
                sels.append(sel)
            src[dest // LANES + t] = source // LANES + t
            mode[dest // LANES + t] = keys[(shift, lanes)]
    return src, mode, np.stack(sels)


def _w_in_prep_kernel(src_ref, mode_ref, a_ref, b_ref, sel_ref, o_ref):
    lane = lax.broadcasted_iota(jnp.int32, b_ref.shape, 1)
    b = jnp.where(lane < _W_IN_MAX_SHIFT, b_ref[...], 0.0)
    ab = jnp.concatenate([a_ref[...], b], axis=1).astype(BF16)
    o_ref[...] = _dot(ab, sel_ref[mode_ref[pl.program_id(1)]]).astype(BF16)


def _w_in_prep(w_in):
    depth, d, n_src = w_in.shape
    src, mode, sels = _w_in_tables()
    n_tiles = src.shape[0]
    assert int(src.max()) + 1 <= (n_src - 1) // LANES
    return pl.pallas_call(
        _w_in_prep_kernel,
        out_shape=jax.ShapeDtypeStruct((depth, d, n_tiles * LANES), BF16),
        grid_spec=pltpu.PrefetchScalarGridSpec(
            num_scalar_prefetch=2,
            grid=(depth, n_tiles),
            in_specs=[
                pl.BlockSpec((None, d, LANES), lambda l, j, src, mode: (l, 0, src[j])),
                pl.BlockSpec((None, d, LANES), lambda l, j, src, mode: (l, 0, src[j] + 1)),
                pl.BlockSpec(sels.shape, lambda l, j, src, mode: (0, 0, 0)),
            ],
            out_specs=pl.BlockSpec((None, d, LANES), lambda l, j, src, mode: (l, 0, j)),
        ),
        compiler_params=_cparams(("parallel", "arbitrary")),
        name="w_in_prep",
    )(jnp.asarray(src), jnp.asarray(mode), w_in, w_in, jnp.asarray(sels, BF16))


def _split3(x):
    hi = x.astype(BF16)
    r1 = x - hi.astype(F32)
    mid = r1.astype(BF16)
    lo = (r1 - mid.astype(F32)).astype(BF16)
    return hi, mid, lo


def _ssd_constants(L):
    t = np.arange(L)
    shifts = np.stack([(t[:, None] - t[None, :] == j) for j in range(1, SSM_CONV)])
    tril = t[:, None] >= t[None, :]
    expand = np.zeros((LANES, SSM_D_INNER), bool)
    for h in range(SSM_HEADS):
        expand[h, h * SSM_HEAD_DIM:(h + 1) * SSM_HEAD_DIM] = True
    expand2 = np.concatenate([expand, expand])
    expand4 = np.concatenate([expand, expand, expand, np.zeros_like(expand)])
    return tuple(jnp.asarray(m, BF16) for m in (shifts, tril, expand2, expand4))


def _ssd_kernel(xs_ref, bc_ref, z_ref, dt_ref, cwx_ref, cbx_ref, cwb_ref, cbb_ref,
                dtb_ref, alog_ref, dskip_ref, nw_ref, shift_ref, tril_ref, e2_ref, e4_ref,
                o_ref, xtail_ref, btail_ref, state_ref):
    L = xs_ref.shape[0]
    P = SSM_HEAD_DIM
    N = SSM_STATE
    HG = SSM_HEADS // SSM_GROUPS
    GW = HG * P
    T = 8

    @pl.when(pl.program_id(1) == 0)
    def _():
        xtail_ref[0:T, :] = jnp.zeros((T, xtail_ref.shape[1]), F32)
        btail_ref[0:T, :] = jnp.zeros((T, btail_ref.shape[1]), F32)
        state_ref[...] = jnp.zeros(state_ref.shape, F32)

    def conv_silu(tail_ref, in_ref, w_ref, b_ref):
        x16 = in_ref[...]
        xf = x16.astype(F32)
        acc = b_ref[...] + w_ref[SSM_CONV - 1:SSM_CONV, :] * xf
        for j in range(1, SSM_CONV):
            acc = acc + w_ref[SSM_CONV - 1 - j:SSM_CONV - j, :] * _dot(shift_ref[j - 1], x16)
        tail_ref[T:2 * T, :] = xf[0:T, :]
        head = b_ref[...] + w_ref[0:1, :] * tail_ref[T - 3:2 * T - 3, :]
        for j in range(1, SSM_CONV):
            head = head + w_ref[j:j + 1, :] * tail_ref[T - 3 + j:2 * T - 3 + j, :]
        tail_ref[0:T, :] = xf[L - T:L, :]
        return _silu(jnp.concatenate([head, acc[T:, :]], axis=0))

    xc = conv_silu(xtail_ref, xs_ref, cwx_ref, cbx_ref)
    bcc = conv_silu(btail_ref, bc_ref, cwb_ref, cbb_ref)

    dtr = dt_ref[...] + dtb_ref[...]
    dt = jnp.maximum(dtr, 0.0) + jnp.log1p(jnp.exp(-jnp.abs(dtr)))
    adt = dt * (-jnp.exp(alog_ref[...]))
    tril = tril_ref[...]
    hi, mid, lo = _split3(adt)
    acs = _dot(tril, hi) + _dot(tril, mid) + _dot(tril, lo)
    acs_t = acs.T

    hi, mid, _ = _split3(dt)
    dtx = _dot(jnp.concatenate([hi, mid], axis=1), e2_ref[...])
    hi, mid, lo = _split3(acs)
    ax = _dot(jnp.concatenate([hi, mid, lo, jnp.zeros_like(lo)], axis=1), e4_ref[...])
    a_last = ax[L - 1:L, :]
    xd = xc * dtx
    xd16 = xd.astype(BF16)
    xdd16 = (xd * jnp.exp(a_last - ax)).astype(BF16)
    e_ax = jnp.exp(ax)
    e_last = jnp.exp(a_last)

    causal = (lax.broadcasted_iota(jnp.int32, (L, L), 0)
              >= lax.broadcasted_iota(jnp.int32, (L, L), 1))
    first_half = lax.broadcasted_iota(jnp.int32, (L, LANES), 1) < P
    tiles = []
    for g in range(SSM_GROUPS):
        bm = bcc[:, g * N:(g + 1) * N]
        cm16 = bcc[:, (SSM_GROUPS + g) * N:(SSM_GROUPS + g + 1) * N].astype(BF16)
        cb = _dot_nt(cm16, bm.astype(BF16))
        gs = slice(g * GW, (g + 1) * GW)
        prev = state_ref[g]
        y_off = _dot(cm16, prev.astype(BF16)) * e_ax[:, gs]
        state_ref[g] = prev * e_last[:, gs] + _dot(bm.T.astype(BF16), xdd16[:, gs])
        for t in range(GW // LANES):
            ts = slice(g * GW + t * LANES, g * GW + (t + 1) * LANES)
            pair = []
            for k in range(LANES // P):
                h = (g * GW + t * LANES) // P + k
                seg = acs[:, h:h + 1] - acs_t[h:h + 1, :]
                decay = jnp.exp(jnp.where(causal, seg, -jnp.inf))
                pair.append(_dot((cb * decay).astype(BF16), xd16[:, ts]))
            tiles.append(jnp.where(first_half, pair[0], pair[1])
                         + y_off[:, t * LANES:(t + 1) * LANES])

    y = jnp.concatenate(tiles, axis=1)
    y = (y + dskip_ref[...] * xc) * _silu(z_ref[...].astype(F32))
    for g in range(SSM_GROUPS):
        gs = slice(g * GW, (g + 1) * GW)
        yg = y[:, gs]
        o_ref[:, gs] = (yg * _rms(yg, GW) * nw_ref[:, gs]).astype(BF16)


def _ssd(proj, dt_raw, cwx, cbx, cwb, cbb, dtb, alog, dskip, nw, *, batch, seq):
    L = math.gcd(seq, SSD_CHUNK)
    nc = seq // L
    di = SSM_D_INNER
    bcw = 2 * SSM_GROUPS * SSM_STATE
    tok = lambda b, c: b * nc + c
    consts = _ssd_constants(L)
    params = (cwx, cbx, cwb, cbb, dtb, alog, dskip, nw) + consts
    return pl.pallas_call(
        _ssd_kernel,
        out_shape=jax.ShapeDtypeStruct((batch * seq, di), BF16),
        grid=(batch, nc),
        in_specs=[
            pl.BlockSpec((L, di), lambda b, c: (tok(b, c), P_XS // di)),
            pl.BlockSpec((L, bcw), lambda b, c: (tok(b, c), P_BC // bcw)),
            pl.BlockSpec((L, di), lambda b, c: (tok(b, c), P_Z // di)),
            pl.BlockSpec((L, LANES), lambda b, c: (tok(b, c), 0)),
        ] + [_resident(p.shape) for p in params],
        out_specs=pl.BlockSpec((L, di), lambda b, c: (tok(b, c), 0)),
        scratch_shapes=[
            pltpu.VMEM((16, di), F32),
            pltpu.VMEM((16, bcw), F32),
            pltpu.VMEM((SSM_GROUPS, SSM_STATE, di // SSM_GROUPS), F32),
        ],
        compiler_params=_cparams(("parallel", "arbitrary")),
        name="ssd",
    )(proj, proj, proj, dt_raw, *params)


def _rope_tile(x, c, s1, s2):
    q = MLA_ROPE // 2
    return x * c + pltpu.roll(x, LANES - q, 1) * s1 + pltpu.roll(x, q, 1) * s2


def _mla_prep_kernel(ql_ref, ckv_ref, kpe_ref, c_ref, s1_ref, s2_ref, qan_ref, kvan_ref,
                     qn_ref, kn_ref, wq_ref, wkv_ref, q_out, k_out, v_out, *, q_scale):
    ql = ql_ref[...].astype(F32)
    qa = (ql * _rms(ql, MLA_Q_LORA) * qan_ref[...]).astype(BF16)
    ckv = ckv_ref[...].astype(F32)
    kva = (ckv * _rms(ckv, MLA_KV_LORA) * kvan_ref[...]).astype(BF16)
    kpe = kpe_ref[...].astype(F32)
    kpe_ss = jnp.sum(kpe * kpe, axis=-1, keepdims=True)
    c, s1, s2 = c_ref[...], s1_ref[...], s2_ref[...]
    qn, kn = qn_ref[...] * q_scale, kn_ref[...]
    kpe_rot = _rope_tile(kpe * kn[:, MLA_NOPE:], c, s1, s2)
    ones = jnp.ones((kpe.shape[0], MLA_V), BF16)

    def project(h):
        return _dot(qa, wq_ref[h]), _dot(kva, wkv_ref[h])

    nxt = project(0)
    for h in range(MLA_HEADS):
        qh, kvh = nxt
        if h + 1 < MLA_HEADS:
            nxt = project(h + 1)
        qh = qh * _rms(qh, MLA_QK_DIM) * qn
        q_out[h, :, 0:MLA_NOPE] = qh[:, 0:MLA_NOPE].astype(BF16)
        q_out[h, :, MLA_NOPE:] = _rope_tile(qh[:, MLA_NOPE:], c, s1, s2).astype(BF16)
        kno = kvh[:, 0:MLA_NOPE]
        r = lax.rsqrt((jnp.sum(kno * kno, axis=-1, keepdims=True) + kpe_ss)
                      * (1.0 / MLA_QK_DIM) + NORM_EPS)
        k_out[h, :, 0:MLA_NOPE] = (kno * r * kn[:, 0:MLA_NOPE]).astype(BF16)
        k_out[h, :, MLA_NOPE:] = (kpe_rot * r).astype(BF16)
        v_out[h, :, 0:MLA_V] = kvh[:, MLA_NOPE:].astype(BF16)
        v_out[h, :, MLA_V:] = ones


def _mla_prep(proj, c, s1, s2, qan, kvan, qn, kn, wq, wkv, *, tm=512):
    m = proj.shape[0]
    tm = min(tm, m)
    row = lambda i: (i, 0)
    return pl.pallas_call(
        functools.partial(_mla_prep_kernel, q_scale=MLA_QK_DIM ** -0.5 * math.log2(math.e)),
        out_shape=(
            jax.ShapeDtypeStruct((MLA_HEADS, m, MLA_QK_PAD), BF16),
            jax.ShapeDtypeStruct((MLA_HEADS, m, MLA_QK_PAD), BF16),
            jax.ShapeDtypeStruct((MLA_HEADS, m, 2 * MLA_V), BF16),
        ),
        grid=(m // tm,),
        in_specs=[
            pl.BlockSpec((tm, MLA_Q_LORA), lambda i: (i, P_QLAT // MLA_Q_LORA)),
            pl.BlockSpec((tm, MLA_KV_LORA), lambda i: (i, P_CKV // MLA_KV_LORA)),
            pl.BlockSpec((tm, LANES), lambda i: (i, P_KPE // LANES)),
            pl.BlockSpec((tm, LANES), row), pl.BlockSpec((tm, LANES), row),
            pl.BlockSpec((tm, LANES), row),
            _resident(qan.shape), _resident(kvan.shape), _resident(qn.shape),
            _resident(kn.shape), _resident(wq.shape), _resident(wkv.shape),
        ],
        out_specs=(
            pl.BlockSpec((MLA_HEADS, tm, MLA_QK_PAD), lambda i: (0, i, 0)),
            pl.BlockSpec((MLA_HEADS, tm, MLA_QK_PAD), lambda i: (0, i, 0)),
            pl.BlockSpec((MLA_HEADS, tm, 2 * MLA_V), lambda i: (0, i, 0)),
        ),
        compiler_params=_cparams(("parallel",)),
        name="mla_prep",
    )(proj, proj, proj, c, s1, s2, qan, kvan, qn, kn, wq, wkv)


def _attn_kernel(q_ref, k_ref, v_ref, o_ref, *, blk, heads, rows):
    i = pl.program_id(2)
    chains = [(h, r) for h in range(heads) for r in range(blk // rows)]

    def step(j, carry, masked):
        start = pl.multiple_of(j * blk, blk)
        scores = []
        for h, r in chains:
            s = _dot_nt(q_ref[h, r * rows:(r + 1) * rows, :], k_ref[h, pl.ds(start, blk), :])
            if masked:
                row = lax.broadcasted_iota(jnp.int32, (rows, blk), 0) + r * rows
                col = lax.broadcasted_iota(jnp.int32, (rows, blk), 1)
                s = jnp.where(row >= col, s, -jnp.inf)
            scores.append(s)
        probs = []
        for (m, acc), s in zip(carry, scores):
            m_new = jnp.maximum(m, jnp.max(s, axis=-1, keepdims=True))
            probs.append((m_new, jnp.exp2(m - m_new) * acc, jnp.exp2(s - m_new).astype(BF16)))
        out = []
        for (h, r), (m_new, acc, p) in zip(chains, probs):
            out.append((m_new, acc + _dot(p, v_ref[h, pl.ds(start, blk), :])))
        return tuple(out)

    init = tuple((jnp.full((rows, 1), -jnp.inf, F32), jnp.zeros((rows, 2 * MLA_V), F32))
                 for _ in chains)
    carry = lax.fori_loop(0, i, lambda j, c: step(j, c, False), init)
    carry = step(i, carry, True)
    for (h, r), (_, acc) in zip(chains, carry):
        o_ref[r * rows:(r + 1) * rows, h * MLA_V:(h + 1) * MLA_V] = (
            acc[:, :MLA_V] / acc[:, MLA_V:]).astype(BF16)


def _attention(q, k, v, *, batch, seq, blk=512, heads=4, rows=512):
    blk = min(blk, seq)
    nq = seq // blk
    return pl.pallas_call(
        functools.partial(_attn_kernel, blk=blk, heads=heads, rows=min(rows, blk)),
        out_shape=jax.ShapeDtypeStruct((batch * seq, MLA_HEADS * MLA_V), BF16),
        grid=(batch, MLA_HEADS // heads, nq),
        in_specs=[
            pl.BlockSpec((heads, blk, MLA_QK_PAD), lambda b, h, i: (h, b * nq + i, 0)),
            pl.BlockSpec((heads, seq, MLA_QK_PAD), lambda b, h, i: (h, b, 0)),
            pl.BlockSpec((heads, seq, 2 * MLA_V), lambda b, h, i: (h, b, 0)),
        ],
        out_specs=pl.BlockSpec((blk, heads * MLA_V), lambda b, h, i: (b * nq + i, h)),
        compiler_params=_cparams(("parallel", "parallel", "arbitrary")),
        name="mla_attention",
    )(q, k, v)


def _ret_log_gamma():
    expo = 5.0 + 7.0 * np.arange(RET_HEADS, dtype=np.float32) / np.float32(RET_HEADS - 1)
    return [float(v) for v in np.log1p(-np.exp2(-expo).astype(np.float32)).astype(np.float32)]


def _ret_kernel(rq_ref, rk_ref, rv_ref, rg_ref, cos_ref, sin_ref, nw_ref, o_ref, state_ref,
                *, log_gamma):
    L = rq_ref.shape[0]
    D = RET_HEAD
    half = D // 2

    @pl.when(pl.program_id(1) == 0)
    def _():
        state_ref[...] = jnp.zeros(state_ref.shape, F32)

    cos, sin = cos_ref[...], sin_ref[...]
    row = lax.broadcasted_iota(jnp.int32, (L, L), 0)
    col = lax.broadcasted_iota(jnp.int32, (L, L), 1)
    rel = (row - col).astype(F32)
    pos = lax.broadcasted_iota(jnp.int32, (L, D), 0).astype(F32)

    def rope(ref, hs):
        x1 = ref[:, hs.start:hs.start + half].astype(F32)
        x2 = ref[:, hs.start + half:hs.stop].astype(F32)
        return jnp.concatenate([x1 * cos - x2 * sin, x1 * sin + x2 * cos], axis=-1)

    for h in range(RET_HEADS):
        lg = log_gamma[h]
        hs = slice(h * D, (h + 1) * D)
        q = rope(rq_ref, hs)
        k = rope(rk_ref, hs) * (RET_HEAD ** -0.5)
        v16 = rv_ref[:, hs]
        q16 = q.astype(BF16)
        dmask = jnp.where(rel >= 0, jnp.exp(rel * lg), 0.0)
        scores = _dot_nt(q16, k.astype(BF16)) * dmask
        y = _dot(scores.astype(BF16), v16)
        prev = state_ref[h]
        y = y + _dot(q16, prev.astype(BF16)) * jnp.exp((pos + 1.0) * lg)
        kd = k * jnp.exp((L - 1.0 - pos) * lg)
        state_ref[h] = prev * math.exp(L * lg) + _dot(kd.T.astype(BF16), v16)
        y = y * _rms(y, D) * nw_ref[:, hs]
        o_ref[:, hs] = (_silu(rg_ref[:, hs].astype(F32)) * y).astype(BF16)


def _retention(proj, cos, sin, nw, *, batch, seq):
    L = math.gcd(seq, RET_CHUNK)
    nc = seq // L
    w = RET_HEADS * RET_HEAD
    tok = lambda b, c: b * nc + c
    return pl.pallas_call(
        functools.partial(_ret_kernel, log_gamma=_ret_log_gamma()),
        out_shape=jax.ShapeDtypeStruct((batch * seq, w), BF16),
        grid=(batch, nc),
        in_specs=[
            pl.BlockSpec((L, w), lambda b, c: (tok(b, c), P_RQ // w)),
            pl.BlockSpec((L, w), lambda b, c: (tok(b, c), P_RK // w)),
            pl.BlockSpec((L, w), lambda b, c: (tok(b, c), P_RV // w)),
            pl.BlockSpec((L, w), lambda b, c: (tok(b, c), P_RG // w)),
            pl.BlockSpec((L, RET_HEAD // 2), lambda b, c: (tok(b, c), 0)),
            pl.BlockSpec((L, RET_HEAD // 2), lambda b, c: (tok(b, c), 0)),
            _resident(nw.shape),
        ],
        out_specs=pl.BlockSpec((L, w), lambda b, c: (tok(b, c), 0)),
        scratch_shapes=[pltpu.VMEM((RET_HEADS, RET_HEAD, RET_HEAD), F32)],
        compiler_params=_cparams(("parallel", "arbitrary")),
        name="retention",
    )(proj, proj, proj, proj, cos, sin, nw)


def _merge_kernel(ys_ref, ym_ref, yr_ref, g0_ref, g1_ref, g2_ref, gb_ref, x_ref,
                  w0_ref, w1_ref, w2_ref, wo_ref, o_ref):
    d = x_ref.shape[-1]

    def branch(k, y_ref, g_ref, w_ref):
        gate = _sigmoid(g_ref[...].astype(F32) + gb_ref[:, k * d:(k + 1) * d])
        return gate * _dot(y_ref[...], w_ref[...])

    merged = (branch(0, ys_ref, g0_ref, w0_ref) + branch(1, ym_ref, g1_ref, w1_ref)
              + branch(2, yr_ref, g2_ref, w2_ref))
    o_ref[...] = x_ref[...] + _dot(merged.astype(BF16), wo_ref[...])


def _merge(ys, ym, yr, proj, gb, x, w0, w1, w2, wo, *, tm=256):
    m, d = x.shape
    tm = min(tm, m)
    bw = ys.shape[1]
    row = lambda i: (i, 0)
    return pl.pallas_call(
        _merge_kernel,
        out_shape=jax.ShapeDtypeStruct((m, d), F32),
        grid=(m // tm,),
        in_specs=[
            pl.BlockSpec((tm, bw), row), pl.BlockSpec((tm, bw), row), pl.BlockSpec((tm, bw), row),
            pl.BlockSpec((tm, d), lambda i: (i, P_GATES // d)),
            pl.BlockSpec((tm, d), lambda i: (i, P_GATES // d + 1)),
            pl.BlockSpec((tm, d), lambda i: (i, P_GATES // d + 2)),
            _resident(gb.shape),
            pl.BlockSpec((tm, d), row),
            _resident(w0.shape), _resident(w1.shape), _resident(w2.shape), _resident(wo.shape),
        ],
        out_specs=pl.BlockSpec((tm, d), row),
        compiler_params=_cparams(("parallel",)),
        name="merge",
    )(ys, ym, yr, proj, proj, proj, gb, x, w0, w1, w2, wo)


def _rope_angles(positions, dim):
    inv = 1.0 / (ROPE_THETA ** (jnp.arange(0, dim, 2, dtype=F32) / dim))
    return positions.astype(F32).reshape(-1, 1) * inv


def _pad_lanes(v, n):
    return jnp.concatenate([v, jnp.zeros((n - v.shape[0],), v.dtype)]).reshape(1, n)


def kernel(x, positions, ffn1_norm, ffn1_w_gate, ffn1_w_up, ffn1_w_down, mix_norm, w_in, gate_b, conv_w, conv_b, dt_bias, a_log, d_skip, ssm_norm, q_a_norm, w_q_b, kv_a_norm, w_kv_b, q_norm, k_norm, ret_norm, w_br_ssm, w_br_mla, w_br_ret, w_out, ffn2_norm, ffn2_w_gate, ffn2_w_up, ffn2_w_down):
    batch, seq, d = x.shape
    depth = w_in.shape[0]
    x = x.reshape(batch * seq, d)

    ang = _rope_angles(positions, MLA_ROPE)
    cm, sm = jnp.cos(ang), jnp.sin(ang)
    zq = jnp.zeros_like(cm)
    rope_c = jnp.concatenate([cm, cm, zq, zq], axis=1)
    rope_s1 = jnp.concatenate([-sm, zq, zq, zq], axis=1)
    rope_s2 = jnp.concatenate([zq, sm, zq, zq], axis=1)
    ang = _rope_angles(positions, RET_HEAD)
    cos_ret, sin_ret = jnp.cos(ang), jnp.sin(ang)

    w_in16 = _w_in_prep(w_in)
    row = lambda v: v.reshape(1, -1)
    for l in range(depth):
        x = _ffn(x, row(ffn1_norm[l]), ffn1_w_gate[l].astype(BF16), ffn1_w_up[l].astype(BF16),
                 ffn1_w_down[l].astype(BF16))

        proj, dt_raw = _inproj(x, row(mix_norm[l]), w_in16, l)

        y_ssm = _ssd(
            proj, dt_raw,
            conv_w[l][:, :SSM_D_INNER], row(conv_b[l][:SSM_D_INNER]),
            conv_w[l][:, SSM_D_INNER:], row(conv_b[l][SSM_D_INNER:]),
            _pad_lanes(dt_bias[l], LANES), _pad_lanes(a_log[l], LANES),
            row(jnp.repeat(d_skip[l], SSM_HEAD_DIM)), row(ssm_norm[l]),
            batch=batch, seq=seq)

        wq = w_q_b[l].reshape(MLA_Q_LORA, MLA_HEADS, MLA_QK_DIM)
        wq = jnp.pad(wq, ((0, 0), (0, 0), (0, MLA_QK_PAD - MLA_QK_DIM)))
        wq = wq.transpose(1, 0, 2).astype(BF16)
        wkv = w_kv_b[l].reshape(MLA_KV_LORA, MLA_HEADS, MLA_NOPE + MLA_V)
        wkv = wkv.transpose(1, 0, 2).astype(BF16)
        q, k, v = _mla_prep(
            proj, rope_c, rope_s1, rope_s2, row(q_a_norm[l]), row(kv_a_norm[l]),
            _pad_lanes(q_norm[l], MLA_QK_PAD), _pad_lanes(k_norm[l], MLA_QK_PAD), wq, wkv)
        y_mla = _attention(q, k, v, batch=batch, seq=seq)

        y_ret = _retention(proj, cos_ret, sin_ret, row(ret_norm[l]), batch=batch, seq=seq)

        x = _merge(y_ssm, y_mla, y_ret, proj, row(gate_b[l]), x,
                   w_br_ssm[l].astype(BF16), w_br_mla[l].astype(BF16),
                   w_br_ret[l].astype(BF16), w_out[l].astype(BF16))

        x = _ffn(x, row(ffn2_norm[l]), ffn2_w_gate[l].astype(BF16), ffn2_w_up[l].astype(BF16),
                 ffn2_w_down[l].astype(BF16))
    return x.reshape(batch, seq, d)
```

```python
import functools
import math

import jax
import jax.numpy as jnp
import numpy as np
from jax import lax
from jax.experimental import pallas as pl
from jax.experimental.pallas import tpu as pltpu

F32 = jnp.float32
BF16 = jnp.bfloat16

NORM_EPS = 1e-6
ROPE_THETA = 10000.0

SSM_HEADS = 16
SSM_HEAD_DIM = 64
SSM_D_INNER = SSM_HEADS * SSM_HEAD_DIM
SSM_GROUPS = 2
SSM_STATE = 128
SSM_CONV = 4
SSD_CHUNK = 256
MLA_HEADS = 8
MLA_Q_LORA = 768
MLA_KV_LORA = 512
MLA_NOPE = 128
MLA_ROPE = 64
MLA_QK_DIM = MLA_NOPE + MLA_ROPE
MLA_V = 128
RET_HEADS = 4
RET_HEAD = 256
RET_CHUNK = 256
N_BRANCH = 3

LANES = 128
MLA_QK_PAD = MLA_NOPE + LANES
VMEM_LIMIT = 56 * 1024 * 1024


def _cparams(sem):
    return pltpu.CompilerParams(dimension_semantics=sem, vmem_limit_bytes=VMEM_LIMIT)


def _resident(shape):
    nd = len(shape)
    return pl.BlockSpec(shape, lambda *_: (0,) * nd, pipeline_mode=pl.Buffered(1))


def _rms(x, n):
    return lax.rsqrt(jnp.sum(x * x, axis=-1, keepdims=True) * (1.0 / n) + NORM_EPS)


def _sigmoid(x):
    return 0.5 * jnp.tanh(0.5 * x) + 0.5


def _silu(x):
    h = 0.5 * x
    return h * jnp.tanh(h) + h


def _dot(a, b):
    return jnp.dot(a, b, preferred_element_type=F32)


def _dot_nt(a, b):
    return lax.dot_general(a, b, (((1,), (1,)), ((), ())), preferred_element_type=F32)


def _ffn_kernel(x_ref, nw_ref, wg_ref, wu_ref, wd_ref, o_ref, xn_ref):
    d = x_ref.shape[-1]

    @pl.when(pl.program_id(1) == 0)
    def _():
        x = x_ref[...]
        xn_ref[...] = (x * _rms(x, d) * nw_ref[...]).astype(BF16)
        o_ref[...] = x

    xn = xn_ref[...]
    g = _dot(xn, wg_ref[...])
    u = _dot(xn, wu_ref[...])
    o_ref[...] += _dot((_silu(g) * (0.5 * u)).astype(BF16), wd_ref[...])


def _ffn(x, nw, wg, wu, wd, *, tm=1024, tf=512):
    m, d = x.shape
    f = wg.shape[1]
    tm = min(tm, m)
    return pl.pallas_call(
        _ffn_kernel,
        out_shape=jax.ShapeDtypeStruct((m, d), F32),
        grid=(m // tm, f // tf),
        in_specs=[
            pl.BlockSpec((tm, d), lambda i, j: (i, 0)),
            pl.BlockSpec((1, d), lambda i, j: (0, 0)),
            pl.BlockSpec((d, tf), lambda i, j: (0, j)),
            pl.BlockSpec((d, tf), lambda i, j: (0, j)),
            pl.BlockSpec((tf, d), lambda i, j: (j, 0)),
        ],
        out_specs=pl.BlockSpec((tm, d), lambda i, j: (i, 0)),
        scratch_shapes=[pltpu.VMEM((tm, d), BF16)],
        compiler_params=_cparams(("parallel", "arbitrary")),
        name="ffn",
    )(x, nw, wg, wu, wd)


P_GATES = 0
P_RQ = 6144
P_RK = 7168
P_RV = 8192
P_RG = 9216
P_Z = 10240
P_XS = 11264
P_BC = 12288
P_KPE = 12800
P_QLAT = 13056
P_CKV = 13824
P_TOTAL = 14336


def _inproj_kernel(x_ref, nw_ref, w_ref, wdt_ref, o_ref, dt_ref, xn_ref):
    j = pl.program_id(1)
    d = x_ref.shape[-1]

    @pl.when(j == 0)
    def _():
        x = x_ref[...]
        xn = (x * _rms(x, d) * nw_ref[...]).astype(BF16)
        xn_ref[...] = xn
        dt_ref[...] = _dot(xn, wdt_ref[...])

    o_ref[...] = _dot(xn_ref[...], w_ref[...]).astype(BF16)


def _inproj(x, nw, w_all, layer, *, tm=1024, tn=1024):
    m, d = x.shape
    n = P_TOTAL
    tm = min(tm, m)
    return pl.pallas_call(
        _inproj_kernel,
        out_shape=(jax.ShapeDtypeStruct((m, n), BF16), jax.ShapeDtypeStruct((m, LANES), F32)),
        grid=(m // tm, n // tn),
        in_specs=[
            pl.BlockSpec((tm, d), lambda i, j: (i, 0)),
            pl.BlockSpec((1, d), lambda i, j: (0, 0)),
            pl.BlockSpec((None, d, tn), lambda i, j: (layer, 0, j)),
            pl.BlockSpec((None, d, LANES), lambda i, j: (layer, 0, P_TOTAL // LANES)),
        ],
        out_specs=(
            pl.BlockSpec((tm, tn), lambda i, j: (i, j)),
            pl.BlockSpec((tm, LANES), lambda i, j: (i, 0)),
        ),
        scratch_shapes=[pltpu.VMEM((tm, d), BF16)],
        compiler_params=_cparams(("parallel", "arbitrary")),
        name="inproj",
    )(x, nw, w_all, w_all)


_W_IN_SEGMENTS = (
    (P_GATES, 8016, 6144), (P_RQ, 3920, 1024), (P_RK, 4944, 1024), (P_RV, 5968, 1024),
    (P_RG, 6992, 1024), (P_Z, 0, 1024), (P_XS, 1024, 1536), (P_KPE, 3856, MLA_ROPE),
    (P_QLAT, 2576, MLA_Q_LORA), (P_CKV, 3344, MLA_KV_LORA), (P_TOTAL, 2560, SSM_HEADS))
_W_IN_MAX_SHIFT = 80


def _w_in_tables():
    n_tiles = P_TOTAL // LANES + 1
    src = np.zeros(n_tiles, np.int32)
    mode = np.zeros(n_tiles, np.int32)
    sels = [np.zeros((2 * LANES, LANES), bool)]
    keys = {}
    for dest, source, width in _W_IN_SEGMENTS:
        for t in range(-(-width // LANES)):
            shift, lanes = source % LANES, min(LANES, width - t * LANES)
            assert shift <= _W_IN_MAX_SHIFT
            if (shift, lanes) not in keys:
                sel = np.zeros((2 * LANES, LANES), bool)
                sel[np.arange(lanes) + shift, np.arange(lanes)] = True
                keys[(shift, lanes)] = len(sels)
                sels.append(sel)
            src[dest // LANES + t] = source // LANES + t
            mode[dest // LANES + t] = keys[(shift, lanes)]
    return src, mode, np.stack(sels)


def _w_in_prep_kernel(src_ref, mode_ref, a_ref, b_ref, sel_ref, o_ref):
    ab = jnp.concatenate([a_ref[...], b_ref[...]], axis=1)
    o_ref[...] = _dot(ab, sel_ref[mode_ref[pl.program_id(1)]]).astype(BF16)


def _w_in_prep(w_in):
    depth, d, n_src = w_in.shape
    src, mode, sels = _w_in_tables()
    n_tiles = src.shape[0]
    w_in = jnp.pad(w_in.astype(BF16), ((0, 0), (0, 0), (0, -n_src % LANES)))
    assert (int(src.max()) + 2) * LANES <= w_in.shape[2]
    return pl.pallas_call(
        _w_in_prep_kernel,
        out_shape=jax.ShapeDtypeStruct((depth, d, n_tiles * LANES), BF16),
        grid_spec=pltpu.PrefetchScalarGridSpec(
            num_scalar_prefetch=2,
            grid=(depth, n_tiles),
            in_specs=[
                pl.BlockSpec((None, d, LANES), lambda l, j, src, mode: (l, 0, src[j])),
                pl.BlockSpec((None, d, LANES), lambda l, j, src, mode: (l, 0, src[j] + 1)),
                pl.BlockSpec(sels.shape, lambda l, j, src, mode: (0, 0, 0)),
            ],
            out_specs=pl.BlockSpec((None, d, LANES), lambda l, j, src, mode: (l, 0, j)),
        ),
        compiler_params=_cparams(("parallel", "arbitrary")),
        name="w_in_prep",
    )(jnp.asarray(src), jnp.asarray(mode), w_in, w_in, jnp.asarray(sels, BF16))


def _split3(x):
    hi = x.astype(BF16)
    r1 = x - hi.astype(F32)
    mid = r1.astype(BF16)
    lo = (r1 - mid.astype(F32)).astype(BF16)
    return hi, mid, lo


def _ssd_constants(L):
    t = np.arange(L)
    shifts = np.stack([(t[:, None] - t[None, :] == j) for j in range(1, SSM_CONV)])
    tril = t[:, None] >= t[None, :]
    expand = np.zeros((LANES, SSM_D_INNER), bool)
    for h in range(SSM_HEADS):
        expand[h, h * SSM_HEAD_DIM:(h + 1) * SSM_HEAD_DIM] = True
    expand2 = np.concatenate([expand, expand])
    expand4 = np.concatenate([expand, expand, expand, np.zeros_like(expand)])
    return tuple(jnp.asarray(m, BF16) for m in (shifts, tril, expand2, expand4))


def _ssd_kernel(xs_ref, bc_ref, z_ref, dt_ref, cwx_ref, cbx_ref, cwb_ref, cbb_ref,
                dtb_ref, alog_ref, dskip_ref, nw_ref, shift_ref, tril_ref, e2_ref, e4_ref,
                o_ref, xtail_ref, btail_ref, state_ref):
    L = xs_ref.shape[0]
    P = SSM_HEAD_DIM
    N = SSM_STATE
    HG = SSM_HEADS // SSM_GROUPS
    GW = HG * P
    T = 8

    @pl.when(pl.program_id(1) == 0)
    def _():
        xtail_ref[0:T, :] = jnp.zeros((T, xtail_ref.shape[1]), F32)
        btail_ref[0:T, :] = jnp.zeros((T, btail_ref.shape[1]), F32)
        state_ref[...] = jnp.zeros(state_ref.shape, F32)

    def conv_silu(tail_ref, in_ref, w_ref, b_ref):
        x16 = in_ref[...]
        xf = x16.astype(F32)
        acc = b_ref[...] + w_ref[SSM_CONV - 1:SSM_CONV, :] * xf
        for j in range(1, SSM_CONV):
            acc = acc + w_ref[SSM_CONV - 1 - j:SSM_CONV - j, :] * _dot(shift_ref[j - 1], x16)
        tail_ref[T:2 * T, :] = xf[0:T, :]
        head = b_ref[...] + w_ref[0:1, :] * tail_ref[T - 3:2 * T - 3, :]
        for j in range(1, SSM_CONV):
            head = head + w_ref[j:j + 1, :] * tail_ref[T - 3 + j:2 * T - 3 + j, :]
        tail_ref[0:T, :] = xf[L - T:L, :]
        return _silu(jnp.concatenate([head, acc[T:, :]], axis=0))

    xc = conv_silu(xtail_ref, xs_ref, cwx_ref, cbx_ref)
    bcc = conv_silu(btail_ref, bc_ref, cwb_ref, cbb_ref)

    dtr = dt_ref[...] + dtb_ref[...]
    dt = jnp.maximum(dtr, 0.0) + jnp.log1p(jnp.exp(-jnp.abs(dtr)))
    adt = dt * (-jnp.exp(alog_ref[...]))
    tril = tril_ref[...]
    hi, mid, lo = _split3(adt)
    acs = _dot(tril, hi) + _dot(tril, mid) + _dot(tril, lo)
    acs_t = acs.T

    hi, mid, _ = _split3(dt)
    dtx = _dot(jnp.concatenate([hi, mid], axis=1), e2_ref[...])
    hi, mid, lo = _split3(acs)
    ax = _dot(jnp.concatenate([hi, mid, lo, jnp.zeros_like(lo)], axis=1), e4_ref[...])
    a_last = ax[L - 1:L, :]
    xd = xc * dtx
    xd16 = xd.astype(BF16)
    xdd16 = (xd * jnp.exp(a_last - ax)).astype(BF16)
    e_ax = jnp.exp(ax)
    e_last = jnp.exp(a_last)

    causal = (lax.broadcasted_iota(jnp.int32, (L, L), 0)
              >= lax.broadcasted_iota(jnp.int32, (L, L), 1))
    first_half = lax.broadcasted_iota(jnp.int32, (L, LANES), 1) < P
    tiles = []
    for g in range(SSM_GROUPS):
        bm = bcc[:, g * N:(g + 1) * N]
        cm16 = bcc[:, (SSM_GROUPS + g) * N:(SSM_GROUPS + g + 1) * N].astype(BF16)
        cb = _dot_nt(cm16, bm.astype(BF16))
        gs = slice(g * GW, (g + 1) * GW)
        prev = state_ref[g]
        y_off = _dot(cm16, prev.astype(BF16)) * e_ax[:, gs]
        state_ref[g] = prev * e_last[:, gs] + _dot(bm.T.astype(BF16), xdd16[:, gs])
        for t in range(GW // LANES):
            ts = slice(g * GW + t * LANES, g * GW + (t + 1) * LANES)
            pair = []
            for k in range(LANES // P):
                h = (g * GW + t * LANES) // P + k
                seg = acs[:, h:h + 1] - acs_t[h:h + 1, :]
                decay = jnp.exp(jnp.where(causal, seg, -jnp.inf))
                pair.append(_dot((cb * decay).astype(BF16), xd16[:, ts]))
            tiles.append(jnp.where(first_half, pair[0], pair[1])
                         + y_off[:, t * LANES:(t + 1) * LANES])

    y = jnp.concatenate(tiles, axis=1)
    y = (y + dskip_ref[...] * xc) * _silu(z_ref[...].astype(F32))
    for g in range(SSM_GROUPS):
        gs = slice(g * GW, (g + 1) * GW)
        yg = y[:, gs]
        o_ref[:, gs] = (yg * _rms(yg, GW) * nw_ref[:, gs]).astype(BF16)


def _ssd(proj, dt_raw, cwx, cbx, cwb, cbb, dtb, alog, dskip, nw, *, batch, seq):
    L = math.gcd(seq, SSD_CHUNK)
    nc = seq // L
    di = SSM_D_INNER
    bcw = 2 * SSM_GROUPS * SSM_STATE
    tok = lambda b, c: b * nc + c
    consts = _ssd_constants(L)
    params = (cwx, cbx, cwb, cbb, dtb, alog, dskip, nw) + consts
    return pl.pallas_call(
        _ssd_kernel,
        out_shape=jax.ShapeDtypeStruct((batch * seq, di), BF16),
        grid=(batch, nc),
        in_specs=[
            pl.BlockSpec((L, di), lambda b, c: (tok(b, c), P_XS // di)),
            pl.BlockSpec((L, bcw), lambda b, c: (tok(b, c), P_BC // bcw)),
            pl.BlockSpec((L, di), lambda b, c: (tok(b, c), P_Z // di)),
            pl.BlockSpec((L, LANES), lambda b, c: (tok(b, c), 0)),
        ] + [_resident(p.shape) for p in params],
        out_specs=pl.BlockSpec((L, di), lambda b, c: (tok(b, c), 0)),
        scratch_shapes=[
            pltpu.VMEM((16, di), F32),
            pltpu.VMEM((16, bcw), F32),
            pltpu.VMEM((SSM_GROUPS, SSM_STATE, di // SSM_GROUPS), F32),
        ],
        compiler_params=_cparams(("parallel", "arbitrary")),
        name="ssd",
    )(proj, proj, proj, dt_raw, *params)


def _rope_tile(x, c, s1, s2):
    q = MLA_ROPE // 2
    return x * c + pltpu.roll(x, LANES - q, 1) * s1 + pltpu.roll(x, q, 1) * s2


def _mla_prep_kernel(ql_ref, ckv_ref, kpe_ref, c_ref, s1_ref, s2_ref, qan_ref, kvan_ref,
                     qn_ref, kn_ref, wq_ref, wkv_ref, q_out, k_out, v_out, *, q_scale):
    ql = ql_ref[...].astype(F32)
    qa = (ql * _rms(ql, MLA_Q_LORA) * qan_ref[...]).astype(BF16)
    ckv = ckv_ref[...].astype(F32)
    kva = (ckv * _rms(ckv, MLA_KV_LORA) * kvan_ref[...]).astype(BF16)
    kpe = kpe_ref[...].astype(F32)
    kpe_ss = jnp.sum(kpe * kpe, axis=-1, keepdims=True)
    c, s1, s2 = c_ref[...], s1_ref[...], s2_ref[...]
    qn, kn = qn_ref[...] * q_scale, kn_ref[...]
    kpe_rot = _rope_tile(kpe * kn[:, MLA_NOPE:], c, s1, s2)
    ones = jnp.ones((kpe.shape[0], MLA_V), BF16)

    def project(h):
        return _dot(qa, wq_ref[h]), _dot(kva, wkv_ref[h])

    nxt = project(0)
    for h in range(MLA_HEADS):
        qh, kvh = nxt
        if h + 1 < MLA_HEADS:
            nxt = project(h + 1)
        qh = qh * _rms(qh, MLA_QK_DIM) * qn
        q_out[h, :, 0:MLA_NOPE] = qh[:, 0:MLA_NOPE].astype(BF16)
        q_out[h, :, MLA_NOPE:] = _rope_tile(qh[:, MLA_NOPE:], c, s1, s2).astype(BF16)
        kno = kvh[:, 0:MLA_NOPE]
        r = lax.rsqrt((jnp.sum(kno * kno, axis=-1, keepdims=True) + kpe_ss)
                      * (1.0 / MLA_QK_DIM) + NORM_EPS)
        k_out[h, :, 0:MLA_NOPE] = (kno * r * kn[:, 0:MLA_NOPE]).astype(BF16)
        k_out[h, :, MLA_NOPE:] = (kpe_rot * r).astype(BF16)
        v_out[h, :, 0:MLA_V] = kvh[:, MLA_NOPE:].astype(BF16)
        v_out[h, :, MLA_V:] = ones


def _mla_prep(proj, c, s1, s2, qan, kvan, qn, kn, wq, wkv, *, tm=512):
    m = proj.shape[0]
    tm = min(tm, m)
    row = lambda i: (i, 0)
    return pl.pallas_call(
        functools.partial(_mla_prep_kernel, q_scale=MLA_QK_DIM ** -0.5 * math.log2(math.e)),
        out_shape=(
            jax.ShapeDtypeStruct((MLA_HEADS, m, MLA_QK_PAD), BF16),
            jax.ShapeDtypeStruct((MLA_HEADS, m, MLA_QK_PAD), BF16),
            jax.ShapeDtypeStruct((MLA_HEADS, m, 2 * MLA_V), BF16),
        ),
        grid=(m // tm,),
        in_specs=[
            pl.BlockSpec((tm, MLA_Q_LORA), lambda i: (i, P_QLAT // MLA_Q_LORA)),
            pl.BlockSpec((tm, MLA_KV_LORA), lambda i: (i, P_CKV // MLA_KV_LORA)),
            pl.BlockSpec((tm, LANES), lambda i: (i, P_KPE // LANES)),
            pl.BlockSpec((tm, LANES), row), pl.BlockSpec((tm, LANES), row),
            pl.BlockSpec((tm, LANES), row),
            _resident(qan.shape), _resident(kvan.shape), _resident(qn.shape),
            _resident(kn.shape), _resident(wq.shape), _resident(wkv.shape),
        ],
        out_specs=(
            pl.BlockSpec((MLA_HEADS, tm, MLA_QK_PAD), lambda i: (0, i, 0)),
            pl.BlockSpec((MLA_HEADS, tm, MLA_QK_PAD), lambda i: (0, i, 0)),
            pl.BlockSpec((MLA_HEADS, tm, 2 * MLA_V), lambda i: (0, i, 0)),
        ),
        compiler_params=_cparams(("parallel",)),
        name="mla_prep",
    )(proj, proj, proj, c, s1, s2, qan, kvan, qn, kn, wq, wkv)


def _attn_kernel(q_ref, k_ref, v_ref, o_ref, s_ref, *, blk, heads):
    i = pl.program_id(2)

    def scores(j, slot):
        start = pl.multiple_of(j * blk, blk)
        for h in range(heads):
            s_ref[slot, h] = _dot_nt(q_ref[h], k_ref[h, pl.ds(start, blk), :])

    def consume(j, slot, carry, masked):
        start = pl.multiple_of(j * blk, blk)
        probs = []
        for h, (m, acc) in enumerate(carry):
            s = s_ref[slot, h]
            if masked:
                row = lax.broadcasted_iota(jnp.int32, (blk, blk), 0)
                col = lax.broadcasted_iota(jnp.int32, (blk, blk), 1)
                s = jnp.where(row >= col, s, -jnp.inf)
            m_new = jnp.maximum(m, jnp.max(s, axis=-1, keepdims=True))
            probs.append((m_new, jnp.exp2(m - m_new), jnp.exp2(s - m_new).astype(BF16)))
        return tuple((m_new, alpha * acc + _dot(p, v_ref[h, pl.ds(start, blk), :]))
                     for h, ((_, acc), (m_new, alpha, p)) in enumerate(zip(carry, probs)))

    def finish(carry):
        for h, (_, acc) in enumerate(carry):
            o_ref[:, h * MLA_V:(h + 1) * MLA_V] = (acc[:, :MLA_V] / acc[:, MLA_V:]).astype(BF16)

    def pair(t, carry):
        j = 2 * t
        scores(j + 1, 1)
        carry = consume(j, 0, carry, False)
        scores(j + 2, 0)
        return consume(j + 1, 1, carry, False)

    scores(0, 0)
    init = tuple((jnp.full((blk, 1), -jnp.inf, F32), jnp.zeros((blk, 2 * MLA_V), F32))
                 for _ in range(heads))
    carry = lax.fori_loop(0, i // 2, pair, init)
    last_even = 2 * (i // 2)

    @pl.when(i % 2 == 0)
    def _():
        finish(consume(last_even, 0, carry, True))

    @pl.when(i % 2 == 1)
    def _():
        scores(last_even + 1, 1)
        finish(consume(last_even + 1, 1, consume(last_even, 0, carry, False), True))


def _attention(q, k, v, *, batch, seq, blk=512, heads=4):
    blk = min(blk, seq)
    nq = seq // blk
    return pl.pallas_call(
        functools.partial(_attn_kernel, blk=blk, heads=heads),
        out_shape=jax.ShapeDtypeStruct((batch * seq, MLA_HEADS * MLA_V), BF16),
        grid=(batch, MLA_HEADS // heads, nq),
        in_specs=[
            pl.BlockSpec((heads, blk, MLA_QK_PAD), lambda b, h, i: (h, b * nq + i, 0)),
            pl.BlockSpec((heads, seq, MLA_QK_PAD), lambda b, h, i: (h, b, 0)),
            pl.BlockSpec((heads, seq, 2 * MLA_V), lambda b, h, i: (h, b, 0)),
        ],
        out_specs=pl.BlockSpec((blk, heads * MLA_V), lambda b, h, i: (b * nq + i, h)),
        scratch_shapes=[pltpu.VMEM((2, heads, blk, blk), F32)],
        compiler_params=_cparams(("parallel", "parallel", "arbitrary")),
        name="mla_attention",
    )(q, k, v)


def _ret_log_gamma():
    expo = 5.0 + 7.0 * np.arange(RET_HEADS, dtype=np.float32) / np.float32(RET_HEADS - 1)
    return [float(v) for v in np.log1p(-np.exp2(-expo).astype(np.float32)).astype(np.float32)]


def _ret_kernel(rq_ref, rk_ref, rv_ref, rg_ref, cos_ref, sin_ref, nw_ref, o_ref, state_ref,
                *, log_gamma):
    L = rq_ref.shape[0]
    D = RET_HEAD
    half = D // 2

    @pl.when(pl.program_id(1) == 0)
    def _():
        state_ref[...] = jnp.zeros(state_ref.shape, F32)

    cos, sin = cos_ref[...], sin_ref[...]
    row = lax.broadcasted_iota(jnp.int32, (L, L), 0)
    col = lax.broadcasted_iota(jnp.int32, (L, L), 1)
    rel = (row - col).astype(F32)
    pos = lax.broadcasted_iota(jnp.int32, (L, D), 0).astype(F32)

    def rope(ref, hs):
        x1 = ref[:, hs.start:hs.start + half].astype(F32)
        x2 = ref[:, hs.start + half:hs.stop].astype(F32)
        return jnp.concatenate([x1 * cos - x2 * sin, x1 * sin + x2 * cos], axis=-1)

    for h in range(RET_HEADS):
        lg = log_gamma[h]
        hs = slice(h * D, (h + 1) * D)
        q = rope(rq_ref, hs)
        k = rope(rk_ref, hs) * (RET_HEAD ** -0.5)
        v16 = rv_ref[:, hs]
        q16 = q.astype(BF16)
        dmask = jnp.where(rel >= 0, jnp.exp(rel * lg), 0.0)
        scores = _dot_nt(q16, k.astype(BF16)) * dmask
        y = _dot(scores.astype(BF16), v16)
        prev = state_ref[h]
        y = y + _dot(q16, prev.astype(BF16)) * jnp.exp((pos + 1.0) * lg)
        kd = k * jnp.exp((L - 1.0 - pos) * lg)
        state_ref[h] = prev * math.exp(L * lg) + _dot(kd.T.astype(BF16), v16)
        y = y * _rms(y, D) * nw_ref[:, hs]
        o_ref[:, hs] = (_silu(rg_ref[:, hs].astype(F32)) * y).astype(BF16)


def _retention(proj, cos, sin, nw, *, batch, seq):
    L = math.gcd(seq, RET_CHUNK)
    nc = seq // L
    w = RET_HEADS * RET_HEAD
    tok = lambda b, c: b * nc + c
    return pl.pallas_call(
        functools.partial(_ret_kernel, log_gamma=_ret_log_gamma()),
        out_shape=jax.ShapeDtypeStruct((batch * seq, w), BF16),
        grid=(batch, nc),
        in_specs=[
            pl.BlockSpec((L, w), lambda b, c: (tok(b, c), P_RQ // w)),
            pl.BlockSpec((L, w), lambda b, c: (tok(b, c), P_RK // w)),
            pl.BlockSpec((L, w), lambda b, c: (tok(b, c), P_RV // w)),
            pl.BlockSpec((L, w), lambda b, c: (tok(b, c), P_RG // w)),
            pl.BlockSpec((L, RET_HEAD // 2), lambda b, c: (tok(b, c), 0)),
            pl.BlockSpec((L, RET_HEAD // 2), lambda b, c: (tok(b, c), 0)),
            _resident(nw.shape),
        ],
        out_specs=pl.BlockSpec((L, w), lambda b, c: (tok(b, c), 0)),
        scratch_shapes=[pltpu.VMEM((RET_HEADS, RET_HEAD, RET_HEAD), F32)],
        compiler_params=_cparams(("parallel", "arbitrary")),
        name="retention",
    )(proj, proj, proj, proj, cos, sin, nw)


def _merge_kernel(ys_ref, ym_ref, yr_ref, g0_ref, g1_ref, g2_ref, gb_ref, x_ref,
                  w0_ref, w1_ref, w2_ref, wo_ref, o_ref):
    d = x_ref.shape[-1]

    def branch(k, y_ref, g_ref, w_ref):
        gate = _sigmoid(g_ref[...].astype(F32) + gb_ref[:, k * d:(k + 1) * d])
        return gate * _dot(y_ref[...], w_ref[...])

    merged = (branch(0, ys_ref, g0_ref, w0_ref) + branch(1, ym_ref, g1_ref, w1_ref)
              + branch(2, yr_ref, g2_ref, w2_ref))
    o_ref[...] = x_ref[...] + _dot(merged.astype(BF16), wo_ref[...])


def _merge(ys, ym, yr, proj, gb, x, w0, w1, w2, wo, *, tm=256):
    m, d = x.shape
    tm = min(tm, m)
    bw = ys.shape[1]
    row = lambda i: (i, 0)
    return pl.pallas_call(
        _merge_kernel,
        out_shape=jax.ShapeDtypeStruct((m, d), F32),
        grid=(m // tm,),
        in_specs=[
            pl.BlockSpec((tm, bw), row), pl.BlockSpec((tm, bw), row), pl.BlockSpec((tm, bw), row),
            pl.BlockSpec((tm, d), lambda i: (i, P_GATES // d)),
            pl.BlockSpec((tm, d), lambda i: (i, P_GATES // d + 1)),
            pl.BlockSpec((tm, d), lambda i: (i, P_GATES // d + 2)),
            _resident(gb.shape),
            pl.BlockSpec((tm, d), row),
            _resident(w0.shape), _resident(w1.shape), _resident(w2.shape), _resident(wo.shape),
        ],
        out_specs=pl.BlockSpec((tm, d), row),
        compiler_params=_cparams(("parallel",)),
        name="merge",
    )(ys, ym, yr, proj, proj, proj, gb, x, w0, w1, w2, wo)


def _rope_angles(positions, dim):
    inv = 1.0 / (ROPE_THETA ** (jnp.arange(0, dim, 2, dtype=F32) / dim))
    return positions.astype(F32).reshape(-1, 1) * inv


def _pad_lanes(v, n):
    return jnp.concatenate([v, jnp.zeros((n - v.shape[0],), v.dtype)]).reshape(1, n)


def kernel(x, positions, ffn1_norm, ffn1_w_gate, ffn1_w_up, ffn1_w_down, mix_norm, w_in, gate_b, conv_w, conv_b, dt_bias, a_log, d_skip, ssm_norm, q_a_norm, w_q_b, kv_a_norm, w_kv_b, q_norm, k_norm, ret_norm, w_br_ssm, w_br_mla, w_br_ret, w_out, ffn2_norm, ffn2_w_gate, ffn2_w_up, ffn2_w_down):
    batch, seq, d = x.shape
    depth = w_in.shape[0]
    x = x.reshape(batch * seq, d)

    ang = _rope_angles(positions, MLA_ROPE)
    cm, sm = jnp.cos(ang), jnp.sin(ang)
    zq = jnp.zeros_like(cm)
    rope_c = jnp.concatenate([cm, cm, zq, zq], axis=1)
    rope_s1 = jnp.concatenate([-sm, zq, zq, zq], axis=1)
    rope_s2 = jnp.concatenate([zq, sm, zq, zq], axis=1)
    ang = _rope_angles(positions, RET_HEAD)
    cos_ret, sin_ret = jnp.cos(ang), jnp.sin(ang)

    w_in16 = _w_in_prep(w_in)
    row = lambda v: v.reshape(1, -1)
    for l in range(depth):
        x = _ffn(x, row(ffn1_norm[l]), ffn1_w_gate[l].astype(BF16), ffn1_w_up[l].astype(BF16),
                 ffn1_w_down[l].astype(BF16))

        proj, dt_raw = _inproj(x, row(mix_norm[l]), w_in16, l)

        y_ssm = _ssd(
            proj, dt_raw,
            conv_w[l][:, :SSM_D_INNER], row(conv_b[l][:SSM_D_INNER]),
            conv_w[l][:, SSM_D_INNER:], row(conv_b[l][SSM_D_INNER:]),
            _pad_lanes(dt_bias[l], LANES), _pad_lanes(a_log[l], LANES),
            row(jnp.repeat(d_skip[l], SSM_HEAD_DIM)), row(ssm_norm[l]),
            batch=batch, seq=seq)

        wq = w_q_b[l].reshape(MLA_Q_LORA, MLA_HEADS, MLA_QK_DIM)
        wq = jnp.pad(wq, ((0, 0), (0, 0), (0, MLA_QK_PAD - MLA_QK_DIM)))
        wq = wq.transpose(1, 0, 2).astype(BF16)
        wkv = w_kv_b[l].reshape(MLA_KV_LORA, MLA_HEADS, MLA_NOPE + MLA_V)
        wkv = wkv.transpose(1, 0, 2).astype(BF16)
        q, k, v = _mla_prep(
            proj, rope_c, rope_s1, rope_s2, row(q_a_norm[l]), row(kv_a_norm[l]),
            _pad_lanes(q_norm[l], MLA_QK_PAD), _pad_lanes(k_norm[l], MLA_QK_PAD), wq, wkv)
        y_mla = _attention(q, k, v, batch=batch, seq=seq)

        y_ret = _retention(proj, cos_ret, sin_ret, row(ret_norm[l]), batch=batch, seq=seq)

        x = _merge(y_ssm, y_mla, y_ret, proj, row(gate_b[l]), x,
                   w_br_ssm[l].astype(BF16), w_br_mla[l].astype(BF16),
                   w_br_ret[l].astype(BF16), w_out[l].astype(BF16))

        x = _ffn(x, row(ffn2_norm[l]), ffn2_w_gate[l].astype(BF16), ffn2_w_up[l].astype(BF16),
                 ffn2_w_down[l].astype(BF16))
    return x.reshape(batch, seq, d)
```

```python
import functools
import math

import jax
import jax.numpy as jnp
import numpy as np
from jax import lax
from jax.experimental import pallas as pl
from jax.experimental.pallas import tpu as pltpu

F32 = jnp.float32
BF16 = jnp.bfloat16

NORM_EPS = 1e-6
ROPE_THETA = 10000.0

SSM_HEADS = 16
SSM_HEAD_DIM = 64
SSM_D_INNER = SSM_HEADS * SSM_HEAD_DIM
SSM_GROUPS = 2
SSM_STATE = 128
SSM_CONV = 4
SSD_CHUNK = 256
MLA_HEADS = 8
MLA_Q_LORA = 768
MLA_KV_LORA = 512
MLA_NOPE = 128
MLA_ROPE = 64
MLA_QK_DIM = MLA_NOPE + MLA_ROPE
MLA_V = 128
RET_HEADS = 4
RET_HEAD = 256
RET_CHUNK = 256
N_BRANCH = 3

LANES = 128
MLA_QK_PAD = MLA_NOPE + LANES
VMEM_LIMIT = 56 * 1024 * 1024


def _cparams(sem):
    return pltpu.CompilerParams(dimension_semantics=sem, vmem_limit_bytes=VMEM_LIMIT)


def _resident(shape):
    nd = len(shape)
    return pl.BlockSpec(shape, lambda *_: (0,) * nd, pipeline_mode=pl.Buffered(1))


def _rms(x, n):
    return lax.rsqrt(jnp.sum(x * x, axis=-1, keepdims=True) * (1.0 / n) + NORM_EPS)


def _sigmoid(x):
    return 0.5 * jnp.tanh(0.5 * x) + 0.5


def _silu(x):
    h = 0.5 * x
    return h * jnp.tanh(h) + h


def _dot(a, b):
    return jnp.dot(a, b, preferred_element_type=F32)


def _dot_nt(a, b):
    return lax.dot_general(a, b, (((1,), (1,)), ((), ())), preferred_element_type=F32)


def _ffn_kernel(x_ref, nw_ref, wg_ref, wu_ref, wd_ref, o_ref, xn_ref):
    d = x_ref.shape[-1]

    @pl.when(pl.program_id(1) == 0)
    def _():
        x = x_ref[...]
        xn_ref[...] = (x * _rms(x, d) * nw_ref[...]).astype(BF16)
        o_ref[...] = x

    xn = xn_ref[...]
    g = _dot(xn, wg_ref[...])
    u = _dot(xn, wu_ref[...])
    o_ref[...] += _dot((_silu(g) * (0.5 * u)).astype(BF16), wd_ref[...])


def _ffn(x, nw, wg, wu, wd, *, tm=1024, tf=512):
    m, d = x.shape
    f = wg.shape[1]
    tm = min(tm, m)
    return pl.pallas_call(
        _ffn_kernel,
        out_shape=jax.ShapeDtypeStruct((m, d), F32),
        grid=(m // tm, f // tf),
        in_specs=[
            pl.BlockSpec((tm, d), lambda i, j: (i, 0)),
            pl.BlockSpec((1, d), lambda i, j: (0, 0)),
            pl.BlockSpec((d, tf), lambda i, j: (0, j)),
            pl.BlockSpec((d, tf), lambda i, j: (0, j)),
            pl.BlockSpec((tf, d), lambda i, j: (j, 0)),
        ],
        out_specs=pl.BlockSpec((tm, d), lambda i, j: (i, 0)),
        scratch_shapes=[pltpu.VMEM((tm, d), BF16)],
        compiler_params=_cparams(("parallel", "arbitrary")),
        name="ffn",
    )(x, nw, wg, wu, wd)


P_GATES = 0
P_RQ = 6144
P_RK = 7168
P_RV = 8192
P_RG = 9216
P_Z = 10240
P_XS = 11264
P_BC = 12288
P_KPE = 12800
P_QLAT = 13056
P_CKV = 13824
P_TOTAL = 14336


def _inproj_kernel(x_ref, nw_ref, w_ref, wdt_ref, o_ref, dt_ref, xn_ref):
    j = pl.program_id(1)
    d = x_ref.shape[-1]

    @pl.when(j == 0)
    def _():
        x = x_ref[...]
        xn = (x * _rms(x, d) * nw_ref[...]).astype(BF16)
        xn_ref[...] = xn
        dt_ref[...] = _dot(xn, wdt_ref[...])

    o_ref[...] = _dot(xn_ref[...], w_ref[...]).astype(BF16)


def _inproj(x, nw, w_all, layer, *, tm=1024, tn=1024):
    m, d = x.shape
    n = P_TOTAL
    tm = min(tm, m)
    return pl.pallas_call(
        _inproj_kernel,
        out_shape=(jax.ShapeDtypeStruct((m, n), BF16), jax.ShapeDtypeStruct((m, LANES), F32)),
        grid=(m // tm, n // tn),
        in_specs=[
            pl.BlockSpec((tm, d), lambda i, j: (i, 0)),
            pl.BlockSpec((1, d), lambda i, j: (0, 0)),
            pl.BlockSpec((None, d, tn), lambda i, j: (layer, 0, j)),
            pl.BlockSpec((None, d, LANES), lambda i, j: (layer, 0, P_TOTAL // LANES)),
        ],
        out_specs=(
            pl.BlockSpec((tm, tn), lambda i, j: (i, j)),
            pl.BlockSpec((tm, LANES), lambda i, j: (i, 0)),
        ),
        scratch_shapes=[pltpu.VMEM((tm, d), BF16)],
        compiler_params=_cparams(("parallel", "arbitrary")),
        name="inproj",
    )(x, nw, w_all, w_all)


_W_IN_SEGMENTS = (
    (P_GATES, 8016, 6144), (P_RQ, 3920, 1024), (P_RK, 4944, 1024), (P_RV, 5968, 1024),
    (P_RG, 6992, 1024), (P_Z, 0, 1024), (P_XS, 1024, 1536), (P_KPE, 3856, MLA_ROPE),
    (P_QLAT, 2576, MLA_Q_LORA), (P_CKV, 3344, MLA_KV_LORA), (P_TOTAL, 2560, SSM_HEADS))
W_IN_OUT = P_TOTAL + LANES


def _w_in_prep_kernel(x_ref, o_ref):
    n_src = x_ref.shape[1]
    segs = sorted(_W_IN_SEGMENTS)
    for k, (dest, _, width) in enumerate(segs):
        full = dest + width // LANES * LANES
        nxt = segs[k + 1][0] if k + 1 < len(segs) else o_ref.shape[1]
        if nxt > full:
            o_ref[:, full:nxt] = jnp.zeros((o_ref.shape[0], nxt - full), BF16)
    for dest, source, width in segs:
        lo = source // LANES * LANES
        hi = min(-(-(source + width) // LANES) * LANES, n_src)
        o_ref[:, dest:dest + width] = (
            x_ref[:, lo:hi][:, source - lo:source - lo + width].astype(BF16))


def _w_in_prep(w_in, *, rows=128):
    depth, d, n_src = w_in.shape
    return pl.pallas_call(
        _w_in_prep_kernel,
        out_shape=jax.ShapeDtypeStruct((depth, d, W_IN_OUT), BF16),
        grid=(depth, d // rows),
        in_specs=[pl.BlockSpec((None, rows, n_src), lambda l, i: (l, i, 0))],
        out_specs=pl.BlockSpec((None, rows, W_IN_OUT), lambda l, i: (l, i, 0)),
        compiler_params=_cparams(("parallel", "parallel")),
        name="w_in_prep",
    )(w_in)


def _split3(x):
    hi = x.astype(BF16)
    r1 = x - hi.astype(F32)
    mid = r1.astype(BF16)
    lo = (r1 - mid.astype(F32)).astype(BF16)
    return hi, mid, lo


def _ssd_constants(L):
    t = np.arange(L)
    shifts = np.stack([(t[:, None] - t[None, :] == j) for j in range(1, SSM_CONV)])
    tril = t[:, None] >= t[None, :]
    expand = np.zeros((LANES, SSM_D_INNER), bool)
    for h in range(SSM_HEADS):
        expand[h, h * SSM_HEAD_DIM:(h + 1) * SSM_HEAD_DIM] = True
    expand2 = np.concatenate([expand, expand])
    expand4 = np.concatenate([expand, expand, expand, np.zeros_like(expand)])
    return tuple(jnp.asarray(m, BF16) for m in (shifts, tril, expand2, expand4))


def _ssd_kernel(xs_ref, bc_ref, z_ref, dt_ref, cwx_ref, cbx_ref, cwb_ref, cbb_ref,
                dtb_ref, alog_ref, dskip_ref, nw_ref, shift_ref, tril_ref, e2_ref, e4_ref,
                o_ref, xtail_ref, btail_ref, state_ref):
    L = xs_ref.shape[0]
    P = SSM_HEAD_DIM
    N = SSM_STATE
    HG = SSM_HEADS // SSM_GROUPS
    GW = HG * P
    T = 8

    @pl.when(pl.program_id(1) == 0)
    def _():
        xtail_ref[0:T, :] = jnp.zeros((T, xtail_ref.shape[1]), F32)
        btail_ref[0:T, :] = jnp.zeros((T, btail_ref.shape[1]), F32)
        state_ref[...] = jnp.zeros(state_ref.shape, F32)

    def conv_silu(tail_ref, in_ref, w_ref, b_ref):
        x16 = in_ref[...]
        xf = x16.astype(F32)
        acc = b_ref[...] + w_ref[SSM_CONV - 1:SSM_CONV, :] * xf
        for j in range(1, SSM_CONV):
            acc = acc + w_ref[SSM_CONV - 1 - j:SSM_CONV - j, :] * _dot(shift_ref[j - 1], x16)
        tail_ref[T:2 * T, :] = xf[0:T, :]
        head = b_ref[...] + w_ref[0:1, :] * tail_ref[T - 3:2 * T - 3, :]
        for j in range(1, SSM_CONV):
            head = head + w_ref[j:j + 1, :] * tail_ref[T - 3 + j:2 * T - 3 + j, :]
        tail_ref[0:T, :] = xf[L - T:L, :]
        return _silu(jnp.concatenate([head, acc[T:, :]], axis=0))

    xc = conv_silu(xtail_ref, xs_ref, cwx_ref, cbx_ref)
    bcc = conv_silu(btail_ref, bc_ref, cwb_ref, cbb_ref)

    dtr = dt_ref[...] + dtb_ref[...]
    dt = jnp.maximum(dtr, 0.0) + jnp.log1p(jnp.exp(-jnp.abs(dtr)))
    adt = dt * (-jnp.exp(alog_ref[...]))
    tril = tril_ref[...]
    hi, mid, lo = _split3(adt)
    acs = _dot(tril, hi) + _dot(tril, mid) + _dot(tril, lo)
    acs_t = acs.T

    hi, mid, _ = _split3(dt)
    dtx = _dot(jnp.concatenate([hi, mid], axis=1), e2_ref[...])
    hi, mid, lo = _split3(acs)
    ax = _dot(jnp.concatenate([hi, mid, lo, jnp.zeros_like(lo)], axis=1), e4_ref[...])
    a_last = ax[L - 1:L, :]
    xd = xc * dtx
    xd16 = xd.astype(BF16)
    xdd16 = (xd * jnp.exp(a_last - ax)).astype(BF16)
    e_ax = jnp.exp(ax)
    e_last = jnp.exp(a_last)

    causal = (lax.broadcasted_iota(jnp.int32, (L, L), 0)
              >= lax.broadcasted_iota(jnp.int32, (L, L), 1))
    first_half = lax.broadcasted_iota(jnp.int32, (L, LANES), 1) < P
    tiles = []
    for g in range(SSM_GROUPS):
        bm = bcc[:, g * N:(g + 1) * N]
        cm16 = bcc[:, (SSM_GROUPS + g) * N:(SSM_GROUPS + g + 1) * N].astype(BF16)
        cb = _dot_nt(cm16, bm.astype(BF16))
        gs = slice(g * GW, (g + 1) * GW)
        prev = state_ref[g]
        y_off = _dot(cm16, prev.astype(BF16)) * e_ax[:, gs]
        state_ref[g] = prev * e_last[:, gs] + _dot(bm.T.astype(BF16), xdd16[:, gs])
        for t in range(GW // LANES):
            ts = slice(g * GW + t * LANES, g * GW + (t + 1) * LANES)
            pair = []
            for k in range(LANES // P):
                h = (g * GW + t * LANES) // P + k
                seg = acs[:, h:h + 1] - acs_t[h:h + 1, :]
                decay = jnp.exp(jnp.where(causal, seg, -jnp.inf))
                pair.append(_dot((cb * decay).astype(BF16), xd16[:, ts]))
            tiles.append(jnp.where(first_half, pair[0], pair[1])
                         + y_off[:, t * LANES:(t + 1) * LANES])

    y = jnp.concatenate(tiles, axis=1)
    y = (y + dskip_ref[...] * xc) * _silu(z_ref[...].astype(F32))
    for g in range(SSM_GROUPS):
        gs = slice(g * GW, (g + 1) * GW)
        yg = y[:, gs]
        o_ref[:, gs] = (yg * _rms(yg, GW) * nw_ref[:, gs]).astype(BF16)


def _ssd(proj, dt_raw, cwx, cbx, cwb, cbb, dtb, alog, dskip, nw, *, batch, seq):
    L = math.gcd(seq, SSD_CHUNK)
    nc = seq // L
    di = SSM_D_INNER
    bcw = 2 * SSM_GROUPS * SSM_STATE
    tok = lambda b, c: b * nc + c
    consts = _ssd_constants(L)
    params = (cwx, cbx, cwb, cbb, dtb, alog, dskip, nw) + consts
    return pl.pallas_call(
        _ssd_kernel,
        out_shape=jax.ShapeDtypeStruct((batch * seq, di), BF16),
        grid=(batch, nc),
        in_specs=[
            pl.BlockSpec((L, di), lambda b, c: (tok(b, c), P_XS // di)),
            pl.BlockSpec((L, bcw), lambda b, c: (tok(b, c), P_BC // bcw)),
            pl.BlockSpec((L, di), lambda b, c: (tok(b, c), P_Z // di)),
            pl.BlockSpec((L, LANES), lambda b, c: (tok(b, c), 0)),
        ] + [_resident(p.shape) for p in params],
        out_specs=pl.BlockSpec((L, di), lambda b, c: (tok(b, c), 0)),
        scratch_shapes=[
            pltpu.VMEM((16, di), F32),
            pltpu.VMEM((16, bcw), F32),
            pltpu.VMEM((SSM_GROUPS, SSM_STATE, di // SSM_GROUPS), F32),
        ],
        compiler_params=_cparams(("parallel", "arbitrary")),
        name="ssd",
    )(proj, proj, proj, dt_raw, *params)


def _rope_tile(x, c, s1, s2):
    q = MLA_ROPE // 2
    return x * c + pltpu.roll(x, LANES - q, 1) * s1 + pltpu.roll(x, q, 1) * s2


def _mla_prep_kernel(ql_ref, ckv_ref, kpe_ref, c_ref, s1_ref, s2_ref, qan_ref, kvan_ref,
                     qn_ref, kn_ref, wq_ref, wkv_ref, q_out, k_out, v_out, *, q_scale):
    ql = ql_ref[...].astype(F32)
    qa = (ql * _rms(ql, MLA_Q_LORA) * qan_ref[...]).astype(BF16)
    ckv = ckv_ref[...].astype(F32)
    kva = (ckv * _rms(ckv, MLA_KV_LORA) * kvan_ref[...]).astype(BF16)
    kpe = kpe_ref[...].astype(F32)
    kpe_ss = jnp.sum(kpe * kpe, axis=-1, keepdims=True)
    c, s1, s2 = c_ref[...], s1_ref[...], s2_ref[...]
    qn, kn = qn_ref[...] * q_scale, kn_ref[...]
    kpe_rot = _rope_tile(kpe * kn[:, MLA_NOPE:], c, s1, s2)
    ones = jnp.ones((kpe.shape[0], MLA_V), BF16)

    def project(h):
        return _dot(qa, wq_ref[h]), _dot(kva, wkv_ref[h])

    nxt = project(0)
    for h in range(MLA_HEADS):
        qh, kvh = nxt
        if h + 1 < MLA_HEADS:
            nxt = project(h + 1)
        qh = qh * _rms(qh, MLA_QK_DIM) * qn
        q_out[h, :, 0:MLA_NOPE] = qh[:, 0:MLA_NOPE].astype(BF16)
        q_out[h, :, MLA_NOPE:] = _rope_tile(qh[:, MLA_NOPE:], c, s1, s2).astype(BF16)
        kno = kvh[:, 0:MLA_NOPE]
        r = lax.rsqrt((jnp.sum(kno * kno, axis=-1, keepdims=True) + kpe_ss)
                      * (1.0 / MLA_QK_DIM) + NORM_EPS)
        k_out[h, :, 0:MLA_NOPE] = (kno * r * kn[:, 0:MLA_NOPE]).astype(BF16)
        k_out[h, :, MLA_NOPE:] = (kpe_rot * r).astype(BF16)
        v_out[h, :, 0:MLA_V] = kvh[:, MLA_NOPE:].astype(BF16)
        v_out[h, :, MLA_V:] = ones


def _mla_prep(proj, c, s1, s2, qan, kvan, qn, kn, wq, wkv, *, tm=512):
    m = proj.shape[0]
    tm = min(tm, m)
    row = lambda i: (i, 0)
    return pl.pallas_call(
        functools.partial(_mla_prep_kernel, q_scale=MLA_QK_DIM ** -0.5 * math.log2(math.e)),
        out_shape=(
            jax.ShapeDtypeStruct((MLA_HEADS, m, MLA_QK_PAD), BF16),
            jax.ShapeDtypeStruct((MLA_HEADS, m, MLA_QK_PAD), BF16),
            jax.ShapeDtypeStruct((MLA_HEADS, m, 2 * MLA_V), BF16),
        ),
        grid=(m // tm,),
        in_specs=[
            pl.BlockSpec((tm, MLA_Q_LORA), lambda i: (i, P_QLAT // MLA_Q_LORA)),
            pl.BlockSpec((tm, MLA_KV_LORA), lambda i: (i, P_CKV // MLA_KV_LORA)),
            pl.BlockSpec((tm, LANES), lambda i: (i, P_KPE // LANES)),
            pl.BlockSpec((tm, LANES), row), pl.BlockSpec((tm, LANES), row),
            pl.BlockSpec((tm, LANES), row),
            _resident(qan.shape), _resident(kvan.shape), _resident(qn.shape),
            _resident(kn.shape), _resident(wq.shape), _resident(wkv.shape),
        ],
        out_specs=(
            pl.BlockSpec((MLA_HEADS, tm, MLA_QK_PAD), lambda i: (0, i, 0)),
            pl.BlockSpec((MLA_HEADS, tm, MLA_QK_PAD), lambda i: (0, i, 0)),
            pl.BlockSpec((MLA_HEADS, tm, 2 * MLA_V), lambda i: (0, i, 0)),
        ),
        compiler_params=_cparams(("parallel",)),
        name="mla_prep",
    )(proj, proj, proj, c, s1, s2, qan, kvan, qn, kn, wq, wkv)


def _attn_kernel(q_ref, k_ref, v_ref, o_ref, s_ref, *, blk, heads):
    i = pl.program_id(2)

    def scores(j, slot):
        start = pl.multiple_of(j * blk, blk)
        for h in range(heads):
            s_ref[slot, h] = _dot_nt(q_ref[h], k_ref[h, pl.ds(start, blk), :])

    def consume(j, slot, carry, masked):
        start = pl.multiple_of(j * blk, blk)
        probs = []
        for h, (m, acc) in enumerate(carry):
            s = s_ref[slot, h]
            if masked:
                row = lax.broadcasted_iota(jnp.int32, (blk, blk), 0)
                col = lax.broadcasted_iota(jnp.int32, (blk, blk), 1)
                s = jnp.where(row >= col, s, -jnp.inf)
            m_new = jnp.maximum(m, jnp.max(s, axis=-1, keepdims=True))
            probs.append((m_new, jnp.exp2(m - m_new), jnp.exp2(s - m_new).astype(BF16)))
        return tuple((m_new, alpha * acc + _dot(p, v_ref[h, pl.ds(start, blk), :]))
                     for h, ((_, acc), (m_new, alpha, p)) in enumerate(zip(carry, probs)))

    def finish(carry):
        for h, (_, acc) in enumerate(carry):
            o_ref[:, h * MLA_V:(h + 1) * MLA_V] = (acc[:, :MLA_V] / acc[:, MLA_V:]).astype(BF16)

    def pair(t, carry):
        j = 2 * t
        scores(j + 1, 1)
        carry = consume(j, 0, carry, False)
        scores(j + 2, 0)
        return consume(j + 1, 1, carry, False)

    scores(0, 0)
    init = tuple((jnp.full((blk, 1), -jnp.inf, F32), jnp.zeros((blk, 2 * MLA_V), F32))
                 for _ in range(heads))
    carry = lax.fori_loop(0, i // 2, pair, init)
    last_even = 2 * (i // 2)

    @pl.when(i % 2 == 0)
    def _():
        finish(consume(last_even, 0, carry, True))

    @pl.when(i % 2 == 1)
    def _():
        scores(last_even + 1, 1)
        finish(consume(last_even + 1, 1, consume(last_even, 0, carry, False), True))


def _attention(q, k, v, *, batch, seq, blk=512, heads=4):
    blk = min(blk, seq)
    nq = seq // blk
    return pl.pallas_call(
        functools.partial(_attn_kernel, blk=blk, heads=heads),
        out_shape=jax.ShapeDtypeStruct((batch * seq, MLA_HEADS * MLA_V), BF16),
        grid=(batch, MLA_HEADS // heads, nq),
        in_specs=[
            pl.BlockSpec((heads, blk, MLA_QK_PAD), lambda b, h, i: (h, b * nq + i, 0)),
            pl.BlockSpec((heads, seq, MLA_QK_PAD), lambda b, h, i: (h, b, 0)),
            pl.BlockSpec((heads, seq, 2 * MLA_V), lambda b, h, i: (h, b, 0)),
        ],
        out_specs=pl.BlockSpec((blk, heads * MLA_V), lambda b, h, i: (b * nq + i, h)),
        scratch_shapes=[pltpu.VMEM((2, heads, blk, blk), F32)],
        compiler_params=_cparams(("parallel", "parallel", "arbitrary")),
        name="mla_attention",
    )(q, k, v)


def _ret_log_gamma():
    expo = 5.0 + 7.0 * np.arange(RET_HEADS, dtype=np.float32) / np.float32(RET_HEADS - 1)
    return [float(v) for v in np.log1p(-np.exp2(-expo).astype(np.float32)).astype(np.float32)]


def _ret_kernel(rq_ref, rk_ref, rv_ref, rg_ref, cos_ref, sin_ref, nw_ref, o_ref, state_ref,
                *, log_gamma):
    L = rq_ref.shape[0]
    D = RET_HEAD
    half = D // 2

    @pl.when(pl.program_id(1) == 0)
    def _():
        state_ref[...] = jnp.zeros(state_ref.shape, F32)

    cos, sin = cos_ref[...], sin_ref[...]
    row = lax.broadcasted_iota(jnp.int32, (L, L), 0)
    col = lax.broadcasted_iota(jnp.int32, (L, L), 1)
    rel = (row - col).astype(F32)
    pos = lax.broadcasted_iota(jnp.int32, (L, D), 0).astype(F32)

    def rope(ref, hs):
        x1 = ref[:, hs.start:hs.start + half].astype(F32)
        x2 = ref[:, hs.start + half:hs.stop].astype(F32)
        return jnp.concatenate([x1 * cos - x2 * sin, x1 * sin + x2 * cos], axis=-1)

    for h in range(RET_HEADS):
        lg = log_gamma[h]
        hs = slice(h * D, (h + 1) * D)
        q = rope(rq_ref, hs)
        k = rope(rk_ref, hs) * (RET_HEAD ** -0.5)
        v16 = rv_ref[:, hs]
        q16 = q.astype(BF16)
        dmask = jnp.where(rel >= 0, jnp.exp(rel * lg), 0.0)
        scores = _dot_nt(q16, k.astype(BF16)) * dmask
        y = _dot(scores.astype(BF16), v16)
        prev = state_ref[h]
        y = y + _dot(q16, prev.astype(BF16)) * jnp.exp((pos + 1.0) * lg)
        kd = k * jnp.exp((L - 1.0 - pos) * lg)
        state_ref[h] = prev * math.exp(L * lg) + _dot(kd.T.astype(BF16), v16)
        y = y * _rms(y, D) * nw_ref[:, hs]
        o_ref[:, hs] = (_silu(rg_ref[:, hs].astype(F32)) * y).astype(BF16)


def _retention(proj, cos, sin, nw, *, batch, seq):
    L = math.gcd(seq, RET_CHUNK)
    nc = seq // L
    w = RET_HEADS * RET_HEAD
    tok = lambda b, c: b * nc + c
    return pl.pallas_call(
        functools.partial(_ret_kernel, log_gamma=_ret_log_gamma()),
        out_shape=jax.ShapeDtypeStruct((batch * seq, w), BF16),
        grid=(batch, nc),
        in_specs=[
            pl.BlockSpec((L, w), lambda b, c: (tok(b, c), P_RQ // w)),
            pl.BlockSpec((L, w), lambda b, c: (tok(b, c), P_RK // w)),
            pl.BlockSpec((L, w), lambda b, c: (tok(b, c), P_RV // w)),
            pl.BlockSpec((L, w), lambda b, c: (tok(b, c), P_RG // w)),
            pl.BlockSpec((L, RET_HEAD // 2), lambda b, c: (tok(b, c), 0)),
            pl.BlockSpec((L, RET_HEAD // 2), lambda b, c: (tok(b, c), 0)),
            _resident(nw.shape),
        ],
        out_specs=pl.BlockSpec((L, w), lambda b, c: (tok(b, c), 0)),
        scratch_shapes=[pltpu.VMEM((RET_HEADS, RET_HEAD, RET_HEAD), F32)],
        compiler_params=_cparams(("parallel", "arbitrary")),
        name="retention",
    )(proj, proj, proj, proj, cos, sin, nw)


def _merge_kernel(ys_ref, ym_ref, yr_ref, g0_ref, g1_ref, g2_ref, gb_ref, x_ref,
                  w0_ref, w1_ref, w2_ref, wo_ref, o_ref):
    d = x_ref.shape[-1]

    def branch(k, y_ref, g_ref, w_ref):
        gate = _sigmoid(g_ref[...].astype(F32) + gb_ref[:, k * d:(k + 1) * d])
        return gate * _dot(y_ref[...], w_ref[...])

    merged = (branch(0, ys_ref, g0_ref, w0_ref) + branch(1, ym_ref, g1_ref, w1_ref)
              + branch(2, yr_ref, g2_ref, w2_ref))
    o_ref[...] = x_ref[...] + _dot(merged.astype(BF16), wo_ref[...])


def _merge(ys, ym, yr, proj, gb, x, w0, w1, w2, wo, *, tm=256):
    m, d = x.shape
    tm = min(tm, m)
    bw = ys.shape[1]
    row = lambda i: (i, 0)
    return pl.pallas_call(
        _merge_kernel,
        out_shape=jax.ShapeDtypeStruct((m, d), F32),
        grid=(m // tm,),
        in_specs=[
            pl.BlockSpec((tm, bw), row), pl.BlockSpec((tm, bw), row), pl.BlockSpec((tm, bw), row),
            pl.BlockSpec((tm, d), lambda i: (i, P_GATES // d)),
            pl.BlockSpec((tm, d), lambda i: (i, P_GATES // d + 1)),
            pl.BlockSpec((tm, d), lambda i: (i, P_GATES // d + 2)),
            _resident(gb.shape),
            pl.BlockSpec((tm, d), row),
            _resident(w0.shape), _resident(w1.shape), _resident(w2.shape), _resident(wo.shape),
        ],
        out_specs=pl.BlockSpec((tm, d), row),
        compiler_params=_cparams(("parallel",)),
        name="merge",
    )(ys, ym, yr, proj, proj, proj, gb, x, w0, w1, w2, wo)


def _rope_angles(positions, dim):
    inv = 1.0 / (ROPE_THETA ** (jnp.arange(0, dim, 2, dtype=F32) / dim))
    return positions.astype(F32).reshape(-1, 1) * inv


def _pad_lanes(v, n):
    return jnp.concatenate([v, jnp.zeros((n - v.shape[0],), v.dtype)]).reshape(1, n)


def kernel(x, positions, ffn1_norm, ffn1_w_gate, ffn1_w_up, ffn1_w_down, mix_norm, w_in, gate_b, conv_w, conv_b, dt_bias, a_log, d_skip, ssm_norm, q_a_norm, w_q_b, kv_a_norm, w_kv_b, q_norm, k_norm, ret_norm, w_br_ssm, w_br_mla, w_br_ret, w_out, ffn2_norm, ffn2_w_gate, ffn2_w_up, ffn2_w_down):
    batch, seq, d = x.shape
    depth = w_in.shape[0]
    x = x.reshape(batch * seq, d)

    ang = _rope_angles(positions, MLA_ROPE)
    cm, sm = jnp.cos(ang), jnp.sin(ang)
    zq = jnp.zeros_like(cm)
    rope_c = jnp.concatenate([cm, cm, zq, zq], axis=1)
    rope_s1 = jnp.concatenate([-sm, zq, zq, zq], axis=1)
    rope_s2 = jnp.concatenate([zq, sm, zq, zq], axis=1)
    ang = _rope_angles(positions, RET_HEAD)
    cos_ret, sin_ret = jnp.cos(ang), jnp.sin(ang)

    w_in16 = _w_in_prep(w_in)
    row = lambda v: v.reshape(1, -1)
    for l in range(depth):
        x = _ffn(x, row(ffn1_norm[l]), ffn1_w_gate[l].astype(BF16), ffn1_w_up[l].astype(BF16),
                 ffn1_w_down[l].astype(BF16))

        proj, dt_raw = _inproj(x, row(mix_norm[l]), w_in16, l)

        y_ssm = _ssd(
            proj, dt_raw,
            conv_w[l][:, :SSM_D_INNER], row(conv_b[l][:SSM_D_INNER]),
            conv_w[l][:, SSM_D_INNER:], row(conv_b[l][SSM_D_INNER:]),
            _pad_lanes(dt_bias[l], LANES), _pad_lanes(a_log[l], LANES),
            row(jnp.repeat(d_skip[l], SSM_HEAD_DIM)), row(ssm_norm[l]),
            batch=batch, seq=seq)

        wq = w_q_b[l].reshape(MLA_Q_LORA, MLA_HEADS, MLA_QK_DIM)
        wq = jnp.pad(wq, ((0, 0), (0, 0), (0, MLA_QK_PAD - MLA_QK_DIM)))
        wq = wq.transpose(1, 0, 2).astype(BF16)
        wkv = w_kv_b[l].reshape(MLA_KV_LORA, MLA_HEADS, MLA_NOPE + MLA_V)
        wkv = wkv.transpose(1, 0, 2).astype(BF16)
        q, k, v = _mla_prep(
            proj, rope_c, rope_s1, rope_s2, row(q_a_norm[l]), row(kv_a_norm[l]),
            _pad_lanes(q_norm[l], MLA_QK_PAD), _pad_lanes(k_norm[l], MLA_QK_PAD), wq, wkv)
        y_mla = _attention(q, k, v, batch=batch, seq=seq)

        y_ret = _retention(proj, cos_ret, sin_ret, row(ret_norm[l]), batch=batch, seq=seq)

        x = _merge(y_ssm, y_mla, y_ret, proj, row(gate_b[l]), x,
                   w_br_ssm[l].astype(BF16), w_br_mla[l].astype(BF16),
                   w_br_ret[l].astype(BF16), w_out[l].astype(BF16))

        x = _ffn(x, row(ffn2_norm[l]), ffn2_w_gate[l].astype(BF16), ffn2_w_up[l].astype(BF16),
                 ffn2_w_down[l].astype(BF16))
    return x.reshape(batch, seq, d)
```

```python
import functools
import math

import jax
import jax.numpy as jnp
import numpy as np
from jax import lax
from jax.experimental import pallas as pl
from jax.experimental.pallas import tpu as pltpu

F32 = jnp.float32
BF16 = jnp.bfloat16

NORM_EPS = 1e-6
ROPE_THETA = 10000.0

SSM_HEADS = 16
SSM_HEAD_DIM = 64
SSM_D_INNER = SSM_HEADS * SSM_HEAD_DIM
SSM_GROUPS = 2
SSM_STATE = 128
SSM_CONV = 4
SSD_CHUNK = 256
MLA_HEADS = 8
MLA_Q_LORA = 768
MLA_KV_LORA = 512
MLA_NOPE = 128
MLA_ROPE = 64
MLA_QK_DIM = MLA_NOPE + MLA_ROPE
MLA_V = 128
RET_HEADS = 4
RET_HEAD = 256
RET_CHUNK = 256
N_BRANCH = 3

LANES = 128
MLA_QK_PAD = MLA_NOPE + LANES
VMEM_LIMIT = 56 * 1024 * 1024


def _cparams(sem):
    return pltpu.CompilerParams(dimension_semantics=sem, vmem_limit_bytes=VMEM_LIMIT)


def _resident(shape):
    nd = len(shape)
    return pl.BlockSpec(shape, lambda *_: (0,) * nd, pipeline_mode=pl.Buffered(1))


def _resident_layer(shape, layer):
    nd = len(shape) - 1
    return pl.BlockSpec((None,) + tuple(shape[1:]), lambda *_: (layer,) + (0,) * nd,
                        pipeline_mode=pl.Buffered(1))


def _rms(x, n):
    return lax.rsqrt(jnp.sum(x * x, axis=-1, keepdims=True) * (1.0 / n) + NORM_EPS)


def _sigmoid(x):
    return 0.5 * jnp.tanh(0.5 * x) + 0.5


def _silu(x):
    h = 0.5 * x
    return h * jnp.tanh(h) + h


def _dot(a, b):
    return jnp.dot(a, b, preferred_element_type=F32)


def _dot_nt(a, b):
    return lax.dot_general(a, b, (((1,), (1,)), ((), ())), preferred_element_type=F32)


def _ffn_kernel(x_ref, nw_ref, wg_ref, wu_ref, wd_ref, o_ref, xn_ref):
    d = x_ref.shape[-1]

    @pl.when(pl.program_id(1) == 0)
    def _():
        x = x_ref[...]
        xn_ref[...] = (x * _rms(x, d) * nw_ref[...]).astype(BF16)
        o_ref[...] = x

    xn = xn_ref[...]
    g = _dot(xn, wg_ref[...])
    u = _dot(xn, wu_ref[...])
    o_ref[...] += _dot((_silu(g) * (0.5 * u)).astype(BF16), wd_ref[...])


def _ffn(x, nw, wg, wu, wd, layer, *, tm=1024, tf=512):
    m, d = x.shape
    f = wg.shape[2]
    tm = min(tm, m)
    return pl.pallas_call(
        _ffn_kernel,
        out_shape=jax.ShapeDtypeStruct((m, d), F32),
        grid=(m // tm, f // tf),
        in_specs=[
            pl.BlockSpec((tm, d), lambda i, j: (i, 0)),
            pl.BlockSpec((1, d), lambda i, j: (0, 0)),
            pl.BlockSpec((None, d, tf), lambda i, j: (layer, 0, j)),
            pl.BlockSpec((None, d, tf), lambda i, j: (layer, 0, j)),
            pl.BlockSpec((None, tf, d), lambda i, j: (layer, j, 0)),
        ],
        out_specs=pl.BlockSpec((tm, d), lambda i, j: (i, 0)),
        scratch_shapes=[pltpu.VMEM((tm, d), BF16)],
        compiler_params=_cparams(("parallel", "arbitrary")),
        name="ffn",
    )(x, nw, wg, wu, wd)


P_GATES = 0
P_RQ = 6144
P_RK = 7168
P_RV = 8192
P_RG = 9216
P_Z = 10240
P_XS = 11264
P_BC = 12288
P_KPE = 12800
P_QLAT = 13056
P_CKV = 13824
P_TOTAL = 14336


def _inproj_kernel(x_ref, nw_ref, w_ref, wdt_ref, o_ref, dt_ref, xn_ref):
    j = pl.program_id(1)
    d = x_ref.shape[-1]

    @pl.when(j == 0)
    def _():
        x = x_ref[...]
        xn = (x * _rms(x, d) * nw_ref[...]).astype(BF16)
        xn_ref[...] = xn
        dt_ref[...] = _dot(xn, wdt_ref[...])

    o_ref[...] = _dot(xn_ref[...], w_ref[...]).astype(BF16)


def _inproj(x, nw, w_all, layer, *, tm=1024, tn=2048):
    m, d = x.shape
    n = P_TOTAL
    tm = min(tm, m)
    return pl.pallas_call(
        _inproj_kernel,
        out_shape=(jax.ShapeDtypeStruct((m, n), BF16), jax.ShapeDtypeStruct((m, LANES), F32)),
        grid=(m // tm, n // tn),
        in_specs=[
            pl.BlockSpec((tm, d), lambda i, j: (i, 0)),
            pl.BlockSpec((1, d), lambda i, j: (0, 0)),
            pl.BlockSpec((None, d, tn), lambda i, j: (layer, 0, j)),
            pl.BlockSpec((None, d, LANES), lambda i, j: (layer, 0, P_TOTAL // LANES)),
        ],
        out_specs=(
            pl.BlockSpec((tm, tn), lambda i, j: (i, j)),
            pl.BlockSpec((tm, LANES), lambda i, j: (i, 0)),
        ),
        scratch_shapes=[pltpu.VMEM((tm, d), BF16)],
        compiler_params=_cparams(("parallel", "arbitrary")),
        name="inproj",
    )(x, nw, w_all, w_all)


_W_IN_SEGMENTS = (
    (P_GATES, 8016, 6144), (P_RQ, 3920, 1024), (P_RK, 4944, 1024), (P_RV, 5968, 1024),
    (P_RG, 6992, 1024), (P_Z, 0, 1024), (P_XS, 1024, 1536), (P_KPE, 3856, MLA_ROPE),
    (P_QLAT, 2576, MLA_Q_LORA), (P_CKV, 3344, MLA_KV_LORA), (P_TOTAL, 2560, SSM_HEADS))
W_IN_OUT = P_TOTAL + LANES


def _w_in_prep_kernel(x_ref, o_ref):
    n_src = x_ref.shape[1]
    segs = sorted(_W_IN_SEGMENTS)
    for k, (dest, _, width) in enumerate(segs):
        full = dest + width // LANES * LANES
        nxt = segs[k + 1][0] if k + 1 < len(segs) else o_ref.shape[1]
        if nxt > full:
            o_ref[:, full:nxt] = jnp.zeros((o_ref.shape[0], nxt - full), BF16)
    for dest, source, width in segs:
        lo = source // LANES * LANES
        hi = min(-(-(source + width) // LANES) * LANES, n_src)
        o_ref[:, dest:dest + width] = (
            x_ref[:, lo:hi][:, source - lo:source - lo + width].astype(BF16))


def _w_in_prep(w_in, *, rows=128):
    depth, d, n_src = w_in.shape
    return pl.pallas_call(
        _w_in_prep_kernel,
        out_shape=jax.ShapeDtypeStruct((depth, d, W_IN_OUT), BF16),
        grid=(depth, d // rows),
        in_specs=[pl.BlockSpec((None, rows, n_src), lambda l, i: (l, i, 0))],
        out_specs=pl.BlockSpec((None, rows, W_IN_OUT), lambda l, i: (l, i, 0)),
        compiler_params=_cparams(("parallel", "parallel")),
        name="w_in_prep",
    )(w_in)


def _split3(x):
    hi = x.astype(BF16)
    r1 = x - hi.astype(F32)
    mid = r1.astype(BF16)
    lo = (r1 - mid.astype(F32)).astype(BF16)
    return hi, mid, lo


def _ssd_constants(L):
    t = np.arange(L)
    shifts = np.stack([(t[:, None] - t[None, :] == j) for j in range(1, SSM_CONV)])
    tril = t[:, None] >= t[None, :]
    expand = np.zeros((LANES, SSM_D_INNER), bool)
    for h in range(SSM_HEADS):
        expand[h, h * SSM_HEAD_DIM:(h + 1) * SSM_HEAD_DIM] = True
    expand2 = np.concatenate([expand, expand])
    expand4 = np.concatenate([expand, expand, expand, np.zeros_like(expand)])
    return tuple(jnp.asarray(m, BF16) for m in (shifts, tril, expand2, expand4))


def _ssd_kernel(xs_ref, bc_ref, z_ref, dt_ref, cwx_ref, cbx_ref, cwb_ref, cbb_ref,
                dtb_ref, alog_ref, dskip_ref, nw_ref, shift_ref, tril_ref, e2_ref, e4_ref,
                o_ref, xtail_ref, btail_ref, state_ref):
    L = xs_ref.shape[0]
    P = SSM_HEAD_DIM
    N = SSM_STATE
    HG = SSM_HEADS // SSM_GROUPS
    GW = HG * P
    T = 8

    @pl.when(pl.program_id(1) == 0)
    def _():
        xtail_ref[0:T, :] = jnp.zeros((T, xtail_ref.shape[1]), F32)
        btail_ref[0:T, :] = jnp.zeros((T, btail_ref.shape[1]), F32)
        state_ref[...] = jnp.zeros(state_ref.shape, F32)

    def conv_silu(tail_ref, in_ref, w_ref, b_ref):
        x16 = in_ref[...]
        xf = x16.astype(F32)
        acc = b_ref[...] + w_ref[SSM_CONV - 1:SSM_CONV, :] * xf
        for j in range(1, SSM_CONV):
            acc = acc + w_ref[SSM_CONV - 1 - j:SSM_CONV - j, :] * _dot(shift_ref[j - 1], x16)
        tail_ref[T:2 * T, :] = xf[0:T, :]
        head = b_ref[...] + w_ref[0:1, :] * tail_ref[T - 3:2 * T - 3, :]
        for j in range(1, SSM_CONV):
            head = head + w_ref[j:j + 1, :] * tail_ref[T - 3 + j:2 * T - 3 + j, :]
        tail_ref[0:T, :] = xf[L - T:L, :]
        return _silu(jnp.concatenate([head, acc[T:, :]], axis=0))

    xc = conv_silu(xtail_ref, xs_ref, cwx_ref, cbx_ref)
    bcc = conv_silu(btail_ref, bc_ref, cwb_ref, cbb_ref)

    dtr = dt_ref[...] + dtb_ref[...]
    dt = jnp.maximum(dtr, 0.0) + jnp.log1p(jnp.exp(-jnp.abs(dtr)))
    adt = dt * (-jnp.exp(alog_ref[...]))
    tril = tril_ref[...]
    hi, mid, lo = _split3(adt)
    acs = _dot(tril, hi) + _dot(tril, mid) + _dot(tril, lo)
    acs_t = acs.T

    hi, mid, _ = _split3(dt)
    dtx = _dot(jnp.concatenate([hi, mid], axis=1), e2_ref[...])
    hi, mid, lo = _split3(acs)
    ax = _dot(jnp.concatenate([hi, mid, lo, jnp.zeros_like(lo)], axis=1), e4_ref[...])
    a_last = ax[L - 1:L, :]
    xd = xc * dtx
    xd16 = xd.astype(BF16)
    xdd16 = (xd * jnp.exp(a_last - ax)).astype(BF16)
    e_ax = jnp.exp(ax)
    e_last = jnp.exp(a_last)

    causal = (lax.broadcasted_iota(jnp.int32, (L, L), 0)
              >= lax.broadcasted_iota(jnp.int32, (L, L), 1))
    first_half = lax.broadcasted_iota(jnp.int32, (L, LANES), 1) < P
    tiles = []
    for g in range(SSM_GROUPS):
        bm = bcc[:, g * N:(g + 1) * N]
        cm16 = bcc[:, (SSM_GROUPS + g) * N:(SSM_GROUPS + g + 1) * N].astype(BF16)
        cb = _dot_nt(cm16, bm.astype(BF16))
        gs = slice(g * GW, (g + 1) * GW)
        prev = state_ref[g]
        y_off = _dot(cm16, prev.astype(BF16)) * e_ax[:, gs]
        state_ref[g] = prev * e_last[:, gs] + _dot(bm.T.astype(BF16), xdd16[:, gs])
        for t in range(GW // LANES):
            ts = slice(g * GW + t * LANES, g * GW + (t + 1) * LANES)
            pair = []
            for k in range(LANES // P):
                h = (g * GW + t * LANES) // P + k
                seg = acs[:, h:h + 1] - acs_t[h:h + 1, :]
                decay = jnp.exp(jnp.where(causal, seg, -jnp.inf))
                pair.append(_dot((cb * decay).astype(BF16), xd16[:, ts]))
            tiles.append(jnp.where(first_half, pair[0], pair[1])
                         + y_off[:, t * LANES:(t + 1) * LANES])

    y = jnp.concatenate(tiles, axis=1)
    y = (y + dskip_ref[...] * xc) * _silu(z_ref[...].astype(F32))
    for g in range(SSM_GROUPS):
        gs = slice(g * GW, (g + 1) * GW)
        yg = y[:, gs]
        o_ref[:, gs] = (yg * _rms(yg, GW) * nw_ref[:, gs]).astype(BF16)


def _ssd(proj, dt_raw, cwx, cbx, cwb, cbb, dtb, alog, dskip, nw, *, batch, seq):
    L = math.gcd(seq, SSD_CHUNK)
    nc = seq // L
    di = SSM_D_INNER
    bcw = 2 * SSM_GROUPS * SSM_STATE
    tok = lambda b, c: b * nc + c
    consts = _ssd_constants(L)
    params = (cwx, cbx, cwb, cbb, dtb, alog, dskip, nw) + consts
    return pl.pallas_call(
        _ssd_kernel,
        out_shape=jax.ShapeDtypeStruct((batch * seq, di), BF16),
        grid=(batch, nc),
        in_specs=[
            pl.BlockSpec((L, di), lambda b, c: (tok(b, c), P_XS // di)),
            pl.BlockSpec((L, bcw), lambda b, c: (tok(b, c), P_BC // bcw)),
            pl.BlockSpec((L, di), lambda b, c: (tok(b, c), P_Z // di)),
            pl.BlockSpec((L, LANES), lambda b, c: (tok(b, c), 0)),
        ] + [_resident(p.shape) for p in params],
        out_specs=pl.BlockSpec((L, di), lambda b, c: (tok(b, c), 0)),
        scratch_shapes=[
            pltpu.VMEM((16, di), F32),
            pltpu.VMEM((16, bcw), F32),
            pltpu.VMEM((SSM_GROUPS, SSM_STATE, di // SSM_GROUPS), F32),
        ],
        compiler_params=_cparams(("parallel", "arbitrary")),
        name="ssd",
    )(proj, proj, proj, dt_raw, *params)


def _rope_tile(x, c, s1, s2):
    q = MLA_ROPE // 2
    return x * c + pltpu.roll(x, LANES - q, 1) * s1 + pltpu.roll(x, q, 1) * s2


def _mla_prep_kernel(ql_ref, ckv_ref, kpe_ref, c_ref, s1_ref, s2_ref, qan_ref, kvan_ref,
                     qn_ref, kn_ref, wq_ref, wkv_ref, q_out, k_out, v_out, *, q_scale):
    ql = ql_ref[...].astype(F32)
    qa = (ql * _rms(ql, MLA_Q_LORA) * qan_ref[...]).astype(BF16)
    ckv = ckv_ref[...].astype(F32)
    kva = (ckv * _rms(ckv, MLA_KV_LORA) * kvan_ref[...]).astype(BF16)
    kpe = kpe_ref[...].astype(F32)
    kpe_ss = jnp.sum(kpe * kpe, axis=-1, keepdims=True)
    c, s1, s2 = c_ref[...], s1_ref[...], s2_ref[...]
    qn, kn = qn_ref[...] * q_scale, kn_ref[...]
    kpe_rot = _rope_tile(kpe * kn[:, MLA_NOPE:], c, s1, s2)
    ones = jnp.ones((kpe.shape[0], MLA_V), BF16)

    def project(h):
        return _dot(qa, wq_ref[h]), _dot(kva, wkv_ref[h])

    nxt = project(0)
    for h in range(MLA_HEADS):
        qh, kvh = nxt
        if h + 1 < MLA_HEADS:
            nxt = project(h + 1)
        qh = qh * _rms(qh, MLA_QK_DIM) * qn
        q_out[h, :, 0:MLA_NOPE] = qh[:, 0:MLA_NOPE].astype(BF16)
        q_out[h, :, MLA_NOPE:] = _rope_tile(qh[:, MLA_NOPE:], c, s1, s2).astype(BF16)
        kno = kvh[:, 0:MLA_NOPE]
        r = lax.rsqrt((jnp.sum(kno * kno, axis=-1, keepdims=True) + kpe_ss)
                      * (1.0 / MLA_QK_DIM) + NORM_EPS)
        k_out[h, :, 0:MLA_NOPE] = (kno * r * kn[:, 0:MLA_NOPE]).astype(BF16)
        k_out[h, :, MLA_NOPE:] = (kpe_rot * r).astype(BF16)
        v_out[h, :, 0:MLA_V] = kvh[:, MLA_NOPE:].astype(BF16)
        v_out[h, :, MLA_V:] = ones


def _mla_prep(proj, c, s1, s2, qan, kvan, qn, kn, wq, wkv, *, tm=512):
    m = proj.shape[0]
    tm = min(tm, m)
    row = lambda i: (i, 0)
    return pl.pallas_call(
        functools.partial(_mla_prep_kernel, q_scale=MLA_QK_DIM ** -0.5 * math.log2(math.e)),
        out_shape=(
            jax.ShapeDtypeStruct((MLA_HEADS, m, MLA_QK_PAD), BF16),
            jax.ShapeDtypeStruct((MLA_HEADS, m, MLA_QK_PAD), BF16),
            jax.ShapeDtypeStruct((MLA_HEADS, m, 2 * MLA_V), BF16),
        ),
        grid=(m // tm,),
        in_specs=[
            pl.BlockSpec((tm, MLA_Q_LORA), lambda i: (i, P_QLAT // MLA_Q_LORA)),
            pl.BlockSpec((tm, MLA_KV_LORA), lambda i: (i, P_CKV // MLA_KV_LORA)),
            pl.BlockSpec((tm, LANES), lambda i: (i, P_KPE // LANES)),
            pl.BlockSpec((tm, LANES), row), pl.BlockSpec((tm, LANES), row),
            pl.BlockSpec((tm, LANES), row),
            _resident(qan.shape), _resident(kvan.shape), _resident(qn.shape),
            _resident(kn.shape), _resident(wq.shape), _resident(wkv.shape),
        ],
        out_specs=(
            pl.BlockSpec((MLA_HEADS, tm, MLA_QK_PAD), lambda i: (0, i, 0)),
            pl.BlockSpec((MLA_HEADS, tm, MLA_QK_PAD), lambda i: (0, i, 0)),
            pl.BlockSpec((MLA_HEADS, tm, 2 * MLA_V), lambda i: (0, i, 0)),
        ),
        compiler_params=_cparams(("parallel",)),
        name="mla_prep",
    )(proj, proj, proj, c, s1, s2, qan, kvan, qn, kn, wq, wkv)


def _attn_kernel(q_ref, k_ref, v_ref, o_ref, s_ref, *, blk, heads):
    i = pl.program_id(2)

    def scores(j, slot):
        start = pl.multiple_of(j * blk, blk)
        for h in range(heads):
            s_ref[slot, h] = _dot_nt(q_ref[h], k_ref[h, pl.ds(start, blk), :])

    def consume(j, slot, carry, masked):
        start = pl.multiple_of(j * blk, blk)
        probs = []
        for h, (m, acc) in enumerate(carry):
            s = s_ref[slot, h]
            if masked:
                row = lax.broadcasted_iota(jnp.int32, (blk, blk), 0)
                col = lax.broadcasted_iota(jnp.int32, (blk, blk), 1)
                s = jnp.where(row >= col, s, -jnp.inf)
            m_new = jnp.maximum(m, jnp.max(s, axis=-1, keepdims=True))
            probs.append((m_new, jnp.exp2(m - m_new), jnp.exp2(s - m_new).astype(BF16)))
        return tuple((m_new, alpha * acc + _dot(p, v_ref[h, pl.ds(start, blk), :]))
                     for h, ((_, acc), (m_new, alpha, p)) in enumerate(zip(carry, probs)))

    def finish(carry):
        for h, (_, acc) in enumerate(carry):
            o_ref[:, h * MLA_V:(h + 1) * MLA_V] = (acc[:, :MLA_V] / acc[:, MLA_V:]).astype(BF16)

    def pair(t, carry):
        j = 2 * t
        scores(j + 1, 1)
        carry = consume(j, 0, carry, False)
        scores(j + 2, 0)
        return consume(j + 1, 1, carry, False)

    scores(0, 0)
    init = tuple((jnp.full((blk, 1), -jnp.inf, F32), jnp.zeros((blk, 2 * MLA_V), F32))
                 for _ in range(heads))
    carry = lax.fori_loop(0, i // 2, pair, init)
    last_even = 2 * (i // 2)

    @pl.when(i % 2 == 0)
    def _():
        finish(consume(last_even, 0, carry, True))

    @pl.when(i % 2 == 1)
    def _():
        scores(last_even + 1, 1)
        finish(consume(last_even + 1, 1, consume(last_even, 0, carry, False), True))


def _attention(q, k, v, *, batch, seq, blk=512, heads=4):
    blk = min(blk, seq)
    nq = seq // blk
    return pl.pallas_call(
        functools.partial(_attn_kernel, blk=blk, heads=heads),
        out_shape=jax.ShapeDtypeStruct((batch * seq, MLA_HEADS * MLA_V), BF16),
        grid=(batch, MLA_HEADS // heads, nq),
        in_specs=[
            pl.BlockSpec((heads, blk, MLA_QK_PAD), lambda b, h, i: (h, b * nq + i, 0)),
            pl.BlockSpec((heads, seq, MLA_QK_PAD), lambda b, h, i: (h, b, 0)),
            pl.BlockSpec((heads, seq, 2 * MLA_V), lambda b, h, i: (h, b, 0)),
        ],
        out_specs=pl.BlockSpec((blk, heads * MLA_V), lambda b, h, i: (b * nq + i, h)),
        scratch_shapes=[pltpu.VMEM((2, heads, blk, blk), F32)],
        compiler_params=_cparams(("parallel", "parallel", "arbitrary")),
        name="mla_attention",
    )(q, k, v)


def _ret_log_gamma():
    expo = 5.0 + 7.0 * np.arange(RET_HEADS, dtype=np.float32) / np.float32(RET_HEADS - 1)
    return [float(v) for v in np.log1p(-np.exp2(-expo).astype(np.float32)).astype(np.float32)]


def _ret_kernel(rq_ref, rk_ref, rv_ref, rg_ref, cos_ref, sin_ref, nw_ref, o_ref, state_ref,
                *, log_gamma):
    L = rq_ref.shape[0]
    D = RET_HEAD
    half = D // 2

    @pl.when(pl.program_id(1) == 0)
    def _():
        state_ref[...] = jnp.zeros(state_ref.shape, F32)

    cos, sin = cos_ref[...], sin_ref[...]
    row = lax.broadcasted_iota(jnp.int32, (L, L), 0)
    col = lax.broadcasted_iota(jnp.int32, (L, L), 1)
    rel = (row - col).astype(F32)
    pos = lax.broadcasted_iota(jnp.int32, (L, D), 0).astype(F32)

    def rope(ref, hs):
        x1 = ref[:, hs.start:hs.start + half].astype(F32)
        x2 = ref[:, hs.start + half:hs.stop].astype(F32)
        return jnp.concatenate([x1 * cos - x2 * sin, x1 * sin + x2 * cos], axis=-1)

    for h in range(RET_HEADS):
        lg = log_gamma[h]
        hs = slice(h * D, (h + 1) * D)
        q = rope(rq_ref, hs)
        k = rope(rk_ref, hs) * (RET_HEAD ** -0.5)
        v16 = rv_ref[:, hs]
        q16 = q.astype(BF16)
        dmask = jnp.where(rel >= 0, jnp.exp(rel * lg), 0.0)
        scores = _dot_nt(q16, k.astype(BF16)) * dmask
        y = _dot(scores.astype(BF16), v16)
        prev = state_ref[h]
        y = y + _dot(q16, prev.astype(BF16)) * jnp.exp((pos + 1.0) * lg)
        kd = k * jnp.exp((L - 1.0 - pos) * lg)
        state_ref[h] = prev * math.exp(L * lg) + _dot(kd.T.astype(BF16), v16)
        y = y * _rms(y, D) * nw_ref[:, hs]
        o_ref[:, hs] = (_silu(rg_ref[:, hs].astype(F32)) * y).astype(BF16)


def _retention(proj, cos, sin, nw, *, batch, seq):
    L = math.gcd(seq, RET_CHUNK)
    nc = seq // L
    w = RET_HEADS * RET_HEAD
    tok = lambda b, c: b * nc + c
    return pl.pallas_call(
        functools.partial(_ret_kernel, log_gamma=_ret_log_gamma()),
        out_shape=jax.ShapeDtypeStruct((batch * seq, w), BF16),
        grid=(batch, nc),
        in_specs=[
            pl.BlockSpec((L, w), lambda b, c: (tok(b, c), P_RQ // w)),
            pl.BlockSpec((L, w), lambda b, c: (tok(b, c), P_RK // w)),
            pl.BlockSpec((L, w), lambda b, c: (tok(b, c), P_RV // w)),
            pl.BlockSpec((L, w), lambda b, c: (tok(b, c), P_RG // w)),
            pl.BlockSpec((L, RET_HEAD // 2), lambda b, c: (tok(b, c), 0)),
            pl.BlockSpec((L, RET_HEAD // 2), lambda b, c: (tok(b, c), 0)),
            _resident(nw.shape),
        ],
        out_specs=pl.BlockSpec((L, w), lambda b, c: (tok(b, c), 0)),
        scratch_shapes=[pltpu.VMEM((RET_HEADS, RET_HEAD, RET_HEAD), F32)],
        compiler_params=_cparams(("parallel", "arbitrary")),
        name="retention",
    )(proj, proj, proj, proj, cos, sin, nw)


def _merge_kernel(ys_ref, ym_ref, yr_ref, g0_ref, g1_ref, g2_ref, gb_ref, x_ref,
                  w0_ref, w1_ref, w2_ref, wo_ref, o_ref):
    d = x_ref.shape[-1]

    def branch(k, y_ref, g_ref, w_ref):
        gate = _sigmoid(g_ref[...].astype(F32) + gb_ref[:, k * d:(k + 1) * d])
        return gate * _dot(y_ref[...], w_ref[...])

    merged = (branch(0, ys_ref, g0_ref, w0_ref) + branch(1, ym_ref, g1_ref, w1_ref)
              + branch(2, yr_ref, g2_ref, w2_ref))
    o_ref[...] = x_ref[...] + _dot(merged.astype(BF16), wo_ref[...])


def _merge(ys, ym, yr, proj, gb, x, w0, w1, w2, wo, layer, *, tm=256):
    m, d = x.shape
    tm = min(tm, m)
    bw = ys.shape[1]
    row = lambda i: (i, 0)
    return pl.pallas_call(
        _merge_kernel,
        out_shape=jax.ShapeDtypeStruct((m, d), F32),
        grid=(m // tm,),
        in_specs=[
            pl.BlockSpec((tm, bw), row), pl.BlockSpec((tm, bw), row), pl.BlockSpec((tm, bw), row),
            pl.BlockSpec((tm, d), lambda i: (i, P_GATES // d)),
            pl.BlockSpec((tm, d), lambda i: (i, P_GATES // d + 1)),
            pl.BlockSpec((tm, d), lambda i: (i, P_GATES // d + 2)),
            _resident(gb.shape),
            pl.BlockSpec((tm, d), row),
            _resident_layer(w0.shape, layer), _resident_layer(w1.shape, layer),
            _resident_layer(w2.shape, layer), _resident_layer(wo.shape, layer),
        ],
        out_specs=pl.BlockSpec((tm, d), row),
        compiler_params=_cparams(("parallel",)),
        name="merge",
    )(ys, ym, yr, proj, proj, proj, gb, x, w0, w1, w2, wo)


def _rope_angles(positions, dim):
    inv = 1.0 / (ROPE_THETA ** (jnp.arange(0, dim, 2, dtype=F32) / dim))
    return positions.astype(F32).reshape(-1, 1) * inv


def _pad_lanes(v, n):
    return jnp.concatenate([v, jnp.zeros((n - v.shape[0],), v.dtype)]).reshape(1, n)


def kernel(x, positions, ffn1_norm, ffn1_w_gate, ffn1_w_up, ffn1_w_down, mix_norm, w_in, gate_b, conv_w, conv_b, dt_bias, a_log, d_skip, ssm_norm, q_a_norm, w_q_b, kv_a_norm, w_kv_b, q_norm, k_norm, ret_norm, w_br_ssm, w_br_mla, w_br_ret, w_out, ffn2_norm, ffn2_w_gate, ffn2_w_up, ffn2_w_down):
    batch, seq, d = x.shape
    depth = w_in.shape[0]
    x = x.reshape(batch * seq, d)

    ang = _rope_angles(positions, MLA_ROPE)
    cm, sm = jnp.cos(ang), jnp.sin(ang)
    zq = jnp.zeros_like(cm)
    rope_c = jnp.concatenate([cm, cm, zq, zq], axis=1)
    rope_s1 = jnp.concatenate([-sm, zq, zq, zq], axis=1)
    rope_s2 = jnp.concatenate([zq, sm, zq, zq], axis=1)
    ang = _rope_angles(positions, RET_HEAD)
    cos_ret, sin_ret = jnp.cos(ang), jnp.sin(ang)

    w_in16 = _w_in_prep(w_in.astype(BF16))
    ffn1 = tuple(w.astype(BF16) for w in (ffn1_w_gate, ffn1_w_up, ffn1_w_down))
    ffn2 = tuple(w.astype(BF16) for w in (ffn2_w_gate, ffn2_w_up, ffn2_w_down))
    w_merge = tuple(w.astype(BF16) for w in (w_br_ssm, w_br_mla, w_br_ret, w_out))
    row = lambda v: v.reshape(1, -1)
    for l in range(depth):
        x = _ffn(x, row(ffn1_norm[l]), *ffn1, l)

        proj, dt_raw = _inproj(x, row(mix_norm[l]), w_in16, l)

        y_ssm = _ssd(
            proj, dt_raw,
            conv_w[l][:, :SSM_D_INNER], row(conv_b[l][:SSM_D_INNER]),
            conv_w[l][:, SSM_D_INNER:], row(conv_b[l][SSM_D_INNER:]),
            _pad_lanes(dt_bias[l], LANES), _pad_lanes(a_log[l], LANES),
            row(jnp.repeat(d_skip[l], SSM_HEAD_DIM)), row(ssm_norm[l]),
            batch=batch, seq=seq)

        wq = w_q_b[l].reshape(MLA_Q_LORA, MLA_HEADS, MLA_QK_DIM)
        wq = jnp.pad(wq, ((0, 0), (0, 0), (0, MLA_QK_PAD - MLA_QK_DIM)))
        wq = wq.transpose(1, 0, 2).astype(BF16)
        wkv = w_kv_b[l].reshape(MLA_KV_LORA, MLA_HEADS, MLA_NOPE + MLA_V)
        wkv = wkv.transpose(1, 0, 2).astype(BF16)
        q, k, v = _mla_prep(
            proj, rope_c, rope_s1, rope_s2, row(q_a_norm[l]), row(kv_a_norm[l]),
            _pad_lanes(q_norm[l], MLA_QK_PAD), _pad_lanes(k_norm[l], MLA_QK_PAD), wq, wkv)
        y_mla = _attention(q, k, v, batch=batch, seq=seq)

        y_ret = _retention(proj, cos_ret, sin_ret, row(ret_norm[l]), batch=batch, seq=seq)

        x = _merge(y_ssm, y_mla, y_ret, proj, row(gate_b[l]), x, *w_merge, l)

        x = _ffn(x, row(ffn2_norm[l]), *ffn2, l)
    return x.reshape(batch, seq, d)
```

```python
import functools
import math

import jax
import jax.numpy as jnp
import numpy as np
from jax import lax
from jax.experimental import pallas as pl
from jax.experimental.pallas import tpu as pltpu

F32 = jnp.float32
BF16 = jnp.bfloat16

NORM_EPS = 1e-6
ROPE_THETA = 10000.0

SSM_HEADS = 16
SSM_HEAD_DIM = 64
SSM_D_INNER = SSM_HEADS * SSM_HEAD_DIM
SSM_GROUPS = 2
SSM_STATE = 128
SSM_CONV = 4
SSD_CHUNK = 256
MLA_HEADS = 8
MLA_Q_LORA = 768
MLA_KV_LORA = 512
MLA_NOPE = 128
MLA_ROPE = 64
MLA_QK_DIM = MLA_NOPE + MLA_ROPE
MLA_V = 128
RET_HEADS = 4
RET_HEAD = 256
RET_CHUNK = 256
N_BRANCH = 3

LANES = 128
MLA_QK_PAD = MLA_NOPE + LANES
VMEM_LIMIT = 56 * 1024 * 1024


def _cparams(sem):
    return pltpu.CompilerParams(dimension_semantics=sem, vmem_limit_bytes=VMEM_LIMIT)


def _resident(shape):
    nd = len(shape)
    return pl.BlockSpec(shape, lambda *_: (0,) * nd, pipeline_mode=pl.Buffered(1))


def _resident_layer(shape, layer):
    nd = len(shape) - 1
    return pl.BlockSpec((None,) + tuple(shape[1:]), lambda *_: (layer,) + (0,) * nd,
                        pipeline_mode=pl.Buffered(1))


def _rms(x, n):
    return lax.rsqrt(jnp.sum(x * x, axis=-1, keepdims=True) * (1.0 / n) + NORM_EPS)


def _sigmoid(x):
    return 0.5 * jnp.tanh(0.5 * x) + 0.5


def _silu(x):
    h = 0.5 * x
    return h * jnp.tanh(h) + h


def _dot(a, b):
    return jnp.dot(a, b, preferred_element_type=F32)


def _dot_nt(a, b):
    return lax.dot_general(a, b, (((1,), (1,)), ((), ())), preferred_element_type=F32)


def _ffn_kernel(x_ref, nw_ref, wg_ref, wu_ref, wd_ref, o_ref, xn_ref):
    d = x_ref.shape[-1]

    @pl.when(pl.program_id(1) == 0)
    def _():
        x = x_ref[...]
        xn_ref[...] = (x * _rms(x, d) * nw_ref[...]).astype(BF16)
        o_ref[...] = x

    xn = xn_ref[...]
    g = _dot(xn, wg_ref[...])
    u = _dot(xn, wu_ref[...])
    o_ref[...] += _dot((_silu(g) * (0.5 * u)).astype(BF16), wd_ref[...])


def _ffn(x, nw, wg, wu, wd, layer, *, tm=1024, tf=512):
    m, d = x.shape
    f = wg.shape[2]
    tm = min(tm, m)
    return pl.pallas_call(
        _ffn_kernel,
        out_shape=jax.ShapeDtypeStruct((m, d), F32),
        grid=(m // tm, f // tf),
        in_specs=[
            pl.BlockSpec((tm, d), lambda i, j: (i, 0)),
            pl.BlockSpec((1, d), lambda i, j: (0, 0)),
            pl.BlockSpec((None, d, tf), lambda i, j: (layer, 0, j)),
            pl.BlockSpec((None, d, tf), lambda i, j: (layer, 0, j)),
            pl.BlockSpec((None, tf, d), lambda i, j: (layer, j, 0)),
        ],
        out_specs=pl.BlockSpec((tm, d), lambda i, j: (i, 0)),
        scratch_shapes=[pltpu.VMEM((tm, d), BF16)],
        compiler_params=_cparams(("parallel", "arbitrary")),
        name="ffn",
    )(x, nw, wg, wu, wd)


P_GATES = 0
P_RQ = 6144
P_RK = 7168
P_RV = 8192
P_RG = 9216
P_Z = 10240
P_XS = 11264
P_BC = 12288
P_KPE = 12800
P_QLAT = 13056
P_CKV = 13824
P_TOTAL = 14336


def _inproj_kernel(x_ref, nw_ref, w_ref, wdt_ref, o_ref, dt_ref, xn_ref):
    j = pl.program_id(1)
    d = x_ref.shape[-1]

    @pl.when(j == 0)
    def _():
        x = x_ref[...]
        xn = (x * _rms(x, d) * nw_ref[...]).astype(BF16)
        xn_ref[...] = xn
        dt_ref[...] = _dot(xn, wdt_ref[...])

    o_ref[...] = _dot(xn_ref[...], w_ref[...]).astype(BF16)


def _inproj(x, nw, w_all, layer, *, tm=1024, tn=2048):
    m, d = x.shape
    n = P_TOTAL
    tm = min(tm, m)
    return pl.pallas_call(
        _inproj_kernel,
        out_shape=(jax.ShapeDtypeStruct((m, n), BF16), jax.ShapeDtypeStruct((m, LANES), F32)),
        grid=(m // tm, n // tn),
        in_specs=[
            pl.BlockSpec((tm, d), lambda i, j: (i, 0)),
            pl.BlockSpec((1, d), lambda i, j: (0, 0)),
            pl.BlockSpec((None, d, tn), lambda i, j: (layer, 0, j)),
            pl.BlockSpec((None, d, LANES), lambda i, j: (layer, 0, P_TOTAL // LANES)),
        ],
        out_specs=(
            pl.BlockSpec((tm, tn), lambda i, j: (i, j)),
            pl.BlockSpec((tm, LANES), lambda i, j: (i, 0)),
        ),
        scratch_shapes=[pltpu.VMEM((tm, d), BF16)],
        compiler_params=_cparams(("parallel", "arbitrary")),
        name="inproj",
    )(x, nw, w_all, w_all)


_W_IN_SEGMENTS = (
    (P_GATES, 8016, 6144), (P_RQ, 3920, 1024), (P_RK, 4944, 1024), (P_RV, 5968, 1024),
    (P_RG, 6992, 1024), (P_Z, 0, 1024), (P_XS, 1024, 1536), (P_KPE, 3856, MLA_ROPE),
    (P_QLAT, 2576, MLA_Q_LORA), (P_CKV, 3344, MLA_KV_LORA), (P_TOTAL, 2560, SSM_HEADS))
W_IN_OUT = P_TOTAL + LANES


def _w_in_prep_kernel(x_ref, o_ref):
    n_src = x_ref.shape[1]
    segs = sorted(_W_IN_SEGMENTS)
    for k, (dest, _, width) in enumerate(segs):
        full = dest + width // LANES * LANES
        nxt = segs[k + 1][0] if k + 1 < len(segs) else o_ref.shape[1]
        if nxt > full:
            o_ref[:, full:nxt] = jnp.zeros((o_ref.shape[0], nxt - full), BF16)
    for dest, source, width in segs:
        lo = source // LANES * LANES
        hi = min(-(-(source + width) // LANES) * LANES, n_src)
        o_ref[:, dest:dest + width] = (
            x_ref[:, lo:hi][:, source - lo:source - lo + width].astype(BF16))


def _w_in_prep(w_in, *, rows=128):
    depth, d, n_src = w_in.shape
    return pl.pallas_call(
        _w_in_prep_kernel,
        out_shape=jax.ShapeDtypeStruct((depth, d, W_IN_OUT), BF16),
        grid=(depth, d // rows),
        in_specs=[pl.BlockSpec((None, rows, n_src), lambda l, i: (l, i, 0))],
        out_specs=pl.BlockSpec((None, rows, W_IN_OUT), lambda l, i: (l, i, 0)),
        compiler_params=_cparams(("parallel", "parallel")),
        name="w_in_prep",
    )(w_in)


def _split3(x):
    hi = x.astype(BF16)
    r1 = x - hi.astype(F32)
    mid = r1.astype(BF16)
    lo = (r1 - mid.astype(F32)).astype(BF16)
    return hi, mid, lo


def _ssd_constants(L):
    t = np.arange(L)
    shifts = np.stack([(t[:, None] - t[None, :] == j) for j in range(1, SSM_CONV)])
    tril = t[:, None] >= t[None, :]
    expand = np.zeros((LANES, SSM_D_INNER), bool)
    for h in range(SSM_HEADS):
        expand[h, h * SSM_HEAD_DIM:(h + 1) * SSM_HEAD_DIM] = True
    expand2 = np.concatenate([expand, expand])
    expand4 = np.concatenate([expand, expand, expand, np.zeros_like(expand)])
    return tuple(jnp.asarray(m, BF16) for m in (shifts, tril, expand2, expand4))


def _ssd_kernel(xs_ref, bc_ref, z_ref, dt_ref, cwx_ref, cbx_ref, cwb_ref, cbb_ref,
                dtb_ref, alog_ref, dskip_ref, nw_ref, shift_ref, tril_ref, e2_ref, e4_ref,
                o_ref, xtail_ref, btail_ref, state_ref):
    L = xs_ref.shape[0]
    P = SSM_HEAD_DIM
    N = SSM_STATE
    HG = SSM_HEADS // SSM_GROUPS
    GW = HG * P
    T = 8

    @pl.when(pl.program_id(1) == 0)
    def _():
        xtail_ref[0:T, :] = jnp.zeros((T, xtail_ref.shape[1]), F32)
        btail_ref[0:T, :] = jnp.zeros((T, btail_ref.shape[1]), F32)
        state_ref[...] = jnp.zeros(state_ref.shape, F32)

    def conv_silu(tail_ref, in_ref, w_ref, b_ref):
        x16 = in_ref[...]
        xf = x16.astype(F32)
        acc = b_ref[...] + w_ref[SSM_CONV - 1:SSM_CONV, :] * xf
        for j in range(1, SSM_CONV):
            acc = acc + w_ref[SSM_CONV - 1 - j:SSM_CONV - j, :] * _dot(shift_ref[j - 1], x16)
        tail_ref[T:2 * T, :] = xf[0:T, :]
        head = b_ref[...] + w_ref[0:1, :] * tail_ref[T - 3:2 * T - 3, :]
        for j in range(1, SSM_CONV):
            head = head + w_ref[j:j + 1, :] * tail_ref[T - 3 + j:2 * T - 3 + j, :]
        tail_ref[0:T, :] = xf[L - T:L, :]
        return _silu(jnp.concatenate([head, acc[T:, :]], axis=0))

    xc = conv_silu(xtail_ref, xs_ref, cwx_ref, cbx_ref)
    bcc = conv_silu(btail_ref, bc_ref, cwb_ref, cbb_ref)

    dtr = dt_ref[...] + dtb_ref[...]
    dt = jnp.maximum(dtr, 0.0) + jnp.log1p(jnp.exp(-jnp.abs(dtr)))
    adt = dt * (-jnp.exp(alog_ref[...]))
    tril = tril_ref[...]
    hi, mid, lo = _split3(adt)
    acs = _dot(tril, hi) + _dot(tril, mid) + _dot(tril, lo)
    acs_t = acs.T

    hi, mid, _ = _split3(dt)
    dtx = _dot(jnp.concatenate([hi, mid], axis=1), e2_ref[...])
    hi, mid, lo = _split3(acs)
    ax = _dot(jnp.concatenate([hi, mid, lo, jnp.zeros_like(lo)], axis=1), e4_ref[...])
    a_last = ax[L - 1:L, :]
    xd = xc * dtx
    xd16 = xd.astype(BF16)
    xdd16 = (xd * jnp.exp(a_last - ax)).astype(BF16)
    e_ax = jnp.exp(ax)
    e_last = jnp.exp(a_last)

    causal = (lax.broadcasted_iota(jnp.int32, (L, L), 0)
              >= lax.broadcasted_iota(jnp.int32, (L, L), 1))
    first_half = lax.broadcasted_iota(jnp.int32, (L, LANES), 1) < P
    tiles = []
    for g in range(SSM_GROUPS):
        bm = bcc[:, g * N:(g + 1) * N]
        cm16 = bcc[:, (SSM_GROUPS + g) * N:(SSM_GROUPS + g + 1) * N].astype(BF16)
        cb = _dot_nt(cm16, bm.astype(BF16))
        gs = slice(g * GW, (g + 1) * GW)
        prev = state_ref[g]
        y_off = _dot(cm16, prev.astype(BF16)) * e_ax[:, gs]
        state_ref[g] = prev * e_last[:, gs] + _dot(bm.T.astype(BF16), xdd16[:, gs])
        for t in range(GW // LANES):
            ts = slice(g * GW + t * LANES, g * GW + (t + 1) * LANES)
            pair = []
            for k in range(LANES // P):
                h = (g * GW + t * LANES) // P + k
                seg = acs[:, h:h + 1] - acs_t[h:h + 1, :]
                decay = jnp.exp(jnp.where(causal, seg, -jnp.inf))
                pair.append(_dot((cb * decay).astype(BF16), xd16[:, ts]))
            tiles.append(jnp.where(first_half, pair[0], pair[1])
                         + y_off[:, t * LANES:(t + 1) * LANES])

    y = jnp.concatenate(tiles, axis=1)
    y = (y + dskip_ref[...] * xc) * _silu(z_ref[...].astype(F32))
    for g in range(SSM_GROUPS):
        gs = slice(g * GW, (g + 1) * GW)
        yg = y[:, gs]
        o_ref[:, gs] = (yg * _rms(yg, GW) * nw_ref[:, gs]).astype(BF16)


def _ssd(proj, dt_raw, cwx, cbx, cwb, cbb, dtb, alog, dskip, nw, *, batch, seq):
    L = math.gcd(seq, SSD_CHUNK)
    nc = seq // L
    di = SSM_D_INNER
    bcw = 2 * SSM_GROUPS * SSM_STATE
    tok = lambda b, c: b * nc + c
    consts = _ssd_constants(L)
    params = (cwx, cbx, cwb, cbb, dtb, alog, dskip, nw) + consts
    return pl.pallas_call(
        _ssd_kernel,
        out_shape=jax.ShapeDtypeStruct((batch * seq, di), BF16),
        grid=(batch, nc),
        in_specs=[
            pl.BlockSpec((L, di), lambda b, c: (tok(b, c), P_XS // di)),
            pl.BlockSpec((L, bcw), lambda b, c: (tok(b, c), P_BC // bcw)),
            pl.BlockSpec((L, di), lambda b, c: (tok(b, c), P_Z // di)),
            pl.BlockSpec((L, LANES), lambda b, c: (tok(b, c), 0)),
        ] + [_resident(p.shape) for p in params],
        out_specs=pl.BlockSpec((L, di), lambda b, c: (tok(b, c), 0)),
        scratch_shapes=[
            pltpu.VMEM((16, di), F32),
            pltpu.VMEM((16, bcw), F32),
            pltpu.VMEM((SSM_GROUPS, SSM_STATE, di // SSM_GROUPS), F32),
        ],
        compiler_params=_cparams(("parallel", "arbitrary")),
        name="ssd",
    )(proj, proj, proj, dt_raw, *params)


def _rope_tile(x, c, s1, s2):
    q = MLA_ROPE // 2
    return x * c + pltpu.roll(x, LANES - q, 1) * s1 + pltpu.roll(x, q, 1) * s2


def _mla_prep_kernel(ql_ref, ckv_ref, kpe_ref, c_ref, s1_ref, s2_ref, qan_ref, kvan_ref,
                     qn_ref, kn_ref, wq_ref, wkv_ref, q_out, k_out, v_out, *, q_scale):
    ql = ql_ref[...].astype(F32)
    qa = (ql * _rms(ql, MLA_Q_LORA) * qan_ref[...]).astype(BF16)
    ckv = ckv_ref[...].astype(F32)
    kva = (ckv * _rms(ckv, MLA_KV_LORA) * kvan_ref[...]).astype(BF16)
    kpe = kpe_ref[...].astype(F32)
    kpe_sq = kpe * kpe
    c, s1, s2 = c_ref[...], s1_ref[...], s2_ref[...]
    qn, kn = qn_ref[...] * q_scale, kn_ref[...]
    kpe_rot = _rope_tile(kpe * kn[:, MLA_NOPE:], c, s1, s2)
    ones = jnp.ones((kpe.shape[0], MLA_V), BF16)
    tile_sum = jnp.where(
        (lax.broadcasted_iota(jnp.int32, (2 * LANES, 2 * LANES), 0) < LANES)
        == (lax.broadcasted_iota(jnp.int32, (2 * LANES, 2 * LANES), 1) < LANES),
        1.0, 0.0).astype(BF16)

    def inv_rms_pair(sq_a, sq_b):
        ss = _dot(jnp.concatenate([sq_a, sq_b], axis=1).astype(BF16), tile_sum)
        r = lax.rsqrt(ss * (1.0 / MLA_QK_DIM) + NORM_EPS)
        return r[:, :LANES], r[:, LANES:]

    def project(h):
        return [(_dot(qa, wq_ref[h + i]), _dot(kva, wkv_ref[h + i])) for i in range(2)]

    nxt = project(0)
    for h0 in range(0, MLA_HEADS, 2):
        cur = nxt
        if h0 + 2 < MLA_HEADS:
            nxt = project(h0 + 2)
        q_sq = [qh * qh for qh, _ in cur]
        rq = inv_rms_pair(*(sq[:, :LANES] + sq[:, LANES:] for sq in q_sq))
        rk = inv_rms_pair(*(kvh[:, :MLA_NOPE] * kvh[:, :MLA_NOPE] + kpe_sq for _, kvh in cur))
        for i, (qh, kvh) in enumerate(cur):
            h = h0 + i
            qh = qh * jnp.concatenate([rq[i], rq[i]], axis=1) * qn
            q_out[h, :, 0:MLA_NOPE] = qh[:, 0:MLA_NOPE].astype(BF16)
            q_out[h, :, MLA_NOPE:] = _rope_tile(qh[:, MLA_NOPE:], c, s1, s2).astype(BF16)
            k_out[h, :, 0:MLA_NOPE] = (kvh[:, 0:MLA_NOPE] * rk[i] * kn[:, 0:MLA_NOPE]).astype(BF16)
            k_out[h, :, MLA_NOPE:] = (kpe_rot * rk[i]).astype(BF16)
            v_out[h, :, 0:MLA_V] = kvh[:, MLA_NOPE:].astype(BF16)
            v_out[h, :, MLA_V:] = ones


def _mla_prep(proj, c, s1, s2, qan, kvan, qn, kn, wq, wkv, *, tm=512):
    m = proj.shape[0]
    tm = min(tm, m)
    row = lambda i: (i, 0)
    return pl.pallas_call(
        functools.partial(_mla_prep_kernel, q_scale=MLA_QK_DIM ** -0.5 * math.log2(math.e)),
        out_shape=(
            jax.ShapeDtypeStruct((MLA_HEADS, m, MLA_QK_PAD), BF16),
            jax.ShapeDtypeStruct((MLA_HEADS, m, MLA_QK_PAD), BF16),
            jax.ShapeDtypeStruct((MLA_HEADS, m, 2 * MLA_V), BF16),
        ),
        grid=(m // tm,),
        in_specs=[
            pl.BlockSpec((tm, MLA_Q_LORA), lambda i: (i, P_QLAT // MLA_Q_LORA)),
            pl.BlockSpec((tm, MLA_KV_LORA), lambda i: (i, P_CKV // MLA_KV_LORA)),
            pl.BlockSpec((tm, LANES), lambda i: (i, P_KPE // LANES)),
            pl.BlockSpec((tm, LANES), row), pl.BlockSpec((tm, LANES), row),
            pl.BlockSpec((tm, LANES), row),
            _resident(qan.shape), _resident(kvan.shape), _resident(qn.shape),
            _resident(kn.shape), _resident(wq.shape), _resident(wkv.shape),
        ],
        out_specs=(
            pl.BlockSpec((MLA_HEADS, tm, MLA_QK_PAD), lambda i: (0, i, 0)),
            pl.BlockSpec((MLA_HEADS, tm, MLA_QK_PAD), lambda i: (0, i, 0)),
            pl.BlockSpec((MLA_HEADS, tm, 2 * MLA_V), lambda i: (0, i, 0)),
        ),
        compiler_params=_cparams(("parallel",)),
        name="mla_prep",
    )(proj, proj, proj, c, s1, s2, qan, kvan, qn, kn, wq, wkv)


def _attn_kernel(q_ref, k_ref, v_ref, o_ref, s_ref, *, blk, heads):
    i = pl.program_id(2)

    def scores(j, slot):
        start = pl.multiple_of(j * blk, blk)
        for h in range(heads):
            s_ref[slot, h] = _dot_nt(q_ref[h], k_ref[h, pl.ds(start, blk), :])

    def consume(j, slot, carry, masked):
        start = pl.multiple_of(j * blk, blk)
        probs = []
        for h, (m, acc) in enumerate(carry):
            s = s_ref[slot, h]
            if masked:
                row = lax.broadcasted_iota(jnp.int32, (blk, blk), 0)
                col = lax.broadcasted_iota(jnp.int32, (blk, blk), 1)
                s = jnp.where(row >= col, s, -jnp.inf)
            m_new = jnp.maximum(m, jnp.max(s, axis=-1, keepdims=True))
            probs.append((m_new, jnp.exp2(m - m_new), jnp.exp2(s - m_new).astype(BF16)))
        return tuple((m_new, alpha * acc + _dot(p, v_ref[h, pl.ds(start, blk), :]))
                     for h, ((_, acc), (m_new, alpha, p)) in enumerate(zip(carry, probs)))

    def finish(carry):
        for h, (_, acc) in enumerate(carry):
            o_ref[:, h * MLA_V:(h + 1) * MLA_V] = (acc[:, :MLA_V] / acc[:, MLA_V:]).astype(BF16)

    def pair(t, carry):
        j = 2 * t
        scores(j + 1, 1)
        carry = consume(j, 0, carry, False)
        scores(j + 2, 0)
        return consume(j + 1, 1, carry, False)

    scores(0, 0)
    init = tuple((jnp.full((blk, 1), -jnp.inf, F32), jnp.zeros((blk, 2 * MLA_V), F32))
                 for _ in range(heads))
    carry = lax.fori_loop(0, i // 2, pair, init)
    last_even = 2 * (i // 2)

    @pl.when(i % 2 == 0)
    def _():
        finish(consume(last_even, 0, carry, True))

    @pl.when(i % 2 == 1)
    def _():
        scores(last_even + 1, 1)
        finish(consume(last_even + 1, 1, consume(last_even, 0, carry, False), True))


def _attention(q, k, v, *, batch, seq, blk=512, heads=4):
    blk = min(blk, seq)
    nq = seq // blk
    return pl.pallas_call(
        functools.partial(_attn_kernel, blk=blk, heads=heads),
        out_shape=jax.ShapeDtypeStruct((batch * seq, MLA_HEADS * MLA_V), BF16),
        grid=(batch, MLA_HEADS // heads, nq),
        in_specs=[
            pl.BlockSpec((heads, blk, MLA_QK_PAD), lambda b, h, i: (h, b * nq + i, 0)),
            pl.BlockSpec((heads, seq, MLA_QK_PAD), lambda b, h, i: (h, b, 0)),
            pl.BlockSpec((heads, seq, 2 * MLA_V), lambda b, h, i: (h, b, 0)),
        ],
        out_specs=pl.BlockSpec((blk, heads * MLA_V), lambda b, h, i: (b * nq + i, h)),
        scratch_shapes=[pltpu.VMEM((2, heads, blk, blk), F32)],
        compiler_params=_cparams(("parallel", "parallel", "arbitrary")),
        name="mla_attention",
    )(q, k, v)


def _ret_constants(L):
    expo = 5.0 + 7.0 * np.arange(RET_HEADS, dtype=np.float32) / np.float32(RET_HEADS - 1)
    log_gamma = np.log1p(-np.exp2(-expo)).astype(np.float32)[:, None, None]
    pos = np.arange(L, dtype=np.float32)
    rel = pos[:, None] - pos[None, :]
    dmask = np.where(rel >= 0, np.exp(rel * log_gamma), 0.0).astype(np.float32)
    lanes = np.ones((1, 1, RET_HEAD), np.float32)
    q_dec = np.exp((pos + 1.0)[None, :, None] * log_gamma).astype(np.float32) * lanes
    k_dec = np.exp((L - 1.0 - pos)[None, :, None] * log_gamma).astype(np.float32) * lanes
    chunk_dec = [float(v) for v in np.exp(L * log_gamma[:, 0, 0])]
    return jnp.asarray(dmask), jnp.asarray(q_dec), jnp.asarray(k_dec), chunk_dec


def _ret_kernel(rq_ref, rk_ref, rv_ref, rg_ref, cos_ref, sin_ref, nw_ref, dmask_ref, qdec_ref,
                kdec_ref, o_ref, state_ref, *, chunk_dec):
    D = RET_HEAD
    half = D // 2

    @pl.when(pl.program_id(1) == 0)
    def _():
        state_ref[...] = jnp.zeros(state_ref.shape, F32)

    cos, sin = cos_ref[...], sin_ref[...]

    def rope(ref, hs):
        x1 = ref[:, hs.start:hs.start + half].astype(F32)
        x2 = ref[:, hs.start + half:hs.stop].astype(F32)
        return jnp.concatenate([x1 * cos - x2 * sin, x1 * sin + x2 * cos], axis=-1)

    for h in range(RET_HEADS):
        hs = slice(h * D, (h + 1) * D)
        q = rope(rq_ref, hs)
        k = rope(rk_ref, hs) * (RET_HEAD ** -0.5)
        v16 = rv_ref[:, hs]
        q16 = q.astype(BF16)
        scores = _dot_nt(q16, k.astype(BF16)) * dmask_ref[h]
        y = _dot(scores.astype(BF16), v16)
        prev = state_ref[h]
        y = y + _dot(q16, prev.astype(BF16)) * qdec_ref[h]
        kd = k * kdec_ref[h]
        state_ref[h] = prev * chunk_dec[h] + _dot(kd.T.astype(BF16), v16)
        y = y * _rms(y, D) * nw_ref[:, hs]
        o_ref[:, hs] = (_silu(rg_ref[:, hs].astype(F32)) * y).astype(BF16)


def _retention(proj, cos, sin, nw, *, batch, seq):
    L = math.gcd(seq, RET_CHUNK)
    nc = seq // L
    w = RET_HEADS * RET_HEAD
    tok = lambda b, c: b * nc + c
    dmask, q_dec, k_dec, chunk_dec = _ret_constants(L)
    return pl.pallas_call(
        functools.partial(_ret_kernel, chunk_dec=chunk_dec),
        out_shape=jax.ShapeDtypeStruct((batch * seq, w), BF16),
        grid=(batch, nc),
        in_specs=[
            pl.BlockSpec((L, w), lambda b, c: (tok(b, c), P_RQ // w)),
            pl.BlockSpec((L, w), lambda b, c: (tok(b, c), P_RK // w)),
            pl.BlockSpec((L, w), lambda b, c: (tok(b, c), P_RV // w)),
            pl.BlockSpec((L, w), lambda b, c: (tok(b, c), P_RG // w)),
            pl.BlockSpec((L, RET_HEAD // 2), lambda b, c: (tok(b, c), 0)),
            pl.BlockSpec((L, RET_HEAD // 2), lambda b, c: (tok(b, c), 0)),
            _resident(nw.shape), _resident(dmask.shape), _resident(q_dec.shape),
            _resident(k_dec.shape),
        ],
        out_specs=pl.BlockSpec((L, w), lambda b, c: (tok(b, c), 0)),
        scratch_shapes=[pltpu.VMEM((RET_HEADS, RET_HEAD, RET_HEAD), F32)],
        compiler_params=_cparams(("parallel", "arbitrary")),
        name="retention",
    )(proj, proj, proj, proj, cos, sin, nw, dmask, q_dec, k_dec)


def _merge_kernel(ys_ref, ym_ref, yr_ref, g0_ref, g1_ref, g2_ref, gb_ref, x_ref,
                  w0_ref, w1_ref, w2_ref, wo_ref, o_ref):
    d = x_ref.shape[-1]

    def branch(k, y_ref, g_ref, w_ref):
        gate = _sigmoid(g_ref[...].astype(F32) + gb_ref[:, k * d:(k + 1) * d])
        return gate * _dot(y_ref[...], w_ref[...])

    merged = (branch(0, ys_ref, g0_ref, w0_ref) + branch(1, ym_ref, g1_ref, w1_ref)
              + branch(2, yr_ref, g2_ref, w2_ref))
    o_ref[...] = x_ref[...] + _dot(merged.astype(BF16), wo_ref[...])


def _merge(ys, ym, yr, proj, gb, x, w0, w1, w2, wo, layer, *, tm=256):
    m, d = x.shape
    tm = min(tm, m)
    bw = ys.shape[1]
    row = lambda i: (i, 0)
    return pl.pallas_call(
        _merge_kernel,
        out_shape=jax.ShapeDtypeStruct((m, d), F32),
        grid=(m // tm,),
        in_specs=[
            pl.BlockSpec((tm, bw), row), pl.BlockSpec((tm, bw), row), pl.BlockSpec((tm, bw), row),
            pl.BlockSpec((tm, d), lambda i: (i, P_GATES // d)),
            pl.BlockSpec((tm, d), lambda i: (i, P_GATES // d + 1)),
            pl.BlockSpec((tm, d), lambda i: (i, P_GATES // d + 2)),
            _resident(gb.shape),
            pl.BlockSpec((tm, d), row),
            _resident_layer(w0.shape, layer), _resident_layer(w1.shape, layer),
            _resident_layer(w2.shape, layer), _resident_layer(wo.shape, layer),
        ],
        out_specs=pl.BlockSpec((tm, d), row),
        compiler_params=_cparams(("parallel",)),
        name="merge",
    )(ys, ym, yr, proj, proj, proj, gb, x, w0, w1, w2, wo)


def _rope_angles(positions, dim):
    inv = 1.0 / (ROPE_THETA ** (jnp.arange(0, dim, 2, dtype=F32) / dim))
    return positions.astype(F32).reshape(-1, 1) * inv


def _pad_lanes(v, n):
    return jnp.concatenate([v, jnp.zeros((n - v.shape[0],), v.dtype)]).reshape(1, n)


def kernel(x, positions, ffn1_norm, ffn1_w_gate, ffn1_w_up, ffn1_w_down, mix_norm, w_in, gate_b, conv_w, conv_b, dt_bias, a_log, d_skip, ssm_norm, q_a_norm, w_q_b, kv_a_norm, w_kv_b, q_norm, k_norm, ret_norm, w_br_ssm, w_br_mla, w_br_ret, w_out, ffn2_norm, ffn2_w_gate, ffn2_w_up, ffn2_w_down):
    batch, seq, d = x.shape
    depth = w_in.shape[0]
    x = x.reshape(batch * seq, d)

    ang = _rope_angles(positions, MLA_ROPE)
    cm, sm = jnp.cos(ang), jnp.sin(ang)
    zq = jnp.zeros_like(cm)
    rope_c = jnp.concatenate([cm, cm, zq, zq], axis=1)
    rope_s1 = jnp.concatenate([-sm, zq, zq, zq], axis=1)
    rope_s2 = jnp.concatenate([zq, sm, zq, zq], axis=1)
    ang = _rope_angles(positions, RET_HEAD)
    cos_ret, sin_ret = jnp.cos(ang), jnp.sin(ang)

    w_in16 = _w_in_prep(w_in.astype(BF16))
    ffn1 = tuple(w.astype(BF16) for w in (ffn1_w_gate, ffn1_w_up, ffn1_w_down))
    ffn2 = tuple(w.astype(BF16) for w in (ffn2_w_gate, ffn2_w_up, ffn2_w_down))
    w_merge = tuple(w.astype(BF16) for w in (w_br_ssm, w_br_mla, w_br_ret, w_out))
    row = lambda v: v.reshape(1, -1)
    for l in range(depth):
        x = _ffn(x, row(ffn1_norm[l]), *ffn1, l)

        proj, dt_raw = _inproj(x, row(mix_norm[l]), w_in16, l)

        y_ssm = _ssd(
            proj, dt_raw,
            conv_w[l][:, :SSM_D_INNER], row(conv_b[l][:SSM_D_INNER]),
            conv_w[l][:, SSM_D_INNER:], row(conv_b[l][SSM_D_INNER:]),
            _pad_lanes(dt_bias[l], LANES), _pad_lanes(a_log[l], LANES),
            row(jnp.repeat(d_skip[l], SSM_HEAD_DIM)), row(ssm_norm[l]),
            batch=batch, seq=seq)

        wq = w_q_b[l].reshape(MLA_Q_LORA, MLA_HEADS, MLA_QK_DIM)
        wq = jnp.pad(wq, ((0, 0), (0, 0), (0, MLA_QK_PAD - MLA_QK_DIM)))
        wq = wq.transpose(1, 0, 2).astype(BF16)
        wkv = w_kv_b[l].reshape(MLA_KV_LORA, MLA_HEADS, MLA_NOPE + MLA_V)
        wkv = wkv.transpose(1, 0, 2).astype(BF16)
        q, k, v = _mla_prep(
            proj, rope_c, rope_s1, rope_s2, row(q_a_norm[l]), row(kv_a_norm[l]),
            _pad_lanes(q_norm[l], MLA_QK_PAD), _pad_lanes(k_norm[l], MLA_QK_PAD), wq, wkv)
        y_mla = _attention(q, k, v, batch=batch, seq=seq)

        y_ret = _retention(proj, cos_ret, sin_ret, row(ret_norm[l]), batch=batch, seq=seq)

        x = _merge(y_ssm, y_mla, y_ret, proj, row(gate_b[l]), x, *w_merge, l)

        x = _ffn(x, row(ffn2_norm[l]), *ffn2, l)
    return x.reshape(batch, seq, d)
```

```python
import functools
import math

import jax
import jax.numpy as jnp
import numpy as np
from jax import lax
from jax.experimental import pallas as pl
from jax.experimental.pallas import tpu as pltpu

F32 = jnp.float32
BF16 = jnp.bfloat16

NORM_EPS = 1e-6
ROPE_THETA = 10000.0

SSM_HEADS = 16
SSM_HEAD_DIM = 64
SSM_D_INNER = SSM_HEADS * SSM_HEAD_DIM
SSM_GROUPS = 2
SSM_STATE = 128
SSM_CONV = 4
SSD_CHUNK = 256
MLA_HEADS = 8
MLA_Q_LORA = 768
MLA_KV_LORA = 512
MLA_NOPE = 128
MLA_ROPE = 64
MLA_QK_DIM = MLA_NOPE + MLA_ROPE
MLA_V = 128
RET_HEADS = 4
RET_HEAD = 256
RET_CHUNK = 256
N_BRANCH = 3

LANES = 128
MLA_QK_PAD = MLA_NOPE + LANES
VMEM_LIMIT = 56 * 1024 * 1024


def _cparams(sem):
    return pltpu.CompilerParams(dimension_semantics=sem, vmem_limit_bytes=VMEM_LIMIT)


def _resident(shape):
    nd = len(shape)
    return pl.BlockSpec(shape, lambda *_: (0,) * nd, pipeline_mode=pl.Buffered(1))


def _resident_layer(shape, layer):
    nd = len(shape) - 1
    return pl.BlockSpec((None,) + tuple(shape[1:]), lambda *_: (layer,) + (0,) * nd,
                        pipeline_mode=pl.Buffered(1))


def _rms(x, n):
    return lax.rsqrt(jnp.sum(x * x, axis=-1, keepdims=True) * (1.0 / n) + NORM_EPS)


def _sigmoid(x):
    return 0.5 * jnp.tanh(0.5 * x) + 0.5


def _silu(x):
    h = 0.5 * x
    return h * jnp.tanh(h) + h


def _dot(a, b):
    return jnp.dot(a, b, preferred_element_type=F32)


def _dot_nt(a, b):
    return lax.dot_general(a, b, (((1,), (1,)), ((), ())), preferred_element_type=F32)


def _ffn_kernel(x_ref, nw_ref, wg_ref, wu_ref, wd_ref, o_ref, xn_ref):
    d = x_ref.shape[-1]
    j = pl.program_id(1)

    def half_ffn(xn):
        g = _dot(xn, wg_ref[...])
        u = _dot(xn, wu_ref[...])
        return _dot((_silu(g) * (0.5 * u)).astype(BF16), wd_ref[...])

    @pl.when(j == 0)
    def _():
        x = x_ref[...]
        xn = (x * _rms(x, d) * nw_ref[...]).astype(BF16)
        xn_ref[...] = xn
        o_ref[...] = x + half_ffn(xn)

    @pl.when(j > 0)
    def _():
        o_ref[...] += half_ffn(xn_ref[...])


def _ffn(x, nw, wg, wu, wd, layer, *, tm=1024, tf=512):
    m, d = x.shape
    f = wg.shape[2]
    tm = min(tm, m)
    return pl.pallas_call(
        _ffn_kernel,
        out_shape=jax.ShapeDtypeStruct((m, d), F32),
        grid=(m // tm, f // tf),
        in_specs=[
            pl.BlockSpec((tm, d), lambda i, j: (i, 0)),
            pl.BlockSpec((1, d), lambda i, j: (0, 0)),
            pl.BlockSpec((None, d, tf), lambda i, j: (layer, 0, j)),
            pl.BlockSpec((None, d, tf), lambda i, j: (layer, 0, j)),
            pl.BlockSpec((None, tf, d), lambda i, j: (layer, j, 0)),
        ],
        out_specs=pl.BlockSpec((tm, d), lambda i, j: (i, 0)),
        scratch_shapes=[pltpu.VMEM((tm, d), BF16)],
        compiler_params=_cparams(("parallel", "arbitrary")),
        name="ffn",
    )(x, nw, wg, wu, wd)


P_GATES = 0
P_RQ = 6144
P_RK = 7168
P_RV = 8192
P_RG = 9216
P_Z = 10240
P_XS = 11264
P_BC = 12288
P_KPE = 12800
P_QLAT = 13056
P_CKV = 13824
P_TOTAL = 14336


def _inproj_kernel(x_ref, nw_ref, w_ref, wdt_ref, o_ref, dt_ref, xn_ref):
    j = pl.program_id(1)
    d = x_ref.shape[-1]

    @pl.when(j == 0)
    def _():
        x = x_ref[...]
        xn = (x * _rms(x, d) * nw_ref[...]).astype(BF16)
        xn_ref[...] = xn
        dt_ref[...] = _dot(xn, wdt_ref[...])
        o_ref[...] = _dot(xn, w_ref[...]).astype(BF16)

    @pl.when(j > 0)
    def _():
        o_ref[...] = _dot(xn_ref[...], w_ref[...]).astype(BF16)


def _inproj(x, nw, w_all, layer, *, tm=1024, tn=2048):
    m, d = x.shape
    n = P_TOTAL
    tm = min(tm, m)
    return pl.pallas_call(
        _inproj_kernel,
        out_shape=(jax.ShapeDtypeStruct((m, n), BF16), jax.ShapeDtypeStruct((m, LANES), F32)),
        grid=(m // tm, n // tn),
        in_specs=[
            pl.BlockSpec((tm, d), lambda i, j: (i, 0)),
            pl.BlockSpec((1, d), lambda i, j: (0, 0)),
            pl.BlockSpec((None, d, tn), lambda i, j: (layer, 0, j)),
            pl.BlockSpec((None, d, LANES), lambda i, j: (layer, 0, P_TOTAL // LANES)),
        ],
        out_specs=(
            pl.BlockSpec((tm, tn), lambda i, j: (i, j)),
            pl.BlockSpec((tm, LANES), lambda i, j: (i, 0)),
        ),
        scratch_shapes=[pltpu.VMEM((tm, d), BF16)],
        compiler_params=_cparams(("parallel", "arbitrary")),
        name="inproj",
    )(x, nw, w_all, w_all)


_W_IN_SEGMENTS = (
    (P_GATES, 8016, 6144), (P_RQ, 3920, 1024), (P_RK, 4944, 1024), (P_RV, 5968, 1024),
    (P_RG, 6992, 1024), (P_Z, 0, 1024), (P_XS, 1024, 1536), (P_KPE, 3856, MLA_ROPE),
    (P_QLAT, 2576, MLA_Q_LORA), (P_CKV, 3344, MLA_KV_LORA), (P_TOTAL, 2560, SSM_HEADS))
W_IN_OUT = P_TOTAL + LANES


def _w_in_prep_kernel(x_ref, o_ref):
    n_src = x_ref.shape[1]
    segs = sorted(_W_IN_SEGMENTS)
    for k, (dest, _, width) in enumerate(segs):
        full = dest + width // LANES * LANES
        nxt = segs[k + 1][0] if k + 1 < len(segs) else o_ref.shape[1]
        if nxt > full:
            o_ref[:, full:nxt] = jnp.zeros((o_ref.shape[0], nxt - full), BF16)
    for dest, source, width in segs:
        lo = source // LANES * LANES
        hi = min(-(-(source + width) // LANES) * LANES, n_src)
        o_ref[:, dest:dest + width] = (
            x_ref[:, lo:hi][:, source - lo:source - lo + width].astype(BF16))


def _w_in_prep(w_in, *, rows=128):
    depth, d, n_src = w_in.shape
    return pl.pallas_call(
        _w_in_prep_kernel,
        out_shape=jax.ShapeDtypeStruct((depth, d, W_IN_OUT), BF16),
        grid=(depth, d // rows),
        in_specs=[pl.BlockSpec((None, rows, n_src), lambda l, i: (l, i, 0))],
        out_specs=pl.BlockSpec((None, rows, W_IN_OUT), lambda l, i: (l, i, 0)),
        compiler_params=_cparams(("parallel", "parallel")),
        name="w_in_prep",
    )(w_in)


def _split3(x):
    hi = x.astype(BF16)
    r1 = x - hi.astype(F32)
    mid = r1.astype(BF16)
    lo = (r1 - mid.astype(F32)).astype(BF16)
    return hi, mid, lo


def _ssd_constants(L):
    t = np.arange(L)
    shifts = np.stack([(t[:, None] - t[None, :] == j) for j in range(1, SSM_CONV)])
    tril = t[:, None] >= t[None, :]
    expand = np.zeros((LANES, SSM_D_INNER), bool)
    for h in range(SSM_HEADS):
        expand[h, h * SSM_HEAD_DIM:(h + 1) * SSM_HEAD_DIM] = True
    expand2 = np.concatenate([expand, expand])
    expand4 = np.concatenate([expand, expand, expand, np.zeros_like(expand)])
    return tuple(jnp.asarray(m, BF16) for m in (shifts, tril, expand2, expand4))


def _ssd_kernel(xs_ref, bc_ref, z_ref, dt_ref, cwx_ref, cbx_ref, cwb_ref, cbb_ref,
                dtb_ref, alog_ref, dskip_ref, nw_ref, shift_ref, tril_ref, e2_ref, e4_ref,
                o_ref, xtail_ref, btail_ref, state_ref):
    L = xs_ref.shape[0]
    P = SSM_HEAD_DIM
    N = SSM_STATE
    HG = SSM_HEADS // SSM_GROUPS
    GW = HG * P
    T = 8

    @pl.when(pl.program_id(1) == 0)
    def _():
        xtail_ref[0:T, :] = jnp.zeros((T, xtail_ref.shape[1]), F32)
        btail_ref[0:T, :] = jnp.zeros((T, btail_ref.shape[1]), F32)
        state_ref[...] = jnp.zeros(state_ref.shape, F32)

    def conv_silu(tail_ref, in_ref, w_ref, b_ref):
        x16 = in_ref[...]
        xf = x16.astype(F32)
        acc = b_ref[...] + w_ref[SSM_CONV - 1:SSM_CONV, :] * xf
        for j in range(1, SSM_CONV):
            acc = acc + w_ref[SSM_CONV - 1 - j:SSM_CONV - j, :] * _dot(shift_ref[j - 1], x16)
        tail_ref[T:2 * T, :] = xf[0:T, :]
        head = b_ref[...] + w_ref[0:1, :] * tail_ref[T - 3:2 * T - 3, :]
        for j in range(1, SSM_CONV):
            head = head + w_ref[j:j + 1, :] * tail_ref[T - 3 + j:2 * T - 3 + j, :]
        tail_ref[0:T, :] = xf[L - T:L, :]
        return _silu(jnp.concatenate([head, acc[T:, :]], axis=0))

    xc = conv_silu(xtail_ref, xs_ref, cwx_ref, cbx_ref)
    bcc = conv_silu(btail_ref, bc_ref, cwb_ref, cbb_ref)

    dtr = dt_ref[...] + dtb_ref[...]
    dt = jnp.maximum(dtr, 0.0) + jnp.log1p(jnp.exp(-jnp.abs(dtr)))
    adt = dt * (-jnp.exp(alog_ref[...]))
    tril = tril_ref[...]
    hi, mid, lo = _split3(adt)
    acs = _dot(tril, hi) + _dot(tril, mid) + _dot(tril, lo)
    acs_t = acs.T

    hi, mid, _ = _split3(dt)
    dtx = _dot(jnp.concatenate([hi, mid], axis=1), e2_ref[...])
    hi, mid, lo = _split3(acs)
    ax = _dot(jnp.concatenate([hi, mid, lo, jnp.zeros_like(lo)], axis=1), e4_ref[...])
    a_last = ax[L - 1:L, :]
    xd = xc * dtx
    xd16 = xd.astype(BF16)
    xdd16 = (xd * jnp.exp(a_last - ax)).astype(BF16)
    e_ax = jnp.exp(ax)
    e_last = jnp.exp(a_last)

    causal = (lax.broadcasted_iota(jnp.int32, (L, L), 0)
              >= lax.broadcasted_iota(jnp.int32, (L, L), 1))
    first_half = lax.broadcasted_iota(jnp.int32, (L, LANES), 1) < P
    tiles = []
    for g in range(SSM_GROUPS):
        bm = bcc[:, g * N:(g + 1) * N]
        cm16 = bcc[:, (SSM_GROUPS + g) * N:(SSM_GROUPS + g + 1) * N].astype(BF16)
        cb = _dot_nt(cm16, bm.astype(BF16))
        gs = slice(g * GW, (g + 1) * GW)
        prev = state_ref[g]
        y_off = _dot(cm16, prev.astype(BF16)) * e_ax[:, gs]
        state_ref[g] = prev * e_last[:, gs] + _dot(bm.T.astype(BF16), xdd16[:, gs])
        for t in range(GW // LANES):
            ts = slice(g * GW + t * LANES, g * GW + (t + 1) * LANES)
            pair = []
            for k in range(LANES // P):
                h = (g * GW + t * LANES) // P + k
                seg = acs[:, h:h + 1] - acs_t[h:h + 1, :]
                decay = jnp.exp(jnp.where(causal, seg, -jnp.inf))
                pair.append(_dot((cb * decay).astype(BF16), xd16[:, ts]))
            tiles.append(jnp.where(first_half, pair[0], pair[1])
                         + y_off[:, t * LANES:(t + 1) * LANES])

    y = jnp.concatenate(tiles, axis=1)
    y = (y + dskip_ref[...] * xc) * _silu(z_ref[...].astype(F32))
    for g in range(SSM_GROUPS):
        gs = slice(g * GW, (g + 1) * GW)
        yg = y[:, gs]
        o_ref[:, gs] = (yg * _rms(yg, GW) * nw_ref[:, gs]).astype(BF16)


def _ssd(proj, dt_raw, cwx, cbx, cwb, cbb, dtb, alog, dskip, nw, *, batch, seq):
    L = math.gcd(seq, SSD_CHUNK)
    nc = seq // L
    di = SSM_D_INNER
    bcw = 2 * SSM_GROUPS * SSM_STATE
    tok = lambda b, c: b * nc + c
    consts = _ssd_constants(L)
    params = (cwx, cbx, cwb, cbb, dtb, alog, dskip, nw) + consts
    return pl.pallas_call(
        _ssd_kernel,
        out_shape=jax.ShapeDtypeStruct((batch * seq, di), BF16),
        grid=(batch, nc),
        in_specs=[
            pl.BlockSpec((L, di), lambda b, c: (tok(b, c), P_XS // di)),
            pl.BlockSpec((L, bcw), lambda b, c: (tok(b, c), P_BC // bcw)),
            pl.BlockSpec((L, di), lambda b, c: (tok(b, c), P_Z // di)),
            pl.BlockSpec((L, LANES), lambda b, c: (tok(b, c), 0)),
        ] + [_resident(p.shape) for p in params],
        out_specs=pl.BlockSpec((L, di), lambda b, c: (tok(b, c), 0)),
        scratch_shapes=[
            pltpu.VMEM((16, di), F32),
            pltpu.VMEM((16, bcw), F32),
            pltpu.VMEM((SSM_GROUPS, SSM_STATE, di // SSM_GROUPS), F32),
        ],
        compiler_params=_cparams(("parallel", "arbitrary")),
        name="ssd",
    )(proj, proj, proj, dt_raw, *params)


def _rope_tile(x, c, s1, s2):
    q = MLA_ROPE // 2
    return x * c + pltpu.roll(x, LANES - q, 1) * s1 + pltpu.roll(x, q, 1) * s2


def _mla_prep_kernel(ql_ref, ckv_ref, kpe_ref, c_ref, s1_ref, s2_ref, qan_ref, kvan_ref,
                     qn_ref, kn_ref, wq_ref, wkv_ref, q_out, k_out, v_out, *, q_scale):
    ql = ql_ref[...].astype(F32)
    qa = (ql * _rms(ql, MLA_Q_LORA) * qan_ref[...]).astype(BF16)
    ckv = ckv_ref[...].astype(F32)
    kva = (ckv * _rms(ckv, MLA_KV_LORA) * kvan_ref[...]).astype(BF16)
    kpe = kpe_ref[...].astype(F32)
    kpe_sq = kpe * kpe
    c, s1, s2 = c_ref[...], s1_ref[...], s2_ref[...]
    qn, kn = qn_ref[...] * q_scale, kn_ref[...]
    kpe_rot = _rope_tile(kpe * kn[:, MLA_NOPE:], c, s1, s2)
    ones = jnp.ones((kpe.shape[0], MLA_V), BF16)
    tile_sum = jnp.where(
        (lax.broadcasted_iota(jnp.int32, (2 * LANES, 2 * LANES), 0) < LANES)
        == (lax.broadcasted_iota(jnp.int32, (2 * LANES, 2 * LANES), 1) < LANES),
        1.0, 0.0).astype(BF16)

    def inv_rms_pair(sq_a, sq_b):
        ss = _dot(jnp.concatenate([sq_a, sq_b], axis=1).astype(BF16), tile_sum)
        r = lax.rsqrt(ss * (1.0 / MLA_QK_DIM) + NORM_EPS)
        return r[:, :LANES], r[:, LANES:]

    def project(h):
        return [(_dot(qa, wq_ref[h + i]), _dot(kva, wkv_ref[h + i])) for i in range(2)]

    nxt = project(0)
    for h0 in range(0, MLA_HEADS, 2):
        cur = nxt
        if h0 + 2 < MLA_HEADS:
            nxt = project(h0 + 2)
        q_sq = [qh * qh for qh, _ in cur]
        rq = inv_rms_pair(*(sq[:, :LANES] + sq[:, LANES:] for sq in q_sq))
        rk = inv_rms_pair(*(kvh[:, :MLA_NOPE] * kvh[:, :MLA_NOPE] + kpe_sq for _, kvh in cur))
        for i, (qh, kvh) in enumerate(cur):
            h = h0 + i
            qh = qh * jnp.concatenate([rq[i], rq[i]], axis=1) * qn
            q_out[h, :, 0:MLA_NOPE] = qh[:, 0:MLA_NOPE].astype(BF16)
            q_out[h, :, MLA_NOPE:] = _rope_tile(qh[:, MLA_NOPE:], c, s1, s2).astype(BF16)
            k_out[h, :, 0:MLA_NOPE] = (kvh[:, 0:MLA_NOPE] * rk[i] * kn[:, 0:MLA_NOPE]).astype(BF16)
            k_out[h, :, MLA_NOPE:] = (kpe_rot * rk[i]).astype(BF16)
            v_out[h, :, 0:MLA_V] = kvh[:, MLA_NOPE:].astype(BF16)
            v_out[h, :, MLA_V:] = ones


def _mla_prep(proj, c, s1, s2, qan, kvan, qn, kn, wq, wkv, *, tm=512):
    m = proj.shape[0]
    tm = min(tm, m)
    row = lambda i: (i, 0)
    return pl.pallas_call(
        functools.partial(_mla_prep_kernel, q_scale=MLA_QK_DIM ** -0.5 * math.log2(math.e)),
        out_shape=(
            jax.ShapeDtypeStruct((MLA_HEADS, m, MLA_QK_PAD), BF16),
            jax.ShapeDtypeStruct((MLA_HEADS, m, MLA_QK_PAD), BF16),
            jax.ShapeDtypeStruct((MLA_HEADS, m, 2 * MLA_V), BF16),
        ),
        grid=(m // tm,),
        in_specs=[
            pl.BlockSpec((tm, MLA_Q_LORA), lambda i: (i, P_QLAT // MLA_Q_LORA)),
            pl.BlockSpec((tm, MLA_KV_LORA), lambda i: (i, P_CKV // MLA_KV_LORA)),
            pl.BlockSpec((tm, LANES), lambda i: (i, P_KPE // LANES)),
            pl.BlockSpec((tm, LANES), row), pl.BlockSpec((tm, LANES), row),
            pl.BlockSpec((tm, LANES), row),
            _resident(qan.shape), _resident(kvan.shape), _resident(qn.shape),
            _resident(kn.shape), _resident(wq.shape), _resident(wkv.shape),
        ],
        out_specs=(
            pl.BlockSpec((MLA_HEADS, tm, MLA_QK_PAD), lambda i: (0, i, 0)),
            pl.BlockSpec((MLA_HEADS, tm, MLA_QK_PAD), lambda i: (0, i, 0)),
            pl.BlockSpec((MLA_HEADS, tm, 2 * MLA_V), lambda i: (0, i, 0)),
        ),
        compiler_params=_cparams(("parallel",)),
        name="mla_prep",
    )(proj, proj, proj, c, s1, s2, qan, kvan, qn, kn, wq, wkv)


def _attn_kernel(q_ref, k_ref, v_ref, o_ref, s_ref, *, blk, heads):
    i = pl.program_id(2)

    def scores(j, slot):
        start = pl.multiple_of(j * blk, blk)
        for h in range(heads):
            s_ref[slot, h] = _dot_nt(q_ref[h], k_ref[h, pl.ds(start, blk), :])

    def consume(j, slot, carry, masked):
        start = pl.multiple_of(j * blk, blk)
        probs = []
        for h, (m, acc) in enumerate(carry):
            s = s_ref[slot, h]
            if masked:
                row = lax.broadcasted_iota(jnp.int32, (blk, blk), 0)
                col = lax.broadcasted_iota(jnp.int32, (blk, blk), 1)
                s = jnp.where(row >= col, s, -jnp.inf)
            m_new = jnp.maximum(m, jnp.max(s, axis=-1, keepdims=True))
            probs.append((m_new, jnp.exp2(m - m_new), jnp.exp2(s - m_new).astype(BF16)))
        return tuple((m_new, alpha * acc + _dot(p, v_ref[h, pl.ds(start, blk), :]))
                     for h, ((_, acc), (m_new, alpha, p)) in enumerate(zip(carry, probs)))

    def finish(carry):
        for h, (_, acc) in enumerate(carry):
            o_ref[:, h * MLA_V:(h + 1) * MLA_V] = (acc[:, :MLA_V] / acc[:, MLA_V:]).astype(BF16)

    def pair(t, carry):
        j = 2 * t
        scores(j + 1, 1)
        carry = consume(j, 0, carry, False)
        scores(j + 2, 0)
        return consume(j + 1, 1, carry, False)

    scores(0, 0)
    init = tuple((jnp.full((blk, 1), -jnp.inf, F32), jnp.zeros((blk, 2 * MLA_V), F32))
                 for _ in range(heads))
    carry = lax.fori_loop(0, i // 2, pair, init)
    last_even = 2 * (i // 2)

    @pl.when(i % 2 == 0)
    def _():
        finish(consume(last_even, 0, carry, True))

    @pl.when(i % 2 == 1)
    def _():
        scores(last_even + 1, 1)
        finish(consume(last_even + 1, 1, consume(last_even, 0, carry, False), True))


def _attention(q, k, v, *, batch, seq, blk=512, heads=4):
    blk = min(blk, seq)
    nq = seq // blk
    return pl.pallas_call(
        functools.partial(_attn_kernel, blk=blk, heads=heads),
        out_shape=jax.ShapeDtypeStruct((batch * seq, MLA_HEADS * MLA_V), BF16),
        grid=(batch, MLA_HEADS // heads, nq),
        in_specs=[
            pl.BlockSpec((heads, blk, MLA_QK_PAD), lambda b, h, i: (h, b * nq + i, 0)),
            pl.BlockSpec((heads, seq, MLA_QK_PAD), lambda b, h, i: (h, b, 0)),
            pl.BlockSpec((heads, seq, 2 * MLA_V), lambda b, h, i: (h, b, 0)),
        ],
        out_specs=pl.BlockSpec((blk, heads * MLA_V), lambda b, h, i: (b * nq + i, h)),
        scratch_shapes=[pltpu.VMEM((2, heads, blk, blk), F32)],
        compiler_params=_cparams(("parallel", "parallel", "arbitrary")),
        name="mla_attention",
    )(q, k, v)


def _ret_constants(L):
    expo = 5.0 + 7.0 * np.arange(RET_HEADS, dtype=np.float32) / np.float32(RET_HEADS - 1)
    log_gamma = np.log1p(-np.exp2(-expo)).astype(np.float32)[:, None, None]
    pos = np.arange(L, dtype=np.float32)
    rel = pos[:, None] - pos[None, :]
    dmask = np.where(rel >= 0, np.exp(rel * log_gamma), 0.0).astype(np.float32)
    lanes = np.ones((1, 1, RET_HEAD), np.float32)
    q_dec = np.exp((pos + 1.0)[None, :, None] * log_gamma).astype(np.float32) * lanes
    k_dec = np.exp((L - 1.0 - pos)[None, :, None] * log_gamma).astype(np.float32) * lanes
    chunk_dec = [float(v) for v in np.exp(L * log_gamma[:, 0, 0])]
    return jnp.asarray(dmask), jnp.asarray(q_dec), jnp.asarray(k_dec), chunk_dec


def _ret_kernel(rq_ref, rk_ref, rv_ref, rg_ref, cos_ref, sin_ref, nw_ref, dmask_ref, qdec_ref,
                kdec_ref, o_ref, state_ref, *, chunk_dec):
    D = RET_HEAD
    half = D // 2

    @pl.when(pl.program_id(1) == 0)
    def _():
        state_ref[...] = jnp.zeros(state_ref.shape, F32)

    cos, sin = cos_ref[...], sin_ref[...]

    def rope(ref, hs):
        x1 = ref[:, hs.start:hs.start + half].astype(F32)
        x2 = ref[:, hs.start + half:hs.stop].astype(F32)
        return jnp.concatenate([x1 * cos - x2 * sin, x1 * sin + x2 * cos], axis=-1)

    for h in range(RET_HEADS):
        hs = slice(h * D, (h + 1) * D)
        q = rope(rq_ref, hs)
        k = rope(rk_ref, hs) * (RET_HEAD ** -0.5)
        v16 = rv_ref[:, hs]
        q16 = q.astype(BF16)
        scores = _dot_nt(q16, k.astype(BF16)) * dmask_ref[h]
        y = _dot(scores.astype(BF16), v16)
        prev = state_ref[h]
        y = y + _dot(q16, prev.astype(BF16)) * qdec_ref[h]
        kd = k * kdec_ref[h]
        state_ref[h] = prev * chunk_dec[h] + _dot(kd.T.astype(BF16), v16)
        y = y * _rms(y, D) * nw_ref[:, hs]
        o_ref[:, hs] = (_silu(rg_ref[:, hs].astype(F32)) * y).astype(BF16)


def _retention(proj, cos, sin, nw, *, batch, seq):
    L = math.gcd(seq, RET_CHUNK)
    nc = seq // L
    w = RET_HEADS * RET_HEAD
    tok = lambda b, c: b * nc + c
    dmask, q_dec, k_dec, chunk_dec = _ret_constants(L)
    return pl.pallas_call(
        functools.partial(_ret_kernel, chunk_dec=chunk_dec),
        out_shape=jax.ShapeDtypeStruct((batch * seq, w), BF16),
        grid=(batch, nc),
        in_specs=[
            pl.BlockSpec((L, w), lambda b, c: (tok(b, c), P_RQ // w)),
            pl.BlockSpec((L, w), lambda b, c: (tok(b, c), P_RK // w)),
            pl.BlockSpec((L, w), lambda b, c: (tok(b, c), P_RV // w)),
            pl.BlockSpec((L, w), lambda b, c: (tok(b, c), P_RG // w)),
            pl.BlockSpec((L, RET_HEAD // 2), lambda b, c: (tok(b, c), 0)),
            pl.BlockSpec((L, RET_HEAD // 2), lambda b, c: (tok(b, c), 0)),
            _resident(nw.shape), _resident(dmask.shape), _resident(q_dec.shape),
            _resident(k_dec.shape),
        ],
        out_specs=pl.BlockSpec((L, w), lambda b, c: (tok(b, c), 0)),
        scratch_shapes=[pltpu.VMEM((RET_HEADS, RET_HEAD, RET_HEAD), F32)],
        compiler_params=_cparams(("parallel", "arbitrary")),
        name="retention",
    )(proj, proj, proj, proj, cos, sin, nw, dmask, q_dec, k_dec)


def _merge_kernel(ys_ref, ym_ref, yr_ref, g0_ref, g1_ref, g2_ref, gb_ref, x_ref,
                  w0_ref, w1_ref, w2_ref, wo_ref, o_ref):
    d = x_ref.shape[-1]

    def branch(k, y_ref, g_ref, w_ref):
        gate = _sigmoid(g_ref[...].astype(F32) + gb_ref[:, k * d:(k + 1) * d])
        return gate * _dot(y_ref[...], w_ref[...])

    merged = (branch(0, ys_ref, g0_ref, w0_ref) + branch(1, ym_ref, g1_ref, w1_ref)
              + branch(2, yr_ref, g2_ref, w2_ref))
    o_ref[...] = x_ref[...] + _dot(merged.astype(BF16), wo_ref[...])


def _merge(ys, ym, yr, proj, gb, x, w0, w1, w2, wo, layer, *, tm=256):
    m, d = x.shape
    tm = min(tm, m)
    bw = ys.shape[1]
    row = lambda i: (i, 0)
    return pl.pallas_call(
        _merge_kernel,
        out_shape=jax.ShapeDtypeStruct((m, d), F32),
        grid=(m // tm,),
        in_specs=[
            pl.BlockSpec((tm, bw), row), pl.BlockSpec((tm, bw), row), pl.BlockSpec((tm, bw), row),
            pl.BlockSpec((tm, d), lambda i: (i, P_GATES // d)),
            pl.BlockSpec((tm, d), lambda i: (i, P_GATES // d + 1)),
            pl.BlockSpec((tm, d), lambda i: (i, P_GATES // d + 2)),
            _resident(gb.shape),
            pl.BlockSpec((tm, d), row),
            _resident_layer(w0.shape, layer), _resident_layer(w1.shape, layer),
            _resident_layer(w2.shape, layer), _resident_layer(wo.shape, layer),
        ],
        out_specs=pl.BlockSpec((tm, d), row),
        compiler_params=_cparams(("parallel",)),
        name="merge",
    )(ys, ym, yr, proj, proj, proj, gb, x, w0, w1, w2, wo)


def _rope_angles(positions, dim):
    inv = 1.0 / (ROPE_THETA ** (jnp.arange(0, dim, 2, dtype=F32) / dim))
    return positions.astype(F32).reshape(-1, 1) * inv


def _pad_lanes(v, n):
    return jnp.concatenate([v, jnp.zeros((n - v.shape[0],), v.dtype)]).reshape(1, n)


def kernel(x, positions, ffn1_norm, ffn1_w_gate, ffn1_w_up, ffn1_w_down, mix_norm, w_in, gate_b, conv_w, conv_b, dt_bias, a_log, d_skip, ssm_norm, q_a_norm, w_q_b, kv_a_norm, w_kv_b, q_norm, k_norm, ret_norm, w_br_ssm, w_br_mla, w_br_ret, w_out, ffn2_norm, ffn2_w_gate, ffn2_w_up, ffn2_w_down):
    batch, seq, d = x.shape
    depth = w_in.shape[0]
    x = x.reshape(batch * seq, d)

    ang = _rope_angles(positions, MLA_ROPE)
    cm, sm = jnp.cos(ang), jnp.sin(ang)
    zq = jnp.zeros_like(cm)
    rope_c = jnp.concatenate([cm, cm, zq, zq], axis=1)
    rope_s1 = jnp.concatenate([-sm, zq, zq, zq], axis=1)
    rope_s2 = jnp.concatenate([zq, sm, zq, zq], axis=1)
    ang = _rope_angles(positions, RET_HEAD)
    cos_ret, sin_ret = jnp.cos(ang), jnp.sin(ang)

    w_in16 = _w_in_prep(w_in.astype(BF16))
    ffn1 = tuple(w.astype(BF16) for w in (ffn1_w_gate, ffn1_w_up, ffn1_w_down))
    ffn2 = tuple(w.astype(BF16) for w in (ffn2_w_gate, ffn2_w_up, ffn2_w_down))
    w_merge = tuple(w.astype(BF16) for w in (w_br_ssm, w_br_mla, w_br_ret, w_out))
    row = lambda v: v.reshape(1, -1)
    for l in range(depth):
        x = _ffn(x, row(ffn1_norm[l]), *ffn1, l)

        proj, dt_raw = _inproj(x, row(mix_norm[l]), w_in16, l)

        y_ssm = _ssd(
            proj, dt_raw,
            conv_w[l][:, :SSM_D_INNER], row(conv_b[l][:SSM_D_INNER]),
            conv_w[l][:, SSM_D_INNER:], row(conv_b[l][SSM_D_INNER:]),
            _pad_lanes(dt_bias[l], LANES), _pad_lanes(a_log[l], LANES),
            row(jnp.repeat(d_skip[l], SSM_HEAD_DIM)), row(ssm_norm[l]),
            batch=batch, seq=seq)

        wq = w_q_b[l].reshape(MLA_Q_LORA, MLA_HEADS, MLA_QK_DIM)
        wq = jnp.pad(wq, ((0, 0), (0, 0), (0, MLA_QK_PAD - MLA_QK_DIM)))
        wq = wq.transpose(1, 0, 2).astype(BF16)
        wkv = w_kv_b[l].reshape(MLA_KV_LORA, MLA_HEADS, MLA_NOPE + MLA_V)
        wkv = wkv.transpose(1, 0, 2).astype(BF16)
        q, k, v = _mla_prep(
            proj, rope_c, rope_s1, rope_s2, row(q_a_norm[l]), row(kv_a_norm[l]),
            _pad_lanes(q_norm[l], MLA_QK_PAD), _pad_lanes(k_norm[l], MLA_QK_PAD), wq, wkv)
        y_mla = _attention(q, k, v, batch=batch, seq=seq)

        y_ret = _retention(proj, cos_ret, sin_ret, row(ret_norm[l]), batch=batch, seq=seq)

        x = _merge(y_ssm, y_mla, y_ret, proj, row(gate_b[l]), x, *w_merge, l)

        x = _ffn(x, row(ffn2_norm[l]), *ffn2, l)
    return x.reshape(batch, seq, d)
```

```python
import functools
import math

import jax
import jax.numpy as jnp
import numpy as np
from jax import lax
from jax.experimental import pallas as pl
from jax.experimental.pallas import tpu as pltpu

F32 = jnp.float32
BF16 = jnp.bfloat16

NORM_EPS = 1e-6
ROPE_THETA = 10000.0

SSM_HEADS = 16
SSM_HEAD_DIM = 64
SSM_D_INNER = SSM_HEADS * SSM_HEAD_DIM
SSM_GROUPS = 2
SSM_STATE = 128
SSM_CONV = 4
SSD_CHUNK = 256
MLA_HEADS = 8
MLA_Q_LORA = 768
MLA_KV_LORA = 512
MLA_NOPE = 128
MLA_ROPE = 64
MLA_QK_DIM = MLA_NOPE + MLA_ROPE
MLA_V = 128
RET_HEADS = 4
RET_HEAD = 256
RET_CHUNK = 256
N_BRANCH = 3

LANES = 128
MLA_QK_PAD = MLA_NOPE + LANES
VMEM_LIMIT = 56 * 1024 * 1024


def _cparams(sem):
    return pltpu.CompilerParams(dimension_semantics=sem, vmem_limit_bytes=VMEM_LIMIT)


def _resident(shape):
    nd = len(shape)
    return pl.BlockSpec(shape, lambda *_: (0,) * nd, pipeline_mode=pl.Buffered(1))


def _resident_layer(shape, layer):
    nd = len(shape) - 1
    return pl.BlockSpec((None,) + tuple(shape[1:]), lambda *_: (layer,) + (0,) * nd,
                        pipeline_mode=pl.Buffered(1))


def _rms(x, n):
    return lax.rsqrt(jnp.sum(x * x, axis=-1, keepdims=True) * (1.0 / n) + NORM_EPS)


def _sigmoid(x):
    return 0.5 * jnp.tanh(0.5 * x) + 0.5


def _silu(x):
    h = 0.5 * x
    return h * jnp.tanh(h) + h


def _dot(a, b):
    return jnp.dot(a, b, preferred_element_type=F32)


def _dot_nt(a, b):
    return lax.dot_general(a, b, (((1,), (1,)), ((), ())), preferred_element_type=F32)


def _ffn_kernel(x_ref, nw_ref, wg_ref, wu_ref, wd_ref, o_ref, xn_ref):
    d = x_ref.shape[-1]
    j = pl.program_id(1)

    def half_ffn(xn):
        g = _dot(xn, wg_ref[...])
        u = _dot(xn, wu_ref[...])
        return _dot((_silu(g) * (0.5 * u)).astype(BF16), wd_ref[...])

    @pl.when(j == 0)
    def _():
        x = x_ref[...]
        xn = (x * _rms(x, d) * nw_ref[...]).astype(BF16)
        xn_ref[...] = xn
        o_ref[...] = x + half_ffn(xn)

    @pl.when(j > 0)
    def _():
        o_ref[...] += half_ffn(xn_ref[...])


def _ffn(x, nw, wg, wu, wd, layer, *, tm=1024, tf=512):
    m, d = x.shape
    f = wg.shape[2]
    tm = min(tm, m)
    return pl.pallas_call(
        _ffn_kernel,
        out_shape=jax.ShapeDtypeStruct((m, d), F32),
        grid=(m // tm, f // tf),
        in_specs=[
            pl.BlockSpec((tm, d), lambda i, j: (i, 0)),
            pl.BlockSpec((1, d), lambda i, j: (0, 0)),
            pl.BlockSpec((None, d, tf), lambda i, j: (layer, 0, j)),
            pl.BlockSpec((None, d, tf), lambda i, j: (layer, 0, j)),
            pl.BlockSpec((None, tf, d), lambda i, j: (layer, j, 0)),
        ],
        out_specs=pl.BlockSpec((tm, d), lambda i, j: (i, 0)),
        scratch_shapes=[pltpu.VMEM((tm, d), BF16)],
        compiler_params=_cparams(("parallel", "arbitrary")),
        name="ffn",
    )(x, nw, wg, wu, wd)


P_GATES = 0
P_RQ = 6144
P_RK = 7168
P_RV = 8192
P_RG = 9216
P_Z = 10240
P_XS = 11264
P_BC = 12288
P_KPE = 12800
P_QLAT = 13056
P_CKV = 13824
P_TOTAL = 14336


def _inproj_kernel(x_ref, nw_ref, w_ref, wdt_ref, o_ref, dt_ref, xn_ref):
    j = pl.program_id(1)
    d = x_ref.shape[-1]

    @pl.when(j == 0)
    def _():
        x = x_ref[...]
        xn = (x * _rms(x, d) * nw_ref[...]).astype(BF16)
        xn_ref[...] = xn
        dt_ref[...] = _dot(xn, wdt_ref[...])
        o_ref[...] = _dot(xn, w_ref[...]).astype(BF16)

    @pl.when(j > 0)
    def _():
        o_ref[...] = _dot(xn_ref[...], w_ref[...]).astype(BF16)


def _inproj(x, nw, w_all, layer, *, tm=1024, tn=2048):
    m, d = x.shape
    n = P_TOTAL
    tm = min(tm, m)
    return pl.pallas_call(
        _inproj_kernel,
        out_shape=(jax.ShapeDtypeStruct((m, n), BF16), jax.ShapeDtypeStruct((m, LANES), F32)),
        grid=(m // tm, n // tn),
        in_specs=[
            pl.BlockSpec((tm, d), lambda i, j: (i, 0)),
            pl.BlockSpec((1, d), lambda i, j: (0, 0)),
            pl.BlockSpec((None, d, tn), lambda i, j: (layer, 0, j)),
            pl.BlockSpec((None, d, LANES), lambda i, j: (layer, 0, P_TOTAL // LANES)),
        ],
        out_specs=(
            pl.BlockSpec((tm, tn), lambda i, j: (i, j)),
            pl.BlockSpec((tm, LANES), lambda i, j: (i, 0)),
        ),
        scratch_shapes=[pltpu.VMEM((tm, d), BF16)],
        compiler_params=_cparams(("parallel", "arbitrary")),
        name="inproj",
    )(x, nw, w_all, w_all)


_W_IN_SEGMENTS = (
    (P_GATES, 8016, 6144), (P_RQ, 3920, 1024), (P_RK, 4944, 1024), (P_RV, 5968, 1024),
    (P_RG, 6992, 1024), (P_Z, 0, 1024), (P_XS, 1024, 1536), (P_KPE, 3856, MLA_ROPE),
    (P_QLAT, 2576, MLA_Q_LORA), (P_CKV, 3344, MLA_KV_LORA), (P_TOTAL, 2560, SSM_HEADS))
W_IN_OUT = P_TOTAL + LANES


def _w_in_prep_kernel(x_ref, o_ref):
    n_src = x_ref.shape[1]
    segs = sorted(_W_IN_SEGMENTS)
    for k, (dest, _, width) in enumerate(segs):
        full = dest + width // LANES * LANES
        nxt = segs[k + 1][0] if k + 1 < len(segs) else o_ref.shape[1]
        if nxt > full:
            o_ref[:, full:nxt] = jnp.zeros((o_ref.shape[0], nxt - full), BF16)
    for dest, source, width in segs:
        lo = source // LANES * LANES
        hi = min(-(-(source + width) // LANES) * LANES, n_src)
        o_ref[:, dest:dest + width] = (
            x_ref[:, lo:hi][:, source - lo:source - lo + width].astype(BF16))


def _w_in_prep(w_in, *, rows=128):
    depth, d, n_src = w_in.shape
    return pl.pallas_call(
        _w_in_prep_kernel,
        out_shape=jax.ShapeDtypeStruct((depth, d, W_IN_OUT), BF16),
        grid=(depth, d // rows),
        in_specs=[pl.BlockSpec((None, rows, n_src), lambda l, i: (l, i, 0))],
        out_specs=pl.BlockSpec((None, rows, W_IN_OUT), lambda l, i: (l, i, 0)),
        compiler_params=_cparams(("parallel", "parallel")),
        name="w_in_prep",
    )(w_in)


def _split3(x):
    hi = x.astype(BF16)
    r1 = x - hi.astype(F32)
    mid = r1.astype(BF16)
    lo = (r1 - mid.astype(F32)).astype(BF16)
    return hi, mid, lo


def _ssd_constants(L):
    t = np.arange(L)
    shifts = np.stack([(t[:, None] - t[None, :] == j) for j in range(1, SSM_CONV)])
    tril = t[:, None] >= t[None, :]
    expand = np.zeros((LANES, SSM_D_INNER), bool)
    for h in range(SSM_HEADS):
        expand[h, h * SSM_HEAD_DIM:(h + 1) * SSM_HEAD_DIM] = True
    expand2 = np.concatenate([expand, expand])
    expand4 = np.concatenate([expand, expand, expand, np.zeros_like(expand)])
    return tuple(jnp.asarray(m, BF16) for m in (shifts, tril, expand2, expand4))


def _ssd_kernel(xs_ref, bc_ref, z_ref, dt_ref, cwx_ref, cbx_ref, cwb_ref, cbb_ref,
                dtb_ref, alog_ref, dskip_ref, nw_ref, shift_ref, tril_ref, e2_ref, e4_ref,
                o_ref, xtail_ref, btail_ref, state_ref):
    L = xs_ref.shape[0]
    P = SSM_HEAD_DIM
    N = SSM_STATE
    HG = SSM_HEADS // SSM_GROUPS
    GW = HG * P
    T = 8

    @pl.when(pl.program_id(1) == 0)
    def _():
        xtail_ref[0:T, :] = jnp.zeros((T, xtail_ref.shape[1]), F32)
        btail_ref[0:T, :] = jnp.zeros((T, btail_ref.shape[1]), F32)
        state_ref[...] = jnp.zeros(state_ref.shape, F32)

    def conv_silu(tail_ref, in_ref, w_ref, b_ref):
        x16 = in_ref[...]
        xf = x16.astype(F32)
        acc = b_ref[...] + w_ref[SSM_CONV - 1:SSM_CONV, :] * xf
        for j in range(1, SSM_CONV):
            acc = acc + w_ref[SSM_CONV - 1 - j:SSM_CONV - j, :] * _dot(shift_ref[j - 1], x16)
        tail_ref[T:2 * T, :] = xf[0:T, :]
        head = b_ref[...] + w_ref[0:1, :] * tail_ref[T - 3:2 * T - 3, :]
        for j in range(1, SSM_CONV):
            head = head + w_ref[j:j + 1, :] * tail_ref[T - 3 + j:2 * T - 3 + j, :]
        tail_ref[0:T, :] = xf[L - T:L, :]
        return _silu(jnp.concatenate([head, acc[T:, :]], axis=0))

    xc = conv_silu(xtail_ref, xs_ref, cwx_ref, cbx_ref)
    bcc = conv_silu(btail_ref, bc_ref, cwb_ref, cbb_ref)

    dtr = dt_ref[...] + dtb_ref[...]
    dt = jnp.maximum(dtr, 0.0) + jnp.log1p(jnp.exp(-jnp.abs(dtr)))
    adt = dt * (-jnp.exp(alog_ref[...]))
    tril = tril_ref[...]
    hi, mid, lo = _split3(adt)
    acs = _dot(tril, hi) + _dot(tril, mid) + _dot(tril, lo)
    acs_t = acs.T

    hi, mid, _ = _split3(dt)
    dtx = _dot(jnp.concatenate([hi, mid], axis=1), e2_ref[...])
    hi, mid, lo = _split3(acs)
    ax = _dot(jnp.concatenate([hi, mid, lo, jnp.zeros_like(lo)], axis=1), e4_ref[...])
    a_last = ax[L - 1:L, :]
    xd = xc * dtx
    xd16 = xd.astype(BF16)
    xdd16 = (xd * jnp.exp(a_last - ax)).astype(BF16)
    e_ax = jnp.exp(ax)
    e_last = jnp.exp(a_last)

    causal = (lax.broadcasted_iota(jnp.int32, (L, L), 0)
              >= lax.broadcasted_iota(jnp.int32, (L, L), 1))
    first_half = lax.broadcasted_iota(jnp.int32, (L, LANES), 1) < P
    tiles = []
    for g in range(SSM_GROUPS):
        bm = bcc[:, g * N:(g + 1) * N]
        cm16 = bcc[:, (SSM_GROUPS + g) * N:(SSM_GROUPS + g + 1) * N].astype(BF16)
        cb = _dot_nt(cm16, bm.astype(BF16))
        gs = slice(g * GW, (g + 1) * GW)
        prev = state_ref[g]
        y_off = _dot(cm16, prev.astype(BF16)) * e_ax[:, gs]
        state_ref[g] = prev * e_last[:, gs] + _dot(bm.T.astype(BF16), xdd16[:, gs])
        for t in range(GW // LANES):
            ts = slice(g * GW + t * LANES, g * GW + (t + 1) * LANES)
            pair = []
            for k in range(LANES // P):
                h = (g * GW + t * LANES) // P + k
                seg = acs[:, h:h + 1] - acs_t[h:h + 1, :]
                decay = jnp.exp(jnp.where(causal, seg, -jnp.inf))
                pair.append(_dot((cb * decay).astype(BF16), xd16[:, ts]))
            tiles.append(jnp.where(first_half, pair[0], pair[1])
                         + y_off[:, t * LANES:(t + 1) * LANES])

    y = jnp.concatenate(tiles, axis=1)
    y = (y + dskip_ref[...] * xc) * _silu(z_ref[...].astype(F32))
    for g in range(SSM_GROUPS):
        gs = slice(g * GW, (g + 1) * GW)
        yg = y[:, gs]
        o_ref[:, gs] = (yg * _rms(yg, GW) * nw_ref[:, gs]).astype(BF16)


def _ssd(proj, dt_raw, cwx, cbx, cwb, cbb, dtb, alog, dskip, nw, *, batch, seq):
    L = math.gcd(seq, SSD_CHUNK)
    nc = seq // L
    di = SSM_D_INNER
    bcw = 2 * SSM_GROUPS * SSM_STATE
    tok = lambda b, c: b * nc + c
    consts = _ssd_constants(L)
    params = (cwx, cbx, cwb, cbb, dtb, alog, dskip, nw) + consts
    return pl.pallas_call(
        _ssd_kernel,
        out_shape=jax.ShapeDtypeStruct((batch * seq, di), BF16),
        grid=(batch, nc),
        in_specs=[
            pl.BlockSpec((L, di), lambda b, c: (tok(b, c), P_XS // di)),
            pl.BlockSpec((L, bcw), lambda b, c: (tok(b, c), P_BC // bcw)),
            pl.BlockSpec((L, di), lambda b, c: (tok(b, c), P_Z // di)),
            pl.BlockSpec((L, LANES), lambda b, c: (tok(b, c), 0)),
        ] + [_resident(p.shape) for p in params],
        out_specs=pl.BlockSpec((L, di), lambda b, c: (tok(b, c), 0)),
        scratch_shapes=[
            pltpu.VMEM((16, di), F32),
            pltpu.VMEM((16, bcw), F32),
            pltpu.VMEM((SSM_GROUPS, SSM_STATE, di // SSM_GROUPS), F32),
        ],
        compiler_params=_cparams(("parallel", "arbitrary")),
        name="ssd",
    )(proj, proj, proj, dt_raw, *params)


def _rope_tile(x, c, s1, s2):
    q = MLA_ROPE // 2
    return x * c + pltpu.roll(x, LANES - q, 1) * s1 + pltpu.roll(x, q, 1) * s2


def _mla_prep_kernel(ql_ref, ckv_ref, kpe_ref, c_ref, s1_ref, s2_ref, qan_ref, kvan_ref,
                     qn_ref, kn_ref, wq_ref, wkv_ref, q_out, k_out, v_out, *, q_scale):
    ql = ql_ref[...].astype(F32)
    qa = (ql * _rms(ql, MLA_Q_LORA) * qan_ref[...]).astype(BF16)
    ckv = ckv_ref[...].astype(F32)
    kva = (ckv * _rms(ckv, MLA_KV_LORA) * kvan_ref[...]).astype(BF16)
    kpe = kpe_ref[...].astype(F32)
    kpe_sq = kpe * kpe
    c, s1, s2 = c_ref[...], s1_ref[...], s2_ref[...]
    qn, kn = qn_ref[...] * q_scale, kn_ref[...]
    kpe_rot = _rope_tile(kpe * kn[:, MLA_NOPE:], c, s1, s2)
    ones = jnp.ones((kpe.shape[0], MLA_V), BF16)
    tile_sum = jnp.where(
        (lax.broadcasted_iota(jnp.int32, (2 * LANES, 2 * LANES), 0) < LANES)
        == (lax.broadcasted_iota(jnp.int32, (2 * LANES, 2 * LANES), 1) < LANES),
        1.0, 0.0).astype(BF16)

    def inv_rms_pair(sq_a, sq_b):
        ss = _dot(jnp.concatenate([sq_a, sq_b], axis=1).astype(BF16), tile_sum)
        r = lax.rsqrt(ss * (1.0 / MLA_QK_DIM) + NORM_EPS)
        return r[:, :LANES], r[:, LANES:]

    def project(h):
        return [(_dot(qa, wq_ref[h + i]), _dot(kva, wkv_ref[h + i])) for i in range(2)]

    nxt = project(0)
    for h0 in range(0, MLA_HEADS, 2):
        cur = nxt
        if h0 + 2 < MLA_HEADS:
            nxt = project(h0 + 2)
        q_sq = [qh * qh for qh, _ in cur]
        rq = inv_rms_pair(*(sq[:, :LANES] + sq[:, LANES:] for sq in q_sq))
        rk = inv_rms_pair(*(kvh[:, :MLA_NOPE] * kvh[:, :MLA_NOPE] + kpe_sq for _, kvh in cur))
        for i, (qh, kvh) in enumerate(cur):
            h = h0 + i
            qh = qh * jnp.concatenate([rq[i], rq[i]], axis=1) * qn
            q_out[h, :, 0:MLA_NOPE] = qh[:, 0:MLA_NOPE].astype(BF16)
            q_out[h, :, MLA_NOPE:] = _rope_tile(qh[:, MLA_NOPE:], c, s1, s2).astype(BF16)
            k_out[h, :, 0:MLA_NOPE] = (kvh[:, 0:MLA_NOPE] * rk[i] * kn[:, 0:MLA_NOPE]).astype(BF16)
            k_out[h, :, MLA_NOPE:] = (kpe_rot * rk[i]).astype(BF16)
            v_out[h, :, 0:MLA_V] = kvh[:, MLA_NOPE:].astype(BF16)
            v_out[h, :, MLA_V:] = ones


def _mla_prep(proj, c, s1, s2, qan, kvan, qn, kn, wq, wkv, *, tm=512):
    m = proj.shape[0]
    tm = min(tm, m)
    row = lambda i: (i, 0)
    return pl.pallas_call(
        functools.partial(_mla_prep_kernel, q_scale=MLA_QK_DIM ** -0.5 * math.log2(math.e)),
        out_shape=(
            jax.ShapeDtypeStruct((MLA_HEADS, m, MLA_QK_PAD), BF16),
            jax.ShapeDtypeStruct((MLA_HEADS, m, MLA_QK_PAD), BF16),
            jax.ShapeDtypeStruct((MLA_HEADS, m, 2 * MLA_V), BF16),
        ),
        grid=(m // tm,),
        in_specs=[
            pl.BlockSpec((tm, MLA_Q_LORA), lambda i: (i, P_QLAT // MLA_Q_LORA)),
            pl.BlockSpec((tm, MLA_KV_LORA), lambda i: (i, P_CKV // MLA_KV_LORA)),
            pl.BlockSpec((tm, LANES), lambda i: (i, P_KPE // LANES)),
            pl.BlockSpec((tm, LANES), row), pl.BlockSpec((tm, LANES), row),
            pl.BlockSpec((tm, LANES), row),
            _resident(qan.shape), _resident(kvan.shape), _resident(qn.shape),
            _resident(kn.shape), _resident(wq.shape), _resident(wkv.shape),
        ],
        out_specs=(
            pl.BlockSpec((MLA_HEADS, tm, MLA_QK_PAD), lambda i: (0, i, 0)),
            pl.BlockSpec((MLA_HEADS, tm, MLA_QK_PAD), lambda i: (0, i, 0)),
            pl.BlockSpec((MLA_HEADS, tm, 2 * MLA_V), lambda i: (0, i, 0)),
        ),
        compiler_params=_cparams(("parallel",)),
        name="mla_prep",
    )(proj, proj, proj, c, s1, s2, qan, kvan, qn, kn, wq, wkv)


def _attn_kernel(q_ref, k_ref, v_ref, o_ref, s_ref, smax_ref, *, blk, heads):
    i = pl.program_id(2)

    def lanes(x, n):
        return jnp.concatenate([x] * n, axis=1)

    def row_max(s):
        return jnp.broadcast_to(jnp.max(s, axis=-1, keepdims=True), (s.shape[0], LANES))

    def scores(j, slot):
        start = pl.multiple_of(j * blk, blk)
        for h in range(heads):
            s = _dot_nt(q_ref[h], k_ref[h, pl.ds(start, blk), :])
            s_ref[slot, h] = s
            smax_ref[slot, h] = row_max(s)

    def consume(j, slot, carry, masked):
        start = pl.multiple_of(j * blk, blk)
        probs = []
        for h, (m, acc) in enumerate(carry):
            s = s_ref[slot, h]
            if masked:
                row = lax.broadcasted_iota(jnp.int32, (blk, blk), 0)
                col = lax.broadcasted_iota(jnp.int32, (blk, blk), 1)
                s = jnp.where(row >= col, s, -jnp.inf)
                m_new = jnp.maximum(m, row_max(s))
            else:
                m_new = jnp.maximum(m, smax_ref[slot, h])
            p = jnp.exp2(s - lanes(m_new, blk // LANES)).astype(BF16)
            probs.append((m_new, jnp.exp2(m - m_new), p))
        return tuple(
            (m_new, lanes(alpha, 2 * MLA_V // LANES) * acc + _dot(p, v_ref[h, pl.ds(start, blk), :]))
            for h, ((_, acc), (m_new, alpha, p)) in enumerate(zip(carry, probs)))

    def finish(carry):
        for h, (_, acc) in enumerate(carry):
            o_ref[:, h * MLA_V:(h + 1) * MLA_V] = (acc[:, :MLA_V] / acc[:, MLA_V:]).astype(BF16)

    def pair(t, carry):
        j = 2 * t
        scores(j + 1, 1)
        carry = consume(j, 0, carry, False)
        scores(j + 2, 0)
        return consume(j + 1, 1, carry, False)

    scores(0, 0)
    init = tuple((jnp.full((blk, LANES), -jnp.inf, F32), jnp.zeros((blk, 2 * MLA_V), F32))
                 for _ in range(heads))
    carry = lax.fori_loop(0, i // 2, pair, init)
    last_even = 2 * (i // 2)

    @pl.when(i % 2 == 0)
    def _():
        finish(consume(last_even, 0, carry, True))

    @pl.when(i % 2 == 1)
    def _():
        scores(last_even + 1, 1)
        finish(consume(last_even + 1, 1, consume(last_even, 0, carry, False), True))


def _attention(q, k, v, *, batch, seq, blk=512, heads=4):
    blk = min(blk, seq)
    nq = seq // blk
    return pl.pallas_call(
        functools.partial(_attn_kernel, blk=blk, heads=heads),
        out_shape=jax.ShapeDtypeStruct((batch * seq, MLA_HEADS * MLA_V), BF16),
        grid=(batch, MLA_HEADS // heads, nq),
        in_specs=[
            pl.BlockSpec((heads, blk, MLA_QK_PAD), lambda b, h, i: (h, b * nq + i, 0)),
            pl.BlockSpec((heads, seq, MLA_QK_PAD), lambda b, h, i: (h, b, 0)),
            pl.BlockSpec((heads, seq, 2 * MLA_V), lambda b, h, i: (h, b, 0)),
        ],
        out_specs=pl.BlockSpec((blk, heads * MLA_V), lambda b, h, i: (b * nq + i, h)),
        scratch_shapes=[pltpu.VMEM((2, heads, blk, blk), F32),
                        pltpu.VMEM((2, heads, blk, LANES), F32)],
        compiler_params=_cparams(("parallel", "parallel", "arbitrary")),
        name="mla_attention",
    )(q, k, v)


def _ret_constants(L):
    expo = 5.0 + 7.0 * np.arange(RET_HEADS, dtype=np.float32) / np.float32(RET_HEADS - 1)
    log_gamma = np.log1p(-np.exp2(-expo)).astype(np.float32)[:, None, None]
    pos = np.arange(L, dtype=np.float32)
    rel = pos[:, None] - pos[None, :]
    dmask = np.where(rel >= 0, np.exp(rel * log_gamma), 0.0).astype(np.float32)
    lanes = np.ones((1, 1, RET_HEAD), np.float32)
    q_dec = np.exp((pos + 1.0)[None, :, None] * log_gamma).astype(np.float32) * lanes
    k_dec = np.exp((L - 1.0 - pos)[None, :, None] * log_gamma).astype(np.float32) * lanes
    chunk_dec = [float(v) for v in np.exp(L * log_gamma[:, 0, 0])]
    return jnp.asarray(dmask), jnp.asarray(q_dec), jnp.asarray(k_dec), chunk_dec


def _ret_kernel(rq_ref, rk_ref, rv_ref, rg_ref, cos_ref, sin_ref, nw_ref, dmask_ref, qdec_ref,
                kdec_ref, o_ref, state_ref, *, chunk_dec):
    D = RET_HEAD
    half = D // 2

    @pl.when(pl.program_id(1) == 0)
    def _():
        state_ref[...] = jnp.zeros(state_ref.shape, F32)

    cos, sin = cos_ref[...], sin_ref[...]

    def rope(ref, hs):
        x1 = ref[:, hs.start:hs.start + half].astype(F32)
        x2 = ref[:, hs.start + half:hs.stop].astype(F32)
        return jnp.concatenate([x1 * cos - x2 * sin, x1 * sin + x2 * cos], axis=-1)

    for h in range(RET_HEADS):
        hs = slice(h * D, (h + 1) * D)
        q = rope(rq_ref, hs)
        k = rope(rk_ref, hs) * (RET_HEAD ** -0.5)
        v16 = rv_ref[:, hs]
        q16 = q.astype(BF16)
        scores = _dot_nt(q16, k.astype(BF16)) * dmask_ref[h]
        y = _dot(scores.astype(BF16), v16)
        prev = state_ref[h]
        y = y + _dot(q16, prev.astype(BF16)) * qdec_ref[h]
        kd = k * kdec_ref[h]
        state_ref[h] = prev * chunk_dec[h] + _dot(kd.T.astype(BF16), v16)
        y = y * _rms(y, D) * nw_ref[:, hs]
        o_ref[:, hs] = (_silu(rg_ref[:, hs].astype(F32)) * y).astype(BF16)


def _retention(proj, cos, sin, nw, *, batch, seq):
    L = math.gcd(seq, RET_CHUNK)
    nc = seq // L
    w = RET_HEADS * RET_HEAD
    tok = lambda b, c: b * nc + c
    dmask, q_dec, k_dec, chunk_dec = _ret_constants(L)
    return pl.pallas_call(
        functools.partial(_ret_kernel, chunk_dec=chunk_dec),
        out_shape=jax.ShapeDtypeStruct((batch * seq, w), BF16),
        grid=(batch, nc),
        in_specs=[
            pl.BlockSpec((L, w), lambda b, c: (tok(b, c), P_RQ // w)),
            pl.BlockSpec((L, w), lambda b, c: (tok(b, c), P_RK // w)),
            pl.BlockSpec((L, w), lambda b, c: (tok(b, c), P_RV // w)),
            pl.BlockSpec((L, w), lambda b, c: (tok(b, c), P_RG // w)),
            pl.BlockSpec((L, RET_HEAD // 2), lambda b, c: (tok(b, c), 0)),
            pl.BlockSpec((L, RET_HEAD // 2), lambda b, c: (tok(b, c), 0)),
            _resident(nw.shape), _resident(dmask.shape), _resident(q_dec.shape),
            _resident(k_dec.shape),
        ],
        out_specs=pl.BlockSpec((L, w), lambda b, c: (tok(b, c), 0)),
        scratch_shapes=[pltpu.VMEM((RET_HEADS, RET_HEAD, RET_HEAD), F32)],
        compiler_params=_cparams(("parallel", "arbitrary")),
        name="retention",
    )(proj, proj, proj, proj, cos, sin, nw, dmask, q_dec, k_dec)


def _merge_kernel(ys_ref, ym_ref, yr_ref, g0_ref, g1_ref, g2_ref, gb_ref, x_ref,
                  w0_ref, w1_ref, w2_ref, wo_ref, o_ref):
    d = x_ref.shape[-1]

    def branch(k, y_ref, g_ref, w_ref):
        gate = _sigmoid(g_ref[...].astype(F32) + gb_ref[:, k * d:(k + 1) * d])
        return gate * _dot(y_ref[...], w_ref[...])

    merged = (branch(0, ys_ref, g0_ref, w0_ref) + branch(1, ym_ref, g1_ref, w1_ref)
              + branch(2, yr_ref, g2_ref, w2_ref))
    o_ref[...] = x_ref[...] + _dot(merged.astype(BF16), wo_ref[...])


def _merge(ys, ym, yr, proj, gb, x, w0, w1, w2, wo, layer, *, tm=256):
    m, d = x.shape
    tm = min(tm, m)
    bw = ys.shape[1]
    row = lambda i: (i, 0)
    return pl.pallas_call(
        _merge_kernel,
        out_shape=jax.ShapeDtypeStruct((m, d), F32),
        grid=(m // tm,),
        in_specs=[
            pl.BlockSpec((tm, bw), row), pl.BlockSpec((tm, bw), row), pl.BlockSpec((tm, bw), row),
            pl.BlockSpec((tm, d), lambda i: (i, P_GATES // d)),
            pl.BlockSpec((tm, d), lambda i: (i, P_GATES // d + 1)),
            pl.BlockSpec((tm, d), lambda i: (i, P_GATES // d + 2)),
            _resident(gb.shape),
            pl.BlockSpec((tm, d), row),
            _resident_layer(w0.shape, layer), _resident_layer(w1.shape, layer),
            _resident_layer(w2.shape, layer), _resident_layer(wo.shape, layer),
        ],
        out_specs=pl.BlockSpec((tm, d), row),
        compiler_params=_cparams(("parallel",)),
        name="merge",
    )(ys, ym, yr, proj, proj, proj, gb, x, w0, w1, w2, wo)


def _rope_angles(positions, dim):
    inv = 1.0 / (ROPE_THETA ** (jnp.arange(0, dim, 2, dtype=F32) / dim))
    return positions.astype(F32).reshape(-1, 1) * inv


def _pad_lanes(v, n):
    return jnp.concatenate([v, jnp.zeros((n - v.shape[0],), v.dtype)]).reshape(1, n)


def kernel(x, positions, ffn1_norm, ffn1_w_gate, ffn1_w_up, ffn1_w_down, mix_norm, w_in, gate_b, conv_w, conv_b, dt_bias, a_log, d_skip, ssm_norm, q_a_norm, w_q_b, kv_a_norm, w_kv_b, q_norm, k_norm, ret_norm, w_br_ssm, w_br_mla, w_br_ret, w_out, ffn2_norm, ffn2_w_gate, ffn2_w_up, ffn2_w_down):
    batch, seq, d = x.shape
    depth = w_in.shape[0]
    x = x.reshape(batch * seq, d)

    ang = _rope_angles(positions, MLA_ROPE)
    cm, sm = jnp.cos(ang), jnp.sin(ang)
    zq = jnp.zeros_like(cm)
    rope_c = jnp.concatenate([cm, cm, zq, zq], axis=1)
    rope_s1 = jnp.concatenate([-sm, zq, zq, zq], axis=1)
    rope_s2 = jnp.concatenate([zq, sm, zq, zq], axis=1)
    ang = _rope_angles(positions, RET_HEAD)
    cos_ret, sin_ret = jnp.cos(ang), jnp.sin(ang)

    w_in16 = _w_in_prep(jnp.pad(w_in, ((0, 0), (0, 0), (0, -w_in.shape[2] % LANES))).astype(BF16))
    ffn1 = tuple(w.astype(BF16) for w in (ffn1_w_gate, ffn1_w_up, ffn1_w_down))
    ffn2 = tuple(w.astype(BF16) for w in (ffn2_w_gate, ffn2_w_up, ffn2_w_down))
    w_merge = tuple(w.astype(BF16) for w in (w_br_ssm, w_br_mla, w_br_ret, w_out))
    row = lambda v: v.reshape(1, -1)
    for l in range(depth):
        x = _ffn(x, row(ffn1_norm[l]), *ffn1, l)

        proj, dt_raw = _inproj(x, row(mix_norm[l]), w_in16, l)

        y_ssm = _ssd(
            proj, dt_raw,
            conv_w[l][:, :SSM_D_INNER], row(conv_b[l][:SSM_D_INNER]),
            conv_w[l][:, SSM_D_INNER:], row(conv_b[l][SSM_D_INNER:]),
            _pad_lanes(dt_bias[l], LANES), _pad_lanes(a_log[l], LANES),
            row(jnp.repeat(d_skip[l], SSM_HEAD_DIM)), row(ssm_norm[l]),
            batch=batch, seq=seq)

        wq = w_q_b[l].reshape(MLA_Q_LORA, MLA_HEADS, MLA_QK_DIM)
        wq = jnp.pad(wq, ((0, 0), (0, 0), (0, MLA_QK_PAD - MLA_QK_DIM)))
        wq = wq.transpose(1, 0, 2).astype(BF16)
        wkv = w_kv_b[l].reshape(MLA_KV_LORA, MLA_HEADS, MLA_NOPE + MLA_V)
        wkv = wkv.transpose(1, 0, 2).astype(BF16)
        q, k, v = _mla_prep(
            proj, rope_c, rope_s1, rope_s2, row(q_a_norm[l]), row(kv_a_norm[l]),
            _pad_lanes(q_norm[l], MLA_QK_PAD), _pad_lanes(k_norm[l], MLA_QK_PAD), wq, wkv)
        y_mla = _attention(q, k, v, batch=batch, seq=seq)

        y_ret = _retention(proj, cos_ret, sin_ret, row(ret_norm[l]), batch=batch, seq=seq)

        x = _merge(y_ssm, y_mla, y_ret, proj, row(gate_b[l]), x, *w_merge, l)

        x = _ffn(x, row(ffn2_norm[l]), *ffn2, l)
    return x.reshape(batch, seq, d)
```

```python
import functools
import math

import jax
import jax.numpy as jnp
import numpy as np
from jax import lax
from jax.experimental import pallas as pl
from jax.experimental.pallas import tpu as pltpu

F32 = jnp.float32
BF16 = jnp.bfloat16

NORM_EPS = 1e-6
ROPE_THETA = 10000.0

SSM_HEADS = 16
SSM_HEAD_DIM = 64
SSM_D_INNER = SSM_HEADS * SSM_HEAD_DIM
SSM_GROUPS = 2
SSM_STATE = 128
SSM_CONV = 4
SSD_CHUNK = 256
MLA_HEADS = 8
MLA_Q_LORA = 768
MLA_KV_LORA = 512
MLA_NOPE = 128
MLA_ROPE = 64
MLA_QK_DIM = MLA_NOPE + MLA_ROPE
MLA_V = 128
RET_HEADS = 4
RET_HEAD = 256
RET_CHUNK = 256
N_BRANCH = 3

LANES = 128
MLA_QK_PAD = MLA_NOPE + LANES
VMEM_LIMIT = 60 * 1024 * 1024


def _cparams(sem):
    return pltpu.CompilerParams(dimension_semantics=sem, vmem_limit_bytes=VMEM_LIMIT)


def _resident(shape):
    nd = len(shape)
    return pl.BlockSpec(shape, lambda *_: (0,) * nd, pipeline_mode=pl.Buffered(1))


BF16_SUBLANES = 16


def _cast_specs(arrays, layer, grid):
    steps = grid[0] * grid[1]
    in_specs, out_shapes, out_specs = [], [], []
    for a in arrays:
        _, rows, cols = a.shape
        br = next(b for b in range(BF16_SUBLANES, rows + 1, BF16_SUBLANES)
                  if rows % b == 0 and rows // b <= steps)
        block = lambda i, j, last=rows // br - 1: jnp.minimum(i * grid[1] + j, last)
        in_specs.append(pl.BlockSpec((None, br, cols),
                                     lambda i, j, block=block: (layer, block(i, j), 0)))
        out_shapes.append(jax.ShapeDtypeStruct((rows, cols), BF16))
        out_specs.append(pl.BlockSpec((br, cols), lambda i, j, block=block: (block(i, j), 0)))
    return in_specs, out_shapes, out_specs


def _cast_blocks(src_refs, dst_refs):
    for src, dst in zip(src_refs, dst_refs):
        dst[...] = src[...].astype(BF16)


def _rms(x, n):
    return lax.rsqrt(jnp.sum(x * x, axis=-1, keepdims=True) * (1.0 / n) + NORM_EPS)


def _sigmoid(x):
    return 0.5 * jnp.tanh(0.5 * x) + 0.5


def _silu(x):
    h = 0.5 * x
    return h * jnp.tanh(h) + h


def _dot(a, b):
    return jnp.dot(a, b, preferred_element_type=F32)


def _dot_nt(a, b):
    return lax.dot_general(a, b, (((1,), (1,)), ((), ())), preferred_element_type=F32)


def _ffn_kernel(x_ref, nw_ref, wg_ref, wu_ref, wd_ref, *refs):
    n_cast = (len(refs) - 2) // 2
    cast_in, o_ref, cast_out, xn_ref = (refs[:n_cast], refs[n_cast],
                                        refs[n_cast + 1:2 * n_cast + 1], refs[-1])
    d = x_ref.shape[-1]
    j = pl.program_id(1)

    def half_ffn(xn):
        _cast_blocks(cast_in, cast_out)
        g = _dot(xn, wg_ref[...])
        u = _dot(xn, wu_ref[...])
        return _dot((_silu(g) * (0.5 * u)).astype(BF16), wd_ref[...])

    @pl.when(j == 0)
    def _():
        x = x_ref[...]
        xn = (x * _rms(x, d) * nw_ref[...]).astype(BF16)
        xn_ref[...] = xn
        o_ref[...] = x + half_ffn(xn)

    @pl.when(j > 0)
    def _():
        o_ref[...] += half_ffn(xn_ref[...])


def _ffn(x, nw, wg, wu, wd, *, cast=(), cast_layer=0, tm=1024, tf=512):
    m, d = x.shape
    f = wg.shape[1]
    tm = min(tm, m)
    grid = (m // tm, f // tf)
    cast_in, cast_shapes, cast_out = _cast_specs(cast, cast_layer, grid)
    out = pl.pallas_call(
        _ffn_kernel,
        out_shape=[jax.ShapeDtypeStruct((m, d), F32)] + cast_shapes,
        grid=grid,
        in_specs=[
            pl.BlockSpec((tm, d), lambda i, j: (i, 0)),
            pl.BlockSpec((1, d), lambda i, j: (0, 0)),
            pl.BlockSpec((d, tf), lambda i, j: (0, j)),
            pl.BlockSpec((d, tf), lambda i, j: (0, j)),
            pl.BlockSpec((tf, d), lambda i, j: (j, 0)),
        ] + cast_in,
        out_specs=[pl.BlockSpec((tm, d), lambda i, j: (i, 0))] + cast_out,
        scratch_shapes=[pltpu.VMEM((tm, d), BF16)],
        compiler_params=_cparams(("parallel", "arbitrary")),
        name="ffn",
    )(x, nw, wg, wu, wd, *cast)
    return out[0], tuple(out[1:])


P_GATES = 0
P_RQ = 6144
P_RK = 7168
P_RV = 8192
P_RG = 9216
P_Z = 10240
P_XS = 11264
P_BC = 12288
P_KPE = 12800
P_QLAT = 13056
P_CKV = 13824
P_TOTAL = 14336


def _inproj_kernel(x_ref, nw_ref, w_ref, wdt_ref, *refs):
    n_cast = (len(refs) - 3) // 2
    cast_in, (o_ref, dt_ref), cast_out, xn_ref = (refs[:n_cast], refs[n_cast:n_cast + 2],
                                                  refs[n_cast + 2:2 * n_cast + 2], refs[-1])
    j = pl.program_id(1)
    d = x_ref.shape[-1]

    @pl.when(j == 0)
    def _():
        _cast_blocks(cast_in, cast_out)
        x = x_ref[...]
        xn = (x * _rms(x, d) * nw_ref[...]).astype(BF16)
        xn_ref[...] = xn
        dt_ref[...] = _dot(xn, wdt_ref[...])
        o_ref[...] = _dot(xn, w_ref[...]).astype(BF16)

    @pl.when(j > 0)
    def _():
        _cast_blocks(cast_in, cast_out)
        o_ref[...] = _dot(xn_ref[...], w_ref[...]).astype(BF16)


def _inproj(x, nw, w_all, layer, *, cast=(), tm=1024, tn=2048):
    m, d = x.shape
    n = P_TOTAL
    tm = min(tm, m)
    grid = (m // tm, n // tn)
    cast_in, cast_shapes, cast_out = _cast_specs(cast, layer, grid)
    out = pl.pallas_call(
        _inproj_kernel,
        out_shape=[jax.ShapeDtypeStruct((m, n), BF16),
                   jax.ShapeDtypeStruct((m, LANES), F32)] + cast_shapes,
        grid=grid,
        in_specs=[
            pl.BlockSpec((tm, d), lambda i, j: (i, 0)),
            pl.BlockSpec((1, d), lambda i, j: (0, 0)),
            pl.BlockSpec((None, d, tn), lambda i, j: (layer, 0, j)),
            pl.BlockSpec((None, d, LANES), lambda i, j: (layer, 0, P_TOTAL // LANES)),
        ] + cast_in,
        out_specs=[
            pl.BlockSpec((tm, tn), lambda i, j: (i, j)),
            pl.BlockSpec((tm, LANES), lambda i, j: (i, 0)),
        ] + cast_out,
        scratch_shapes=[pltpu.VMEM((tm, d), BF16)],
        compiler_params=_cparams(("parallel", "arbitrary")),
        name="inproj",
    )(x, nw, w_all, w_all, *cast)
    return out[0], out[1], tuple(out[2:])


_W_IN_SEGMENTS = (
    (P_GATES, 8016, 6144), (P_RQ, 3920, 1024), (P_RK, 4944, 1024), (P_RV, 5968, 1024),
    (P_RG, 6992, 1024), (P_Z, 0, 1024), (P_XS, 1024, 1536), (P_KPE, 3856, MLA_ROPE),
    (P_QLAT, 2576, MLA_Q_LORA), (P_CKV, 3344, MLA_KV_LORA), (P_TOTAL, 2560, SSM_HEADS))
W_IN_OUT = P_TOTAL + LANES


def _w_in_prep_kernel(x_ref, o_ref):
    n_src = x_ref.shape[1]
    segs = sorted(_W_IN_SEGMENTS)
    for k, (dest, _, width) in enumerate(segs):
        full = dest + width // LANES * LANES
        nxt = segs[k + 1][0] if k + 1 < len(segs) else o_ref.shape[1]
        if nxt > full:
            o_ref[:, full:nxt] = jnp.zeros((o_ref.shape[0], nxt - full), BF16)
    for dest, source, width in segs:
        lo = source // LANES * LANES
        hi = min(-(-(source + width) // LANES) * LANES, n_src)
        o_ref[:, dest:dest + width] = (
            x_ref[:, lo:hi][:, source - lo:source - lo + width].astype(BF16))


def _w_in_prep(w_in, *, rows=128):
    depth, d, n_src = w_in.shape
    return pl.pallas_call(
        _w_in_prep_kernel,
        out_shape=jax.ShapeDtypeStruct((depth, d, W_IN_OUT), BF16),
        grid=(depth, d // rows),
        in_specs=[pl.BlockSpec((None, rows, n_src), lambda l, i: (l, i, 0))],
        out_specs=pl.BlockSpec((None, rows, W_IN_OUT), lambda l, i: (l, i, 0)),
        compiler_params=_cparams(("parallel", "parallel")),
        name="w_in_prep",
    )(w_in)


def _split3(x):
    hi = x.astype(BF16)
    r1 = x - hi.astype(F32)
    mid = r1.astype(BF16)
    lo = (r1 - mid.astype(F32)).astype(BF16)
    return hi, mid, lo


def _ssd_constants(L):
    t = np.arange(L)
    shifts = np.stack([(t[:, None] - t[None, :] == j) for j in range(1, SSM_CONV)])
    tril = t[:, None] >= t[None, :]
    expand = np.zeros((LANES, SSM_D_INNER), bool)
    for h in range(SSM_HEADS):
        expand[h, h * SSM_HEAD_DIM:(h + 1) * SSM_HEAD_DIM] = True
    expand2 = np.concatenate([expand, expand])
    expand4 = np.concatenate([expand, expand, expand, np.zeros_like(expand)])
    return tuple(jnp.asarray(m, BF16) for m in (shifts, tril, expand2, expand4))


def _ssd_kernel(xs_ref, bc_ref, z_ref, dt_ref, cwx_ref, cbx_ref, cwb_ref, cbb_ref,
                dtb_ref, alog_ref, dskip_ref, nw_ref, shift_ref, tril_ref, e2_ref, e4_ref,
                o_ref, xtail_ref, btail_ref, state_ref):
    L = xs_ref.shape[0]
    P = SSM_HEAD_DIM
    N = SSM_STATE
    HG = SSM_HEADS // SSM_GROUPS
    GW = HG * P
    T = 8

    @pl.when(pl.program_id(1) == 0)
    def _():
        xtail_ref[0:T, :] = jnp.zeros((T, xtail_ref.shape[1]), F32)
        btail_ref[0:T, :] = jnp.zeros((T, btail_ref.shape[1]), F32)
        state_ref[...] = jnp.zeros(state_ref.shape, F32)

    def conv_silu(tail_ref, in_ref, w_ref, b_ref):
        x16 = in_ref[...]
        xf = x16.astype(F32)
        acc = b_ref[...] + w_ref[SSM_CONV - 1:SSM_CONV, :] * xf
        for j in range(1, SSM_CONV):
            acc = acc + w_ref[SSM_CONV - 1 - j:SSM_CONV - j, :] * _dot(shift_ref[j - 1], x16)
        tail_ref[T:2 * T, :] = xf[0:T, :]
        head = b_ref[...] + w_ref[0:1, :] * tail_ref[T - 3:2 * T - 3, :]
        for j in range(1, SSM_CONV):
            head = head + w_ref[j:j + 1, :] * tail_ref[T - 3 + j:2 * T - 3 + j, :]
        tail_ref[0:T, :] = xf[L - T:L, :]
        return _silu(jnp.concatenate([head, acc[T:, :]], axis=0))

    xc = conv_silu(xtail_ref, xs_ref, cwx_ref, cbx_ref)
    bcc = conv_silu(btail_ref, bc_ref, cwb_ref, cbb_ref)

    dtr = dt_ref[...] + dtb_ref[...]
    dt = jnp.maximum(dtr, 0.0) + jnp.log1p(jnp.exp(-jnp.abs(dtr)))
    adt = dt * (-jnp.exp(alog_ref[...]))
    tril = tril_ref[...]
    hi, mid, lo = _split3(adt)
    acs = _dot(tril, hi) + _dot(tril, mid) + _dot(tril, lo)
    acs_t = acs.T

    hi, mid, _ = _split3(dt)
    dtx = _dot(jnp.concatenate([hi, mid], axis=1), e2_ref[...])
    hi, mid, lo = _split3(acs)
    ax = _dot(jnp.concatenate([hi, mid, lo, jnp.zeros_like(lo)], axis=1), e4_ref[...])
    a_last = ax[L - 1:L, :]
    xd = xc * dtx
    xd16 = xd.astype(BF16)
    xdd16 = (xd * jnp.exp(a_last - ax)).astype(BF16)
    e_ax = jnp.exp(ax)
    e_last = jnp.exp(a_last)

    causal = (lax.broadcasted_iota(jnp.int32, (L, L), 0)
              >= lax.broadcasted_iota(jnp.int32, (L, L), 1))
    first_half = lax.broadcasted_iota(jnp.int32, (L, LANES), 1) < P
    tiles = []
    for g in range(SSM_GROUPS):
        bm = bcc[:, g * N:(g + 1) * N]
        cm16 = bcc[:, (SSM_GROUPS + g) * N:(SSM_GROUPS + g + 1) * N].astype(BF16)
        cb = _dot_nt(cm16, bm.astype(BF16))
        gs = slice(g * GW, (g + 1) * GW)
        prev = state_ref[g]
        y_off = _dot(cm16, prev.astype(BF16)) * e_ax[:, gs]
        state_ref[g] = prev * e_last[:, gs] + _dot(bm.T.astype(BF16), xdd16[:, gs])
        for t in range(GW // LANES):
            ts = slice(g * GW + t * LANES, g * GW + (t + 1) * LANES)
            pair = []
            for k in range(LANES // P):
                h = (g * GW + t * LANES) // P + k
                seg = acs[:, h:h + 1] - acs_t[h:h + 1, :]
                decay = jnp.exp(jnp.where(causal, seg, -jnp.inf))
                pair.append(_dot((cb * decay).astype(BF16), xd16[:, ts]))
            tiles.append(jnp.where(first_half, pair[0], pair[1])
                         + y_off[:, t * LANES:(t + 1) * LANES])

    y = jnp.concatenate(tiles, axis=1)
    y = (y + dskip_ref[...] * xc) * _silu(z_ref[...].astype(F32))
    for g in range(SSM_GROUPS):
        gs = slice(g * GW, (g + 1) * GW)
        yg = y[:, gs]
        o_ref[:, gs] = (yg * _rms(yg, GW) * nw_ref[:, gs]).astype(BF16)


def _ssd(proj, dt_raw, cwx, cbx, cwb, cbb, dtb, alog, dskip, nw, *, batch, seq):
    L = math.gcd(seq, SSD_CHUNK)
    nc = seq // L
    di = SSM_D_INNER
    bcw = 2 * SSM_GROUPS * SSM_STATE
    tok = lambda b, c: b * nc + c
    consts = _ssd_constants(L)
    params = (cwx, cbx, cwb, cbb, dtb, alog, dskip, nw) + consts
    return pl.pallas_call(
        _ssd_kernel,
        out_shape=jax.ShapeDtypeStruct((batch * seq, di), BF16),
        grid=(batch, nc),
        in_specs=[
            pl.BlockSpec((L, di), lambda b, c: (tok(b, c), P_XS // di)),
            pl.BlockSpec((L, bcw), lambda b, c: (tok(b, c), P_BC // bcw)),
            pl.BlockSpec((L, di), lambda b, c: (tok(b, c), P_Z // di)),
            pl.BlockSpec((L, LANES), lambda b, c: (tok(b, c), 0)),
        ] + [_resident(p.shape) for p in params],
        out_specs=pl.BlockSpec((L, di), lambda b, c: (tok(b, c), 0)),
        scratch_shapes=[
            pltpu.VMEM((16, di), F32),
            pltpu.VMEM((16, bcw), F32),
            pltpu.VMEM((SSM_GROUPS, SSM_STATE, di // SSM_GROUPS), F32),
        ],
        compiler_params=_cparams(("parallel", "arbitrary")),
        name="ssd",
    )(proj, proj, proj, dt_raw, *params)


def _rope_tile(x, c, s1, s2):
    q = MLA_ROPE // 2
    return x * c + pltpu.roll(x, LANES - q, 1) * s1 + pltpu.roll(x, q, 1) * s2


def _mla_prep_kernel(ql_ref, ckv_ref, kpe_ref, c_ref, s1_ref, s2_ref, qan_ref, kvan_ref,
                     qn_ref, kn_ref, wq_ref, wkv_ref, q_out, k_out, v_out, *, q_scale):
    ql = ql_ref[...].astype(F32)
    qa = (ql * _rms(ql, MLA_Q_LORA) * qan_ref[...]).astype(BF16)
    ckv = ckv_ref[...].astype(F32)
    kva = (ckv * _rms(ckv, MLA_KV_LORA) * kvan_ref[...]).astype(BF16)
    kpe = kpe_ref[...].astype(F32)
    kpe_sq = kpe * kpe
    c, s1, s2 = c_ref[...], s1_ref[...], s2_ref[...]
    qn, kn = qn_ref[...] * q_scale, kn_ref[...]
    kpe_rot = _rope_tile(kpe * kn[:, MLA_NOPE:], c, s1, s2)
    ones = jnp.ones((kpe.shape[0], MLA_V), BF16)
    tile_sum = jnp.where(
        (lax.broadcasted_iota(jnp.int32, (2 * LANES, 2 * LANES), 0) < LANES)
        == (lax.broadcasted_iota(jnp.int32, (2 * LANES, 2 * LANES), 1) < LANES),
        1.0, 0.0).astype(BF16)

    def inv_rms_pair(sq_a, sq_b):
        ss = _dot(jnp.concatenate([sq_a, sq_b], axis=1).astype(BF16), tile_sum)
        r = lax.rsqrt(ss * (1.0 / MLA_QK_DIM) + NORM_EPS)
        return r[:, :LANES], r[:, LANES:]

    def project(h):
        return [(_dot(qa, wq_ref[h + i]), _dot(kva, wkv_ref[h + i])) for i in range(2)]

    nxt = project(0)
    for h0 in range(0, MLA_HEADS, 2):
        cur = nxt
        if h0 + 2 < MLA_HEADS:
            nxt = project(h0 + 2)
        q_sq = [qh * qh for qh, _ in cur]
        rq = inv_rms_pair(*(sq[:, :LANES] + sq[:, LANES:] for sq in q_sq))
        rk = inv_rms_pair(*(kvh[:, :MLA_NOPE] * kvh[:, :MLA_NOPE] + kpe_sq for _, kvh in cur))
        for i, (qh, kvh) in enumerate(cur):
            h = h0 + i
            qh = qh * jnp.concatenate([rq[i], rq[i]], axis=1) * qn
            q_out[h, :, 0:MLA_NOPE] = qh[:, 0:MLA_NOPE].astype(BF16)
            q_out[h, :, MLA_NOPE:] = _rope_tile(qh[:, MLA_NOPE:], c, s1, s2).astype(BF16)
            k_out[h, :, 0:MLA_NOPE] = (kvh[:, 0:MLA_NOPE] * rk[i] * kn[:, 0:MLA_NOPE]).astype(BF16)
            k_out[h, :, MLA_NOPE:] = (kpe_rot * rk[i]).astype(BF16)
            v_out[h, :, 0:MLA_V] = kvh[:, MLA_NOPE:].astype(BF16)
            v_out[h, :, MLA_V:] = ones


def _mla_prep(proj, c, s1, s2, qan, kvan, qn, kn, wq, wkv, *, tm=512):
    m = proj.shape[0]
    tm = min(tm, m)
    row = lambda i: (i, 0)
    return pl.pallas_call(
        functools.partial(_mla_prep_kernel, q_scale=MLA_QK_DIM ** -0.5 * math.log2(math.e)),
        out_shape=(
            jax.ShapeDtypeStruct((MLA_HEADS, m, MLA_QK_PAD), BF16),
            jax.ShapeDtypeStruct((MLA_HEADS, m, MLA_QK_PAD), BF16),
            jax.ShapeDtypeStruct((MLA_HEADS, m, 2 * MLA_V), BF16),
        ),
        grid=(m // tm,),
        in_specs=[
            pl.BlockSpec((tm, MLA_Q_LORA), lambda i: (i, P_QLAT // MLA_Q_LORA)),
            pl.BlockSpec((tm, MLA_KV_LORA), lambda i: (i, P_CKV // MLA_KV_LORA)),
            pl.BlockSpec((tm, LANES), lambda i: (i, P_KPE // LANES)),
            pl.BlockSpec((tm, LANES), row), pl.BlockSpec((tm, LANES), row),
            pl.BlockSpec((tm, LANES), row),
            _resident(qan.shape), _resident(kvan.shape), _resident(qn.shape),
            _resident(kn.shape), _resident(wq.shape), _resident(wkv.shape),
        ],
        out_specs=(
            pl.BlockSpec((MLA_HEADS, tm, MLA_QK_PAD), lambda i: (0, i, 0)),
            pl.BlockSpec((MLA_HEADS, tm, MLA_QK_PAD), lambda i: (0, i, 0)),
            pl.BlockSpec((MLA_HEADS, tm, 2 * MLA_V), lambda i: (0, i, 0)),
        ),
        compiler_params=_cparams(("parallel",)),
        name="mla_prep",
    )(proj, proj, proj, c, s1, s2, qan, kvan, qn, kn, wq, wkv)


def _attn_kernel(q_ref, k_ref, v_ref, o_ref, s_ref, *, blk, heads):
    i = pl.program_id(2)

    def scores(j, slot):
        start = pl.multiple_of(j * blk, blk)
        for h in range(heads):
            s_ref[slot, h] = _dot_nt(q_ref[h], k_ref[h, pl.ds(start, blk), :])

    def consume(j, slot, carry, masked):
        start = pl.multiple_of(j * blk, blk)
        probs = []
        for h, (m, acc) in enumerate(carry):
            s = s_ref[slot, h]
            if masked:
                row = lax.broadcasted_iota(jnp.int32, (blk, blk), 0)
                col = lax.broadcasted_iota(jnp.int32, (blk, blk), 1)
                s = jnp.where(row >= col, s, -jnp.inf)
            m_new = jnp.maximum(m, jnp.max(s, axis=-1, keepdims=True))
            probs.append((m_new, jnp.exp2(m - m_new), jnp.exp2(s - m_new).astype(BF16)))
        return tuple((m_new, alpha * acc + _dot(p, v_ref[h, pl.ds(start, blk), :]))
                     for h, ((_, acc), (m_new, alpha, p)) in enumerate(zip(carry, probs)))

    def finish(carry):
        for h, (_, acc) in enumerate(carry):
            o_ref[:, h * MLA_V:(h + 1) * MLA_V] = (acc[:, :MLA_V] / acc[:, MLA_V:]).astype(BF16)

    def pair(t, carry):
        j = 2 * t
        scores(j + 1, 1)
        carry = consume(j, 0, carry, False)
        scores(j + 2, 0)
        return consume(j + 1, 1, carry, False)

    scores(0, 0)
    init = tuple((jnp.full((blk, 1), -jnp.inf, F32), jnp.zeros((blk, 2 * MLA_V), F32))
                 for _ in range(heads))
    carry = lax.fori_loop(0, i // 2, pair, init)
    last_even = 2 * (i // 2)

    @pl.when(i % 2 == 0)
    def _():
        finish(consume(last_even, 0, carry, True))

    @pl.when(i % 2 == 1)
    def _():
        scores(last_even + 1, 1)
        finish(consume(last_even + 1, 1, consume(last_even, 0, carry, False), True))


def _attention(q, k, v, *, batch, seq, blk=512, heads=4):
    blk = min(blk, seq)
    nq = seq // blk
    return pl.pallas_call(
        functools.partial(_attn_kernel, blk=blk, heads=heads),
        out_shape=jax.ShapeDtypeStruct((batch * seq, MLA_HEADS * MLA_V), BF16),
        grid=(batch, MLA_HEADS // heads, nq),
        in_specs=[
            pl.BlockSpec((heads, blk, MLA_QK_PAD), lambda b, h, i: (h, b * nq + i, 0)),
            pl.BlockSpec((heads, seq, MLA_QK_PAD), lambda b, h, i: (h, b, 0)),
            pl.BlockSpec((heads, seq, 2 * MLA_V), lambda b, h, i: (h, b, 0)),
        ],
        out_specs=pl.BlockSpec((blk, heads * MLA_V), lambda b, h, i: (b * nq + i, h)),
        scratch_shapes=[pltpu.VMEM((2, heads, blk, blk), F32)],
        compiler_params=_cparams(("parallel", "parallel", "arbitrary")),
        name="mla_attention",
    )(q, k, v)


def _ret_constants(L):
    expo = 5.0 + 7.0 * np.arange(RET_HEADS, dtype=np.float32) / np.float32(RET_HEADS - 1)
    log_gamma = np.log1p(-np.exp2(-expo)).astype(np.float32)[:, None, None]
    pos = np.arange(L, dtype=np.float32)
    rel = pos[:, None] - pos[None, :]
    dmask = np.where(rel >= 0, np.exp(rel * log_gamma), 0.0).astype(np.float32)
    lanes = np.ones((1, 1, RET_HEAD), np.float32)
    q_dec = np.exp((pos + 1.0)[None, :, None] * log_gamma).astype(np.float32) * lanes
    k_dec = np.exp((L - 1.0 - pos)[None, :, None] * log_gamma).astype(np.float32) * lanes
    chunk_dec = [float(v) for v in np.exp(L * log_gamma[:, 0, 0])]
    return jnp.asarray(dmask), jnp.asarray(q_dec), jnp.asarray(k_dec), chunk_dec


def _ret_kernel(rq_ref, rk_ref, rv_ref, rg_ref, cos_ref, sin_ref, nw_ref, dmask_ref, qdec_ref,
                kdec_ref, o_ref, state_ref, *, chunk_dec):
    D = RET_HEAD
    half = D // 2

    @pl.when(pl.program_id(1) == 0)
    def _():
        state_ref[...] = jnp.zeros(state_ref.shape, F32)

    cos, sin = cos_ref[...], sin_ref[...]

    def rope(ref, hs):
        x1 = ref[:, hs.start:hs.start + half].astype(F32)
        x2 = ref[:, hs.start + half:hs.stop].astype(F32)
        return jnp.concatenate([x1 * cos - x2 * sin, x1 * sin + x2 * cos], axis=-1)

    for h in range(RET_HEADS):
        hs = slice(h * D, (h + 1) * D)
        q = rope(rq_ref, hs)
        k = rope(rk_ref, hs) * (RET_HEAD ** -0.5)
        v16 = rv_ref[:, hs]
        q16 = q.astype(BF16)
        scores = _dot_nt(q16, k.astype(BF16)) * dmask_ref[h]
        y = _dot(scores.astype(BF16), v16)
        prev = state_ref[h]
        y = y + _dot(q16, prev.astype(BF16)) * qdec_ref[h]
        kd = k * kdec_ref[h]
        state_ref[h] = prev * chunk_dec[h] + _dot(kd.T.astype(BF16), v16)
        y = y * _rms(y, D) * nw_ref[:, hs]
        o_ref[:, hs] = (_silu(rg_ref[:, hs].astype(F32)) * y).astype(BF16)


def _retention(proj, cos, sin, nw, *, batch, seq):
    L = math.gcd(seq, RET_CHUNK)
    nc = seq // L
    w = RET_HEADS * RET_HEAD
    tok = lambda b, c: b * nc + c
    dmask, q_dec, k_dec, chunk_dec = _ret_constants(L)
    return pl.pallas_call(
        functools.partial(_ret_kernel, chunk_dec=chunk_dec),
        out_shape=jax.ShapeDtypeStruct((batch * seq, w), BF16),
        grid=(batch, nc),
        in_specs=[
            pl.BlockSpec((L, w), lambda b, c: (tok(b, c), P_RQ // w)),
            pl.BlockSpec((L, w), lambda b, c: (tok(b, c), P_RK // w)),
            pl.BlockSpec((L, w), lambda b, c: (tok(b, c), P_RV // w)),
            pl.BlockSpec((L, w), lambda b, c: (tok(b, c), P_RG // w)),
            pl.BlockSpec((L, RET_HEAD // 2), lambda b, c: (tok(b, c), 0)),
            pl.BlockSpec((L, RET_HEAD // 2), lambda b, c: (tok(b, c), 0)),
            _resident(nw.shape), _resident(dmask.shape), _resident(q_dec.shape),
            _resident(k_dec.shape),
        ],
        out_specs=pl.BlockSpec((L, w), lambda b, c: (tok(b, c), 0)),
        scratch_shapes=[pltpu.VMEM((RET_HEADS, RET_HEAD, RET_HEAD), F32)],
        compiler_params=_cparams(("parallel", "arbitrary")),
        name="retention",
    )(proj, proj, proj, proj, cos, sin, nw, dmask, q_dec, k_dec)


def _merge_kernel(ys_ref, ym_ref, yr_ref, g0_ref, g1_ref, g2_ref, gb_ref, x_ref,
                  w0_ref, w1_ref, w2_ref, wo_ref, o_ref):
    d = x_ref.shape[-1]

    def branch(k, y_ref, g_ref, w_ref):
        gate = _sigmoid(g_ref[...].astype(F32) + gb_ref[:, k * d:(k + 1) * d])
        return gate * _dot(y_ref[...], w_ref[...])

    merged = (branch(0, ys_ref, g0_ref, w0_ref) + branch(1, ym_ref, g1_ref, w1_ref)
              + branch(2, yr_ref, g2_ref, w2_ref))
    o_ref[...] = x_ref[...] + _dot(merged.astype(BF16), wo_ref[...])


def _merge(ys, ym, yr, proj, gb, x, w0, w1, w2, wo, *, tm=256):
    m, d = x.shape
    tm = min(tm, m)
    bw = ys.shape[1]
    row = lambda i: (i, 0)
    return pl.pallas_call(
        _merge_kernel,
        out_shape=jax.ShapeDtypeStruct((m, d), F32),
        grid=(m // tm,),
        in_specs=[
            pl.BlockSpec((tm, bw), row), pl.BlockSpec((tm, bw), row), pl.BlockSpec((tm, bw), row),
            pl.BlockSpec((tm, d), lambda i: (i, P_GATES // d)),
            pl.BlockSpec((tm, d), lambda i: (i, P_GATES // d + 1)),
            pl.BlockSpec((tm, d), lambda i: (i, P_GATES // d + 2)),
            _resident(gb.shape),
            pl.BlockSpec((tm, d), row),
            _resident(w0.shape), _resident(w1.shape), _resident(w2.shape), _resident(wo.shape),
        ],
        out_specs=pl.BlockSpec((tm, d), row),
        compiler_params=_cparams(("parallel",)),
        name="merge",
    )(ys, ym, yr, proj, proj, proj, gb, x, w0, w1, w2, wo)


def _rope_angles(positions, dim):
    inv = 1.0 / (ROPE_THETA ** (jnp.arange(0, dim, 2, dtype=F32) / dim))
    return positions.astype(F32).reshape(-1, 1) * inv


def _pad_lanes(v, n):
    return jnp.concatenate([v, jnp.zeros((n - v.shape[0],), v.dtype)]).reshape(1, n)


def kernel(x, positions, ffn1_norm, ffn1_w_gate, ffn1_w_up, ffn1_w_down, mix_norm, w_in, gate_b, conv_w, conv_b, dt_bias, a_log, d_skip, ssm_norm, q_a_norm, w_q_b, kv_a_norm, w_kv_b, q_norm, k_norm, ret_norm, w_br_ssm, w_br_mla, w_br_ret, w_out, ffn2_norm, ffn2_w_gate, ffn2_w_up, ffn2_w_down):
    batch, seq, d = x.shape
    depth = w_in.shape[0]
    x = x.reshape(batch * seq, d)

    ang = _rope_angles(positions, MLA_ROPE)
    cm, sm = jnp.cos(ang), jnp.sin(ang)
    zq = jnp.zeros_like(cm)
    rope_c = jnp.concatenate([cm, cm, zq, zq], axis=1)
    rope_s1 = jnp.concatenate([-sm, zq, zq, zq], axis=1)
    rope_s2 = jnp.concatenate([zq, sm, zq, zq], axis=1)
    ang = _rope_angles(positions, RET_HEAD)
    cos_ret, sin_ret = jnp.cos(ang), jnp.sin(ang)

    w_in16 = _w_in_prep(w_in.astype(BF16))
    ffn1_f32 = (ffn1_w_gate, ffn1_w_up, ffn1_w_down)
    ffn2_f32 = (ffn2_w_gate, ffn2_w_up, ffn2_w_down)
    merge_f32 = (w_br_ssm, w_br_mla, w_br_ret, w_out)
    ffn1 = tuple(w[0].astype(BF16) for w in ffn1_f32)
    row = lambda v: v.reshape(1, -1)
    for l in range(depth):
        x, ffn2 = _ffn(x, row(ffn1_norm[l]), *ffn1, cast=ffn2_f32, cast_layer=l)

        proj, dt_raw, w_merge = _inproj(x, row(mix_norm[l]), w_in16, l, cast=merge_f32)

        y_ssm = _ssd(
            proj, dt_raw,
            conv_w[l][:, :SSM_D_INNER], row(conv_b[l][:SSM_D_INNER]),
            conv_w[l][:, SSM_D_INNER:], row(conv_b[l][SSM_D_INNER:]),
            _pad_lanes(dt_bias[l], LANES), _pad_lanes(a_log[l], LANES),
            row(jnp.repeat(d_skip[l], SSM_HEAD_DIM)), row(ssm_norm[l]),
            batch=batch, seq=seq)

        wq = w_q_b[l].reshape(MLA_Q_LORA, MLA_HEADS, MLA_QK_DIM)
        wq = jnp.pad(wq, ((0, 0), (0, 0), (0, MLA_QK_PAD - MLA_QK_DIM)))
        wq = wq.transpose(1, 0, 2).astype(BF16)
        wkv = w_kv_b[l].reshape(MLA_KV_LORA, MLA_HEADS, MLA_NOPE + MLA_V)
        wkv = wkv.transpose(1, 0, 2).astype(BF16)
        q, k, v = _mla_prep(
            proj, rope_c, rope_s1, rope_s2, row(q_a_norm[l]), row(kv_a_norm[l]),
            _pad_lanes(q_norm[l], MLA_QK_PAD), _pad_lanes(k_norm[l], MLA_QK_PAD), wq, wkv)
        y_mla = _attention(q, k, v, batch=batch, seq=seq)

        y_ret = _retention(proj, cos_ret, sin_ret, row(ret_norm[l]), batch=batch, seq=seq)

        x = _merge(y_ssm, y_mla, y_ret, proj, row(gate_b[l]), x, *w_merge)

        if l + 1 < depth:
            x, ffn1 = _ffn(x, row(ffn2_norm[l]), *ffn2, cast=ffn1_f32, cast_layer=l + 1)
        else:
            x, _ = _ffn(x, row(ffn2_norm[l]), *ffn2)
    return x.reshape(batch, seq, d)
```

```python
import functools
import math

import jax
import jax.numpy as jnp
import numpy as np
from jax import lax
from jax.experimental import pallas as pl
from jax.experimental.pallas import tpu as pltpu

F32 = jnp.float32
BF16 = jnp.bfloat16

NORM_EPS = 1e-6
ROPE_THETA = 10000.0

SSM_HEADS = 16
SSM_HEAD_DIM = 64
SSM_D_INNER = SSM_HEADS * SSM_HEAD_DIM
SSM_GROUPS = 2
SSM_STATE = 128
SSM_CONV = 4
SSD_CHUNK = 256
MLA_HEADS = 8
MLA_Q_LORA = 768
MLA_KV_LORA = 512
MLA_NOPE = 128
MLA_ROPE = 64
MLA_QK_DIM = MLA_NOPE + MLA_ROPE
MLA_V = 128
RET_HEADS = 4
RET_HEAD = 256
RET_CHUNK = 256
N_BRANCH = 3

LANES = 128
MLA_QK_PAD = MLA_NOPE + LANES
VMEM_LIMIT = 60 * 1024 * 1024


def _cparams(sem):
    return pltpu.CompilerParams(dimension_semantics=sem, vmem_limit_bytes=VMEM_LIMIT)


def _resident(shape):
    nd = len(shape)
    return pl.BlockSpec(shape, lambda *_: (0,) * nd, pipeline_mode=pl.Buffered(1))


BF16_SUBLANES = 16


def _row_blocks(rows, grid):
    steps = grid[0] * grid[1]
    br = next(b for b in range(BF16_SUBLANES, rows + 1, BF16_SUBLANES)
              if rows % b == 0 and rows // b <= steps)
    return br, lambda i, j: jnp.minimum(i * grid[1] + j, rows // br - 1)


def _side_specs(arrays, layer, grid, out_cols=None):
    in_specs, out_shapes, out_specs = [], [], []
    for a in arrays:
        _, rows, cols = a.shape
        br, block = _row_blocks(rows, grid)
        oc = out_cols or cols
        in_specs.append(pl.BlockSpec((None, br, cols),
                                     lambda i, j, block=block: (layer, block(i, j), 0)))
        out_shapes.append(jax.ShapeDtypeStruct((rows, oc), BF16))
        out_specs.append(pl.BlockSpec((br, oc), lambda i, j, block=block: (block(i, j), 0)))
    return in_specs, out_shapes, out_specs


def _cast_blocks(src_refs, dst_refs):
    for src, dst in zip(src_refs, dst_refs):
        dst[...] = src[...].astype(BF16)


def _rms(x, n):
    return lax.rsqrt(jnp.sum(x * x, axis=-1, keepdims=True) * (1.0 / n) + NORM_EPS)


def _sigmoid(x):
    return 0.5 * jnp.tanh(0.5 * x) + 0.5


def _silu(x):
    h = 0.5 * x
    return h * jnp.tanh(h) + h


def _dot(a, b):
    return jnp.dot(a, b, preferred_element_type=F32)


def _dot_nt(a, b):
    return lax.dot_general(a, b, (((1,), (1,)), ((), ())), preferred_element_type=F32)


def _ffn_kernel(x_ref, nw_ref, wg_ref, wu_ref, wd_ref, *refs, n_cast, n_relayout):
    n_side = n_cast + n_relayout
    side_in, o_ref, side_out, xn_ref = (refs[:n_side], refs[n_side],
                                        refs[n_side + 1:2 * n_side + 1], refs[-1])
    d = x_ref.shape[-1]
    j = pl.program_id(1)

    def half_ffn(xn):
        _cast_blocks(side_in[:n_cast], side_out[:n_cast])
        for src, dst in zip(side_in[n_cast:], side_out[n_cast:]):
            _w_in_relayout(src, dst)
        g = _dot(xn, wg_ref[...])
        u = _dot(xn, wu_ref[...])
        return _dot((_silu(g) * (0.5 * u)).astype(BF16), wd_ref[...])

    @pl.when(j == 0)
    def _():
        x = x_ref[...]
        xn = (x * _rms(x, d) * nw_ref[...]).astype(BF16)
        xn_ref[...] = xn
        o_ref[...] = x + half_ffn(xn)

    @pl.when(j > 0)
    def _():
        o_ref[...] += half_ffn(xn_ref[...])


def _ffn(x, nw, wg, wu, wd, *, cast=(), relayout=(), side_layer=0, tm=1024, tf=512):
    m, d = x.shape
    f = wg.shape[1]
    tm = min(tm, m)
    grid = (m // tm, f // tf)
    specs = [_side_specs(cast, side_layer, grid), _side_specs(relayout, side_layer, grid, W_IN_OUT)]
    side_in, side_shapes, side_out = (sum((sp[k] for sp in specs), []) for k in range(3))
    out = pl.pallas_call(
        functools.partial(_ffn_kernel, n_cast=len(cast), n_relayout=len(relayout)),
        out_shape=[jax.ShapeDtypeStruct((m, d), F32)] + side_shapes,
        grid=grid,
        in_specs=[
            pl.BlockSpec((tm, d), lambda i, j: (i, 0)),
            pl.BlockSpec((1, d), lambda i, j: (0, 0)),
            pl.BlockSpec((d, tf), lambda i, j: (0, j)),
            pl.BlockSpec((d, tf), lambda i, j: (0, j)),
            pl.BlockSpec((tf, d), lambda i, j: (j, 0)),
        ] + side_in,
        out_specs=[pl.BlockSpec((tm, d), lambda i, j: (i, 0))] + side_out,
        scratch_shapes=[pltpu.VMEM((tm, d), BF16)],
        compiler_params=_cparams(("parallel", "arbitrary")),
        name="ffn",
    )(x, nw, wg, wu, wd, *cast, *relayout)
    return out[0], tuple(out[1:1 + len(cast)]), tuple(out[1 + len(cast):])


P_GATES = 0
P_RQ = 6144
P_RK = 7168
P_RV = 8192
P_RG = 9216
P_Z = 10240
P_XS = 11264
P_BC = 12288
P_KPE = 12800
P_QLAT = 13056
P_CKV = 13824
P_TOTAL = 14336


def _inproj_kernel(x_ref, nw_ref, w_ref, wdt_ref, *refs):
    n_cast = (len(refs) - 3) // 2
    cast_in, (o_ref, dt_ref), cast_out, xn_ref = (refs[:n_cast], refs[n_cast:n_cast + 2],
                                                  refs[n_cast + 2:2 * n_cast + 2], refs[-1])
    j = pl.program_id(1)
    d = x_ref.shape[-1]

    @pl.when(j == 0)
    def _():
        _cast_blocks(cast_in, cast_out)
        x = x_ref[...]
        xn = (x * _rms(x, d) * nw_ref[...]).astype(BF16)
        xn_ref[...] = xn
        dt_ref[...] = _dot(xn, wdt_ref[...])
        o_ref[...] = _dot(xn, w_ref[...]).astype(BF16)

    @pl.when(j > 0)
    def _():
        _cast_blocks(cast_in, cast_out)
        o_ref[...] = _dot(xn_ref[...], w_ref[...]).astype(BF16)


def _inproj(x, nw, w, *, cast=(), side_layer=0, tm=1024, tn=2048):
    m, d = x.shape
    n = P_TOTAL
    tm = min(tm, m)
    grid = (m // tm, n // tn)
    cast_in, cast_shapes, cast_out = _side_specs(cast, side_layer, grid)
    out = pl.pallas_call(
        _inproj_kernel,
        out_shape=[jax.ShapeDtypeStruct((m, n), BF16),
                   jax.ShapeDtypeStruct((m, LANES), F32)] + cast_shapes,
        grid=grid,
        in_specs=[
            pl.BlockSpec((tm, d), lambda i, j: (i, 0)),
            pl.BlockSpec((1, d), lambda i, j: (0, 0)),
            pl.BlockSpec((d, tn), lambda i, j: (0, j)),
            pl.BlockSpec((d, LANES), lambda i, j: (0, P_TOTAL // LANES)),
        ] + cast_in,
        out_specs=[
            pl.BlockSpec((tm, tn), lambda i, j: (i, j)),
            pl.BlockSpec((tm, LANES), lambda i, j: (i, 0)),
        ] + cast_out,
        scratch_shapes=[pltpu.VMEM((tm, d), BF16)],
        compiler_params=_cparams(("parallel", "arbitrary")),
        name="inproj",
    )(x, nw, w, w, *cast)
    return out[0], out[1], tuple(out[2:])


_W_IN_SEGMENTS = (
    (P_GATES, 8016, 6144), (P_RQ, 3920, 1024), (P_RK, 4944, 1024), (P_RV, 5968, 1024),
    (P_RG, 6992, 1024), (P_Z, 0, 1024), (P_XS, 1024, 1536), (P_KPE, 3856, MLA_ROPE),
    (P_QLAT, 2576, MLA_Q_LORA), (P_CKV, 3344, MLA_KV_LORA), (P_TOTAL, 2560, SSM_HEADS))
W_IN_OUT = P_TOTAL + LANES


def _w_in_relayout(x_ref, o_ref):
    n_src = x_ref.shape[1]
    segs = sorted(_W_IN_SEGMENTS)
    for k, (dest, _, width) in enumerate(segs):
        full = dest + width // LANES * LANES
        nxt = segs[k + 1][0] if k + 1 < len(segs) else o_ref.shape[1]
        if nxt > full:
            o_ref[:, full:nxt] = jnp.zeros((o_ref.shape[0], nxt - full), BF16)
    for dest, source, width in segs:
        lo = source // LANES * LANES
        hi = min(-(-(source + width) // LANES) * LANES, n_src)
        o_ref[:, dest:dest + width] = x_ref[:, lo:hi][:, source - lo:source - lo + width]


def _split3(x):
    hi = x.astype(BF16)
    r1 = x - hi.astype(F32)
    mid = r1.astype(BF16)
    lo = (r1 - mid.astype(F32)).astype(BF16)
    return hi, mid, lo


def _ssd_constants(L):
    t = np.arange(L)
    shifts = np.stack([(t[:, None] - t[None, :] == j) for j in range(1, SSM_CONV)])
    tril = t[:, None] >= t[None, :]
    expand = np.zeros((LANES, SSM_D_INNER), bool)
    for h in range(SSM_HEADS):
        expand[h, h * SSM_HEAD_DIM:(h + 1) * SSM_HEAD_DIM] = True
    expand2 = np.concatenate([expand, expand])
    expand4 = np.concatenate([expand, expand, expand, np.zeros_like(expand)])
    return tuple(jnp.asarray(m, BF16) for m in (shifts, tril, expand2, expand4))


def _ssd_kernel(xs_ref, bc_ref, z_ref, dt_ref, cwx_ref, cbx_ref, cwb_ref, cbb_ref,
                dtb_ref, alog_ref, dskip_ref, nw_ref, shift_ref, tril_ref, e2_ref, e4_ref,
                o_ref, xtail_ref, btail_ref, state_ref):
    L = xs_ref.shape[0]
    P = SSM_HEAD_DIM
    N = SSM_STATE
    HG = SSM_HEADS // SSM_GROUPS
    GW = HG * P
    T = 8

    @pl.when(pl.program_id(1) == 0)
    def _():
        xtail_ref[0:T, :] = jnp.zeros((T, xtail_ref.shape[1]), F32)
        btail_ref[0:T, :] = jnp.zeros((T, btail_ref.shape[1]), F32)
        state_ref[...] = jnp.zeros(state_ref.shape, F32)

    def conv_silu(tail_ref, in_ref, w_ref, b_ref):
        x16 = in_ref[...]
        xf = x16.astype(F32)
        acc = b_ref[...] + w_ref[SSM_CONV - 1:SSM_CONV, :] * xf
        for j in range(1, SSM_CONV):
            acc = acc + w_ref[SSM_CONV - 1 - j:SSM_CONV - j, :] * _dot(shift_ref[j - 1], x16)
        tail_ref[T:2 * T, :] = xf[0:T, :]
        head = b_ref[...] + w_ref[0:1, :] * tail_ref[T - 3:2 * T - 3, :]
        for j in range(1, SSM_CONV):
            head = head + w_ref[j:j + 1, :] * tail_ref[T - 3 + j:2 * T - 3 + j, :]
        tail_ref[0:T, :] = xf[L - T:L, :]
        return _silu(jnp.concatenate([head, acc[T:, :]], axis=0))

    xc = conv_silu(xtail_ref, xs_ref, cwx_ref, cbx_ref)
    bcc = conv_silu(btail_ref, bc_ref, cwb_ref, cbb_ref)

    dtr = dt_ref[...] + dtb_ref[...]
    dt = jnp.maximum(dtr, 0.0) + jnp.log1p(jnp.exp(-jnp.abs(dtr)))
    adt = dt * (-jnp.exp(alog_ref[...]))
    tril = tril_ref[...]
    hi, mid, lo = _split3(adt)
    acs = _dot(tril, hi) + _dot(tril, mid) + _dot(tril, lo)
    acs_t = acs.T

    hi, mid, _ = _split3(dt)
    dtx = _dot(jnp.concatenate([hi, mid], axis=1), e2_ref[...])
    hi, mid, lo = _split3(acs)
    ax = _dot(jnp.concatenate([hi, mid, lo, jnp.zeros_like(lo)], axis=1), e4_ref[...])
    a_last = ax[L - 1:L, :]
    xd = xc * dtx
    xd16 = xd.astype(BF16)
    xdd16 = (xd * jnp.exp(a_last - ax)).astype(BF16)
    e_ax = jnp.exp(ax)
    e_last = jnp.exp(a_last)

    causal = (lax.broadcasted_iota(jnp.int32, (L, L), 0)
              >= lax.broadcasted_iota(jnp.int32, (L, L), 1))
    first_half = lax.broadcasted_iota(jnp.int32, (L, LANES), 1) < P
    tiles = []
    for g in range(SSM_GROUPS):
        bm = bcc[:, g * N:(g + 1) * N]
        cm16 = bcc[:, (SSM_GROUPS + g) * N:(SSM_GROUPS + g + 1) * N].astype(BF16)
        cb = _dot_nt(cm16, bm.astype(BF16))
        gs = slice(g * GW, (g + 1) * GW)
        prev = state_ref[g]
        y_off = _dot(cm16, prev.astype(BF16)) * e_ax[:, gs]
        state_ref[g] = prev * e_last[:, gs] + _dot(bm.T.astype(BF16), xdd16[:, gs])
        for t in range(GW // LANES):
            ts = slice(g * GW + t * LANES, g * GW + (t + 1) * LANES)
            pair = []
            for k in range(LANES // P):
                h = (g * GW + t * LANES) // P + k
                seg = acs[:, h:h + 1] - acs_t[h:h + 1, :]
                decay = jnp.exp(jnp.where(causal, seg, -jnp.inf))
                pair.append(_dot((cb * decay).astype(BF16), xd16[:, ts]))
            tiles.append(jnp.where(first_half, pair[0], pair[1])
                         + y_off[:, t * LANES:(t + 1) * LANES])

    y = jnp.concatenate(tiles, axis=1)
    y = (y + dskip_ref[...] * xc) * _silu(z_ref[...].astype(F32))
    for g in range(SSM_GROUPS):
        gs = slice(g * GW, (g + 1) * GW)
        yg = y[:, gs]
        o_ref[:, gs] = (yg * _rms(yg, GW) * nw_ref[:, gs]).astype(BF16)


def _ssd(proj, dt_raw, cwx, cbx, cwb, cbb, dtb, alog, dskip, nw, *, batch, seq):
    L = math.gcd(seq, SSD_CHUNK)
    nc = seq // L
    di = SSM_D_INNER
    bcw = 2 * SSM_GROUPS * SSM_STATE
    tok = lambda b, c: b * nc + c
    consts = _ssd_constants(L)
    params = (cwx, cbx, cwb, cbb, dtb, alog, dskip, nw) + consts
    return pl.pallas_call(
        _ssd_kernel,
        out_shape=jax.ShapeDtypeStruct((batch * seq, di), BF16),
        grid=(batch, nc),
        in_specs=[
            pl.BlockSpec((L, di), lambda b, c: (tok(b, c), P_XS // di)),
            pl.BlockSpec((L, bcw), lambda b, c: (tok(b, c), P_BC // bcw)),
            pl.BlockSpec((L, di), lambda b, c: (tok(b, c), P_Z // di)),
            pl.BlockSpec((L, LANES), lambda b, c: (tok(b, c), 0)),
        ] + [_resident(p.shape) for p in params],
        out_specs=pl.BlockSpec((L, di), lambda b, c: (tok(b, c), 0)),
        scratch_shapes=[
            pltpu.VMEM((16, di), F32),
            pltpu.VMEM((16, bcw), F32),
            pltpu.VMEM((SSM_GROUPS, SSM_STATE, di // SSM_GROUPS), F32),
        ],
        compiler_params=_cparams(("parallel", "arbitrary")),
        name="ssd",
    )(proj, proj, proj, dt_raw, *params)


def _rope_tile(x, c, s1, s2):
    q = MLA_ROPE // 2
    return x * c + pltpu.roll(x, LANES - q, 1) * s1 + pltpu.roll(x, q, 1) * s2


def _mla_prep_kernel(ql_ref, ckv_ref, kpe_ref, c_ref, s1_ref, s2_ref, qan_ref, kvan_ref,
                     qn_ref, kn_ref, wq_ref, wkv_ref, q_out, k_out, v_out, *, q_scale):
    ql = ql_ref[...].astype(F32)
    qa = (ql * _rms(ql, MLA_Q_LORA) * qan_ref[...]).astype(BF16)
    ckv = ckv_ref[...].astype(F32)
    kva = (ckv * _rms(ckv, MLA_KV_LORA) * kvan_ref[...]).astype(BF16)
    kpe = kpe_ref[...].astype(F32)
    kpe_sq = kpe * kpe
    c, s1, s2 = c_ref[...], s1_ref[...], s2_ref[...]
    qn, kn = qn_ref[...] * q_scale, kn_ref[...]
    kpe_rot = _rope_tile(kpe * kn[:, MLA_NOPE:], c, s1, s2)
    ones = jnp.ones((kpe.shape[0], MLA_V), BF16)
    tile_sum = jnp.where(
        (lax.broadcasted_iota(jnp.int32, (2 * LANES, 2 * LANES), 0) < LANES)
        == (lax.broadcasted_iota(jnp.int32, (2 * LANES, 2 * LANES), 1) < LANES),
        1.0, 0.0).astype(BF16)

    def inv_rms_pair(sq_a, sq_b):
        ss = _dot(jnp.concatenate([sq_a, sq_b], axis=1).astype(BF16), tile_sum)
        r = lax.rsqrt(ss * (1.0 / MLA_QK_DIM) + NORM_EPS)
        return r[:, :LANES], r[:, LANES:]

    def project(h):
        return [(_dot(qa, wq_ref[h + i]), _dot(kva, wkv_ref[h + i])) for i in range(2)]

    nxt = project(0)
    for h0 in range(0, MLA_HEADS, 2):
        cur = nxt
        if h0 + 2 < MLA_HEADS:
            nxt = project(h0 + 2)
        q_sq = [qh * qh for qh, _ in cur]
        rq = inv_rms_pair(*(sq[:, :LANES] + sq[:, LANES:] for sq in q_sq))
        rk = inv_rms_pair(*(kvh[:, :MLA_NOPE] * kvh[:, :MLA_NOPE] + kpe_sq for _, kvh in cur))
        for i, (qh, kvh) in enumerate(cur):
            h = h0 + i
            qh = qh * jnp.concatenate([rq[i], rq[i]], axis=1) * qn
            q_out[h, :, 0:MLA_NOPE] = qh[:, 0:MLA_NOPE].astype(BF16)
            q_out[h, :, MLA_NOPE:] = _rope_tile(qh[:, MLA_NOPE:], c, s1, s2).astype(BF16)
            k_out[h, :, 0:MLA_NOPE] = (kvh[:, 0:MLA_NOPE] * rk[i] * kn[:, 0:MLA_NOPE]).astype(BF16)
            k_out[h, :, MLA_NOPE:] = (kpe_rot * rk[i]).astype(BF16)
            v_out[h, :, 0:MLA_V] = kvh[:, MLA_NOPE:].astype(BF16)
            v_out[h, :, MLA_V:] = ones


def _mla_prep(proj, c, s1, s2, qan, kvan, qn, kn, wq, wkv, *, tm=512):
    m = proj.shape[0]
    tm = min(tm, m)
    row = lambda i: (i, 0)
    return pl.pallas_call(
        functools.partial(_mla_prep_kernel, q_scale=MLA_QK_DIM ** -0.5 * math.log2(math.e)),
        out_shape=(
            jax.ShapeDtypeStruct((MLA_HEADS, m, MLA_QK_PAD), BF16),
            jax.ShapeDtypeStruct((MLA_HEADS, m, MLA_QK_PAD), BF16),
            jax.ShapeDtypeStruct((MLA_HEADS, m, 2 * MLA_V), BF16),
        ),
        grid=(m // tm,),
        in_specs=[
            pl.BlockSpec((tm, MLA_Q_LORA), lambda i: (i, P_QLAT // MLA_Q_LORA)),
            pl.BlockSpec((tm, MLA_KV_LORA), lambda i: (i, P_CKV // MLA_KV_LORA)),
            pl.BlockSpec((tm, LANES), lambda i: (i, P_KPE // LANES)),
            pl.BlockSpec((tm, LANES), row), pl.BlockSpec((tm, LANES), row),
            pl.BlockSpec((tm, LANES), row),
            _resident(qan.shape), _resident(kvan.shape), _resident(qn.shape),
            _resident(kn.shape), _resident(wq.shape), _resident(wkv.shape),
        ],
        out_specs=(
            pl.BlockSpec((MLA_HEADS, tm, MLA_QK_PAD), lambda i: (0, i, 0)),
            pl.BlockSpec((MLA_HEADS, tm, MLA_QK_PAD), lambda i: (0, i, 0)),
            pl.BlockSpec((MLA_HEADS, tm, 2 * MLA_V), lambda i: (0, i, 0)),
        ),
        compiler_params=_cparams(("parallel",)),
        name="mla_prep",
    )(proj, proj, proj, c, s1, s2, qan, kvan, qn, kn, wq, wkv)


def _attn_kernel(q_ref, k_ref, v_ref, o_ref, s_ref, *, blk, heads):
    i = pl.program_id(2)

    def scores(j, slot):
        start = pl.multiple_of(j * blk, blk)
        for h in range(heads):
            s_ref[slot, h] = _dot_nt(q_ref[h], k_ref[h, pl.ds(start, blk), :])

    def consume(j, slot, carry, masked):
        start = pl.multiple_of(j * blk, blk)
        probs = []
        for h, (m, acc) in enumerate(carry):
            s = s_ref[slot, h]
            if masked:
                row = lax.broadcasted_iota(jnp.int32, (blk, blk), 0)
                col = lax.broadcasted_iota(jnp.int32, (blk, blk), 1)
                s = jnp.where(row >= col, s, -jnp.inf)
            m_new = jnp.maximum(m, jnp.max(s, axis=-1, keepdims=True))
            probs.append((m_new, jnp.exp2(m - m_new), jnp.exp2(s - m_new).astype(BF16)))
        return tuple((m_new, alpha * acc + _dot(p, v_ref[h, pl.ds(start, blk), :]))
                     for h, ((_, acc), (m_new, alpha, p)) in enumerate(zip(carry, probs)))

    def finish(carry):
        for h, (_, acc) in enumerate(carry):
            o_ref[:, h * MLA_V:(h + 1) * MLA_V] = (acc[:, :MLA_V] / acc[:, MLA_V:]).astype(BF16)

    def pair(t, carry):
        j = 2 * t
        scores(j + 1, 1)
        carry = consume(j, 0, carry, False)
        scores(j + 2, 0)
        return consume(j + 1, 1, carry, False)

    scores(0, 0)
    init = tuple((jnp.full((blk, 1), -jnp.inf, F32), jnp.zeros((blk, 2 * MLA_V), F32))
                 for _ in range(heads))
    carry = lax.fori_loop(0, i // 2, pair, init)
    last_even = 2 * (i // 2)

    @pl.when(i % 2 == 0)
    def _():
        finish(consume(last_even, 0, carry, True))

    @pl.when(i % 2 == 1)
    def _():
        scores(last_even + 1, 1)
        finish(consume(last_even + 1, 1, consume(last_even, 0, carry, False), True))


def _attention(q, k, v, *, batch, seq, blk=512, heads=4):
    blk = min(blk, seq)
    nq = seq // blk
    return pl.pallas_call(
        functools.partial(_attn_kernel, blk=blk, heads=heads),
        out_shape=jax.ShapeDtypeStruct((batch * seq, MLA_HEADS * MLA_V), BF16),
        grid=(batch, MLA_HEADS // heads, nq),
        in_specs=[
            pl.BlockSpec((heads, blk, MLA_QK_PAD), lambda b, h, i: (h, b * nq + i, 0)),
            pl.BlockSpec((heads, seq, MLA_QK_PAD), lambda b, h, i: (h, b, 0)),
            pl.BlockSpec((heads, seq, 2 * MLA_V), lambda b, h, i: (h, b, 0)),
        ],
        out_specs=pl.BlockSpec((blk, heads * MLA_V), lambda b, h, i: (b * nq + i, h)),
        scratch_shapes=[pltpu.VMEM((2, heads, blk, blk), F32)],
        compiler_params=_cparams(("parallel", "parallel", "arbitrary")),
        name="mla_attention",
    )(q, k, v)


def _ret_constants(L):
    expo = 5.0 + 7.0 * np.arange(RET_HEADS, dtype=np.float32) / np.float32(RET_HEADS - 1)
    log_gamma = np.log1p(-np.exp2(-expo)).astype(np.float32)[:, None, None]
    pos = np.arange(L, dtype=np.float32)
    rel = pos[:, None] - pos[None, :]
    dmask = np.where(rel >= 0, np.exp(rel * log_gamma), 0.0).astype(np.float32)
    lanes = np.ones((1, 1, RET_HEAD), np.float32)
    q_dec = np.exp((pos + 1.0)[None, :, None] * log_gamma).astype(np.float32) * lanes
    k_dec = np.exp((L - 1.0 - pos)[None, :, None] * log_gamma).astype(np.float32) * lanes
    chunk_dec = [float(v) for v in np.exp(L * log_gamma[:, 0, 0])]
    return jnp.asarray(dmask), jnp.asarray(q_dec), jnp.asarray(k_dec), chunk_dec


def _ret_kernel(rq_ref, rk_ref, rv_ref, rg_ref, cos_ref, sin_ref, nw_ref, dmask_ref, qdec_ref,
                kdec_ref, o_ref, state_ref, *, chunk_dec):
    D = RET_HEAD
    half = D // 2

    @pl.when(pl.program_id(1) == 0)
    def _():
        state_ref[...] = jnp.zeros(state_ref.shape, F32)

    cos, sin = cos_ref[...], sin_ref[...]

    def rope(ref, hs):
        x1 = ref[:, hs.start:hs.start + half].astype(F32)
        x2 = ref[:, hs.start + half:hs.stop].astype(F32)
        return jnp.concatenate([x1 * cos - x2 * sin, x1 * sin + x2 * cos], axis=-1)

    for h in range(RET_HEADS):
        hs = slice(h * D, (h + 1) * D)
        q = rope(rq_ref, hs)
        k = rope(rk_ref, hs) * (RET_HEAD ** -0.5)
        v16 = rv_ref[:, hs]
        q16 = q.astype(BF16)
        scores = _dot_nt(q16, k.astype(BF16)) * dmask_ref[h]
        y = _dot(scores.astype(BF16), v16)
        prev = state_ref[h]
        y = y + _dot(q16, prev.astype(BF16)) * qdec_ref[h]
        kd = k * kdec_ref[h]
        state_ref[h] = prev * chunk_dec[h] + _dot(kd.T.astype(BF16), v16)
        y = y * _rms(y, D) * nw_ref[:, hs]
        o_ref[:, hs] = (_silu(rg_ref[:, hs].astype(F32)) * y).astype(BF16)


def _retention(proj, cos, sin, nw, *, batch, seq):
    L = math.gcd(seq, RET_CHUNK)
    nc = seq // L
    w = RET_HEADS * RET_HEAD
    tok = lambda b, c: b * nc + c
    dmask, q_dec, k_dec, chunk_dec = _ret_constants(L)
    return pl.pallas_call(
        functools.partial(_ret_kernel, chunk_dec=chunk_dec),
        out_shape=jax.ShapeDtypeStruct((batch * seq, w), BF16),
        grid=(batch, nc),
        in_specs=[
            pl.BlockSpec((L, w), lambda b, c: (tok(b, c), P_RQ // w)),
            pl.BlockSpec((L, w), lambda b, c: (tok(b, c), P_RK // w)),
            pl.BlockSpec((L, w), lambda b, c: (tok(b, c), P_RV // w)),
            pl.BlockSpec((L, w), lambda b, c: (tok(b, c), P_RG // w)),
            pl.BlockSpec((L, RET_HEAD // 2), lambda b, c: (tok(b, c), 0)),
            pl.BlockSpec((L, RET_HEAD // 2), lambda b, c: (tok(b, c), 0)),
            _resident(nw.shape), _resident(dmask.shape), _resident(q_dec.shape),
            _resident(k_dec.shape),
        ],
        out_specs=pl.BlockSpec((L, w), lambda b, c: (tok(b, c), 0)),
        scratch_shapes=[pltpu.VMEM((RET_HEADS, RET_HEAD, RET_HEAD), F32)],
        compiler_params=_cparams(("parallel", "arbitrary")),
        name="retention",
    )(proj, proj, proj, proj, cos, sin, nw, dmask, q_dec, k_dec)


def _merge_kernel(ys_ref, ym_ref, yr_ref, g0_ref, g1_ref, g2_ref, gb_ref, x_ref,
                  w0_ref, w1_ref, w2_ref, wo_ref, o_ref):
    d = x_ref.shape[-1]

    def branch(k, y_ref, g_ref, w_ref):
        gate = _sigmoid(g_ref[...].astype(F32) + gb_ref[:, k * d:(k + 1) * d])
        return gate * _dot(y_ref[...], w_ref[...])

    merged = (branch(0, ys_ref, g0_ref, w0_ref) + branch(1, ym_ref, g1_ref, w1_ref)
              + branch(2, yr_ref, g2_ref, w2_ref))
    o_ref[...] = x_ref[...] + _dot(merged.astype(BF16), wo_ref[...])


def _merge(ys, ym, yr, proj, gb, x, w0, w1, w2, wo, *, tm=256):
    m, d = x.shape
    tm = min(tm, m)
    bw = ys.shape[1]
    row = lambda i: (i, 0)
    return pl.pallas_call(
        _merge_kernel,
        out_shape=jax.ShapeDtypeStruct((m, d), F32),
        grid=(m // tm,),
        in_specs=[
            pl.BlockSpec((tm, bw), row), pl.BlockSpec((tm, bw), row), pl.BlockSpec((tm, bw), row),
            pl.BlockSpec((tm, d), lambda i: (i, P_GATES // d)),
            pl.BlockSpec((tm, d), lambda i: (i, P_GATES // d + 1)),
            pl.BlockSpec((tm, d), lambda i: (i, P_GATES // d + 2)),
            _resident(gb.shape),
            pl.BlockSpec((tm, d), row),
            _resident(w0.shape), _resident(w1.shape), _resident(w2.shape), _resident(wo.shape),
        ],
        out_specs=pl.BlockSpec((tm, d), row),
        compiler_params=_cparams(("parallel",)),
        name="merge",
    )(ys, ym, yr, proj, proj, proj, gb, x, w0, w1, w2, wo)


def _rope_angles(positions, dim):
    inv = 1.0 / (ROPE_THETA ** (jnp.arange(0, dim, 2, dtype=F32) / dim))
    return positions.astype(F32).reshape(-1, 1) * inv


def _pad_lanes(v, n):
    return jnp.concatenate([v, jnp.zeros((n - v.shape[0],), v.dtype)]).reshape(1, n)


def kernel(x, positions, ffn1_norm, ffn1_w_gate, ffn1_w_up, ffn1_w_down, mix_norm, w_in, gate_b, conv_w, conv_b, dt_bias, a_log, d_skip, ssm_norm, q_a_norm, w_q_b, kv_a_norm, w_kv_b, q_norm, k_norm, ret_norm, w_br_ssm, w_br_mla, w_br_ret, w_out, ffn2_norm, ffn2_w_gate, ffn2_w_up, ffn2_w_down):
    batch, seq, d = x.shape
    depth = w_in.shape[0]
    x = x.reshape(batch * seq, d)

    ang = _rope_angles(positions, MLA_ROPE)
    cm, sm = jnp.cos(ang), jnp.sin(ang)
    zq = jnp.zeros_like(cm)
    rope_c = jnp.concatenate([cm, cm, zq, zq], axis=1)
    rope_s1 = jnp.concatenate([-sm, zq, zq, zq], axis=1)
    rope_s2 = jnp.concatenate([zq, sm, zq, zq], axis=1)
    ang = _rope_angles(positions, RET_HEAD)
    cos_ret, sin_ret = jnp.cos(ang), jnp.sin(ang)

    w_pad = jnp.pad(w_in.astype(BF16), ((0, 0), (0, 0), (0, -w_in.shape[2] % LANES)))
    ffn1_f32 = (ffn1_w_gate, ffn1_w_up, ffn1_w_down)
    ffn2_f32 = (ffn2_w_gate, ffn2_w_up, ffn2_w_down)
    merge_f32 = (w_br_ssm, w_br_mla, w_br_ret, w_out)
    ffn1 = tuple(w[0].astype(BF16) for w in ffn1_f32)
    row = lambda v: v.reshape(1, -1)
    for l in range(depth):
        x, ffn2, (w16,) = _ffn(x, row(ffn1_norm[l]), *ffn1, cast=ffn2_f32, relayout=(w_pad,),
                               side_layer=l)

        proj, dt_raw, w_merge = _inproj(x, row(mix_norm[l]), w16, cast=merge_f32, side_layer=l)

        y_ssm = _ssd(
            proj, dt_raw,
            conv_w[l][:, :SSM_D_INNER], row(conv_b[l][:SSM_D_INNER]),
            conv_w[l][:, SSM_D_INNER:], row(conv_b[l][SSM_D_INNER:]),
            _pad_lanes(dt_bias[l], LANES), _pad_lanes(a_log[l], LANES),
            row(jnp.repeat(d_skip[l], SSM_HEAD_DIM)), row(ssm_norm[l]),
            batch=batch, seq=seq)

        wq = w_q_b[l].reshape(MLA_Q_LORA, MLA_HEADS, MLA_QK_DIM)
        wq = jnp.pad(wq, ((0, 0), (0, 0), (0, MLA_QK_PAD - MLA_QK_DIM)))
        wq = wq.transpose(1, 0, 2).astype(BF16)
        wkv = w_kv_b[l].reshape(MLA_KV_LORA, MLA_HEADS, MLA_NOPE + MLA_V)
        wkv = wkv.transpose(1, 0, 2).astype(BF16)
        q, k, v = _mla_prep(
            proj, rope_c, rope_s1, rope_s2, row(q_a_norm[l]), row(kv_a_norm[l]),
            _pad_lanes(q_norm[l], MLA_QK_PAD), _pad_lanes(k_norm[l], MLA_QK_PAD), wq, wkv)
        y_mla = _attention(q, k, v, batch=batch, seq=seq)

        y_ret = _retention(proj, cos_ret, sin_ret, row(ret_norm[l]), batch=batch, seq=seq)

        x = _merge(y_ssm, y_mla, y_ret, proj, row(gate_b[l]), x, *w_merge)

        if l + 1 < depth:
            x, ffn1, _ = _ffn(x, row(ffn2_norm[l]), *ffn2, cast=ffn1_f32, side_layer=l + 1)
        else:
            x = _ffn(x, row(ffn2_norm[l]), *ffn2)[0]
    return x.reshape(batch, seq, d)
```

```python
import functools
import math

import jax
import jax.numpy as jnp
import numpy as np
from jax import lax
from jax.experimental import pallas as pl
from jax.experimental.pallas import tpu as pltpu

F32 = jnp.float32
BF16 = jnp.bfloat16

NORM_EPS = 1e-6
ROPE_THETA = 10000.0

SSM_HEADS = 16
SSM_HEAD_DIM = 64
SSM_D_INNER = SSM_HEADS * SSM_HEAD_DIM
SSM_GROUPS = 2
SSM_STATE = 128
SSM_CONV = 4
SSD_CHUNK = 256
MLA_HEADS = 8
MLA_Q_LORA = 768
MLA_KV_LORA = 512
MLA_NOPE = 128
MLA_ROPE = 64
MLA_QK_DIM = MLA_NOPE + MLA_ROPE
MLA_V = 128
RET_HEADS = 4
RET_HEAD = 256
RET_CHUNK = 256
N_BRANCH = 3

LANES = 128
MLA_QK_PAD = MLA_NOPE + LANES
VMEM_LIMIT = 60 * 1024 * 1024


def _cparams(sem):
    return pltpu.CompilerParams(dimension_semantics=sem, vmem_limit_bytes=VMEM_LIMIT)


def _resident(shape):
    nd = len(shape)
    return pl.BlockSpec(shape, lambda *_: (0,) * nd, pipeline_mode=pl.Buffered(1))


BF16_SUBLANES = 16


def _row_blocks(rows, grid):
    steps = grid[0] * grid[1]
    br = next(b for b in range(BF16_SUBLANES, rows + 1, BF16_SUBLANES)
              if rows % b == 0 and rows // b <= steps)
    return br, lambda i, j: jnp.minimum(i * grid[1] + j, rows // br - 1)


def _side_specs(arrays, layer, grid, out_cols=None):
    in_specs, out_shapes, out_specs = [], [], []
    for a in arrays:
        _, rows, cols = a.shape
        br, block = _row_blocks(rows, grid)
        oc = out_cols or cols
        in_specs.append(pl.BlockSpec((None, br, cols),
                                     lambda i, j, block=block: (layer, block(i, j), 0)))
        out_shapes.append(jax.ShapeDtypeStruct((rows, oc), BF16))
        out_specs.append(pl.BlockSpec((br, oc), lambda i, j, block=block: (block(i, j), 0)))
    return in_specs, out_shapes, out_specs


def _cast_blocks(src_refs, dst_refs):
    for src, dst in zip(src_refs, dst_refs):
        dst[...] = src[...].astype(BF16)


def _rms(x, n):
    return lax.rsqrt(jnp.sum(x * x, axis=-1, keepdims=True) * (1.0 / n) + NORM_EPS)


def _sigmoid(x):
    return 0.5 * jnp.tanh(0.5 * x) + 0.5


def _silu(x):
    h = 0.5 * x
    return h * jnp.tanh(h) + h


def _dot(a, b):
    return jnp.dot(a, b, preferred_element_type=F32)


def _dot_nt(a, b):
    return lax.dot_general(a, b, (((1,), (1,)), ((), ())), preferred_element_type=F32)


def _ffn_kernel(x_ref, nw_ref, wg_ref, wu_ref, wd_ref, *refs, n_cast, n_relayout):
    n_side = n_cast + n_relayout
    side_in, o_ref, side_out, xn_ref = (refs[:n_side], refs[n_side],
                                        refs[n_side + 1:2 * n_side + 1], refs[-1])
    d = x_ref.shape[-1]
    j = pl.program_id(1)

    def half_ffn(xn):
        _cast_blocks(side_in[:n_cast], side_out[:n_cast])
        for src, dst in zip(side_in[n_cast:], side_out[n_cast:]):
            _w_in_relayout(src, dst)
        g = _dot(xn, wg_ref[...])
        u = _dot(xn, wu_ref[...])
        return _dot((_silu(g) * (0.5 * u)).astype(BF16), wd_ref[...])

    @pl.when(j == 0)
    def _():
        x = x_ref[...]
        xn = (x * _rms(x, d) * nw_ref[...]).astype(BF16)
        xn_ref[...] = xn
        o_ref[...] = x + half_ffn(xn)

    @pl.when(j > 0)
    def _():
        o_ref[...] += half_ffn(xn_ref[...])


def _ffn(x, nw, wg, wu, wd, *, cast=(), relayout=(), side_layer=0, tm=1024, tf=512):
    m, d = x.shape
    f = wg.shape[1]
    tm = min(tm, m)
    grid = (m // tm, f // tf)
    specs = [_side_specs(cast, side_layer, grid), _side_specs(relayout, side_layer, grid, W_IN_OUT)]
    side_in, side_shapes, side_out = (sum((sp[k] for sp in specs), []) for k in range(3))
    out = pl.pallas_call(
        functools.partial(_ffn_kernel, n_cast=len(cast), n_relayout=len(relayout)),
        out_shape=[jax.ShapeDtypeStruct((m, d), F32)] + side_shapes,
        grid=grid,
        in_specs=[
            pl.BlockSpec((tm, d), lambda i, j: (i, 0)),
            pl.BlockSpec((1, d), lambda i, j: (0, 0)),
            pl.BlockSpec((d, tf), lambda i, j: (0, j)),
            pl.BlockSpec((d, tf), lambda i, j: (0, j)),
            pl.BlockSpec((tf, d), lambda i, j: (j, 0)),
        ] + side_in,
        out_specs=[pl.BlockSpec((tm, d), lambda i, j: (i, 0))] + side_out,
        scratch_shapes=[pltpu.VMEM((tm, d), BF16)],
        compiler_params=_cparams(("parallel", "arbitrary")),
        name="ffn",
    )(x, nw, wg, wu, wd, *cast, *relayout)
    return out[0], tuple(out[1:1 + len(cast)]), tuple(out[1 + len(cast):])


P_GATES = 0
P_RQ = 6144
P_RK = 7168
P_RV = 8192
P_RG = 9216
P_Z = 10240
P_XS = 11264
P_BC = 12288
P_KPE = 12800
P_QLAT = 13056
P_CKV = 13824
P_TOTAL = 14336


def _inproj_kernel(x_ref, nw_ref, w_ref, wdt_ref, *refs):
    n_cast = (len(refs) - 3) // 2
    cast_in, (o_ref, dt_ref), cast_out, xn_ref = (refs[:n_cast], refs[n_cast:n_cast + 2],
                                                  refs[n_cast + 2:2 * n_cast + 2], refs[-1])
    j = pl.program_id(1)
    d = x_ref.shape[-1]

    @pl.when(j == 0)
    def _():
        _cast_blocks(cast_in, cast_out)
        x = x_ref[...]
        xn = (x * _rms(x, d) * nw_ref[...]).astype(BF16)
        xn_ref[...] = xn
        dt_ref[...] = _dot(xn, wdt_ref[...])
        o_ref[...] = _dot(xn, w_ref[...]).astype(BF16)

    @pl.when(j > 0)
    def _():
        _cast_blocks(cast_in, cast_out)
        o_ref[...] = _dot(xn_ref[...], w_ref[...]).astype(BF16)


def _inproj(x, nw, w, *, cast=(), side_layer=0, tm=1024, tn=2048):
    m, d = x.shape
    n = P_TOTAL
    tm = min(tm, m)
    grid = (m // tm, n // tn)
    cast_in, cast_shapes, cast_out = _side_specs(cast, side_layer, grid)
    out = pl.pallas_call(
        _inproj_kernel,
        out_shape=[jax.ShapeDtypeStruct((m, n), BF16),
                   jax.ShapeDtypeStruct((m, LANES), F32)] + cast_shapes,
        grid=grid,
        in_specs=[
            pl.BlockSpec((tm, d), lambda i, j: (i, 0)),
            pl.BlockSpec((1, d), lambda i, j: (0, 0)),
            pl.BlockSpec((d, tn), lambda i, j: (0, j)),
            pl.BlockSpec((d, LANES), lambda i, j: (0, P_TOTAL // LANES)),
        ] + cast_in,
        out_specs=[
            pl.BlockSpec((tm, tn), lambda i, j: (i, j)),
            pl.BlockSpec((tm, LANES), lambda i, j: (i, 0)),
        ] + cast_out,
        scratch_shapes=[pltpu.VMEM((tm, d), BF16)],
        compiler_params=_cparams(("parallel", "arbitrary")),
        name="inproj",
    )(x, nw, w, w, *cast)
    return out[0], out[1], tuple(out[2:])


_W_IN_SEGMENTS = (
    (P_GATES, 8016, 6144), (P_RQ, 3920, 1024), (P_RK, 4944, 1024), (P_RV, 5968, 1024),
    (P_RG, 6992, 1024), (P_Z, 0, 1024), (P_XS, 1024, 1536), (P_KPE, 3856, MLA_ROPE),
    (P_QLAT, 2576, MLA_Q_LORA), (P_CKV, 3344, MLA_KV_LORA), (P_TOTAL, 2560, SSM_HEADS))
W_IN_OUT = P_TOTAL + LANES


def _w_in_relayout(x_ref, o_ref):
    n_src = x_ref.shape[1]
    segs = sorted(_W_IN_SEGMENTS)
    for k, (dest, _, width) in enumerate(segs):
        full = dest + width // LANES * LANES
        nxt = segs[k + 1][0] if k + 1 < len(segs) else o_ref.shape[1]
        if nxt > full:
            o_ref[:, full:nxt] = jnp.zeros((o_ref.shape[0], nxt - full), BF16)
    for dest, source, width in segs:
        lo = source // LANES * LANES
        hi = min(-(-(source + width) // LANES) * LANES, n_src)
        o_ref[:, dest:dest + width] = x_ref[:, lo:hi][:, source - lo:source - lo + width]


def _split3(x):
    hi = x.astype(BF16)
    r1 = x - hi.astype(F32)
    mid = r1.astype(BF16)
    lo = (r1 - mid.astype(F32)).astype(BF16)
    return hi, mid, lo


def _ssd_constants(L):
    t = np.arange(L)
    shifts = np.stack([(t[:, None] - t[None, :] == j) for j in range(1, SSM_CONV)])
    tril = t[:, None] >= t[None, :]
    expand = np.zeros((LANES, SSM_D_INNER), bool)
    for h in range(SSM_HEADS):
        expand[h, h * SSM_HEAD_DIM:(h + 1) * SSM_HEAD_DIM] = True
    expand2 = np.concatenate([expand, expand])
    expand4 = np.concatenate([expand, expand, expand, np.zeros_like(expand)])
    return tuple(jnp.asarray(m, BF16) for m in (shifts, tril, expand2, expand4))


def _ssd_kernel(xs_ref, bc_ref, z_ref, dt_ref, cwx_ref, cbx_ref, cwb_ref, cbb_ref,
                dtb_ref, alog_ref, dskip_ref, nw_ref, shift_ref, tril_ref, e2_ref, e4_ref,
                o_ref, xtail_ref, btail_ref, state_ref):
    L = xs_ref.shape[0]
    P = SSM_HEAD_DIM
    N = SSM_STATE
    HG = SSM_HEADS // SSM_GROUPS
    GW = HG * P
    T = 8

    @pl.when(pl.program_id(1) == 0)
    def _():
        xtail_ref[0:T, :] = jnp.zeros((T, xtail_ref.shape[1]), F32)
        btail_ref[0:T, :] = jnp.zeros((T, btail_ref.shape[1]), F32)
        state_ref[...] = jnp.zeros(state_ref.shape, F32)

    def conv_silu(tail_ref, in_ref, w_ref, b_ref):
        x16 = in_ref[...]
        xf = x16.astype(F32)
        acc = b_ref[...] + w_ref[SSM_CONV - 1:SSM_CONV, :] * xf
        for j in range(1, SSM_CONV):
            acc = acc + w_ref[SSM_CONV - 1 - j:SSM_CONV - j, :] * _dot(shift_ref[j - 1], x16)
        tail_ref[T:2 * T, :] = xf[0:T, :]
        head = b_ref[...] + w_ref[0:1, :] * tail_ref[T - 3:2 * T - 3, :]
        for j in range(1, SSM_CONV):
            head = head + w_ref[j:j + 1, :] * tail_ref[T - 3 + j:2 * T - 3 + j, :]
        tail_ref[0:T, :] = xf[L - T:L, :]
        return _silu(jnp.concatenate([head, acc[T:, :]], axis=0))

    xc = conv_silu(xtail_ref, xs_ref, cwx_ref, cbx_ref)
    bcc = conv_silu(btail_ref, bc_ref, cwb_ref, cbb_ref)

    dtr = dt_ref[...] + dtb_ref[...]
    dt = jnp.maximum(dtr, 0.0) + jnp.log1p(jnp.exp(-jnp.abs(dtr)))
    adt = dt * (-jnp.exp(alog_ref[...]))
    tril = tril_ref[...]
    hi, mid, lo = _split3(adt)
    acs = _dot(tril, hi) + _dot(tril, mid) + _dot(tril, lo)
    acs_t = acs.T

    hi, mid, _ = _split3(dt)
    dtx = _dot(jnp.concatenate([hi, mid], axis=1), e2_ref[...])
    hi, mid, lo = _split3(acs)
    ax = _dot(jnp.concatenate([hi, mid, lo, jnp.zeros_like(lo)], axis=1), e4_ref[...])
    a_last = ax[L - 1:L, :]
    xd = xc * dtx
    xd16 = xd.astype(BF16)
    xdd16 = (xd * jnp.exp(a_last - ax)).astype(BF16)
    e_ax = jnp.exp(ax)
    e_last = jnp.exp(a_last)

    causal = (lax.broadcasted_iota(jnp.int32, (L, L), 0)
              >= lax.broadcasted_iota(jnp.int32, (L, L), 1))
    first_half = lax.broadcasted_iota(jnp.int32, (L, LANES), 1) < P
    tiles = []
    for g in range(SSM_GROUPS):
        bm = bcc[:, g * N:(g + 1) * N]
        cm16 = bcc[:, (SSM_GROUPS + g) * N:(SSM_GROUPS + g + 1) * N].astype(BF16)
        cb = _dot_nt(cm16, bm.astype(BF16))
        gs = slice(g * GW, (g + 1) * GW)
        prev = state_ref[g]
        y_off = _dot(cm16, prev.astype(BF16)) * e_ax[:, gs]
        state_ref[g] = prev * e_last[:, gs] + _dot(bm.T.astype(BF16), xdd16[:, gs])
        for t in range(GW // LANES):
            ts = slice(g * GW + t * LANES, g * GW + (t + 1) * LANES)
            pair = []
            for k in range(LANES // P):
                h = (g * GW + t * LANES) // P + k
                seg = acs[:, h:h + 1] - acs_t[h:h + 1, :]
                decay = jnp.exp(jnp.where(causal, seg, -jnp.inf))
                pair.append(_dot((cb * decay).astype(BF16), xd16[:, ts]))
            tiles.append(jnp.where(first_half, pair[0], pair[1])
                         + y_off[:, t * LANES:(t + 1) * LANES])

    y = jnp.concatenate(tiles, axis=1)
    y = (y + dskip_ref[...] * xc) * _silu(z_ref[...].astype(F32))
    for g in range(SSM_GROUPS):
        gs = slice(g * GW, (g + 1) * GW)
        yg = y[:, gs]
        o_ref[:, gs] = (yg * _rms(yg, GW) * nw_ref[:, gs]).astype(BF16)


def _ssd(proj, dt_raw, cwx, cbx, cwb, cbb, dtb, alog, dskip, nw, *, batch, seq):
    L = math.gcd(seq, SSD_CHUNK)
    nc = seq // L
    di = SSM_D_INNER
    bcw = 2 * SSM_GROUPS * SSM_STATE
    tok = lambda b, c: b * nc + c
    consts = _ssd_constants(L)
    params = (cwx, cbx, cwb, cbb, dtb, alog, dskip, nw) + consts
    return pl.pallas_call(
        _ssd_kernel,
        out_shape=jax.ShapeDtypeStruct((batch * seq, di), BF16),
        grid=(batch, nc),
        in_specs=[
            pl.BlockSpec((L, di), lambda b, c: (tok(b, c), P_XS // di)),
            pl.BlockSpec((L, bcw), lambda b, c: (tok(b, c), P_BC // bcw)),
            pl.BlockSpec((L, di), lambda b, c: (tok(b, c), P_Z // di)),
            pl.BlockSpec((L, LANES), lambda b, c: (tok(b, c), 0)),
        ] + [_resident(p.shape) for p in params],
        out_specs=pl.BlockSpec((L, di), lambda b, c: (tok(b, c), 0)),
        scratch_shapes=[
            pltpu.VMEM((16, di), F32),
            pltpu.VMEM((16, bcw), F32),
            pltpu.VMEM((SSM_GROUPS, SSM_STATE, di // SSM_GROUPS), F32),
        ],
        compiler_params=_cparams(("parallel", "arbitrary")),
        name="ssd",
    )(proj, proj, proj, dt_raw, *params)


def _rope_tile(x, c, s1, s2):
    q = MLA_ROPE // 2
    return x * c + pltpu.roll(x, LANES - q, 1) * s1 + pltpu.roll(x, q, 1) * s2


def _mla_prep_kernel(ql_ref, ckv_ref, kpe_ref, c_ref, s1_ref, s2_ref, qan_ref, kvan_ref,
                     qn_ref, kn_ref, wq_ref, wkv_ref, q_out, k_out, v_out, *, q_scale):
    ql = ql_ref[...].astype(F32)
    qa = (ql * _rms(ql, MLA_Q_LORA) * qan_ref[...]).astype(BF16)
    ckv = ckv_ref[...].astype(F32)
    kva = (ckv * _rms(ckv, MLA_KV_LORA) * kvan_ref[...]).astype(BF16)
    kpe = kpe_ref[...].astype(F32)
    kpe_sq = kpe * kpe
    c, s1, s2 = c_ref[...], s1_ref[...], s2_ref[...]
    qn, kn = qn_ref[...] * q_scale, kn_ref[...]
    kpe_rot = _rope_tile(kpe * kn[:, MLA_NOPE:], c, s1, s2)
    ones = jnp.ones((kpe.shape[0], MLA_V), BF16)
    tile_sum = jnp.where(
        (lax.broadcasted_iota(jnp.int32, (2 * LANES, 2 * LANES), 0) < LANES)
        == (lax.broadcasted_iota(jnp.int32, (2 * LANES, 2 * LANES), 1) < LANES),
        1.0, 0.0).astype(BF16)

    def inv_rms_pair(sq_a, sq_b):
        ss = _dot(jnp.concatenate([sq_a, sq_b], axis=1).astype(BF16), tile_sum)
        r = lax.rsqrt(ss * (1.0 / MLA_QK_DIM) + NORM_EPS)
        return r[:, :LANES], r[:, LANES:]

    def project(h):
        return [(_dot(qa, wq_ref[h + i]), _dot(kva, wkv_ref[h + i])) for i in range(2)]

    nxt = project(0)
    for h0 in range(0, MLA_HEADS, 2):
        cur = nxt
        if h0 + 2 < MLA_HEADS:
            nxt = project(h0 + 2)
        q_sq = [qh * qh for qh, _ in cur]
        rq = inv_rms_pair(*(sq[:, :LANES] + sq[:, LANES:] for sq in q_sq))
        rk = inv_rms_pair(*(kvh[:, :MLA_NOPE] * kvh[:, :MLA_NOPE] + kpe_sq for _, kvh in cur))
        for i, (qh, kvh) in enumerate(cur):
            h = h0 + i
            qh = qh * jnp.concatenate([rq[i], rq[i]], axis=1) * qn
            q_out[h, :, 0:MLA_NOPE] = qh[:, 0:MLA_NOPE].astype(BF16)
            q_out[h, :, MLA_NOPE:] = _rope_tile(qh[:, MLA_NOPE:], c, s1, s2).astype(BF16)
            k_out[h, :, 0:MLA_NOPE] = (kvh[:, 0:MLA_NOPE] * rk[i] * kn[:, 0:MLA_NOPE]).astype(BF16)
            k_out[h, :, MLA_NOPE:] = (kpe_rot * rk[i]).astype(BF16)
            v_out[h, :, 0:MLA_V] = kvh[:, MLA_NOPE:].astype(BF16)
            v_out[h, :, MLA_V:] = ones


def _mla_prep(proj, c, s1, s2, qan, kvan, qn, kn, wq, wkv, *, tm=512):
    m = proj.shape[0]
    tm = min(tm, m)
    row = lambda i: (i, 0)
    return pl.pallas_call(
        functools.partial(_mla_prep_kernel, q_scale=MLA_QK_DIM ** -0.5 * math.log2(math.e)),
        out_shape=(
            jax.ShapeDtypeStruct((MLA_HEADS, m, MLA_QK_PAD), BF16),
            jax.ShapeDtypeStruct((MLA_HEADS, m, MLA_QK_PAD), BF16),
            jax.ShapeDtypeStruct((MLA_HEADS, m, 2 * MLA_V), BF16),
        ),
        grid=(m // tm,),
        in_specs=[
            pl.BlockSpec((tm, MLA_Q_LORA), lambda i: (i, P_QLAT // MLA_Q_LORA)),
            pl.BlockSpec((tm, MLA_KV_LORA), lambda i: (i, P_CKV // MLA_KV_LORA)),
            pl.BlockSpec((tm, LANES), lambda i: (i, P_KPE // LANES)),
            pl.BlockSpec((tm, LANES), row), pl.BlockSpec((tm, LANES), row),
            pl.BlockSpec((tm, LANES), row),
            _resident(qan.shape), _resident(kvan.shape), _resident(qn.shape),
            _resident(kn.shape), _resident(wq.shape), _resident(wkv.shape),
        ],
        out_specs=(
            pl.BlockSpec((MLA_HEADS, tm, MLA_QK_PAD), lambda i: (0, i, 0)),
            pl.BlockSpec((MLA_HEADS, tm, MLA_QK_PAD), lambda i: (0, i, 0)),
            pl.BlockSpec((MLA_HEADS, tm, 2 * MLA_V), lambda i: (0, i, 0)),
        ),
        compiler_params=_cparams(("parallel",)),
        name="mla_prep",
    )(proj, proj, proj, c, s1, s2, qan, kvan, qn, kn, wq, wkv)


def _attn_kernel(q_ref, k_ref, v_ref, o_ref, s_ref, acc_ref, *, blk, heads):
    i = pl.program_id(2)

    def scores(j, slot):
        start = pl.multiple_of(j * blk, blk)
        for h in range(heads):
            s_ref[slot, h] = _dot_nt(q_ref[h], k_ref[h, pl.ds(start, blk), :])

    def consume(j, slot, m_old, masked):
        start = pl.multiple_of(j * blk, blk)
        probs = []
        for h, m in enumerate(m_old):
            s = s_ref[slot, h]
            if masked:
                row = lax.broadcasted_iota(jnp.int32, (blk, blk), 0)
                col = lax.broadcasted_iota(jnp.int32, (blk, blk), 1)
                s = jnp.where(row >= col, s, -jnp.inf)
            m_new = jnp.maximum(m, jnp.max(s, axis=-1, keepdims=True))
            probs.append((m_new, jnp.exp2(m - m_new), jnp.exp2(s - m_new).astype(BF16)))
        for h, (_, alpha, p) in enumerate(probs):
            acc_ref[h] = alpha * acc_ref[h] + _dot(p, v_ref[h, pl.ds(start, blk), :])
        return tuple(m_new for m_new, _, _ in probs)

    def finish():
        for h in range(heads):
            acc = acc_ref[h]
            o_ref[:, h * MLA_V:(h + 1) * MLA_V] = (acc[:, :MLA_V] / acc[:, MLA_V:]).astype(BF16)

    def pair(t, m):
        j = 2 * t
        scores(j + 1, 1)
        m = consume(j, 0, m, False)
        scores(j + 2, 0)
        return consume(j + 1, 1, m, False)

    scores(0, 0)
    acc_ref[...] = jnp.zeros(acc_ref.shape, F32)
    m = lax.fori_loop(0, i // 2, pair,
                      tuple(jnp.full((blk, 1), -jnp.inf, F32) for _ in range(heads)))
    last_even = 2 * (i // 2)

    @pl.when(i % 2 == 0)
    def _():
        consume(last_even, 0, m, True)
        finish()

    @pl.when(i % 2 == 1)
    def _():
        scores(last_even + 1, 1)
        consume(last_even + 1, 1, consume(last_even, 0, m, False), True)
        finish()


def _attention(q, k, v, *, batch, seq, blk=512, heads=4):
    blk = min(blk, seq)
    nq = seq // blk
    return pl.pallas_call(
        functools.partial(_attn_kernel, blk=blk, heads=heads),
        out_shape=jax.ShapeDtypeStruct((batch * seq, MLA_HEADS * MLA_V), BF16),
        grid=(batch, MLA_HEADS // heads, nq),
        in_specs=[
            pl.BlockSpec((heads, blk, MLA_QK_PAD), lambda b, h, i: (h, b * nq + i, 0)),
            pl.BlockSpec((heads, seq, MLA_QK_PAD), lambda b, h, i: (h, b, 0)),
            pl.BlockSpec((heads, seq, 2 * MLA_V), lambda b, h, i: (h, b, 0)),
        ],
        out_specs=pl.BlockSpec((blk, heads * MLA_V), lambda b, h, i: (b * nq + i, h)),
        scratch_shapes=[pltpu.VMEM((2, heads, blk, blk), F32),
                        pltpu.VMEM((heads, blk, 2 * MLA_V), F32)],
        compiler_params=_cparams(("parallel", "parallel", "arbitrary")),
        name="mla_attention",
    )(q, k, v)


def _ret_constants(L):
    expo = 5.0 + 7.0 * np.arange(RET_HEADS, dtype=np.float32) / np.float32(RET_HEADS - 1)
    log_gamma = np.log1p(-np.exp2(-expo)).astype(np.float32)[:, None, None]
    pos = np.arange(L, dtype=np.float32)
    rel = pos[:, None] - pos[None, :]
    dmask = np.where(rel >= 0, np.exp(rel * log_gamma), 0.0).astype(np.float32)
    lanes = np.ones((1, 1, RET_HEAD), np.float32)
    q_dec = np.exp((pos + 1.0)[None, :, None] * log_gamma).astype(np.float32) * lanes
    k_dec = np.exp((L - 1.0 - pos)[None, :, None] * log_gamma).astype(np.float32) * lanes
    chunk_dec = [float(v) for v in np.exp(L * log_gamma[:, 0, 0])]
    return jnp.asarray(dmask), jnp.asarray(q_dec), jnp.asarray(k_dec), chunk_dec


def _ret_kernel(rq_ref, rk_ref, rv_ref, rg_ref, cos_ref, sin_ref, nw_ref, dmask_ref, qdec_ref,
                kdec_ref, o_ref, state_ref, *, chunk_dec):
    D = RET_HEAD
    half = D // 2

    @pl.when(pl.program_id(1) == 0)
    def _():
        state_ref[...] = jnp.zeros(state_ref.shape, F32)

    cos, sin = cos_ref[...], sin_ref[...]

    def rope(ref, hs):
        x1 = ref[:, hs.start:hs.start + half].astype(F32)
        x2 = ref[:, hs.start + half:hs.stop].astype(F32)
        return jnp.concatenate([x1 * cos - x2 * sin, x1 * sin + x2 * cos], axis=-1)

    for h in range(RET_HEADS):
        hs = slice(h * D, (h + 1) * D)
        q = rope(rq_ref, hs)
        k = rope(rk_ref, hs) * (RET_HEAD ** -0.5)
        v16 = rv_ref[:, hs]
        q16 = q.astype(BF16)
        scores = _dot_nt(q16, k.astype(BF16)) * dmask_ref[h]
        y = _dot(scores.astype(BF16), v16)
        prev = state_ref[h]
        y = y + _dot(q16, prev.astype(BF16)) * qdec_ref[h]
        kd = k * kdec_ref[h]
        state_ref[h] = prev * chunk_dec[h] + _dot(kd.T.astype(BF16), v16)
        y = y * _rms(y, D) * nw_ref[:, hs]
        o_ref[:, hs] = (_silu(rg_ref[:, hs].astype(F32)) * y).astype(BF16)


def _retention(proj, cos, sin, nw, *, batch, seq):
    L = math.gcd(seq, RET_CHUNK)
    nc = seq // L
    w = RET_HEADS * RET_HEAD
    tok = lambda b, c: b * nc + c
    dmask, q_dec, k_dec, chunk_dec = _ret_constants(L)
    return pl.pallas_call(
        functools.partial(_ret_kernel, chunk_dec=chunk_dec),
        out_shape=jax.ShapeDtypeStruct((batch * seq, w), BF16),
        grid=(batch, nc),
        in_specs=[
            pl.BlockSpec((L, w), lambda b, c: (tok(b, c), P_RQ // w)),
            pl.BlockSpec((L, w), lambda b, c: (tok(b, c), P_RK // w)),
            pl.BlockSpec((L, w), lambda b, c: (tok(b, c), P_RV // w)),
            pl.BlockSpec((L, w), lambda b, c: (tok(b, c), P_RG // w)),
            pl.BlockSpec((L, RET_HEAD // 2), lambda b, c: (tok(b, c), 0)),
            pl.BlockSpec((L, RET_HEAD // 2), lambda b, c: (tok(b, c), 0)),
            _resident(nw.shape), _resident(dmask.shape), _resident(q_dec.shape),
            _resident(k_dec.shape),
        ],
        out_specs=pl.BlockSpec((L, w), lambda b, c: (tok(b, c), 0)),
        scratch_shapes=[pltpu.VMEM((RET_HEADS, RET_HEAD, RET_HEAD), F32)],
        compiler_params=_cparams(("parallel", "arbitrary")),
        name="retention",
    )(proj, proj, proj, proj, cos, sin, nw, dmask, q_dec, k_dec)


def _merge_kernel(ys_ref, ym_ref, yr_ref, g0_ref, g1_ref, g2_ref, gb_ref, x_ref,
                  w0_ref, w1_ref, w2_ref, wo_ref, o_ref):
    d = x_ref.shape[-1]

    def branch(k, y_ref, g_ref, w_ref):
        gate = _sigmoid(g_ref[...].astype(F32) + gb_ref[:, k * d:(k + 1) * d])
        return gate * _dot(y_ref[...], w_ref[...])

    merged = (branch(0, ys_ref, g0_ref, w0_ref) + branch(1, ym_ref, g1_ref, w1_ref)
              + branch(2, yr_ref, g2_ref, w2_ref))
    o_ref[...] = x_ref[...] + _dot(merged.astype(BF16), wo_ref[...])


def _merge(ys, ym, yr, proj, gb, x, w0, w1, w2, wo, *, tm=256):
    m, d = x.shape
    tm = min(tm, m)
    bw = ys.shape[1]
    row = lambda i: (i, 0)
    return pl.pallas_call(
        _merge_kernel,
        out_shape=jax.ShapeDtypeStruct((m, d), F32),
        grid=(m // tm,),
        in_specs=[
            pl.BlockSpec((tm, bw), row), pl.BlockSpec((tm, bw), row), pl.BlockSpec((tm, bw), row),
            pl.BlockSpec((tm, d), lambda i: (i, P_GATES // d)),
            pl.BlockSpec((tm, d), lambda i: (i, P_GATES // d + 1)),
            pl.BlockSpec((tm, d), lambda i: (i, P_GATES // d + 2)),
            _resident(gb.shape),
            pl.BlockSpec((tm, d), row),
            _resident(w0.shape), _resident(w1.shape), _resident(w2.shape), _resident(wo.shape),
        ],
        out_specs=pl.BlockSpec((tm, d), row),
        compiler_params=_cparams(("parallel",)),
        name="merge",
    )(ys, ym, yr, proj, proj, proj, gb, x, w0, w1, w2, wo)


def _rope_angles(positions, dim):
    inv = 1.0 / (ROPE_THETA ** (jnp.arange(0, dim, 2, dtype=F32) / dim))
    return positions.astype(F32).reshape(-1, 1) * inv


def _pad_lanes(v, n):
    return jnp.concatenate([v, jnp.zeros((n - v.shape[0],), v.dtype)]).reshape(1, n)


def kernel(x, positions, ffn1_norm, ffn1_w_gate, ffn1_w_up, ffn1_w_down, mix_norm, w_in, gate_b, conv_w, conv_b, dt_bias, a_log, d_skip, ssm_norm, q_a_norm, w_q_b, kv_a_norm, w_kv_b, q_norm, k_norm, ret_norm, w_br_ssm, w_br_mla, w_br_ret, w_out, ffn2_norm, ffn2_w_gate, ffn2_w_up, ffn2_w_down):
    batch, seq, d = x.shape
    depth = w_in.shape[0]
    x = x.reshape(batch * seq, d)

    ang = _rope_angles(positions, MLA_ROPE)
    cm, sm = jnp.cos(ang), jnp.sin(ang)
    zq = jnp.zeros_like(cm)
    rope_c = jnp.concatenate([cm, cm, zq, zq], axis=1)
    rope_s1 = jnp.concatenate([-sm, zq, zq, zq], axis=1)
    rope_s2 = jnp.concatenate([zq, sm, zq, zq], axis=1)
    ang = _rope_angles(positions, RET_HEAD)
    cos_ret, sin_ret = jnp.cos(ang), jnp.sin(ang)

    w_pad = jnp.pad(w_in.astype(BF16), ((0, 0), (0, 0), (0, -w_in.shape[2] % LANES)))
    ffn1_f32 = (ffn1_w_gate, ffn1_w_up, ffn1_w_down)
    ffn2_f32 = (ffn2_w_gate, ffn2_w_up, ffn2_w_down)
    merge_f32 = (w_br_ssm, w_br_mla, w_br_ret, w_out)
    ffn1 = tuple(w[0].astype(BF16) for w in ffn1_f32)
    row = lambda v: v.reshape(1, -1)
    for l in range(depth):
        x, ffn2, (w16,) = _ffn(x, row(ffn1_norm[l]), *ffn1, cast=ffn2_f32, relayout=(w_pad,),
                               side_layer=l)

        proj, dt_raw, w_merge = _inproj(x, row(mix_norm[l]), w16, cast=merge_f32, side_layer=l)

        y_ssm = _ssd(
            proj, dt_raw,
            conv_w[l][:, :SSM_D_INNER], row(conv_b[l][:SSM_D_INNER]),
            conv_w[l][:, SSM_D_INNER:], row(conv_b[l][SSM_D_INNER:]),
            _pad_lanes(dt_bias[l], LANES), _pad_lanes(a_log[l], LANES),
            row(jnp.repeat(d_skip[l], SSM_HEAD_DIM)), row(ssm_norm[l]),
            batch=batch, seq=seq)

        wq = w_q_b[l].reshape(MLA_Q_LORA, MLA_HEADS, MLA_QK_DIM)
        wq = jnp.pad(wq, ((0, 0), (0, 0), (0, MLA_QK_PAD - MLA_QK_DIM)))
        wq = wq.transpose(1, 0, 2).astype(BF16)
        wkv = w_kv_b[l].reshape(MLA_KV_LORA, MLA_HEADS, MLA_NOPE + MLA_V)
        wkv = wkv.transpose(1, 0, 2).astype(BF16)
        q, k, v = _mla_prep(
            proj, rope_c, rope_s1, rope_s2, row(q_a_norm[l]), row(kv_a_norm[l]),
            _pad_lanes(q_norm[l], MLA_QK_PAD), _pad_lanes(k_norm[l], MLA_QK_PAD), wq, wkv)
        y_mla = _attention(q, k, v, batch=batch, seq=seq)

        y_ret = _retention(proj, cos_ret, sin_ret, row(ret_norm[l]), batch=batch, seq=seq)

        x = _merge(y_ssm, y_mla, y_ret, proj, row(gate_b[l]), x, *w_merge)

        if l + 1 < depth:
            x, ffn1, _ = _ffn(x, row(ffn2_norm[l]), *ffn2, cast=ffn1_f32, side_layer=l + 1)
        else:
            x = _ffn(x, row(ffn2_norm[l]), *ffn2)[0]
    return x.reshape(batch, seq, d)
```

```python
import functools
import math

import jax
import jax.numpy as jnp
import numpy as np
from jax import lax
from jax.experimental import pallas as pl
from jax.experimental.pallas import tpu as pltpu

F32 = jnp.float32
BF16 = jnp.bfloat16

NORM_EPS = 1e-6
ROPE_THETA = 10000.0

SSM_HEADS = 16
SSM_HEAD_DIM = 64
SSM_D_INNER = SSM_HEADS * SSM_HEAD_DIM
SSM_GROUPS = 2
SSM_STATE = 128
SSM_CONV = 4
SSD_CHUNK = 256
MLA_HEADS = 8
MLA_Q_LORA = 768
MLA_KV_LORA = 512
MLA_NOPE = 128
MLA_ROPE = 64
MLA_QK_DIM = MLA_NOPE + MLA_ROPE
MLA_V = 128
RET_HEADS = 4
RET_HEAD = 256
RET_CHUNK = 256
N_BRANCH = 3

LANES = 128
MLA_QK_PAD = MLA_NOPE + LANES
VMEM_LIMIT = 60 * 1024 * 1024


def _cparams(sem):
    return pltpu.CompilerParams(dimension_semantics=sem, vmem_limit_bytes=VMEM_LIMIT)


def _resident(shape):
    nd = len(shape)
    return pl.BlockSpec(shape, lambda *_: (0,) * nd, pipeline_mode=pl.Buffered(1))


BF16_SUBLANES = 16


def _row_blocks(rows, grid):
    steps = grid[0] * grid[1]
    br = next(b for b in range(BF16_SUBLANES, rows + 1, BF16_SUBLANES)
              if rows % b == 0 and rows // b <= steps)
    return br, lambda i, j: jnp.minimum(i * grid[1] + j, rows // br - 1)


def _side_specs(arrays, layer, grid, out_cols=None):
    in_specs, out_shapes, out_specs = [], [], []
    for a in arrays:
        _, rows, cols = a.shape
        br, block = _row_blocks(rows, grid)
        oc = out_cols or cols
        in_specs.append(pl.BlockSpec((None, br, cols),
                                     lambda i, j, block=block: (layer, block(i, j), 0)))
        out_shapes.append(jax.ShapeDtypeStruct((rows, oc), BF16))
        out_specs.append(pl.BlockSpec((br, oc), lambda i, j, block=block: (block(i, j), 0)))
    return in_specs, out_shapes, out_specs


def _cast_blocks(src_refs, dst_refs):
    for src, dst in zip(src_refs, dst_refs):
        dst[...] = src[...].astype(BF16)


def _rms(x, n):
    return lax.rsqrt(jnp.sum(x * x, axis=-1, keepdims=True) * (1.0 / n) + NORM_EPS)


def _sigmoid(x):
    return 0.5 * jnp.tanh(0.5 * x) + 0.5


def _silu(x):
    h = 0.5 * x
    return h * jnp.tanh(h) + h


def _dot(a, b):
    return jnp.dot(a, b, preferred_element_type=F32)


def _dot_nt(a, b):
    return lax.dot_general(a, b, (((1,), (1,)), ((), ())), preferred_element_type=F32)


def _ffn_kernel(x_ref, nw_ref, wg_ref, wu_ref, wd_ref, *refs, n_cast, n_relayout):
    n_side = n_cast + n_relayout
    side_in, o_ref, side_out, xn_ref = (refs[:n_side], refs[n_side],
                                        refs[n_side + 1:2 * n_side + 1], refs[-1])
    d = x_ref.shape[-1]
    j = pl.program_id(1)

    def half_ffn(xn):
        _cast_blocks(side_in[:n_cast], side_out[:n_cast])
        for src, dst in zip(side_in[n_cast:], side_out[n_cast:]):
            _w_in_relayout(src, dst)
        g = _dot(xn, wg_ref[...])
        u = _dot(xn, wu_ref[...])
        return _dot((_silu(g) * (0.5 * u)).astype(BF16), wd_ref[...])

    @pl.when(j == 0)
    def _():
        x = x_ref[...]
        xn = (x * _rms(x, d) * nw_ref[...]).astype(BF16)
        xn_ref[...] = xn
        o_ref[...] = x + half_ffn(xn)

    @pl.when(j > 0)
    def _():
        o_ref[...] += half_ffn(xn_ref[...])


def _ffn(x, nw, wg, wu, wd, *, cast=(), relayout=(), side_layer=0, tm=1024, tf=512):
    m, d = x.shape
    f = wg.shape[1]
    tm = min(tm, m)
    grid = (m // tm, f // tf)
    specs = [_side_specs(cast, side_layer, grid), _side_specs(relayout, side_layer, grid, W_IN_OUT)]
    side_in, side_shapes, side_out = (sum((sp[k] for sp in specs), []) for k in range(3))
    out = pl.pallas_call(
        functools.partial(_ffn_kernel, n_cast=len(cast), n_relayout=len(relayout)),
        out_shape=[jax.ShapeDtypeStruct((m, d), F32)] + side_shapes,
        grid=grid,
        in_specs=[
            pl.BlockSpec((tm, d), lambda i, j: (i, 0)),
            pl.BlockSpec((1, d), lambda i, j: (0, 0)),
            pl.BlockSpec((d, tf), lambda i, j: (0, j)),
            pl.BlockSpec((d, tf), lambda i, j: (0, j)),
            pl.BlockSpec((tf, d), lambda i, j: (j, 0)),
        ] + side_in,
        out_specs=[pl.BlockSpec((tm, d), lambda i, j: (i, 0))] + side_out,
        scratch_shapes=[pltpu.VMEM((tm, d), BF16)],
        compiler_params=_cparams(("parallel", "arbitrary")),
        name="ffn",
    )(x, nw, wg, wu, wd, *cast, *relayout)
    return out[0], tuple(out[1:1 + len(cast)]), tuple(out[1 + len(cast):])


P_GATES = 0
P_RQ = 6144
P_RK = 7168
P_RV = 8192
P_RG = 9216
P_Z = 10240
P_XS = 11264
P_BC = 12288
P_KPE = 12800
P_QLAT = 13056
P_CKV = 13824
P_TOTAL = 14336


def _inproj_kernel(x_ref, nw_ref, w_ref, wdt_ref, *refs):
    n_cast = (len(refs) - 3) // 2
    cast_in, (o_ref, dt_ref), cast_out, xn_ref = (refs[:n_cast], refs[n_cast:n_cast + 2],
                                                  refs[n_cast + 2:2 * n_cast + 2], refs[-1])
    j = pl.program_id(1)
    d = x_ref.shape[-1]

    @pl.when(j == 0)
    def _():
        _cast_blocks(cast_in, cast_out)
        x = x_ref[...]
        xn = (x * _rms(x, d) * nw_ref[...]).astype(BF16)
        xn_ref[...] = xn
        dt_ref[...] = _dot(xn, wdt_ref[...])
        o_ref[...] = _dot(xn, w_ref[...]).astype(BF16)

    @pl.when(j > 0)
    def _():
        _cast_blocks(cast_in, cast_out)
        o_ref[...] = _dot(xn_ref[...], w_ref[...]).astype(BF16)


def _inproj(x, nw, w, *, cast=(), side_layer=0, tm=1024, tn=2048):
    m, d = x.shape
    n = P_TOTAL
    tm = min(tm, m)
    grid = (m // tm, n // tn)
    cast_in, cast_shapes, cast_out = _side_specs(cast, side_layer, grid)
    out = pl.pallas_call(
        _inproj_kernel,
        out_shape=[jax.ShapeDtypeStruct((m, n), BF16),
                   jax.ShapeDtypeStruct((m, LANES), F32)] + cast_shapes,
        grid=grid,
        in_specs=[
            pl.BlockSpec((tm, d), lambda i, j: (i, 0)),
            pl.BlockSpec((1, d), lambda i, j: (0, 0)),
            pl.BlockSpec((d, tn), lambda i, j: (0, j)),
            pl.BlockSpec((d, LANES), lambda i, j: (0, P_TOTAL // LANES)),
        ] + cast_in,
        out_specs=[
            pl.BlockSpec((tm, tn), lambda i, j: (i, j)),
            pl.BlockSpec((tm, LANES), lambda i, j: (i, 0)),
        ] + cast_out,
        scratch_shapes=[pltpu.VMEM((tm, d), BF16)],
        compiler_params=_cparams(("parallel", "arbitrary")),
        name="inproj",
    )(x, nw, w, w, *cast)
    return out[0], out[1], tuple(out[2:])


_W_IN_SEGMENTS = (
    (P_GATES, 8016, 6144), (P_RQ, 3920, 1024), (P_RK, 4944, 1024), (P_RV, 5968, 1024),
    (P_RG, 6992, 1024), (P_Z, 0, 1024), (P_XS, 1024, 1536), (P_KPE, 3856, MLA_ROPE),
    (P_QLAT, 2576, MLA_Q_LORA), (P_CKV, 3344, MLA_KV_LORA), (P_TOTAL, 2560, SSM_HEADS))
W_IN_OUT = P_TOTAL + LANES


def _w_in_relayout(x_ref, o_ref):
    n_src = x_ref.shape[1]
    segs = sorted(_W_IN_SEGMENTS)
    for k, (dest, _, width) in enumerate(segs):
        full = dest + width // LANES * LANES
        nxt = segs[k + 1][0] if k + 1 < len(segs) else o_ref.shape[1]
        if nxt > full:
            o_ref[:, full:nxt] = jnp.zeros((o_ref.shape[0], nxt - full), BF16)
    for dest, source, width in segs:
        lo = source // LANES * LANES
        hi = min(-(-(source + width) // LANES) * LANES, n_src)
        o_ref[:, dest:dest + width] = x_ref[:, lo:hi][:, source - lo:source - lo + width]


def _split3(x):
    hi = x.astype(BF16)
    r1 = x - hi.astype(F32)
    mid = r1.astype(BF16)
    lo = (r1 - mid.astype(F32)).astype(BF16)
    return hi, mid, lo


def _ssd_constants(L):
    t = np.arange(L)
    shifts = np.stack([(t[:, None] - t[None, :] == j) for j in range(1, SSM_CONV)])
    tril = t[:, None] >= t[None, :]
    expand = np.zeros((LANES, SSM_D_INNER), bool)
    for h in range(SSM_HEADS):
        expand[h, h * SSM_HEAD_DIM:(h + 1) * SSM_HEAD_DIM] = True
    expand2 = np.concatenate([expand, expand])
    expand4 = np.concatenate([expand, expand, expand, np.zeros_like(expand)])
    return tuple(jnp.asarray(m, BF16) for m in (shifts, tril, expand2, expand4))


def _ssd_kernel(xs_ref, bc_ref, z_ref, dt_ref, cwx_ref, cbx_ref, cwb_ref, cbb_ref,
                dtb_ref, alog_ref, dskip_ref, nw_ref, shift_ref, tril_ref, e2_ref, e4_ref,
                o_ref, xtail_ref, btail_ref, state_ref):
    L = xs_ref.shape[0]
    P = SSM_HEAD_DIM
    N = SSM_STATE
    HG = SSM_HEADS // SSM_GROUPS
    GW = HG * P
    T = 8

    @pl.when(pl.program_id(1) == 0)
    def _():
        xtail_ref[0:T, :] = jnp.zeros((T, xtail_ref.shape[1]), F32)
        btail_ref[0:T, :] = jnp.zeros((T, btail_ref.shape[1]), F32)
        state_ref[...] = jnp.zeros(state_ref.shape, F32)

    def conv_silu(tail_ref, in_ref, w_ref, b_ref):
        x16 = in_ref[...]
        xf = x16.astype(F32)
        acc = b_ref[...] + w_ref[SSM_CONV - 1:SSM_CONV, :] * xf
        for j in range(1, SSM_CONV):
            acc = acc + w_ref[SSM_CONV - 1 - j:SSM_CONV - j, :] * _dot(shift_ref[j - 1], x16)
        tail_ref[T:2 * T, :] = xf[0:T, :]
        head = b_ref[...] + w_ref[0:1, :] * tail_ref[T - 3:2 * T - 3, :]
        for j in range(1, SSM_CONV):
            head = head + w_ref[j:j + 1, :] * tail_ref[T - 3 + j:2 * T - 3 + j, :]
        tail_ref[0:T, :] = xf[L - T:L, :]
        return _silu(jnp.concatenate([head, acc[T:, :]], axis=0))

    xc = conv_silu(xtail_ref, xs_ref, cwx_ref, cbx_ref)
    bcc = conv_silu(btail_ref, bc_ref, cwb_ref, cbb_ref)

    dtr = dt_ref[...] + dtb_ref[...]
    dt = jnp.maximum(dtr, 0.0) + jnp.log1p(jnp.exp(-jnp.abs(dtr)))
    adt = dt * (-jnp.exp(alog_ref[...]) * math.log2(math.e))
    tril = tril_ref[...]
    hi, mid, lo = _split3(adt)
    acs = _dot(tril, hi) + _dot(tril, mid) + _dot(tril, lo)
    acs_t = acs.T

    hi, mid, _ = _split3(dt)
    dtx = _dot(jnp.concatenate([hi, mid], axis=1), e2_ref[...])
    hi, mid, lo = _split3(acs)
    ax = _dot(jnp.concatenate([hi, mid, lo, jnp.zeros_like(lo)], axis=1), e4_ref[...])
    a_last = ax[L - 1:L, :]
    xd = xc * dtx
    xd16 = xd.astype(BF16)
    xdd16 = (xd * jnp.exp2(a_last - ax)).astype(BF16)
    e_ax = jnp.exp2(ax)
    e_last = jnp.exp2(a_last)

    causal = (lax.broadcasted_iota(jnp.int32, (L, L), 0)
              >= lax.broadcasted_iota(jnp.int32, (L, L), 1))
    first_half = lax.broadcasted_iota(jnp.int32, (L, LANES), 1) < P
    tiles = []
    for g in range(SSM_GROUPS):
        bm = bcc[:, g * N:(g + 1) * N]
        cm16 = bcc[:, (SSM_GROUPS + g) * N:(SSM_GROUPS + g + 1) * N].astype(BF16)
        cb = _dot_nt(cm16, bm.astype(BF16))
        gs = slice(g * GW, (g + 1) * GW)
        prev = state_ref[g]
        y_off = _dot(cm16, prev.astype(BF16)) * e_ax[:, gs]
        state_ref[g] = prev * e_last[:, gs] + _dot(bm.T.astype(BF16), xdd16[:, gs])
        for t in range(GW // LANES):
            ts = slice(g * GW + t * LANES, g * GW + (t + 1) * LANES)
            pair = []
            for k in range(LANES // P):
                h = (g * GW + t * LANES) // P + k
                seg = acs[:, h:h + 1] - acs_t[h:h + 1, :]
                decay = jnp.exp2(jnp.where(causal, seg, -jnp.inf))
                pair.append(_dot((cb * decay).astype(BF16), xd16[:, ts]))
            tiles.append(jnp.where(first_half, pair[0], pair[1])
                         + y_off[:, t * LANES:(t + 1) * LANES])

    y = jnp.concatenate(tiles, axis=1)
    y = (y + dskip_ref[...] * xc) * _silu(z_ref[...].astype(F32))
    for g in range(SSM_GROUPS):
        gs = slice(g * GW, (g + 1) * GW)
        yg = y[:, gs]
        o_ref[:, gs] = (yg * _rms(yg, GW) * nw_ref[:, gs]).astype(BF16)


def _ssd(proj, dt_raw, cwx, cbx, cwb, cbb, dtb, alog, dskip, nw, *, batch, seq):
    L = math.gcd(seq, SSD_CHUNK)
    nc = seq // L
    di = SSM_D_INNER
    bcw = 2 * SSM_GROUPS * SSM_STATE
    tok = lambda b, c: b * nc + c
    consts = _ssd_constants(L)
    params = (cwx, cbx, cwb, cbb, dtb, alog, dskip, nw) + consts
    return pl.pallas_call(
        _ssd_kernel,
        out_shape=jax.ShapeDtypeStruct((batch * seq, di), BF16),
        grid=(batch, nc),
        in_specs=[
            pl.BlockSpec((L, di), lambda b, c: (tok(b, c), P_XS // di)),
            pl.BlockSpec((L, bcw), lambda b, c: (tok(b, c), P_BC // bcw)),
            pl.BlockSpec((L, di), lambda b, c: (tok(b, c), P_Z // di)),
            pl.BlockSpec((L, LANES), lambda b, c: (tok(b, c), 0)),
        ] + [_resident(p.shape) for p in params],
        out_specs=pl.BlockSpec((L, di), lambda b, c: (tok(b, c), 0)),
        scratch_shapes=[
            pltpu.VMEM((16, di), F32),
            pltpu.VMEM((16, bcw), F32),
            pltpu.VMEM((SSM_GROUPS, SSM_STATE, di // SSM_GROUPS), F32),
        ],
        compiler_params=_cparams(("parallel", "arbitrary")),
        name="ssd",
    )(proj, proj, proj, dt_raw, *params)


def _rope_tile(x, c, s1, s2):
    q = MLA_ROPE // 2
    return x * c + pltpu.roll(x, LANES - q, 1) * s1 + pltpu.roll(x, q, 1) * s2


def _mla_prep_kernel(ql_ref, ckv_ref, kpe_ref, c_ref, s1_ref, s2_ref, qan_ref, kvan_ref,
                     qn_ref, kn_ref, wq_ref, wkv_ref, q_out, k_out, v_out, *, q_scale):
    ql = ql_ref[...].astype(F32)
    qa = (ql * _rms(ql, MLA_Q_LORA) * qan_ref[...]).astype(BF16)
    ckv = ckv_ref[...].astype(F32)
    kva = (ckv * _rms(ckv, MLA_KV_LORA) * kvan_ref[...]).astype(BF16)
    kpe = kpe_ref[...].astype(F32)
    kpe_sq = kpe * kpe
    c, s1, s2 = c_ref[...], s1_ref[...], s2_ref[...]
    qn, kn = qn_ref[...] * q_scale, kn_ref[...]
    kpe_rot = _rope_tile(kpe * kn[:, MLA_NOPE:], c, s1, s2)
    ones = jnp.ones((kpe.shape[0], MLA_V), BF16)
    tile_sum = jnp.where(
        (lax.broadcasted_iota(jnp.int32, (2 * LANES, 2 * LANES), 0) < LANES)
        == (lax.broadcasted_iota(jnp.int32, (2 * LANES, 2 * LANES), 1) < LANES),
        1.0, 0.0).astype(BF16)

    def inv_rms_pair(sq_a, sq_b):
        ss = _dot(jnp.concatenate([sq_a, sq_b], axis=1).astype(BF16), tile_sum)
        r = lax.rsqrt(ss * (1.0 / MLA_QK_DIM) + NORM_EPS)
        return r[:, :LANES], r[:, LANES:]

    def project(h):
        return [(_dot(qa, wq_ref[h + i]), _dot(kva, wkv_ref[h + i])) for i in range(2)]

    nxt = project(0)
    for h0 in range(0, MLA_HEADS, 2):
        cur = nxt
        if h0 + 2 < MLA_HEADS:
            nxt = project(h0 + 2)
        q_sq = [qh * qh for qh, _ in cur]
        rq = inv_rms_pair(*(sq[:, :LANES] + sq[:, LANES:] for sq in q_sq))
        rk = inv_rms_pair(*(kvh[:, :MLA_NOPE] * kvh[:, :MLA_NOPE] + kpe_sq for _, kvh in cur))
        for i, (qh, kvh) in enumerate(cur):
            h = h0 + i
            qh = qh * jnp.concatenate([rq[i], rq[i]], axis=1) * qn
            q_out[h, :, 0:MLA_NOPE] = qh[:, 0:MLA_NOPE].astype(BF16)
            q_out[h, :, MLA_NOPE:] = _rope_tile(qh[:, MLA_NOPE:], c, s1, s2).astype(BF16)
            k_out[h, :, 0:MLA_NOPE] = (kvh[:, 0:MLA_NOPE] * rk[i] * kn[:, 0:MLA_NOPE]).astype(BF16)
            k_out[h, :, MLA_NOPE:] = (kpe_rot * rk[i]).astype(BF16)
            v_out[h, :, 0:MLA_V] = kvh[:, MLA_NOPE:].astype(BF16)
            v_out[h, :, MLA_V:] = ones


def _mla_prep(proj, c, s1, s2, qan, kvan, qn, kn, wq, wkv, *, tm=512):
    m = proj.shape[0]
    tm = min(tm, m)
    row = lambda i: (i, 0)
    return pl.pallas_call(
        functools.partial(_mla_prep_kernel, q_scale=MLA_QK_DIM ** -0.5 * math.log2(math.e)),
        out_shape=(
            jax.ShapeDtypeStruct((MLA_HEADS, m, MLA_QK_PAD), BF16),
            jax.ShapeDtypeStruct((MLA_HEADS, m, MLA_QK_PAD), BF16),
            jax.ShapeDtypeStruct((MLA_HEADS, m, 2 * MLA_V), BF16),
        ),
        grid=(m // tm,),
        in_specs=[
            pl.BlockSpec((tm, MLA_Q_LORA), lambda i: (i, P_QLAT // MLA_Q_LORA)),
            pl.BlockSpec((tm, MLA_KV_LORA), lambda i: (i, P_CKV // MLA_KV_LORA)),
            pl.BlockSpec((tm, LANES), lambda i: (i, P_KPE // LANES)),
            pl.BlockSpec((tm, LANES), row), pl.BlockSpec((tm, LANES), row),
            pl.BlockSpec((tm, LANES), row),
            _resident(qan.shape), _resident(kvan.shape), _resident(qn.shape),
            _resident(kn.shape), _resident(wq.shape), _resident(wkv.shape),
        ],
        out_specs=(
            pl.BlockSpec((MLA_HEADS, tm, MLA_QK_PAD), lambda i: (0, i, 0)),
            pl.BlockSpec((MLA_HEADS, tm, MLA_QK_PAD), lambda i: (0, i, 0)),
            pl.BlockSpec((MLA_HEADS, tm, 2 * MLA_V), lambda i: (0, i, 0)),
        ),
        compiler_params=_cparams(("parallel",)),
        name="mla_prep",
    )(proj, proj, proj, c, s1, s2, qan, kvan, qn, kn, wq, wkv)


def _attn_kernel(q_ref, k_ref, v_ref, o_ref, s_ref, acc_ref, *, blk, heads):
    i = pl.program_id(2)

    def scores(j, slot):
        start = pl.multiple_of(j * blk, blk)
        for h in range(heads):
            s_ref[slot, h] = _dot_nt(q_ref[h], k_ref[h, pl.ds(start, blk), :])

    def consume(j, slot, m_old, masked):
        start = pl.multiple_of(j * blk, blk)
        probs = []
        for h, m in enumerate(m_old):
            s = s_ref[slot, h]
            if masked:
                row = lax.broadcasted_iota(jnp.int32, (blk, blk), 0)
                col = lax.broadcasted_iota(jnp.int32, (blk, blk), 1)
                s = jnp.where(row >= col, s, -jnp.inf)
            m_new = jnp.maximum(m, jnp.max(s, axis=-1, keepdims=True))
            probs.append((m_new, jnp.exp2(m - m_new), jnp.exp2(s - m_new).astype(BF16)))
        for h, (_, alpha, p) in enumerate(probs):
            acc_ref[h] = alpha * acc_ref[h] + _dot(p, v_ref[h, pl.ds(start, blk), :])
        return tuple(m_new for m_new, _, _ in probs)

    def finish():
        for h in range(heads):
            acc = acc_ref[h]
            o_ref[:, h * MLA_V:(h + 1) * MLA_V] = (acc[:, :MLA_V] / acc[:, MLA_V:]).astype(BF16)

    def pair(t, m):
        j = 2 * t
        scores(j + 1, 1)
        m = consume(j, 0, m, False)
        scores(j + 2, 0)
        return consume(j + 1, 1, m, False)

    scores(0, 0)
    acc_ref[...] = jnp.zeros(acc_ref.shape, F32)
    m = lax.fori_loop(0, i // 2, pair,
                      tuple(jnp.full((blk, 1), -jnp.inf, F32) for _ in range(heads)))
    last_even = 2 * (i // 2)

    @pl.when(i % 2 == 0)
    def _():
        consume(last_even, 0, m, True)
        finish()

    @pl.when(i % 2 == 1)
    def _():
        scores(last_even + 1, 1)
        consume(last_even + 1, 1, consume(last_even, 0, m, False), True)
        finish()


def _attention(q, k, v, *, batch, seq, blk=512, heads=4):
    blk = min(blk, seq)
    nq = seq // blk
    return pl.pallas_call(
        functools.partial(_attn_kernel, blk=blk, heads=heads),
        out_shape=jax.ShapeDtypeStruct((batch * seq, MLA_HEADS * MLA_V), BF16),
        grid=(batch, MLA_HEADS // heads, nq),
        in_specs=[
            pl.BlockSpec((heads, blk, MLA_QK_PAD), lambda b, h, i: (h, b * nq + i, 0)),
            pl.BlockSpec((heads, seq, MLA_QK_PAD), lambda b, h, i: (h, b, 0)),
            pl.BlockSpec((heads, seq, 2 * MLA_V), lambda b, h, i: (h, b, 0)),
        ],
        out_specs=pl.BlockSpec((blk, heads * MLA_V), lambda b, h, i: (b * nq + i, h)),
        scratch_shapes=[pltpu.VMEM((2, heads, blk, blk), F32),
                        pltpu.VMEM((heads, blk, 2 * MLA_V), F32)],
        compiler_params=_cparams(("parallel", "parallel", "arbitrary")),
        name="mla_attention",
    )(q, k, v)


def _ret_constants(L):
    expo = 5.0 + 7.0 * np.arange(RET_HEADS, dtype=np.float32) / np.float32(RET_HEADS - 1)
    log_gamma = np.log1p(-np.exp2(-expo)).astype(np.float32)[:, None, None]
    pos = np.arange(L, dtype=np.float32)
    rel = pos[:, None] - pos[None, :]
    dmask = np.where(rel >= 0, np.exp(rel * log_gamma), 0.0).astype(np.float32)
    lanes = np.ones((1, 1, RET_HEAD), np.float32)
    q_dec = np.exp((pos + 1.0)[None, :, None] * log_gamma).astype(np.float32) * lanes
    k_dec = np.exp((L - 1.0 - pos)[None, :, None] * log_gamma).astype(np.float32) * lanes
    chunk_dec = [float(v) for v in np.exp(L * log_gamma[:, 0, 0])]
    return jnp.asarray(dmask), jnp.asarray(q_dec), jnp.asarray(k_dec), chunk_dec


def _ret_kernel(rq_ref, rk_ref, rv_ref, rg_ref, cos_ref, sin_ref, nw_ref, dmask_ref, qdec_ref,
                kdec_ref, o_ref, state_ref, *, chunk_dec):
    D = RET_HEAD
    half = D // 2

    @pl.when(pl.program_id(1) == 0)
    def _():
        state_ref[...] = jnp.zeros(state_ref.shape, F32)

    cos, sin = cos_ref[...], sin_ref[...]

    def rope(ref, hs):
        x1 = ref[:, hs.start:hs.start + half].astype(F32)
        x2 = ref[:, hs.start + half:hs.stop].astype(F32)
        return jnp.concatenate([x1 * cos - x2 * sin, x1 * sin + x2 * cos], axis=-1)

    for h in range(RET_HEADS):
        hs = slice(h * D, (h + 1) * D)
        q = rope(rq_ref, hs)
        k = rope(rk_ref, hs) * (RET_HEAD ** -0.5)
        v16 = rv_ref[:, hs]
        q16 = q.astype(BF16)
        scores = _dot_nt(q16, k.astype(BF16)) * dmask_ref[h]
        y = _dot(scores.astype(BF16), v16)
        prev = state_ref[h]
        y = y + _dot(q16, prev.astype(BF16)) * qdec_ref[h]
        kd = k * kdec_ref[h]
        state_ref[h] = prev * chunk_dec[h] + _dot(kd.T.astype(BF16), v16)
        y = y * _rms(y, D) * nw_ref[:, hs]
        o_ref[:, hs] = (_silu(rg_ref[:, hs].astype(F32)) * y).astype(BF16)


def _retention(proj, cos, sin, nw, *, batch, seq):
    L = math.gcd(seq, RET_CHUNK)
    nc = seq // L
    w = RET_HEADS * RET_HEAD
    tok = lambda b, c: b * nc + c
    dmask, q_dec, k_dec, chunk_dec = _ret_constants(L)
    return pl.pallas_call(
        functools.partial(_ret_kernel, chunk_dec=chunk_dec),
        out_shape=jax.ShapeDtypeStruct((batch * seq, w), BF16),
        grid=(batch, nc),
        in_specs=[
            pl.BlockSpec((L, w), lambda b, c: (tok(b, c), P_RQ // w)),
            pl.BlockSpec((L, w), lambda b, c: (tok(b, c), P_RK // w)),
            pl.BlockSpec((L, w), lambda b, c: (tok(b, c), P_RV // w)),
            pl.BlockSpec((L, w), lambda b, c: (tok(b, c), P_RG // w)),
            pl.BlockSpec((L, RET_HEAD // 2), lambda b, c: (tok(b, c), 0)),
            pl.BlockSpec((L, RET_HEAD // 2), lambda b, c: (tok(b, c), 0)),
            _resident(nw.shape), _resident(dmask.shape), _resident(q_dec.shape),
            _resident(k_dec.shape),
        ],
        out_specs=pl.BlockSpec((L, w), lambda b, c: (tok(b, c), 0)),
        scratch_shapes=[pltpu.VMEM((RET_HEADS, RET_HEAD, RET_HEAD), F32)],
        compiler_params=_cparams(("parallel", "arbitrary")),
        name="retention",
    )(proj, proj, proj, proj, cos, sin, nw, dmask, q_dec, k_dec)


def _merge_kernel(ys_ref, ym_ref, yr_ref, g0_ref, g1_ref, g2_ref, gb_ref, x_ref,
                  w0_ref, w1_ref, w2_ref, wo_ref, o_ref):
    d = x_ref.shape[-1]

    def branch(k, y_ref, g_ref, w_ref):
        gate = _sigmoid(g_ref[...].astype(F32) + gb_ref[:, k * d:(k + 1) * d])
        return gate * _dot(y_ref[...], w_ref[...])

    merged = (branch(0, ys_ref, g0_ref, w0_ref) + branch(1, ym_ref, g1_ref, w1_ref)
              + branch(2, yr_ref, g2_ref, w2_ref))
    o_ref[...] = x_ref[...] + _dot(merged.astype(BF16), wo_ref[...])


def _merge(ys, ym, yr, proj, gb, x, w0, w1, w2, wo, *, tm=256):
    m, d = x.shape
    tm = min(tm, m)
    bw = ys.shape[1]
    row = lambda i: (i, 0)
    return pl.pallas_call(
        _merge_kernel,
        out_shape=jax.ShapeDtypeStruct((m, d), F32),
        grid=(m // tm,),
        in_specs=[
            pl.BlockSpec((tm, bw), row), pl.BlockSpec((tm, bw), row), pl.BlockSpec((tm, bw), row),
            pl.BlockSpec((tm, d), lambda i: (i, P_GATES // d)),
            pl.BlockSpec((tm, d), lambda i: (i, P_GATES // d + 1)),
            pl.BlockSpec((tm, d), lambda i: (i, P_GATES // d + 2)),
            _resident(gb.shape),
            pl.BlockSpec((tm, d), row),
            _resident(w0.shape), _resident(w1.shape), _resident(w2.shape), _resident(wo.shape),
        ],
        out_specs=pl.BlockSpec((tm, d), row),
        compiler_params=_cparams(("parallel",)),
        name="merge",
    )(ys, ym, yr, proj, proj, proj, gb, x, w0, w1, w2, wo)


def _rope_angles(positions, dim):
    inv = 1.0 / (ROPE_THETA ** (jnp.arange(0, dim, 2, dtype=F32) / dim))
    return positions.astype(F32).reshape(-1, 1) * inv


def _pad_lanes(v, n):
    return jnp.concatenate([v, jnp.zeros((n - v.shape[0],), v.dtype)]).reshape(1, n)


def kernel(x, positions, ffn1_norm, ffn1_w_gate, ffn1_w_up, ffn1_w_down, mix_norm, w_in, gate_b, conv_w, conv_b, dt_bias, a_log, d_skip, ssm_norm, q_a_norm, w_q_b, kv_a_norm, w_kv_b, q_norm, k_norm, ret_norm, w_br_ssm, w_br_mla, w_br_ret, w_out, ffn2_norm, ffn2_w_gate, ffn2_w_up, ffn2_w_down):
    batch, seq, d = x.shape
    depth = w_in.shape[0]
    x = x.reshape(batch * seq, d)

    ang = _rope_angles(positions, MLA_ROPE)
    cm, sm = jnp.cos(ang), jnp.sin(ang)
    zq = jnp.zeros_like(cm)
    rope_c = jnp.concatenate([cm, cm, zq, zq], axis=1)
    rope_s1 = jnp.concatenate([-sm, zq, zq, zq], axis=1)
    rope_s2 = jnp.concatenate([zq, sm, zq, zq], axis=1)
    ang = _rope_angles(positions, RET_HEAD)
    cos_ret, sin_ret = jnp.cos(ang), jnp.sin(ang)

    w_pad = jnp.pad(w_in.astype(BF16), ((0, 0), (0, 0), (0, -w_in.shape[2] % LANES)))
    ffn1_f32 = (ffn1_w_gate, ffn1_w_up, ffn1_w_down)
    ffn2_f32 = (ffn2_w_gate, ffn2_w_up, ffn2_w_down)
    merge_f32 = (w_br_ssm, w_br_mla, w_br_ret, w_out)
    ffn1 = tuple(w[0].astype(BF16) for w in ffn1_f32)
    row = lambda v: v.reshape(1, -1)
    for l in range(depth):
        x, ffn2, (w16,) = _ffn(x, row(ffn1_norm[l]), *ffn1, cast=ffn2_f32, relayout=(w_pad,),
                               side_layer=l)

        proj, dt_raw, w_merge = _inproj(x, row(mix_norm[l]), w16, cast=merge_f32, side_layer=l)

        y_ssm = _ssd(
            proj, dt_raw,
            conv_w[l][:, :SSM_D_INNER], row(conv_b[l][:SSM_D_INNER]),
            conv_w[l][:, SSM_D_INNER:], row(conv_b[l][SSM_D_INNER:]),
            _pad_lanes(dt_bias[l], LANES), _pad_lanes(a_log[l], LANES),
            row(jnp.repeat(d_skip[l], SSM_HEAD_DIM)), row(ssm_norm[l]),
            batch=batch, seq=seq)

        wq = w_q_b[l].reshape(MLA_Q_LORA, MLA_HEADS, MLA_QK_DIM)
        wq = jnp.pad(wq, ((0, 0), (0, 0), (0, MLA_QK_PAD - MLA_QK_DIM)))
        wq = wq.transpose(1, 0, 2).astype(BF16)
        wkv = w_kv_b[l].reshape(MLA_KV_LORA, MLA_HEADS, MLA_NOPE + MLA_V)
        wkv = wkv.transpose(1, 0, 2).astype(BF16)
        q, k, v = _mla_prep(
            proj, rope_c, rope_s1, rope_s2, row(q_a_norm[l]), row(kv_a_norm[l]),
            _pad_lanes(q_norm[l], MLA_QK_PAD), _pad_lanes(k_norm[l], MLA_QK_PAD), wq, wkv)
        y_mla = _attention(q, k, v, batch=batch, seq=seq)

        y_ret = _retention(proj, cos_ret, sin_ret, row(ret_norm[l]), batch=batch, seq=seq)

        x = _merge(y_ssm, y_mla, y_ret, proj, row(gate_b[l]), x, *w_merge)

        if l + 1 < depth:
            x, ffn1, _ = _ffn(x, row(ffn2_norm[l]), *ffn2, cast=ffn1_f32, side_layer=l + 1)
        else:
            x = _ffn(x, row(ffn2_norm[l]), *ffn2)[0]
    return x.reshape(batch, seq, d)
```

```python
import functools
import math

import jax
import jax.numpy as jnp
import numpy as np
from jax import lax
from jax.experimental import pallas as pl
from jax.experimental.pallas import tpu as pltpu

F32 = jnp.float32
BF16 = jnp.bfloat16

NORM_EPS = 1e-6
ROPE_THETA = 10000.0

SSM_HEADS = 16
SSM_HEAD_DIM = 64
SSM_D_INNER = SSM_HEADS * SSM_HEAD_DIM
SSM_GROUPS = 2
SSM_STATE = 128
SSM_CONV = 4
SSD_CHUNK = 256
MLA_HEADS = 8
MLA_Q_LORA = 768
MLA_KV_LORA = 512
MLA_NOPE = 128
MLA_ROPE = 64
MLA_QK_DIM = MLA_NOPE + MLA_ROPE
MLA_V = 128
RET_HEADS = 4
RET_HEAD = 256
RET_CHUNK = 256
N_BRANCH = 3

LANES = 128
MLA_QK_PAD = MLA_NOPE + LANES
VMEM_LIMIT = 60 * 1024 * 1024


def _cparams(sem):
    return pltpu.CompilerParams(dimension_semantics=sem, vmem_limit_bytes=VMEM_LIMIT)


def _resident(shape):
    nd = len(shape)
    return pl.BlockSpec(shape, lambda *_: (0,) * nd, pipeline_mode=pl.Buffered(1))


BF16_SUBLANES = 16


def _row_blocks(rows, grid):
    steps = grid[0] * grid[1]
    br = next(b for b in range(BF16_SUBLANES, rows + 1, BF16_SUBLANES)
              if rows % b == 0 and rows // b <= steps)
    return br, lambda i, j: jnp.minimum(i * grid[1] + j, rows // br - 1)


def _side_specs(arrays, layer, grid, out_cols=None):
    in_specs, out_shapes, out_specs = [], [], []
    for a in arrays:
        _, rows, cols = a.shape
        br, block = _row_blocks(rows, grid)
        oc = out_cols or cols
        in_specs.append(pl.BlockSpec((None, br, cols),
                                     lambda i, j, block=block: (layer, block(i, j), 0)))
        out_shapes.append(jax.ShapeDtypeStruct((rows, oc), BF16))
        out_specs.append(pl.BlockSpec((br, oc), lambda i, j, block=block: (block(i, j), 0)))
    return in_specs, out_shapes, out_specs


def _cast_blocks(src_refs, dst_refs):
    for src, dst in zip(src_refs, dst_refs):
        dst[...] = src[...].astype(BF16)


def _rms(x, n):
    return lax.rsqrt(jnp.sum(x * x, axis=-1, keepdims=True) * (1.0 / n) + NORM_EPS)


def _sigmoid(x):
    return 0.5 * jnp.tanh(0.5 * x) + 0.5


def _silu(x):
    h = 0.5 * x
    return h * jnp.tanh(h) + h


def _dot(a, b):
    return jnp.dot(a, b, preferred_element_type=F32)


def _dot_nt(a, b):
    return lax.dot_general(a, b, (((1,), (1,)), ((), ())), preferred_element_type=F32)


def _ffn_kernel(x_ref, nw_ref, wg_ref, wu_ref, wd_ref, *refs, n_cast, n_relayout):
    n_side = n_cast + n_relayout
    side_in, o_ref, side_out, xn_ref = (refs[:n_side], refs[n_side],
                                        refs[n_side + 1:2 * n_side + 1], refs[-1])
    d = x_ref.shape[-1]
    j = pl.program_id(1)

    def half_ffn(xn):
        _cast_blocks(side_in[:n_cast], side_out[:n_cast])
        for src, dst in zip(side_in[n_cast:], side_out[n_cast:]):
            _w_in_relayout(src, dst)
        g = _dot(xn, wg_ref[...])
        u = _dot(xn, wu_ref[...])
        return _dot((_silu(g) * (0.5 * u)).astype(BF16), wd_ref[...])

    @pl.when(j == 0)
    def _():
        x = x_ref[...]
        xn = (x * _rms(x, d) * nw_ref[...]).astype(BF16)
        xn_ref[...] = xn
        o_ref[...] = x + half_ffn(xn)

    @pl.when(j > 0)
    def _():
        o_ref[...] += half_ffn(xn_ref[...])


def _ffn(x, nw, wg, wu, wd, *, cast=(), relayout=(), side_layer=0, tm=1024, tf=512):
    m, d = x.shape
    f = wg.shape[1]
    tm = min(tm, m)
    grid = (m // tm, f // tf)
    specs = [_side_specs(cast, side_layer, grid), _side_specs(relayout, side_layer, grid, W_IN_OUT)]
    side_in, side_shapes, side_out = (sum((sp[k] for sp in specs), []) for k in range(3))
    out = pl.pallas_call(
        functools.partial(_ffn_kernel, n_cast=len(cast), n_relayout=len(relayout)),
        out_shape=[jax.ShapeDtypeStruct((m, d), F32)] + side_shapes,
        grid=grid,
        in_specs=[
            pl.BlockSpec((tm, d), lambda i, j: (i, 0)),
            pl.BlockSpec((1, d), lambda i, j: (0, 0)),
            pl.BlockSpec((d, tf), lambda i, j: (0, j)),
            pl.BlockSpec((d, tf), lambda i, j: (0, j)),
            pl.BlockSpec((tf, d), lambda i, j: (j, 0)),
        ] + side_in,
        out_specs=[pl.BlockSpec((tm, d), lambda i, j: (i, 0))] + side_out,
        scratch_shapes=[pltpu.VMEM((tm, d), BF16)],
        compiler_params=_cparams(("parallel", "arbitrary")),
        name="ffn",
    )(x, nw, wg, wu, wd, *cast, *relayout)
    return out[0], tuple(out[1:1 + len(cast)]), tuple(out[1 + len(cast):])


P_GATES = 0
P_RQ = 6144
P_RK = 7168
P_RV = 8192
P_RG = 9216
P_Z = 10240
P_XS = 11264
P_BC = 12288
P_KPE = 12800
P_QLAT = 13056
P_CKV = 13824
P_TOTAL = 14336


def _inproj_kernel(x_ref, nw_ref, w_ref, wdt_ref, *refs):
    n_cast = (len(refs) - 3) // 2
    cast_in, (o_ref, dt_ref), cast_out, xn_ref = (refs[:n_cast], refs[n_cast:n_cast + 2],
                                                  refs[n_cast + 2:2 * n_cast + 2], refs[-1])
    j = pl.program_id(1)
    d = x_ref.shape[-1]

    @pl.when(j == 0)
    def _():
        _cast_blocks(cast_in, cast_out)
        x = x_ref[...]
        xn = (x * _rms(x, d) * nw_ref[...]).astype(BF16)
        xn_ref[...] = xn
        dt_ref[...] = _dot(xn, wdt_ref[...])
        o_ref[...] = _dot(xn, w_ref[...]).astype(BF16)

    @pl.when(j > 0)
    def _():
        _cast_blocks(cast_in, cast_out)
        o_ref[...] = _dot(xn_ref[...], w_ref[...]).astype(BF16)


def _inproj(x, nw, w, *, cast=(), side_layer=0, tm=1024, tn=2048):
    m, d = x.shape
    n = P_TOTAL
    tm = min(tm, m)
    grid = (m // tm, n // tn)
    cast_in, cast_shapes, cast_out = _side_specs(cast, side_layer, grid)
    out = pl.pallas_call(
        _inproj_kernel,
        out_shape=[jax.ShapeDtypeStruct((m, n), BF16),
                   jax.ShapeDtypeStruct((m, LANES), F32)] + cast_shapes,
        grid=grid,
        in_specs=[
            pl.BlockSpec((tm, d), lambda i, j: (i, 0)),
            pl.BlockSpec((1, d), lambda i, j: (0, 0)),
            pl.BlockSpec((d, tn), lambda i, j: (0, j)),
            pl.BlockSpec((d, LANES), lambda i, j: (0, P_TOTAL // LANES)),
        ] + cast_in,
        out_specs=[
            pl.BlockSpec((tm, tn), lambda i, j: (i, j)),
            pl.BlockSpec((tm, LANES), lambda i, j: (i, 0)),
        ] + cast_out,
        scratch_shapes=[pltpu.VMEM((tm, d), BF16)],
        compiler_params=_cparams(("parallel", "arbitrary")),
        name="inproj",
    )(x, nw, w, w, *cast)
    return out[0], out[1], tuple(out[2:])


_W_IN_SEGMENTS = (
    (P_GATES, 8016, 6144), (P_RQ, 3920, 1024), (P_RK, 4944, 1024), (P_RV, 5968, 1024),
    (P_RG, 6992, 1024), (P_Z, 0, 1024), (P_XS, 1024, 1536), (P_KPE, 3856, MLA_ROPE),
    (P_QLAT, 2576, MLA_Q_LORA), (P_CKV, 3344, MLA_KV_LORA), (P_TOTAL, 2560, SSM_HEADS))
W_IN_OUT = P_TOTAL + LANES


def _w_in_relayout(x_ref, o_ref):
    n_src = x_ref.shape[1]
    segs = sorted(_W_IN_SEGMENTS)
    for k, (dest, _, width) in enumerate(segs):
        full = dest + width // LANES * LANES
        nxt = segs[k + 1][0] if k + 1 < len(segs) else o_ref.shape[1]
        if nxt > full:
            o_ref[:, full:nxt] = jnp.zeros((o_ref.shape[0], nxt - full), BF16)
    for dest, source, width in segs:
        lo = source // LANES * LANES
        hi = min(-(-(source + width) // LANES) * LANES, n_src)
        o_ref[:, dest:dest + width] = x_ref[:, lo:hi][:, source - lo:source - lo + width]


def _split3(x):
    hi = x.astype(BF16)
    r1 = x - hi.astype(F32)
    mid = r1.astype(BF16)
    lo = (r1 - mid.astype(F32)).astype(BF16)
    return hi, mid, lo


def _ssd_constants(L):
    t = np.arange(L)
    shifts = np.stack([(t[:, None] - t[None, :] == j) for j in range(1, SSM_CONV)])
    tril = t[:, None] >= t[None, :]
    expand = np.zeros((LANES, SSM_D_INNER), bool)
    for h in range(SSM_HEADS):
        expand[h, h * SSM_HEAD_DIM:(h + 1) * SSM_HEAD_DIM] = True
    expand2 = np.concatenate([expand, expand])
    expand4 = np.concatenate([expand, expand, expand, np.zeros_like(expand)])
    return tuple(jnp.asarray(m, BF16) for m in (shifts, tril, expand2, expand4))


def _ssd_kernel(xs_ref, bc_ref, z_ref, dt_ref, cwx_ref, cbx_ref, cwb_ref, cbb_ref,
                dtb_ref, alog_ref, dskip_ref, nw_ref, shift_ref, tril_ref, e2_ref, e4_ref,
                o_ref, xtail_ref, btail_ref, state_ref):
    L = xs_ref.shape[0]
    P = SSM_HEAD_DIM
    N = SSM_STATE
    HG = SSM_HEADS // SSM_GROUPS
    GW = HG * P
    T = 8

    @pl.when(pl.program_id(1) == 0)
    def _():
        xtail_ref[0:T, :] = jnp.zeros((T, xtail_ref.shape[1]), F32)
        btail_ref[0:T, :] = jnp.zeros((T, btail_ref.shape[1]), F32)
        state_ref[...] = jnp.zeros(state_ref.shape, F32)

    def conv_silu(tail_ref, in_ref, w_ref, b_ref):
        x16 = in_ref[...]
        xf = x16.astype(F32)
        acc = b_ref[...] + w_ref[SSM_CONV - 1:SSM_CONV, :] * xf
        for j in range(1, SSM_CONV):
            acc = acc + w_ref[SSM_CONV - 1 - j:SSM_CONV - j, :] * _dot(shift_ref[j - 1], x16)
        tail_ref[T:2 * T, :] = xf[0:T, :]
        head = b_ref[...] + w_ref[0:1, :] * tail_ref[T - 3:2 * T - 3, :]
        for j in range(1, SSM_CONV):
            head = head + w_ref[j:j + 1, :] * tail_ref[T - 3 + j:2 * T - 3 + j, :]
        tail_ref[0:T, :] = xf[L - T:L, :]
        return _silu(jnp.concatenate([head, acc[T:, :]], axis=0))

    xc = conv_silu(xtail_ref, xs_ref, cwx_ref, cbx_ref)
    bcc = conv_silu(btail_ref, bc_ref, cwb_ref, cbb_ref)

    dtr = dt_ref[...] + dtb_ref[...]
    dt = jnp.maximum(dtr, 0.0) + jnp.log1p(jnp.exp(-jnp.abs(dtr)))
    adt = dt * (-jnp.exp(alog_ref[...]) * math.log2(math.e))
    tril = tril_ref[...]
    hi, mid, lo = _split3(adt)
    acs = _dot(tril, hi) + _dot(tril, mid) + _dot(tril, lo)
    acs_t = acs.T

    hi, mid, _ = _split3(dt)
    dtx = _dot(jnp.concatenate([hi, mid], axis=1), e2_ref[...])
    hi, mid, lo = _split3(acs)
    ax = _dot(jnp.concatenate([hi, mid, lo, jnp.zeros_like(lo)], axis=1), e4_ref[...])
    a_last = ax[L - 1:L, :]
    xd = xc * dtx
    xd16 = xd.astype(BF16)
    xdd16 = (xd * jnp.exp2(a_last - ax)).astype(BF16)
    e_ax = jnp.exp2(ax)
    e_last = jnp.exp2(a_last)

    causal = (lax.broadcasted_iota(jnp.int32, (L, L), 0)
              >= lax.broadcasted_iota(jnp.int32, (L, L), 1))
    first_half = lax.broadcasted_iota(jnp.int32, (L, LANES), 1) < P
    tiles = []
    for g in range(SSM_GROUPS):
        bm = bcc[:, g * N:(g + 1) * N]
        cm16 = bcc[:, (SSM_GROUPS + g) * N:(SSM_GROUPS + g + 1) * N].astype(BF16)
        cb = _dot_nt(cm16, bm.astype(BF16))
        gs = slice(g * GW, (g + 1) * GW)
        prev = state_ref[g]
        y_off = _dot(cm16, prev.astype(BF16)) * e_ax[:, gs]
        state_ref[g] = prev * e_last[:, gs] + _dot(bm.T.astype(BF16), xdd16[:, gs])
        for t in range(GW // LANES):
            ts = slice(g * GW + t * LANES, g * GW + (t + 1) * LANES)
            pair = []
            for k in range(LANES // P):
                h = (g * GW + t * LANES) // P + k
                seg = acs[:, h:h + 1] - acs_t[h:h + 1, :]
                decay = jnp.exp2(jnp.where(causal, seg, -jnp.inf))
                pair.append(_dot((cb * decay).astype(BF16), xd16[:, ts]))
            tiles.append(jnp.where(first_half, pair[0], pair[1])
                         + y_off[:, t * LANES:(t + 1) * LANES])

    y = jnp.concatenate(tiles, axis=1)
    y = (y + dskip_ref[...] * xc) * _silu(z_ref[...].astype(F32))
    for g in range(SSM_GROUPS):
        gs = slice(g * GW, (g + 1) * GW)
        yg = y[:, gs]
        o_ref[:, gs] = (yg * _rms(yg, GW) * nw_ref[:, gs]).astype(BF16)


def _ssd(proj, dt_raw, cwx, cbx, cwb, cbb, dtb, alog, dskip, nw, *, batch, seq):
    L = math.gcd(seq, SSD_CHUNK)
    nc = seq // L
    di = SSM_D_INNER
    bcw = 2 * SSM_GROUPS * SSM_STATE
    tok = lambda b, c: b * nc + c
    consts = _ssd_constants(L)
    params = (cwx, cbx, cwb, cbb, dtb, alog, dskip, nw) + consts
    return pl.pallas_call(
        _ssd_kernel,
        out_shape=jax.ShapeDtypeStruct((batch * seq, di), BF16),
        grid=(batch, nc),
        in_specs=[
            pl.BlockSpec((L, di), lambda b, c: (tok(b, c), P_XS // di)),
            pl.BlockSpec((L, bcw), lambda b, c: (tok(b, c), P_BC // bcw)),
            pl.BlockSpec((L, di), lambda b, c: (tok(b, c), P_Z // di)),
            pl.BlockSpec((L, LANES), lambda b, c: (tok(b, c), 0)),
        ] + [_resident(p.shape) for p in params],
        out_specs=pl.BlockSpec((L, di), lambda b, c: (tok(b, c), 0)),
        scratch_shapes=[
            pltpu.VMEM((16, di), F32),
            pltpu.VMEM((16, bcw), F32),
            pltpu.VMEM((SSM_GROUPS, SSM_STATE, di // SSM_GROUPS), F32),
        ],
        compiler_params=_cparams(("parallel", "arbitrary")),
        name="ssd",
    )(proj, proj, proj, dt_raw, *params)


def _rope_tile(x, c, s1, s2):
    q = MLA_ROPE // 2
    return x * c + pltpu.roll(x, LANES - q, 1) * s1 + pltpu.roll(x, q, 1) * s2


def _mla_prep_kernel(ql_ref, ckv_ref, kpe_ref, c_ref, s1_ref, s2_ref, qan_ref, kvan_ref,
                     qn_ref, kn_ref, wq_ref, wkv_ref, q_out, k_out, v_out, *, q_scale):
    ql = ql_ref[...].astype(F32)
    qa = (ql * _rms(ql, MLA_Q_LORA) * qan_ref[...]).astype(BF16)
    ckv = ckv_ref[...].astype(F32)
    kva = (ckv * _rms(ckv, MLA_KV_LORA) * kvan_ref[...]).astype(BF16)
    kpe = kpe_ref[...].astype(F32)
    kpe_sq = kpe * kpe
    c, s1, s2 = c_ref[...], s1_ref[...], s2_ref[...]
    qn, kn = qn_ref[...] * q_scale, kn_ref[...]
    kpe_rot = _rope_tile(kpe * kn[:, MLA_NOPE:], c, s1, s2)
    ones = jnp.ones((kpe.shape[0], MLA_V), BF16)
    tile_sum = jnp.where(
        (lax.broadcasted_iota(jnp.int32, (2 * LANES, 2 * LANES), 0) < LANES)
        == (lax.broadcasted_iota(jnp.int32, (2 * LANES, 2 * LANES), 1) < LANES),
        1.0, 0.0).astype(BF16)

    def inv_rms_pair(sq_a, sq_b):
        ss = _dot(jnp.concatenate([sq_a, sq_b], axis=1).astype(BF16), tile_sum)
        r = lax.rsqrt(ss * (1.0 / MLA_QK_DIM) + NORM_EPS)
        return r[:, :LANES], r[:, LANES:]

    def project(h):
        cols = [slice((h + i) * MLA_QK_PAD, (h + i + 1) * MLA_QK_PAD) for i in range(2)]
        return [(_dot(qa, wq_ref[:, c]), _dot(kva, wkv_ref[:, c])) for c in cols]

    nxt = project(0)
    for h0 in range(0, MLA_HEADS, 2):
        cur = nxt
        if h0 + 2 < MLA_HEADS:
            nxt = project(h0 + 2)
        q_sq = [qh * qh for qh, _ in cur]
        rq = inv_rms_pair(*(sq[:, :LANES] + sq[:, LANES:] for sq in q_sq))
        rk = inv_rms_pair(*(kvh[:, :MLA_NOPE] * kvh[:, :MLA_NOPE] + kpe_sq for _, kvh in cur))
        for i, (qh, kvh) in enumerate(cur):
            h = h0 + i
            qh = qh * jnp.concatenate([rq[i], rq[i]], axis=1) * qn
            q_out[h, :, 0:MLA_NOPE] = qh[:, 0:MLA_NOPE].astype(BF16)
            q_out[h, :, MLA_NOPE:] = _rope_tile(qh[:, MLA_NOPE:], c, s1, s2).astype(BF16)
            k_out[h, :, 0:MLA_NOPE] = (kvh[:, 0:MLA_NOPE] * rk[i] * kn[:, 0:MLA_NOPE]).astype(BF16)
            k_out[h, :, MLA_NOPE:] = (kpe_rot * rk[i]).astype(BF16)
            v_out[h, :, 0:MLA_V] = kvh[:, MLA_NOPE:].astype(BF16)
            v_out[h, :, MLA_V:] = ones


def _mla_prep(proj, c, s1, s2, qan, kvan, qn, kn, wq, wkv, *, tm=512):
    m = proj.shape[0]
    tm = min(tm, m)
    row = lambda i: (i, 0)
    return pl.pallas_call(
        functools.partial(_mla_prep_kernel, q_scale=MLA_QK_DIM ** -0.5 * math.log2(math.e)),
        out_shape=(
            jax.ShapeDtypeStruct((MLA_HEADS, m, MLA_QK_PAD), BF16),
            jax.ShapeDtypeStruct((MLA_HEADS, m, MLA_QK_PAD), BF16),
            jax.ShapeDtypeStruct((MLA_HEADS, m, 2 * MLA_V), BF16),
        ),
        grid=(m // tm,),
        in_specs=[
            pl.BlockSpec((tm, MLA_Q_LORA), lambda i: (i, P_QLAT // MLA_Q_LORA)),
            pl.BlockSpec((tm, MLA_KV_LORA), lambda i: (i, P_CKV // MLA_KV_LORA)),
            pl.BlockSpec((tm, LANES), lambda i: (i, P_KPE // LANES)),
            pl.BlockSpec((tm, LANES), row), pl.BlockSpec((tm, LANES), row),
            pl.BlockSpec((tm, LANES), row),
            _resident(qan.shape), _resident(kvan.shape), _resident(qn.shape),
            _resident(kn.shape), _resident(wq.shape), _resident(wkv.shape),
        ],
        out_specs=(
            pl.BlockSpec((MLA_HEADS, tm, MLA_QK_PAD), lambda i: (0, i, 0)),
            pl.BlockSpec((MLA_HEADS, tm, MLA_QK_PAD), lambda i: (0, i, 0)),
            pl.BlockSpec((MLA_HEADS, tm, 2 * MLA_V), lambda i: (0, i, 0)),
        ),
        compiler_params=_cparams(("parallel",)),
        name="mla_prep",
    )(proj, proj, proj, c, s1, s2, qan, kvan, qn, kn, wq, wkv)


def _attn_kernel(q_ref, k_ref, v_ref, o_ref, s_ref, acc_ref, *, blk, heads):
    i = pl.program_id(2)

    def scores(j, slot):
        start = pl.multiple_of(j * blk, blk)
        for h in range(heads):
            s_ref[slot, h] = _dot_nt(q_ref[h], k_ref[h, pl.ds(start, blk), :])

    def consume(j, slot, m_old, masked):
        start = pl.multiple_of(j * blk, blk)
        probs = []
        for h, m in enumerate(m_old):
            s = s_ref[slot, h]
            if masked:
                row = lax.broadcasted_iota(jnp.int32, (blk, blk), 0)
                col = lax.broadcasted_iota(jnp.int32, (blk, blk), 1)
                s = jnp.where(row >= col, s, -jnp.inf)
            m_new = jnp.maximum(m, jnp.max(s, axis=-1, keepdims=True))
            probs.append((m_new, jnp.exp2(m - m_new), jnp.exp2(s - m_new).astype(BF16)))
        for h, (_, alpha, p) in enumerate(probs):
            acc_ref[h] = alpha * acc_ref[h] + _dot(p, v_ref[h, pl.ds(start, blk), :])
        return tuple(m_new for m_new, _, _ in probs)

    def finish():
        for h in range(heads):
            acc = acc_ref[h]
            o_ref[:, h * MLA_V:(h + 1) * MLA_V] = (acc[:, :MLA_V] / acc[:, MLA_V:]).astype(BF16)

    def pair(t, m):
        j = 2 * t
        scores(j + 1, 1)
        m = consume(j, 0, m, False)
        scores(j + 2, 0)
        return consume(j + 1, 1, m, False)

    scores(0, 0)
    acc_ref[...] = jnp.zeros(acc_ref.shape, F32)
    m = lax.fori_loop(0, i // 2, pair,
                      tuple(jnp.full((blk, 1), -jnp.inf, F32) for _ in range(heads)))
    last_even = 2 * (i // 2)

    @pl.when(i % 2 == 0)
    def _():
        consume(last_even, 0, m, True)
        finish()

    @pl.when(i % 2 == 1)
    def _():
        scores(last_even + 1, 1)
        consume(last_even + 1, 1, consume(last_even, 0, m, False), True)
        finish()


def _attention(q, k, v, *, batch, seq, blk=512, heads=4):
    blk = min(blk, seq)
    nq = seq // blk
    return pl.pallas_call(
        functools.partial(_attn_kernel, blk=blk, heads=heads),
        out_shape=jax.ShapeDtypeStruct((batch * seq, MLA_HEADS * MLA_V), BF16),
        grid=(batch, MLA_HEADS // heads, nq),
        in_specs=[
            pl.BlockSpec((heads, blk, MLA_QK_PAD), lambda b, h, i: (h, b * nq + i, 0)),
            pl.BlockSpec((heads, seq, MLA_QK_PAD), lambda b, h, i: (h, b, 0)),
            pl.BlockSpec((heads, seq, 2 * MLA_V), lambda b, h, i: (h, b, 0)),
        ],
        out_specs=pl.BlockSpec((blk, heads * MLA_V), lambda b, h, i: (b * nq + i, h)),
        scratch_shapes=[pltpu.VMEM((2, heads, blk, blk), F32),
                        pltpu.VMEM((heads, blk, 2 * MLA_V), F32)],
        compiler_params=_cparams(("parallel", "parallel", "arbitrary")),
        name="mla_attention",
    )(q, k, v)


def _ret_constants(L):
    expo = 5.0 + 7.0 * np.arange(RET_HEADS, dtype=np.float32) / np.float32(RET_HEADS - 1)
    log_gamma = np.log1p(-np.exp2(-expo)).astype(np.float32)[:, None, None]
    pos = np.arange(L, dtype=np.float32)
    rel = pos[:, None] - pos[None, :]
    dmask = np.where(rel >= 0, np.exp(rel * log_gamma), 0.0).astype(np.float32)
    lanes = np.ones((1, 1, RET_HEAD), np.float32)
    q_dec = np.exp((pos + 1.0)[None, :, None] * log_gamma).astype(np.float32) * lanes
    k_dec = np.exp((L - 1.0 - pos)[None, :, None] * log_gamma).astype(np.float32) * lanes
    chunk_dec = [float(v) for v in np.exp(L * log_gamma[:, 0, 0])]
    return jnp.asarray(dmask), jnp.asarray(q_dec), jnp.asarray(k_dec), chunk_dec


def _ret_kernel(rq_ref, rk_ref, rv_ref, rg_ref, cos_ref, sin_ref, nw_ref, dmask_ref, qdec_ref,
                kdec_ref, o_ref, state_ref, *, chunk_dec):
    D = RET_HEAD
    half = D // 2

    @pl.when(pl.program_id(1) == 0)
    def _():
        state_ref[...] = jnp.zeros(state_ref.shape, F32)

    cos, sin = cos_ref[...], sin_ref[...]

    def rope(ref, hs):
        x1 = ref[:, hs.start:hs.start + half].astype(F32)
        x2 = ref[:, hs.start + half:hs.stop].astype(F32)
        return jnp.concatenate([x1 * cos - x2 * sin, x1 * sin + x2 * cos], axis=-1)

    for h in range(RET_HEADS):
        hs = slice(h * D, (h + 1) * D)
        q = rope(rq_ref, hs)
        k = rope(rk_ref, hs) * (RET_HEAD ** -0.5)
        v16 = rv_ref[:, hs]
        q16 = q.astype(BF16)
        scores = _dot_nt(q16, k.astype(BF16)) * dmask_ref[h]
        y = _dot(scores.astype(BF16), v16)
        prev = state_ref[h]
        y = y + _dot(q16, prev.astype(BF16)) * qdec_ref[h]
        kd = k * kdec_ref[h]
        state_ref[h] = prev * chunk_dec[h] + _dot(kd.T.astype(BF16), v16)
        y = y * _rms(y, D) * nw_ref[:, hs]
        o_ref[:, hs] = (_silu(rg_ref[:, hs].astype(F32)) * y).astype(BF16)


def _retention(proj, cos, sin, nw, *, batch, seq):
    L = math.gcd(seq, RET_CHUNK)
    nc = seq // L
    w = RET_HEADS * RET_HEAD
    tok = lambda b, c: b * nc + c
    dmask, q_dec, k_dec, chunk_dec = _ret_constants(L)
    return pl.pallas_call(
        functools.partial(_ret_kernel, chunk_dec=chunk_dec),
        out_shape=jax.ShapeDtypeStruct((batch * seq, w), BF16),
        grid=(batch, nc),
        in_specs=[
            pl.BlockSpec((L, w), lambda b, c: (tok(b, c), P_RQ // w)),
            pl.BlockSpec((L, w), lambda b, c: (tok(b, c), P_RK // w)),
            pl.BlockSpec((L, w), lambda b, c: (tok(b, c), P_RV // w)),
            pl.BlockSpec((L, w), lambda b, c: (tok(b, c), P_RG // w)),
            pl.BlockSpec((L, RET_HEAD // 2), lambda b, c: (tok(b, c), 0)),
            pl.BlockSpec((L, RET_HEAD // 2), lambda b, c: (tok(b, c), 0)),
            _resident(nw.shape), _resident(dmask.shape), _resident(q_dec.shape),
            _resident(k_dec.shape),
        ],
        out_specs=pl.BlockSpec((L, w), lambda b, c: (tok(b, c), 0)),
        scratch_shapes=[pltpu.VMEM((RET_HEADS, RET_HEAD, RET_HEAD), F32)],
        compiler_params=_cparams(("parallel", "arbitrary")),
        name="retention",
    )(proj, proj, proj, proj, cos, sin, nw, dmask, q_dec, k_dec)


def _merge_kernel(ys_ref, ym_ref, yr_ref, g0_ref, g1_ref, g2_ref, gb_ref, x_ref,
                  w0_ref, w1_ref, w2_ref, wo_ref, o_ref):
    d = x_ref.shape[-1]

    def branch(k, y_ref, g_ref, w_ref):
        gate = _sigmoid(g_ref[...].astype(F32) + gb_ref[:, k * d:(k + 1) * d])
        return gate * _dot(y_ref[...], w_ref[...])

    merged = (branch(0, ys_ref, g0_ref, w0_ref) + branch(1, ym_ref, g1_ref, w1_ref)
              + branch(2, yr_ref, g2_ref, w2_ref))
    o_ref[...] = x_ref[...] + _dot(merged.astype(BF16), wo_ref[...])


def _merge(ys, ym, yr, proj, gb, x, w0, w1, w2, wo, *, tm=256):
    m, d = x.shape
    tm = min(tm, m)
    bw = ys.shape[1]
    row = lambda i: (i, 0)
    return pl.pallas_call(
        _merge_kernel,
        out_shape=jax.ShapeDtypeStruct((m, d), F32),
        grid=(m // tm,),
        in_specs=[
            pl.BlockSpec((tm, bw), row), pl.BlockSpec((tm, bw), row), pl.BlockSpec((tm, bw), row),
            pl.BlockSpec((tm, d), lambda i: (i, P_GATES // d)),
            pl.BlockSpec((tm, d), lambda i: (i, P_GATES // d + 1)),
            pl.BlockSpec((tm, d), lambda i: (i, P_GATES // d + 2)),
            _resident(gb.shape),
            pl.BlockSpec((tm, d), row),
            _resident(w0.shape), _resident(w1.shape), _resident(w2.shape), _resident(wo.shape),
        ],
        out_specs=pl.BlockSpec((tm, d), row),
        compiler_params=_cparams(("parallel",)),
        name="merge",
    )(ys, ym, yr, proj, proj, proj, gb, x, w0, w1, w2, wo)


def _rope_angles(positions, dim):
    inv = 1.0 / (ROPE_THETA ** (jnp.arange(0, dim, 2, dtype=F32) / dim))
    return positions.astype(F32).reshape(-1, 1) * inv


def _pad_lanes(v, n):
    return jnp.concatenate([v, jnp.zeros((n - v.shape[0],), v.dtype)]).reshape(1, n)


def kernel(x, positions, ffn1_norm, ffn1_w_gate, ffn1_w_up, ffn1_w_down, mix_norm, w_in, gate_b, conv_w, conv_b, dt_bias, a_log, d_skip, ssm_norm, q_a_norm, w_q_b, kv_a_norm, w_kv_b, q_norm, k_norm, ret_norm, w_br_ssm, w_br_mla, w_br_ret, w_out, ffn2_norm, ffn2_w_gate, ffn2_w_up, ffn2_w_down):
    batch, seq, d = x.shape
    depth = w_in.shape[0]
    x = x.reshape(batch * seq, d)

    ang = _rope_angles(positions, MLA_ROPE)
    cm, sm = jnp.cos(ang), jnp.sin(ang)
    zq = jnp.zeros_like(cm)
    rope_c = jnp.concatenate([cm, cm, zq, zq], axis=1)
    rope_s1 = jnp.concatenate([-sm, zq, zq, zq], axis=1)
    rope_s2 = jnp.concatenate([zq, sm, zq, zq], axis=1)
    ang = _rope_angles(positions, RET_HEAD)
    cos_ret, sin_ret = jnp.cos(ang), jnp.sin(ang)

    w_pad = jnp.pad(w_in.astype(BF16), ((0, 0), (0, 0), (0, -w_in.shape[2] % LANES)))
    ffn1_f32 = (ffn1_w_gate, ffn1_w_up, ffn1_w_down)
    ffn2_f32 = (ffn2_w_gate, ffn2_w_up, ffn2_w_down)
    merge_f32 = (w_br_ssm, w_br_mla, w_br_ret, w_out)
    ffn1 = tuple(w[0].astype(BF16) for w in ffn1_f32)
    row = lambda v: v.reshape(1, -1)
    for l in range(depth):
        x, ffn2, (w16,) = _ffn(x, row(ffn1_norm[l]), *ffn1, cast=ffn2_f32, relayout=(w_pad,),
                               side_layer=l)

        proj, dt_raw, (*w_merge, wkv) = _inproj(x, row(mix_norm[l]), w16, cast=merge_f32 + (w_kv_b,),
                                                side_layer=l)

        y_ssm = _ssd(
            proj, dt_raw,
            conv_w[l][:, :SSM_D_INNER], row(conv_b[l][:SSM_D_INNER]),
            conv_w[l][:, SSM_D_INNER:], row(conv_b[l][SSM_D_INNER:]),
            _pad_lanes(dt_bias[l], LANES), _pad_lanes(a_log[l], LANES),
            row(jnp.repeat(d_skip[l], SSM_HEAD_DIM)), row(ssm_norm[l]),
            batch=batch, seq=seq)

        wq = w_q_b[l].reshape(MLA_Q_LORA, MLA_HEADS, MLA_QK_DIM)
        wq = jnp.pad(wq, ((0, 0), (0, 0), (0, MLA_QK_PAD - MLA_QK_DIM)))
        wq = wq.reshape(MLA_Q_LORA, MLA_HEADS * MLA_QK_PAD).astype(BF16)
        q, k, v = _mla_prep(
            proj, rope_c, rope_s1, rope_s2, row(q_a_norm[l]), row(kv_a_norm[l]),
            _pad_lanes(q_norm[l], MLA_QK_PAD), _pad_lanes(k_norm[l], MLA_QK_PAD), wq, wkv)
        y_mla = _attention(q, k, v, batch=batch, seq=seq)

        y_ret = _retention(proj, cos_ret, sin_ret, row(ret_norm[l]), batch=batch, seq=seq)

        x = _merge(y_ssm, y_mla, y_ret, proj, row(gate_b[l]), x, *w_merge)

        if l + 1 < depth:
            x, ffn1, _ = _ffn(x, row(ffn2_norm[l]), *ffn2, cast=ffn1_f32, side_layer=l + 1)
        else:
            x = _ffn(x, row(ffn2_norm[l]), *ffn2)[0]
    return x.reshape(batch, seq, d)
```

```python
import functools
import math

import jax
import jax.numpy as jnp
import numpy as np
from jax import lax
from jax.experimental import pallas as pl
from jax.experimental.pallas import tpu as pltpu

F32 = jnp.float32
BF16 = jnp.bfloat16

NORM_EPS = 1e-6
ROPE_THETA = 10000.0

SSM_HEADS = 16
SSM_HEAD_DIM = 64
SSM_D_INNER = SSM_HEADS * SSM_HEAD_DIM
SSM_GROUPS = 2
SSM_STATE = 128
SSM_CONV = 4
SSD_CHUNK = 256
MLA_HEADS = 8
MLA_Q_LORA = 768
MLA_KV_LORA = 512
MLA_NOPE = 128
MLA_ROPE = 64
MLA_QK_DIM = MLA_NOPE + MLA_ROPE
MLA_V = 128
RET_HEADS = 4
RET_HEAD = 256
RET_CHUNK = 256
N_BRANCH = 3

LANES = 128
MLA_QK_PAD = MLA_NOPE + LANES
VMEM_LIMIT = 60 * 1024 * 1024


def _cparams(sem):
    return pltpu.CompilerParams(dimension_semantics=sem, vmem_limit_bytes=VMEM_LIMIT)


def _resident(shape):
    nd = len(shape)
    return pl.BlockSpec(shape, lambda *_: (0,) * nd, pipeline_mode=pl.Buffered(1))


BF16_SUBLANES = 16


def _row_blocks(rows, grid):
    steps = grid[0] * grid[1]
    br = next(b for b in range(BF16_SUBLANES, rows + 1, BF16_SUBLANES)
              if rows % b == 0 and rows // b <= steps)
    return br, lambda i, j: jnp.minimum(i * grid[1] + j, rows // br - 1)


def _side_specs(arrays, layer, grid, out_cols=None):
    in_specs, out_shapes, out_specs = [], [], []
    for a in arrays:
        _, rows, cols = a.shape
        br, block = _row_blocks(rows, grid)
        oc = out_cols or cols
        in_specs.append(pl.BlockSpec((None, br, cols),
                                     lambda i, j, block=block: (layer, block(i, j), 0)))
        out_shapes.append(jax.ShapeDtypeStruct((rows, oc), BF16))
        out_specs.append(pl.BlockSpec((br, oc), lambda i, j, block=block: (block(i, j), 0)))
    return in_specs, out_shapes, out_specs


def _cast_blocks(src_refs, dst_refs):
    for src, dst in zip(src_refs, dst_refs):
        dst[...] = src[...].astype(BF16)


def _rms(x, n):
    return lax.rsqrt(jnp.sum(x * x, axis=-1, keepdims=True) * (1.0 / n) + NORM_EPS)


def _sigmoid(x):
    return 0.5 * jnp.tanh(0.5 * x) + 0.5


def _silu(x):
    h = 0.5 * x
    return h * jnp.tanh(h) + h


def _dot(a, b):
    return jnp.dot(a, b, preferred_element_type=F32)


def _dot_nt(a, b):
    return lax.dot_general(a, b, (((1,), (1,)), ((), ())), preferred_element_type=F32)


def _ffn_kernel(x_ref, nw_ref, wg_ref, wu_ref, wd_ref, *refs, n_cast, n_relayout):
    n_side = n_cast + n_relayout
    side_in, o_ref, side_out, xn_ref = (refs[:n_side], refs[n_side],
                                        refs[n_side + 1:2 * n_side + 1], refs[-1])
    d = x_ref.shape[-1]
    j = pl.program_id(1)

    def half_ffn(xn):
        _cast_blocks(side_in[:n_cast], side_out[:n_cast])
        for src, dst in zip(side_in[n_cast:], side_out[n_cast:]):
            _w_in_relayout(src, dst)
        g = _dot(xn, wg_ref[...])
        u = _dot(xn, wu_ref[...])
        return _dot((_silu(g) * (0.5 * u)).astype(BF16), wd_ref[...])

    @pl.when(j == 0)
    def _():
        x = x_ref[...]
        xn = (x * _rms(x, d) * nw_ref[...]).astype(BF16)
        xn_ref[...] = xn
        o_ref[...] = x + half_ffn(xn)

    @pl.when(j > 0)
    def _():
        o_ref[...] += half_ffn(xn_ref[...])


def _ffn(x, nw, wg, wu, wd, *, cast=(), relayout=(), side_layer=0, tm=1024, tf=512):
    m, d = x.shape
    f = wg.shape[1]
    tm = min(tm, m)
    grid = (m // tm, f // tf)
    specs = [_side_specs(cast, side_layer, grid), _side_specs(relayout, side_layer, grid, W_IN_OUT)]
    side_in, side_shapes, side_out = (sum((sp[k] for sp in specs), []) for k in range(3))
    out = pl.pallas_call(
        functools.partial(_ffn_kernel, n_cast=len(cast), n_relayout=len(relayout)),
        out_shape=[jax.ShapeDtypeStruct((m, d), F32)] + side_shapes,
        grid=grid,
        in_specs=[
            pl.BlockSpec((tm, d), lambda i, j: (i, 0)),
            pl.BlockSpec((1, d), lambda i, j: (0, 0)),
            pl.BlockSpec((d, tf), lambda i, j: (0, j)),
            pl.BlockSpec((d, tf), lambda i, j: (0, j)),
            pl.BlockSpec((tf, d), lambda i, j: (j, 0)),
        ] + side_in,
        out_specs=[pl.BlockSpec((tm, d), lambda i, j: (i, 0))] + side_out,
        scratch_shapes=[pltpu.VMEM((tm, d), BF16)],
        compiler_params=_cparams(("parallel", "arbitrary")),
        name="ffn",
    )(x, nw, wg, wu, wd, *cast, *relayout)
    return out[0], tuple(out[1:1 + len(cast)]), tuple(out[1 + len(cast):])


P_GATES = 0
P_RQ = 6144
P_RK = 7168
P_RV = 8192
P_RG = 9216
P_Z = 10240
P_XS = 11264
P_BC = 12288
P_KPE = 12800
P_QLAT = 13056
P_CKV = 13824
P_TOTAL = 14336


def _inproj_kernel(x_ref, nw_ref, w_ref, wdt_ref, *refs):
    n_cast = (len(refs) - 3) // 2
    cast_in, (o_ref, dt_ref), cast_out, xn_ref = (refs[:n_cast], refs[n_cast:n_cast + 2],
                                                  refs[n_cast + 2:2 * n_cast + 2], refs[-1])
    j = pl.program_id(1)
    d = x_ref.shape[-1]

    @pl.when(j == 0)
    def _():
        _cast_blocks(cast_in, cast_out)
        x = x_ref[...]
        xn = (x * _rms(x, d) * nw_ref[...]).astype(BF16)
        xn_ref[...] = xn
        dt_ref[...] = _dot(xn, wdt_ref[...])
        o_ref[...] = _dot(xn, w_ref[...]).astype(BF16)

    @pl.when(j > 0)
    def _():
        _cast_blocks(cast_in, cast_out)
        o_ref[...] = _dot(xn_ref[...], w_ref[...]).astype(BF16)


def _inproj(x, nw, w, *, cast=(), side_layer=0, tm=1024, tn=2048):
    m, d = x.shape
    n = P_TOTAL
    tm = min(tm, m)
    grid = (m // tm, n // tn)
    cast_in, cast_shapes, cast_out = _side_specs(cast, side_layer, grid)
    out = pl.pallas_call(
        _inproj_kernel,
        out_shape=[jax.ShapeDtypeStruct((m, n), BF16),
                   jax.ShapeDtypeStruct((m, LANES), F32)] + cast_shapes,
        grid=grid,
        in_specs=[
            pl.BlockSpec((tm, d), lambda i, j: (i, 0)),
            pl.BlockSpec((1, d), lambda i, j: (0, 0)),
            pl.BlockSpec((d, tn), lambda i, j: (0, j)),
            pl.BlockSpec((d, LANES), lambda i, j: (0, P_TOTAL // LANES)),
        ] + cast_in,
        out_specs=[
            pl.BlockSpec((tm, tn), lambda i, j: (i, j)),
            pl.BlockSpec((tm, LANES), lambda i, j: (i, 0)),
        ] + cast_out,
        scratch_shapes=[pltpu.VMEM((tm, d), BF16)],
        compiler_params=_cparams(("parallel", "arbitrary")),
        name="inproj",
    )(x, nw, w, w, *cast)
    return out[0], out[1], tuple(out[2:])


_W_IN_SEGMENTS = (
    (P_GATES, 8016, 6144), (P_RQ, 3920, 1024), (P_RK, 4944, 1024), (P_RV, 5968, 1024),
    (P_RG, 6992, 1024), (P_Z, 0, 1024), (P_XS, 1024, 1536), (P_KPE, 3856, MLA_ROPE),
    (P_QLAT, 2576, MLA_Q_LORA), (P_CKV, 3344, MLA_KV_LORA), (P_TOTAL, 2560, SSM_HEADS))
W_IN_OUT = P_TOTAL + LANES


def _w_in_relayout(x_ref, o_ref):
    n_src = x_ref.shape[1]
    segs = sorted(_W_IN_SEGMENTS)
    for k, (dest, _, width) in enumerate(segs):
        full = dest + width // LANES * LANES
        nxt = segs[k + 1][0] if k + 1 < len(segs) else o_ref.shape[1]
        if nxt > full:
            o_ref[:, full:nxt] = jnp.zeros((o_ref.shape[0], nxt - full), BF16)
    for dest, source, width in segs:
        lo = source // LANES * LANES
        hi = min(-(-(source + width) // LANES) * LANES, n_src)
        o_ref[:, dest:dest + width] = x_ref[:, lo:hi][:, source - lo:source - lo + width]


def _split3(x):
    hi = x.astype(BF16)
    r1 = x - hi.astype(F32)
    mid = r1.astype(BF16)
    lo = (r1 - mid.astype(F32)).astype(BF16)
    return hi, mid, lo


def _ssd_constants(L):
    t = np.arange(L)
    shifts = np.stack([(t[:, None] - t[None, :] == j) for j in range(1, SSM_CONV)])
    tril = t[:, None] >= t[None, :]
    expand = np.zeros((LANES, SSM_D_INNER), bool)
    for h in range(SSM_HEADS):
        expand[h, h * SSM_HEAD_DIM:(h + 1) * SSM_HEAD_DIM] = True
    expand2 = np.concatenate([expand, expand])
    expand4 = np.concatenate([expand, expand, expand, np.zeros_like(expand)])
    return tuple(jnp.asarray(m, BF16) for m in (shifts, tril, expand2, expand4))


SSD_TAIL = 8


def _ssd_reset(xtail_ref, btail_ref, state_ref):
    xtail_ref[0:SSD_TAIL, :] = jnp.zeros((SSD_TAIL, xtail_ref.shape[1]), F32)
    btail_ref[0:SSD_TAIL, :] = jnp.zeros((SSD_TAIL, btail_ref.shape[1]), F32)
    state_ref[...] = jnp.zeros(state_ref.shape, F32)


def _ssd_chunk(xs_ref, bc_ref, z_ref, dt_ref, cwx_ref, cbx_ref, cwb_ref, cbb_ref,
               dtb_ref, alog_ref, dskip_ref, nw_ref, shift_ref, tril_ref, e2_ref, e4_ref,
               o_ref, xtail_ref, btail_ref, state_ref):
    L = xs_ref.shape[0]
    P = SSM_HEAD_DIM
    N = SSM_STATE
    HG = SSM_HEADS // SSM_GROUPS
    GW = HG * P
    T = SSD_TAIL

    def conv_silu(tail_ref, in_ref, w_ref, b_ref):
        x16 = in_ref[...]
        xf = x16.astype(F32)
        acc = b_ref[...] + w_ref[SSM_CONV - 1:SSM_CONV, :] * xf
        for j in range(1, SSM_CONV):
            acc = acc + w_ref[SSM_CONV - 1 - j:SSM_CONV - j, :] * _dot(shift_ref[j - 1], x16)
        tail_ref[T:2 * T, :] = xf[0:T, :]
        head = b_ref[...] + w_ref[0:1, :] * tail_ref[T - 3:2 * T - 3, :]
        for j in range(1, SSM_CONV):
            head = head + w_ref[j:j + 1, :] * tail_ref[T - 3 + j:2 * T - 3 + j, :]
        tail_ref[0:T, :] = xf[L - T:L, :]
        return _silu(jnp.concatenate([head, acc[T:, :]], axis=0))

    xc = conv_silu(xtail_ref, xs_ref, cwx_ref, cbx_ref)
    bcc = conv_silu(btail_ref, bc_ref, cwb_ref, cbb_ref)

    dtr = dt_ref[...] + dtb_ref[...]
    dt = jnp.maximum(dtr, 0.0) + jnp.log1p(jnp.exp(-jnp.abs(dtr)))
    adt = dt * (-jnp.exp(alog_ref[...]) * math.log2(math.e))
    tril = tril_ref[...]
    hi, mid, lo = _split3(adt)
    acs = _dot(tril, hi) + _dot(tril, mid) + _dot(tril, lo)
    acs_t = acs.T

    hi, mid, _ = _split3(dt)
    dtx = _dot(jnp.concatenate([hi, mid], axis=1), e2_ref[...])
    hi, mid, lo = _split3(acs)
    ax = _dot(jnp.concatenate([hi, mid, lo, jnp.zeros_like(lo)], axis=1), e4_ref[...])
    a_last = ax[L - 1:L, :]
    xd = xc * dtx
    xd16 = xd.astype(BF16)
    xdd16 = (xd * jnp.exp2(a_last - ax)).astype(BF16)
    e_ax = jnp.exp2(ax)
    e_last = jnp.exp2(a_last)

    causal = (lax.broadcasted_iota(jnp.int32, (L, L), 0)
              >= lax.broadcasted_iota(jnp.int32, (L, L), 1))
    first_half = lax.broadcasted_iota(jnp.int32, (L, LANES), 1) < P
    tiles = []
    for g in range(SSM_GROUPS):
        bm = bcc[:, g * N:(g + 1) * N]
        cm16 = bcc[:, (SSM_GROUPS + g) * N:(SSM_GROUPS + g + 1) * N].astype(BF16)
        cb = _dot_nt(cm16, bm.astype(BF16))
        gs = slice(g * GW, (g + 1) * GW)
        prev = state_ref[g]
        y_off = _dot(cm16, prev.astype(BF16)) * e_ax[:, gs]
        state_ref[g] = prev * e_last[:, gs] + _dot(bm.T.astype(BF16), xdd16[:, gs])
        for t in range(GW // LANES):
            ts = slice(g * GW + t * LANES, g * GW + (t + 1) * LANES)
            pair = []
            for k in range(LANES // P):
                h = (g * GW + t * LANES) // P + k
                seg = acs[:, h:h + 1] - acs_t[h:h + 1, :]
                decay = jnp.exp2(jnp.where(causal, seg, -jnp.inf))
                pair.append(_dot((cb * decay).astype(BF16), xd16[:, ts]))
            tiles.append(jnp.where(first_half, pair[0], pair[1])
                         + y_off[:, t * LANES:(t + 1) * LANES])

    y = jnp.concatenate(tiles, axis=1)
    y = (y + dskip_ref[...] * xc) * _silu(z_ref[...].astype(F32))
    for g in range(SSM_GROUPS):
        gs = slice(g * GW, (g + 1) * GW)
        yg = y[:, gs]
        o_ref[:, gs] = (yg * _rms(yg, GW) * nw_ref[:, gs]).astype(BF16)


def _rope_tile(x, c, s1, s2):
    q = MLA_ROPE // 2
    return x * c + pltpu.roll(x, LANES - q, 1) * s1 + pltpu.roll(x, q, 1) * s2


def _mla_prep_kernel(ql_ref, ckv_ref, kpe_ref, c_ref, s1_ref, s2_ref, qan_ref, kvan_ref,
                     qn_ref, kn_ref, wq_ref, wkv_ref, q_out, k_out, v_out, *, q_scale):
    ql = ql_ref[...].astype(F32)
    qa = (ql * _rms(ql, MLA_Q_LORA) * qan_ref[...]).astype(BF16)
    ckv = ckv_ref[...].astype(F32)
    kva = (ckv * _rms(ckv, MLA_KV_LORA) * kvan_ref[...]).astype(BF16)
    kpe = kpe_ref[...].astype(F32)
    kpe_sq = kpe * kpe
    c, s1, s2 = c_ref[...], s1_ref[...], s2_ref[...]
    qn, kn = qn_ref[...] * q_scale, kn_ref[...]
    kpe_rot = _rope_tile(kpe * kn[:, MLA_NOPE:], c, s1, s2)
    ones = jnp.ones((kpe.shape[0], MLA_V), BF16)
    tile_sum = jnp.where(
        (lax.broadcasted_iota(jnp.int32, (2 * LANES, 2 * LANES), 0) < LANES)
        == (lax.broadcasted_iota(jnp.int32, (2 * LANES, 2 * LANES), 1) < LANES),
        1.0, 0.0).astype(BF16)

    def inv_rms_pair(sq_a, sq_b):
        ss = _dot(jnp.concatenate([sq_a, sq_b], axis=1).astype(BF16), tile_sum)
        r = lax.rsqrt(ss * (1.0 / MLA_QK_DIM) + NORM_EPS)
        return r[:, :LANES], r[:, LANES:]

    def project(h):
        return [(_dot(qa, wq_ref[h + i]), _dot(kva, wkv_ref[h + i])) for i in range(2)]

    nxt = project(0)
    for h0 in range(0, MLA_HEADS, 2):
        cur = nxt
        if h0 + 2 < MLA_HEADS:
            nxt = project(h0 + 2)
        q_sq = [qh * qh for qh, _ in cur]
        rq = inv_rms_pair(*(sq[:, :LANES] + sq[:, LANES:] for sq in q_sq))
        rk = inv_rms_pair(*(kvh[:, :MLA_NOPE] * kvh[:, :MLA_NOPE] + kpe_sq for _, kvh in cur))
        for i, (qh, kvh) in enumerate(cur):
            h = h0 + i
            qh = qh * jnp.concatenate([rq[i], rq[i]], axis=1) * qn
            q_out[h, :, 0:MLA_NOPE] = qh[:, 0:MLA_NOPE].astype(BF16)
            q_out[h, :, MLA_NOPE:] = _rope_tile(qh[:, MLA_NOPE:], c, s1, s2).astype(BF16)
            k_out[h, :, 0:MLA_NOPE] = (kvh[:, 0:MLA_NOPE] * rk[i] * kn[:, 0:MLA_NOPE]).astype(BF16)
            k_out[h, :, MLA_NOPE:] = (kpe_rot * rk[i]).astype(BF16)
            v_out[h, :, 0:MLA_V] = kvh[:, MLA_NOPE:].astype(BF16)
            v_out[h, :, MLA_V:] = ones


def _mla_prep(proj, c, s1, s2, qan, kvan, qn, kn, wq, wkv, *, tm=512):
    m = proj.shape[0]
    tm = min(tm, m)
    row = lambda i: (i, 0)
    return pl.pallas_call(
        functools.partial(_mla_prep_kernel, q_scale=MLA_QK_DIM ** -0.5 * math.log2(math.e)),
        out_shape=(
            jax.ShapeDtypeStruct((MLA_HEADS, m, MLA_QK_PAD), BF16),
            jax.ShapeDtypeStruct((MLA_HEADS, m, MLA_QK_PAD), BF16),
            jax.ShapeDtypeStruct((MLA_HEADS, m, 2 * MLA_V), BF16),
        ),
        grid=(m // tm,),
        in_specs=[
            pl.BlockSpec((tm, MLA_Q_LORA), lambda i: (i, P_QLAT // MLA_Q_LORA)),
            pl.BlockSpec((tm, MLA_KV_LORA), lambda i: (i, P_CKV // MLA_KV_LORA)),
            pl.BlockSpec((tm, LANES), lambda i: (i, P_KPE // LANES)),
            pl.BlockSpec((tm, LANES), row), pl.BlockSpec((tm, LANES), row),
            pl.BlockSpec((tm, LANES), row),
            _resident(qan.shape), _resident(kvan.shape), _resident(qn.shape),
            _resident(kn.shape), _resident(wq.shape), _resident(wkv.shape),
        ],
        out_specs=(
            pl.BlockSpec((MLA_HEADS, tm, MLA_QK_PAD), lambda i: (0, i, 0)),
            pl.BlockSpec((MLA_HEADS, tm, MLA_QK_PAD), lambda i: (0, i, 0)),
            pl.BlockSpec((MLA_HEADS, tm, 2 * MLA_V), lambda i: (0, i, 0)),
        ),
        compiler_params=_cparams(("parallel",)),
        name="mla_prep",
    )(proj, proj, proj, c, s1, s2, qan, kvan, qn, kn, wq, wkv)


def _attn_kernel(q_ref, k_ref, v_ref, o_ref, s_ref, acc_ref, *, blk, heads):
    i = pl.program_id(2)

    def scores(j, slot):
        start = pl.multiple_of(j * blk, blk)
        for h in range(heads):
            s_ref[slot, h] = _dot_nt(q_ref[h], k_ref[h, pl.ds(start, blk), :])

    def consume(j, slot, m_old, masked):
        start = pl.multiple_of(j * blk, blk)
        probs = []
        for h, m in enumerate(m_old):
            s = s_ref[slot, h]
            if masked:
                row = lax.broadcasted_iota(jnp.int32, (blk, blk), 0)
                col = lax.broadcasted_iota(jnp.int32, (blk, blk), 1)
                s = jnp.where(row >= col, s, -jnp.inf)
            m_new = jnp.maximum(m, jnp.max(s, axis=-1, keepdims=True))
            probs.append((m_new, jnp.exp2(m - m_new), jnp.exp2(s - m_new).astype(BF16)))
        for h, (_, alpha, p) in enumerate(probs):
            acc_ref[h] = alpha * acc_ref[h] + _dot(p, v_ref[h, pl.ds(start, blk), :])
        return tuple(m_new for m_new, _, _ in probs)

    def finish():
        for h in range(heads):
            acc = acc_ref[h]
            o_ref[:, h * MLA_V:(h + 1) * MLA_V] = (acc[:, :MLA_V] / acc[:, MLA_V:]).astype(BF16)

    def pair(t, m):
        j = 2 * t
        scores(j + 1, 1)
        m = consume(j, 0, m, False)
        scores(j + 2, 0)
        return consume(j + 1, 1, m, False)

    scores(0, 0)
    acc_ref[...] = jnp.zeros(acc_ref.shape, F32)
    m = lax.fori_loop(0, i // 2, pair,
                      tuple(jnp.full((blk, 1), -jnp.inf, F32) for _ in range(heads)))
    last_even = 2 * (i // 2)

    @pl.when(i % 2 == 0)
    def _():
        consume(last_even, 0, m, True)
        finish()

    @pl.when(i % 2 == 1)
    def _():
        scores(last_even + 1, 1)
        consume(last_even + 1, 1, consume(last_even, 0, m, False), True)
        finish()


def _attention(q, k, v, *, batch, seq, blk=512, heads=4):
    blk = min(blk, seq)
    nq = seq // blk
    return pl.pallas_call(
        functools.partial(_attn_kernel, blk=blk, heads=heads),
        out_shape=jax.ShapeDtypeStruct((batch * seq, MLA_HEADS * MLA_V), BF16),
        grid=(batch, MLA_HEADS // heads, nq),
        in_specs=[
            pl.BlockSpec((heads, blk, MLA_QK_PAD), lambda b, h, i: (h, b * nq + i, 0)),
            pl.BlockSpec((heads, seq, MLA_QK_PAD), lambda b, h, i: (h, b, 0)),
            pl.BlockSpec((heads, seq, 2 * MLA_V), lambda b, h, i: (h, b, 0)),
        ],
        out_specs=pl.BlockSpec((blk, heads * MLA_V), lambda b, h, i: (b * nq + i, h)),
        scratch_shapes=[pltpu.VMEM((2, heads, blk, blk), F32),
                        pltpu.VMEM((heads, blk, 2 * MLA_V), F32)],
        compiler_params=_cparams(("parallel", "parallel", "arbitrary")),
        name="mla_attention",
    )(q, k, v)


def _ret_constants(L):
    expo = 5.0 + 7.0 * np.arange(RET_HEADS, dtype=np.float32) / np.float32(RET_HEADS - 1)
    log_gamma = np.log1p(-np.exp2(-expo)).astype(np.float32)[:, None, None]
    pos = np.arange(L, dtype=np.float32)
    rel = pos[:, None] - pos[None, :]
    dmask = np.where(rel >= 0, np.exp(rel * log_gamma), 0.0).astype(np.float32)
    lanes = np.ones((1, 1, RET_HEAD), np.float32)
    q_dec = np.exp((pos + 1.0)[None, :, None] * log_gamma).astype(np.float32) * lanes
    k_dec = np.exp((L - 1.0 - pos)[None, :, None] * log_gamma).astype(np.float32) * lanes
    chunk_dec = [float(v) for v in np.exp(L * log_gamma[:, 0, 0])]
    return jnp.asarray(dmask), jnp.asarray(q_dec), jnp.asarray(k_dec), chunk_dec


def _ret_chunk(rq_ref, rk_ref, rv_ref, rg_ref, cos_ref, sin_ref, nw_ref, dmask_ref, qdec_ref,
               kdec_ref, o_ref, state_ref, *, chunk_dec):
    D = RET_HEAD
    half = D // 2
    cos, sin = cos_ref[...], sin_ref[...]

    def rope(ref, hs):
        x1 = ref[:, hs.start:hs.start + half].astype(F32)
        x2 = ref[:, hs.start + half:hs.stop].astype(F32)
        return jnp.concatenate([x1 * cos - x2 * sin, x1 * sin + x2 * cos], axis=-1)

    for h in range(RET_HEADS):
        hs = slice(h * D, (h + 1) * D)
        q = rope(rq_ref, hs)
        k = rope(rk_ref, hs) * (RET_HEAD ** -0.5)
        v16 = rv_ref[:, hs]
        q16 = q.astype(BF16)
        scores = _dot_nt(q16, k.astype(BF16)) * dmask_ref[h]
        y = _dot(scores.astype(BF16), v16)
        prev = state_ref[h]
        y = y + _dot(q16, prev.astype(BF16)) * qdec_ref[h]
        kd = k * kdec_ref[h]
        state_ref[h] = prev * chunk_dec[h] + _dot(kd.T.astype(BF16), v16)
        y = y * _rms(y, D) * nw_ref[:, hs]
        o_ref[:, hs] = (_silu(rg_ref[:, hs].astype(F32)) * y).astype(BF16)


N_SSD_IN, N_RET_IN, N_SSD_SCRATCH = 16, 10, 3


def _scans_kernel(*refs, chunk_dec):
    ssd_in, ret_in = refs[:N_SSD_IN], refs[N_SSD_IN:N_SSD_IN + N_RET_IN]
    y_ssm_ref, y_ret_ref = refs[N_SSD_IN + N_RET_IN:N_SSD_IN + N_RET_IN + 2]
    scratch = refs[N_SSD_IN + N_RET_IN + 2:]
    ssd_scratch, ret_state = scratch[:N_SSD_SCRATCH], scratch[N_SSD_SCRATCH]

    @pl.when(pl.program_id(1) == 0)
    def _():
        _ssd_reset(*ssd_scratch)
        ret_state[...] = jnp.zeros(ret_state.shape, F32)

    _ssd_chunk(*ssd_in, y_ssm_ref, *ssd_scratch)
    _ret_chunk(*ret_in, y_ret_ref, ret_state, chunk_dec=chunk_dec)


def _scans(proj, dt_raw, cwx, cbx, cwb, cbb, dtb, alog, dskip, ssm_nw, cos, sin, ret_nw, *,
           batch, seq):
    L = math.gcd(seq, SSD_CHUNK)
    assert L == math.gcd(seq, RET_CHUNK)
    nc = seq // L
    di = SSM_D_INNER
    bcw = 2 * SSM_GROUPS * SSM_STATE
    w = RET_HEADS * RET_HEAD
    tok = lambda b, c: b * nc + c
    blk = lambda width, col: pl.BlockSpec((L, width), lambda b, c: (tok(b, c), col // width))
    ssd_params = (cwx, cbx, cwb, cbb, dtb, alog, dskip, ssm_nw) + _ssd_constants(L)
    dmask, q_dec, k_dec, chunk_dec = _ret_constants(L)
    ret_params = (ret_nw, dmask, q_dec, k_dec)
    assert 4 + len(ssd_params) == N_SSD_IN and 6 + len(ret_params) == N_RET_IN
    return pl.pallas_call(
        functools.partial(_scans_kernel, chunk_dec=chunk_dec),
        out_shape=(jax.ShapeDtypeStruct((batch * seq, di), BF16),
                   jax.ShapeDtypeStruct((batch * seq, w), BF16)),
        grid=(batch, nc),
        in_specs=[blk(di, P_XS), blk(bcw, P_BC), blk(di, P_Z), blk(LANES, 0)]
        + [_resident(p.shape) for p in ssd_params]
        + [blk(w, P_RQ), blk(w, P_RK), blk(w, P_RV), blk(w, P_RG),
           blk(RET_HEAD // 2, 0), blk(RET_HEAD // 2, 0)]
        + [_resident(p.shape) for p in ret_params],
        out_specs=(blk(di, 0), blk(w, 0)),
        scratch_shapes=[
            pltpu.VMEM((2 * SSD_TAIL, di), F32),
            pltpu.VMEM((2 * SSD_TAIL, bcw), F32),
            pltpu.VMEM((SSM_GROUPS, SSM_STATE, di // SSM_GROUPS), F32),
            pltpu.VMEM((RET_HEADS, RET_HEAD, RET_HEAD), F32),
        ],
        compiler_params=_cparams(("parallel", "arbitrary")),
        name="scans",
    )(proj, proj, proj, dt_raw, *ssd_params, proj, proj, proj, proj, cos, sin, *ret_params)


def _merge_kernel(ys_ref, ym_ref, yr_ref, g0_ref, g1_ref, g2_ref, gb_ref, x_ref,
                  w0_ref, w1_ref, w2_ref, wo_ref, o_ref):
    d = x_ref.shape[-1]

    def branch(k, y_ref, g_ref, w_ref):
        gate = _sigmoid(g_ref[...].astype(F32) + gb_ref[:, k * d:(k + 1) * d])
        return gate * _dot(y_ref[...], w_ref[...])

    merged = (branch(0, ys_ref, g0_ref, w0_ref) + branch(1, ym_ref, g1_ref, w1_ref)
              + branch(2, yr_ref, g2_ref, w2_ref))
    o_ref[...] = x_ref[...] + _dot(merged.astype(BF16), wo_ref[...])


def _merge(ys, ym, yr, proj, gb, x, w0, w1, w2, wo, *, tm=256):
    m, d = x.shape
    tm = min(tm, m)
    bw = ys.shape[1]
    row = lambda i: (i, 0)
    return pl.pallas_call(
        _merge_kernel,
        out_shape=jax.ShapeDtypeStruct((m, d), F32),
        grid=(m // tm,),
        in_specs=[
            pl.BlockSpec((tm, bw), row), pl.BlockSpec((tm, bw), row), pl.BlockSpec((tm, bw), row),
            pl.BlockSpec((tm, d), lambda i: (i, P_GATES // d)),
            pl.BlockSpec((tm, d), lambda i: (i, P_GATES // d + 1)),
            pl.BlockSpec((tm, d), lambda i: (i, P_GATES // d + 2)),
            _resident(gb.shape),
            pl.BlockSpec((tm, d), row),
            _resident(w0.shape), _resident(w1.shape), _resident(w2.shape), _resident(wo.shape),
        ],
        out_specs=pl.BlockSpec((tm, d), row),
        compiler_params=_cparams(("parallel",)),
        name="merge",
    )(ys, ym, yr, proj, proj, proj, gb, x, w0, w1, w2, wo)


def _rope_angles(positions, dim):
    inv = 1.0 / (ROPE_THETA ** (jnp.arange(0, dim, 2, dtype=F32) / dim))
    return positions.astype(F32).reshape(-1, 1) * inv


def _pad_lanes(v, n):
    return jnp.concatenate([v, jnp.zeros((n - v.shape[0],), v.dtype)]).reshape(1, n)


def kernel(x, positions, ffn1_norm, ffn1_w_gate, ffn1_w_up, ffn1_w_down, mix_norm, w_in, gate_b, conv_w, conv_b, dt_bias, a_log, d_skip, ssm_norm, q_a_norm, w_q_b, kv_a_norm, w_kv_b, q_norm, k_norm, ret_norm, w_br_ssm, w_br_mla, w_br_ret, w_out, ffn2_norm, ffn2_w_gate, ffn2_w_up, ffn2_w_down):
    batch, seq, d = x.shape
    depth = w_in.shape[0]
    x = x.reshape(batch * seq, d)

    ang = _rope_angles(positions, MLA_ROPE)
    cm, sm = jnp.cos(ang), jnp.sin(ang)
    zq = jnp.zeros_like(cm)
    rope_c = jnp.concatenate([cm, cm, zq, zq], axis=1)
    rope_s1 = jnp.concatenate([-sm, zq, zq, zq], axis=1)
    rope_s2 = jnp.concatenate([zq, sm, zq, zq], axis=1)
    ang = _rope_angles(positions, RET_HEAD)
    cos_ret, sin_ret = jnp.cos(ang), jnp.sin(ang)

    w_pad = jnp.pad(w_in.astype(BF16), ((0, 0), (0, 0), (0, -w_in.shape[2] % LANES)))
    ffn1_f32 = (ffn1_w_gate, ffn1_w_up, ffn1_w_down)
    ffn2_f32 = (ffn2_w_gate, ffn2_w_up, ffn2_w_down)
    merge_f32 = (w_br_ssm, w_br_mla, w_br_ret, w_out)
    ffn1 = tuple(w[0].astype(BF16) for w in ffn1_f32)
    row = lambda v: v.reshape(1, -1)
    for l in range(depth):
        x, ffn2, (w16,) = _ffn(x, row(ffn1_norm[l]), *ffn1, cast=ffn2_f32, relayout=(w_pad,),
                               side_layer=l)

        proj, dt_raw, w_merge = _inproj(x, row(mix_norm[l]), w16, cast=merge_f32, side_layer=l)

        y_ssm, y_ret = _scans(
            proj, dt_raw,
            conv_w[l][:, :SSM_D_INNER], row(conv_b[l][:SSM_D_INNER]),
            conv_w[l][:, SSM_D_INNER:], row(conv_b[l][SSM_D_INNER:]),
            _pad_lanes(dt_bias[l], LANES), _pad_lanes(a_log[l], LANES),
            row(jnp.repeat(d_skip[l], SSM_HEAD_DIM)), row(ssm_norm[l]),
            cos_ret, sin_ret, row(ret_norm[l]), batch=batch, seq=seq)

        wq = w_q_b[l].reshape(MLA_Q_LORA, MLA_HEADS, MLA_QK_DIM)
        wq = jnp.pad(wq, ((0, 0), (0, 0), (0, MLA_QK_PAD - MLA_QK_DIM)))
        wq = wq.transpose(1, 0, 2).astype(BF16)
        wkv = w_kv_b[l].reshape(MLA_KV_LORA, MLA_HEADS, MLA_NOPE + MLA_V)
        wkv = wkv.transpose(1, 0, 2).astype(BF16)
        q, k, v = _mla_prep(
            proj, rope_c, rope_s1, rope_s2, row(q_a_norm[l]), row(kv_a_norm[l]),
            _pad_lanes(q_norm[l], MLA_QK_PAD), _pad_lanes(k_norm[l], MLA_QK_PAD), wq, wkv)
        y_mla = _attention(q, k, v, batch=batch, seq=seq)

        x = _merge(y_ssm, y_mla, y_ret, proj, row(gate_b[l]), x, *w_merge)

        if l + 1 < depth:
            x, ffn1, _ = _ffn(x, row(ffn2_norm[l]), *ffn2, cast=ffn1_f32, side_layer=l + 1)
        else:
            x = _ffn(x, row(ffn2_norm[l]), *ffn2)[0]
    return x.reshape(batch, seq, d)
```

```python
import functools
import math

import jax
import jax.numpy as jnp
import numpy as np
from jax import lax
from jax.experimental import pallas as pl
from jax.experimental.pallas import tpu as pltpu

F32 = jnp.float32
BF16 = jnp.bfloat16

NORM_EPS = 1e-6
ROPE_THETA = 10000.0

SSM_HEADS = 16
SSM_HEAD_DIM = 64
SSM_D_INNER = SSM_HEADS * SSM_HEAD_DIM
SSM_GROUPS = 2
SSM_STATE = 128
SSM_CONV = 4
SSD_CHUNK = 256
MLA_HEADS = 8
MLA_Q_LORA = 768
MLA_KV_LORA = 512
MLA_NOPE = 128
MLA_ROPE = 64
MLA_QK_DIM = MLA_NOPE + MLA_ROPE
MLA_V = 128
RET_HEADS = 4
RET_HEAD = 256
RET_CHUNK = 256
N_BRANCH = 3

LANES = 128
MLA_QK_PAD = MLA_NOPE + LANES
VMEM_LIMIT = 60 * 1024 * 1024


def _cparams(sem):
    return pltpu.CompilerParams(dimension_semantics=sem, vmem_limit_bytes=VMEM_LIMIT)


def _resident(shape):
    nd = len(shape)
    return pl.BlockSpec(shape, lambda *_: (0,) * nd, pipeline_mode=pl.Buffered(1))


BF16_SUBLANES = 16


def _row_blocks(rows, grid):
    steps = grid[0] * grid[1]
    br = next(b for b in range(BF16_SUBLANES, rows + 1, BF16_SUBLANES)
              if rows % b == 0 and rows // b <= steps)
    return br, lambda i, j: jnp.minimum(i * grid[1] + j, rows // br - 1)


def _side_specs(arrays, layer, grid, out_cols=None):
    in_specs, out_shapes, out_specs = [], [], []
    for a in arrays:
        _, rows, cols = a.shape
        br, block = _row_blocks(rows, grid)
        oc = out_cols or cols
        in_specs.append(pl.BlockSpec((None, br, cols),
                                     lambda i, j, block=block: (layer, block(i, j), 0)))
        out_shapes.append(jax.ShapeDtypeStruct((rows, oc), BF16))
        out_specs.append(pl.BlockSpec((br, oc), lambda i, j, block=block: (block(i, j), 0)))
    return in_specs, out_shapes, out_specs


def _cast_blocks(src_refs, dst_refs):
    for src, dst in zip(src_refs, dst_refs):
        dst[...] = src[...].astype(BF16)


def _rms(x, n):
    return lax.rsqrt(jnp.sum(x * x, axis=-1, keepdims=True) * (1.0 / n) + NORM_EPS)


def _sigmoid(x):
    return 0.5 * jnp.tanh(0.5 * x) + 0.5


def _silu(x):
    h = 0.5 * x
    return h * jnp.tanh(h) + h


def _dot(a, b):
    return jnp.dot(a, b, preferred_element_type=F32)


def _dot_nt(a, b):
    return lax.dot_general(a, b, (((1,), (1,)), ((), ())), preferred_element_type=F32)


def _ffn_kernel(x_ref, nw_ref, wg_ref, wu_ref, wd_ref, *refs, n_cast, n_relayout):
    n_side = n_cast + n_relayout
    side_in, o_ref, side_out, xn_ref = (refs[:n_side], refs[n_side],
                                        refs[n_side + 1:2 * n_side + 1], refs[-1])
    d = x_ref.shape[-1]
    j = pl.program_id(1)

    def half_ffn(xn):
        _cast_blocks(side_in[:n_cast], side_out[:n_cast])
        for src, dst in zip(side_in[n_cast:], side_out[n_cast:]):
            _w_in_relayout(src, dst)
        g = _dot(xn, wg_ref[...])
        u = _dot(xn, wu_ref[...])
        return _dot((_silu(g) * (0.5 * u)).astype(BF16), wd_ref[...])

    @pl.when(j == 0)
    def _():
        x = x_ref[...]
        xn = (x * _rms(x, d) * nw_ref[...]).astype(BF16)
        xn_ref[...] = xn
        o_ref[...] = x + half_ffn(xn)

    @pl.when(j > 0)
    def _():
        o_ref[...] += half_ffn(xn_ref[...])


def _ffn(x, nw, wg, wu, wd, *, cast=(), relayout=(), side_layer=0, tm=1024, tf=512):
    m, d = x.shape
    f = wg.shape[1]
    tm = min(tm, m)
    grid = (m // tm, f // tf)
    specs = [_side_specs(cast, side_layer, grid), _side_specs(relayout, side_layer, grid, W_IN_OUT)]
    side_in, side_shapes, side_out = (sum((sp[k] for sp in specs), []) for k in range(3))
    out = pl.pallas_call(
        functools.partial(_ffn_kernel, n_cast=len(cast), n_relayout=len(relayout)),
        out_shape=[jax.ShapeDtypeStruct((m, d), F32)] + side_shapes,
        grid=grid,
        in_specs=[
            pl.BlockSpec((tm, d), lambda i, j: (i, 0)),
            pl.BlockSpec((1, d), lambda i, j: (0, 0)),
            pl.BlockSpec((d, tf), lambda i, j: (0, j)),
            pl.BlockSpec((d, tf), lambda i, j: (0, j)),
            pl.BlockSpec((tf, d), lambda i, j: (j, 0)),
        ] + side_in,
        out_specs=[pl.BlockSpec((tm, d), lambda i, j: (i, 0))] + side_out,
        scratch_shapes=[pltpu.VMEM((tm, d), BF16)],
        compiler_params=_cparams(("parallel", "arbitrary")),
        name="ffn",
    )(x, nw, wg, wu, wd, *cast, *relayout)
    return out[0], tuple(out[1:1 + len(cast)]), tuple(out[1 + len(cast):])


P_GATES = 0
P_RQ = 6144
P_RK = 7168
P_RV = 8192
P_RG = 9216
P_Z = 10240
P_XS = 11264
P_BC = 12288
P_KPE = 12800
P_QLAT = 13056
P_CKV = 13824
P_TOTAL = 14336


def _inproj_kernel(x_ref, nw_ref, w_ref, wdt_ref, *refs):
    n_cast = (len(refs) - 3) // 2
    cast_in, (o_ref, dt_ref), cast_out, xn_ref = (refs[:n_cast], refs[n_cast:n_cast + 2],
                                                  refs[n_cast + 2:2 * n_cast + 2], refs[-1])
    j = pl.program_id(1)
    d = x_ref.shape[-1]

    @pl.when(j == 0)
    def _():
        _cast_blocks(cast_in, cast_out)
        x = x_ref[...]
        xn = (x * _rms(x, d) * nw_ref[...]).astype(BF16)
        xn_ref[...] = xn
        dt_ref[...] = _dot(xn, wdt_ref[...])
        o_ref[...] = _dot(xn, w_ref[...]).astype(BF16)

    @pl.when(j > 0)
    def _():
        _cast_blocks(cast_in, cast_out)
        o_ref[...] = _dot(xn_ref[...], w_ref[...]).astype(BF16)


def _inproj(x, nw, w, *, cast=(), side_layer=0, tm=1024, tn=2048):
    m, d = x.shape
    n = P_TOTAL
    tm = min(tm, m)
    grid = (m // tm, n // tn)
    cast_in, cast_shapes, cast_out = _side_specs(cast, side_layer, grid)
    out = pl.pallas_call(
        _inproj_kernel,
        out_shape=[jax.ShapeDtypeStruct((m, n), BF16),
                   jax.ShapeDtypeStruct((m, LANES), F32)] + cast_shapes,
        grid=grid,
        in_specs=[
            pl.BlockSpec((tm, d), lambda i, j: (i, 0)),
            pl.BlockSpec((1, d), lambda i, j: (0, 0)),
            pl.BlockSpec((d, tn), lambda i, j: (0, j)),
            pl.BlockSpec((d, LANES), lambda i, j: (0, P_TOTAL // LANES)),
        ] + cast_in,
        out_specs=[
            pl.BlockSpec((tm, tn), lambda i, j: (i, j)),
            pl.BlockSpec((tm, LANES), lambda i, j: (i, 0)),
        ] + cast_out,
        scratch_shapes=[pltpu.VMEM((tm, d), BF16)],
        compiler_params=_cparams(("parallel", "arbitrary")),
        name="inproj",
    )(x, nw, w, w, *cast)
    return out[0], out[1], tuple(out[2:])


_W_IN_SEGMENTS = (
    (P_GATES, 8016, 6144), (P_RQ, 3920, 1024), (P_RK, 4944, 1024), (P_RV, 5968, 1024),
    (P_RG, 6992, 1024), (P_Z, 0, 1024), (P_XS, 1024, 1536), (P_KPE, 3856, MLA_ROPE),
    (P_QLAT, 2576, MLA_Q_LORA), (P_CKV, 3344, MLA_KV_LORA), (P_TOTAL, 2560, SSM_HEADS))
W_IN_OUT = P_TOTAL + LANES


def _w_in_relayout(x_ref, o_ref):
    n_src = x_ref.shape[1]
    segs = sorted(_W_IN_SEGMENTS)
    for k, (dest, _, width) in enumerate(segs):
        full = dest + width // LANES * LANES
        nxt = segs[k + 1][0] if k + 1 < len(segs) else o_ref.shape[1]
        if nxt > full:
            o_ref[:, full:nxt] = jnp.zeros((o_ref.shape[0], nxt - full), BF16)
    for dest, source, width in segs:
        lo = source // LANES * LANES
        hi = min(-(-(source + width) // LANES) * LANES, n_src)
        o_ref[:, dest:dest + width] = x_ref[:, lo:hi][:, source - lo:source - lo + width]


def _split3(x):
    hi = x.astype(BF16)
    r1 = x - hi.astype(F32)
    mid = r1.astype(BF16)
    lo = (r1 - mid.astype(F32)).astype(BF16)
    return hi, mid, lo


def _ssd_constants(L):
    t = np.arange(L)
    shifts = np.stack([(t[:, None] - t[None, :] == j) for j in range(1, SSM_CONV)])
    tril = t[:, None] >= t[None, :]
    expand = np.zeros((LANES, SSM_D_INNER), bool)
    for h in range(SSM_HEADS):
        expand[h, h * SSM_HEAD_DIM:(h + 1) * SSM_HEAD_DIM] = True
    expand2 = np.concatenate([expand, expand])
    expand4 = np.concatenate([expand, expand, expand, np.zeros_like(expand)])
    return tuple(jnp.asarray(m, BF16) for m in (shifts, tril, expand2, expand4))


SSD_TAIL = 8


def _ssd_reset(xtail_ref, btail_ref, state_ref):
    xtail_ref[0:SSD_TAIL, :] = jnp.zeros((SSD_TAIL, xtail_ref.shape[1]), F32)
    btail_ref[0:SSD_TAIL, :] = jnp.zeros((SSD_TAIL, btail_ref.shape[1]), F32)
    state_ref[...] = jnp.zeros(state_ref.shape, F32)


def _ssd_chunk(xs_ref, bc_ref, z_ref, dt_ref, cwx_ref, cbx_ref, cwb_ref, cbb_ref,
               dtb_ref, alog_ref, dskip_ref, nw_ref, shift_ref, tril_ref, e2_ref, e4_ref,
               o_ref, xtail_ref, btail_ref, state_ref):
    L = xs_ref.shape[0]
    P = SSM_HEAD_DIM
    N = SSM_STATE
    HG = SSM_HEADS // SSM_GROUPS
    GW = HG * P
    T = SSD_TAIL

    def conv_silu(tail_ref, in_ref, w_ref, b_ref):
        x16 = in_ref[...]
        xf = x16.astype(F32)
        acc = b_ref[...] + w_ref[SSM_CONV - 1:SSM_CONV, :] * xf
        for j in range(1, SSM_CONV):
            acc = acc + w_ref[SSM_CONV - 1 - j:SSM_CONV - j, :] * _dot(shift_ref[j - 1], x16)
        tail_ref[T:2 * T, :] = xf[0:T, :]
        head = b_ref[...] + w_ref[0:1, :] * tail_ref[T - 3:2 * T - 3, :]
        for j in range(1, SSM_CONV):
            head = head + w_ref[j:j + 1, :] * tail_ref[T - 3 + j:2 * T - 3 + j, :]
        tail_ref[0:T, :] = xf[L - T:L, :]
        return _silu(jnp.concatenate([head, acc[T:, :]], axis=0))

    xc = conv_silu(xtail_ref, xs_ref, cwx_ref, cbx_ref)
    bcc = conv_silu(btail_ref, bc_ref, cwb_ref, cbb_ref)

    dtr = dt_ref[...] + dtb_ref[...]
    dt = jnp.maximum(dtr, 0.0) + jnp.log1p(jnp.exp(-jnp.abs(dtr)))
    adt = dt * (-jnp.exp(alog_ref[...]) * math.log2(math.e))
    tril = tril_ref[...]
    hi, mid, lo = _split3(adt)
    acs = _dot(tril, hi) + _dot(tril, mid) + _dot(tril, lo)
    acs_t = acs.T

    hi, mid, _ = _split3(dt)
    dtx = _dot(jnp.concatenate([hi, mid], axis=1), e2_ref[...])
    hi, mid, lo = _split3(acs)
    ax = _dot(jnp.concatenate([hi, mid, lo, jnp.zeros_like(lo)], axis=1), e4_ref[...])
    a_last = ax[L - 1:L, :]
    xd = xc * dtx
    xd16 = xd.astype(BF16)
    xdd16 = (xd * jnp.exp2(a_last - ax)).astype(BF16)
    e_ax = jnp.exp2(ax)
    e_last = jnp.exp2(a_last)

    causal = (lax.broadcasted_iota(jnp.int32, (L, L), 0)
              >= lax.broadcasted_iota(jnp.int32, (L, L), 1))
    first_half = lax.broadcasted_iota(jnp.int32, (L, LANES), 1) < P
    tiles = []
    for g in range(SSM_GROUPS):
        bm = bcc[:, g * N:(g + 1) * N]
        cm16 = bcc[:, (SSM_GROUPS + g) * N:(SSM_GROUPS + g + 1) * N].astype(BF16)
        cb = _dot_nt(cm16, bm.astype(BF16))
        gs = slice(g * GW, (g + 1) * GW)
        prev = state_ref[g]
        y_off = _dot(cm16, prev.astype(BF16)) * e_ax[:, gs]
        state_ref[g] = prev * e_last[:, gs] + _dot(bm.T.astype(BF16), xdd16[:, gs])
        for t in range(GW // LANES):
            ts = slice(g * GW + t * LANES, g * GW + (t + 1) * LANES)
            pair = []
            for k in range(LANES // P):
                h = (g * GW + t * LANES) // P + k
                seg = acs[:, h:h + 1] - acs_t[h:h + 1, :]
                decay = jnp.exp2(jnp.where(causal, seg, -jnp.inf))
                pair.append(_dot((cb * decay).astype(BF16), xd16[:, ts]))
            tiles.append(jnp.where(first_half, pair[0], pair[1])
                         + y_off[:, t * LANES:(t + 1) * LANES])

    y = jnp.concatenate(tiles, axis=1)
    y = (y + dskip_ref[...] * xc) * _silu(z_ref[...].astype(F32))
    for g in range(SSM_GROUPS):
        gs = slice(g * GW, (g + 1) * GW)
        yg = y[:, gs]
        o_ref[:, gs] = (yg * _rms(yg, GW) * nw_ref[:, gs]).astype(BF16)


def _rope_tile(x, c, s1, s2):
    q = MLA_ROPE // 2
    return x * c + pltpu.roll(x, LANES - q, 1) * s1 + pltpu.roll(x, q, 1) * s2


def _mla_prep_kernel(ql_ref, ckv_ref, kpe_ref, c_ref, s1_ref, s2_ref, qan_ref, kvan_ref,
                     qn_ref, kn_ref, wq_ref, wkv_ref, q_out, k_out, v_out, *, q_scale):
    ql = ql_ref[...].astype(F32)
    qa = (ql * _rms(ql, MLA_Q_LORA) * qan_ref[...]).astype(BF16)
    ckv = ckv_ref[...].astype(F32)
    kva = (ckv * _rms(ckv, MLA_KV_LORA) * kvan_ref[...]).astype(BF16)
    kpe = kpe_ref[...].astype(F32)
    kpe_sq = kpe * kpe
    c, s1, s2 = c_ref[...], s1_ref[...], s2_ref[...]
    qn, kn = qn_ref[...] * q_scale, kn_ref[...]
    kpe_rot = _rope_tile(kpe * kn[:, MLA_NOPE:], c, s1, s2)
    ones = jnp.ones((kpe.shape[0], MLA_V), BF16)
    tile_sum = jnp.where(
        (lax.broadcasted_iota(jnp.int32, (2 * LANES, 2 * LANES), 0) < LANES)
        == (lax.broadcasted_iota(jnp.int32, (2 * LANES, 2 * LANES), 1) < LANES),
        1.0, 0.0).astype(BF16)

    def inv_rms_pair(sq_a, sq_b):
        ss = _dot(jnp.concatenate([sq_a, sq_b], axis=1).astype(BF16), tile_sum)
        r = lax.rsqrt(ss * (1.0 / MLA_QK_DIM) + NORM_EPS)
        return r[:, :LANES], r[:, LANES:]

    def project(h):
        return [(_dot(qa, wq_ref[h + i]), _dot(kva, wkv_ref[h + i])) for i in range(2)]

    nxt = project(0)
    for h0 in range(0, MLA_HEADS, 2):
        cur = nxt
        if h0 + 2 < MLA_HEADS:
            nxt = project(h0 + 2)
        q_sq = [qh * qh for qh, _ in cur]
        rq = inv_rms_pair(*(sq[:, :LANES] + sq[:, LANES:] for sq in q_sq))
        rk = inv_rms_pair(*(kvh[:, :MLA_NOPE] * kvh[:, :MLA_NOPE] + kpe_sq for _, kvh in cur))
        for i, (qh, kvh) in enumerate(cur):
            h = h0 + i
            qh = qh * jnp.concatenate([rq[i], rq[i]], axis=1) * qn
            q_out[h, :, 0:MLA_NOPE] = qh[:, 0:MLA_NOPE].astype(BF16)
            q_out[h, :, MLA_NOPE:] = _rope_tile(qh[:, MLA_NOPE:], c, s1, s2).astype(BF16)
            k_out[h, :, 0:MLA_NOPE] = (kvh[:, 0:MLA_NOPE] * rk[i] * kn[:, 0:MLA_NOPE]).astype(BF16)
            k_out[h, :, MLA_NOPE:] = (kpe_rot * rk[i]).astype(BF16)
            v_out[h, :, 0:MLA_V] = kvh[:, MLA_NOPE:].astype(BF16)
            v_out[h, :, MLA_V:] = ones


def _attn_kernel(q_ref, k_ref, v_ref, o_ref, s_ref, acc_ref, *, blk, heads):
    i = pl.program_id(2)

    def scores(j, slot):
        start = pl.multiple_of(j * blk, blk)
        for h in range(heads):
            s_ref[slot, h] = _dot_nt(q_ref[h], k_ref[h, pl.ds(start, blk), :])

    def consume(j, slot, m_old, masked):
        start = pl.multiple_of(j * blk, blk)
        probs = []
        for h, m in enumerate(m_old):
            s = s_ref[slot, h]
            if masked:
                row = lax.broadcasted_iota(jnp.int32, (blk, blk), 0)
                col = lax.broadcasted_iota(jnp.int32, (blk, blk), 1)
                s = jnp.where(row >= col, s, -jnp.inf)
            m_new = jnp.maximum(m, jnp.max(s, axis=-1, keepdims=True))
            probs.append((m_new, jnp.exp2(m - m_new), jnp.exp2(s - m_new).astype(BF16)))
        for h, (_, alpha, p) in enumerate(probs):
            acc_ref[h] = alpha * acc_ref[h] + _dot(p, v_ref[h, pl.ds(start, blk), :])
        return tuple(m_new for m_new, _, _ in probs)

    def finish():
        for h in range(heads):
            acc = acc_ref[h]
            o_ref[:, h * MLA_V:(h + 1) * MLA_V] = (acc[:, :MLA_V] / acc[:, MLA_V:]).astype(BF16)

    def pair(t, m):
        j = 2 * t
        scores(j + 1, 1)
        m = consume(j, 0, m, False)
        scores(j + 2, 0)
        return consume(j + 1, 1, m, False)

    scores(0, 0)
    acc_ref[...] = jnp.zeros(acc_ref.shape, F32)
    m = lax.fori_loop(0, i // 2, pair,
                      tuple(jnp.full((blk, 1), -jnp.inf, F32) for _ in range(heads)))
    last_even = 2 * (i // 2)

    @pl.when(i % 2 == 0)
    def _():
        consume(last_even, 0, m, True)
        finish()

    @pl.when(i % 2 == 1)
    def _():
        scores(last_even + 1, 1)
        consume(last_even + 1, 1, consume(last_even, 0, m, False), True)
        finish()


def _attention(q, k, v, *, batch, seq, blk=512, heads=4):
    blk = min(blk, seq)
    nq = seq // blk
    return pl.pallas_call(
        functools.partial(_attn_kernel, blk=blk, heads=heads),
        out_shape=jax.ShapeDtypeStruct((batch * seq, MLA_HEADS * MLA_V), BF16),
        grid=(batch, MLA_HEADS // heads, nq),
        in_specs=[
            pl.BlockSpec((heads, blk, MLA_QK_PAD), lambda b, h, i: (h, b * nq + i, 0)),
            pl.BlockSpec((heads, seq, MLA_QK_PAD), lambda b, h, i: (h, b, 0)),
            pl.BlockSpec((heads, seq, 2 * MLA_V), lambda b, h, i: (h, b, 0)),
        ],
        out_specs=pl.BlockSpec((blk, heads * MLA_V), lambda b, h, i: (b * nq + i, h)),
        scratch_shapes=[pltpu.VMEM((2, heads, blk, blk), F32),
                        pltpu.VMEM((heads, blk, 2 * MLA_V), F32)],
        compiler_params=_cparams(("parallel", "parallel", "arbitrary")),
        name="mla_attention",
    )(q, k, v)


def _ret_constants(L):
    expo = 5.0 + 7.0 * np.arange(RET_HEADS, dtype=np.float32) / np.float32(RET_HEADS - 1)
    log_gamma = np.log1p(-np.exp2(-expo)).astype(np.float32)[:, None, None]
    pos = np.arange(L, dtype=np.float32)
    rel = pos[:, None] - pos[None, :]
    dmask = np.where(rel >= 0, np.exp(rel * log_gamma), 0.0).astype(np.float32)
    lanes = np.ones((1, 1, RET_HEAD), np.float32)
    q_dec = np.exp((pos + 1.0)[None, :, None] * log_gamma).astype(np.float32) * lanes
    k_dec = np.exp((L - 1.0 - pos)[None, :, None] * log_gamma).astype(np.float32) * lanes
    chunk_dec = [float(v) for v in np.exp(L * log_gamma[:, 0, 0])]
    return jnp.asarray(dmask), jnp.asarray(q_dec), jnp.asarray(k_dec), chunk_dec


def _ret_chunk(rq_ref, rk_ref, rv_ref, rg_ref, cos_ref, sin_ref, nw_ref, dmask_ref, qdec_ref,
               kdec_ref, o_ref, state_ref, *, chunk_dec):
    D = RET_HEAD
    half = D // 2
    cos, sin = cos_ref[...], sin_ref[...]

    def rope(ref, hs):
        x1 = ref[:, hs.start:hs.start + half].astype(F32)
        x2 = ref[:, hs.start + half:hs.stop].astype(F32)
        return jnp.concatenate([x1 * cos - x2 * sin, x1 * sin + x2 * cos], axis=-1)

    for h in range(RET_HEADS):
        hs = slice(h * D, (h + 1) * D)
        q = rope(rq_ref, hs)
        k = rope(rk_ref, hs) * (RET_HEAD ** -0.5)
        v16 = rv_ref[:, hs]
        q16 = q.astype(BF16)
        scores = _dot_nt(q16, k.astype(BF16)) * dmask_ref[h]
        y = _dot(scores.astype(BF16), v16)
        prev = state_ref[h]
        y = y + _dot(q16, prev.astype(BF16)) * qdec_ref[h]
        kd = k * kdec_ref[h]
        state_ref[h] = prev * chunk_dec[h] + _dot(kd.T.astype(BF16), v16)
        y = y * _rms(y, D) * nw_ref[:, hs]
        o_ref[:, hs] = (_silu(rg_ref[:, hs].astype(F32)) * y).astype(BF16)


N_SSD_IN, N_RET_IN, N_MLA_IN, N_SSD_SCRATCH = 16, 10, 12, 3


def _scans_kernel(*refs, chunk_dec, q_scale):
    n_in = N_SSD_IN + N_RET_IN + N_MLA_IN
    ssd_in, ret_in, mla_in = (refs[:N_SSD_IN], refs[N_SSD_IN:N_SSD_IN + N_RET_IN],
                              refs[N_SSD_IN + N_RET_IN:n_in])
    y_ssm_ref, y_ret_ref, q_ref, k_ref, v_ref = refs[n_in:n_in + 5]
    scratch = refs[n_in + 5:]
    ssd_scratch, ret_state = scratch[:N_SSD_SCRATCH], scratch[N_SSD_SCRATCH]

    @pl.when(pl.program_id(1) == 0)
    def _():
        _ssd_reset(*ssd_scratch)
        ret_state[...] = jnp.zeros(ret_state.shape, F32)

    _mla_prep_kernel(*mla_in, q_ref, k_ref, v_ref, q_scale=q_scale)
    _ssd_chunk(*ssd_in, y_ssm_ref, *ssd_scratch)
    _ret_chunk(*ret_in, y_ret_ref, ret_state, chunk_dec=chunk_dec)


def _scans(proj, dt_raw, cwx, cbx, cwb, cbb, dtb, alog, dskip, ssm_nw, cos, sin, ret_nw,
           rope_c, rope_s1, rope_s2, qan, kvan, qn, kn, wq, wkv, *, batch, seq):
    L = math.gcd(seq, SSD_CHUNK)
    assert L == math.gcd(seq, RET_CHUNK)
    nc = seq // L
    di = SSM_D_INNER
    bcw = 2 * SSM_GROUPS * SSM_STATE
    w = RET_HEADS * RET_HEAD
    tok = lambda b, c: b * nc + c
    blk = lambda width, col: pl.BlockSpec((L, width), lambda b, c: (tok(b, c), col // width))
    ssd_params = (cwx, cbx, cwb, cbb, dtb, alog, dskip, ssm_nw) + _ssd_constants(L)
    dmask, q_dec, k_dec, chunk_dec = _ret_constants(L)
    ret_params = (ret_nw, dmask, q_dec, k_dec)
    mla_params = (qan, kvan, qn, kn, wq, wkv)
    assert (4 + len(ssd_params), 6 + len(ret_params), 6 + len(mla_params)) == (
        N_SSD_IN, N_RET_IN, N_MLA_IN)
    m = batch * seq
    heads_spec = lambda width: pl.BlockSpec((MLA_HEADS, L, width), lambda b, c: (0, tok(b, c), 0))
    return pl.pallas_call(
        functools.partial(_scans_kernel, chunk_dec=chunk_dec,
                          q_scale=MLA_QK_DIM ** -0.5 * math.log2(math.e)),
        out_shape=(jax.ShapeDtypeStruct((m, di), BF16), jax.ShapeDtypeStruct((m, w), BF16),
                   jax.ShapeDtypeStruct((MLA_HEADS, m, MLA_QK_PAD), BF16),
                   jax.ShapeDtypeStruct((MLA_HEADS, m, MLA_QK_PAD), BF16),
                   jax.ShapeDtypeStruct((MLA_HEADS, m, 2 * MLA_V), BF16)),
        grid=(batch, nc),
        in_specs=[blk(di, P_XS), blk(bcw, P_BC), blk(di, P_Z), blk(LANES, 0)]
        + [_resident(p.shape) for p in ssd_params]
        + [blk(w, P_RQ), blk(w, P_RK), blk(w, P_RV), blk(w, P_RG),
           blk(RET_HEAD // 2, 0), blk(RET_HEAD // 2, 0)]
        + [_resident(p.shape) for p in ret_params]
        + [blk(MLA_Q_LORA, P_QLAT), blk(MLA_KV_LORA, P_CKV), blk(LANES, P_KPE),
           blk(LANES, 0), blk(LANES, 0), blk(LANES, 0)]
        + [_resident(p.shape) for p in mla_params],
        out_specs=(blk(di, 0), blk(w, 0), heads_spec(MLA_QK_PAD), heads_spec(MLA_QK_PAD),
                   heads_spec(2 * MLA_V)),
        scratch_shapes=[
            pltpu.VMEM((2 * SSD_TAIL, di), F32),
            pltpu.VMEM((2 * SSD_TAIL, bcw), F32),
            pltpu.VMEM((SSM_GROUPS, SSM_STATE, di // SSM_GROUPS), F32),
            pltpu.VMEM((RET_HEADS, RET_HEAD, RET_HEAD), F32),
        ],
        compiler_params=_cparams(("parallel", "arbitrary")),
        name="scans",
    )(proj, proj, proj, dt_raw, *ssd_params, proj, proj, proj, proj, cos, sin, *ret_params,
      proj, proj, proj, rope_c, rope_s1, rope_s2, *mla_params)


def _merge_kernel(ys_ref, ym_ref, yr_ref, g0_ref, g1_ref, g2_ref, gb_ref, x_ref,
                  w0_ref, w1_ref, w2_ref, wo_ref, o_ref):
    d = x_ref.shape[-1]

    def branch(k, y_ref, g_ref, w_ref):
        gate = _sigmoid(g_ref[...].astype(F32) + gb_ref[:, k * d:(k + 1) * d])
        return gate * _dot(y_ref[...], w_ref[...])

    merged = (branch(0, ys_ref, g0_ref, w0_ref) + branch(1, ym_ref, g1_ref, w1_ref)
              + branch(2, yr_ref, g2_ref, w2_ref))
    o_ref[...] = x_ref[...] + _dot(merged.astype(BF16), wo_ref[...])


def _merge(ys, ym, yr, proj, gb, x, w0, w1, w2, wo, *, tm=256):
    m, d = x.shape
    tm = min(tm, m)
    bw = ys.shape[1]
    row = lambda i: (i, 0)
    return pl.pallas_call(
        _merge_kernel,
        out_shape=jax.ShapeDtypeStruct((m, d), F32),
        grid=(m // tm,),
        in_specs=[
            pl.BlockSpec((tm, bw), row), pl.BlockSpec((tm, bw), row), pl.BlockSpec((tm, bw), row),
            pl.BlockSpec((tm, d), lambda i: (i, P_GATES // d)),
            pl.BlockSpec((tm, d), lambda i: (i, P_GATES // d + 1)),
            pl.BlockSpec((tm, d), lambda i: (i, P_GATES // d + 2)),
            _resident(gb.shape),
            pl.BlockSpec((tm, d), row),
            _resident(w0.shape), _resident(w1.shape), _resident(w2.shape), _resident(wo.shape),
        ],
        out_specs=pl.BlockSpec((tm, d), row),
        compiler_params=_cparams(("parallel",)),
        name="merge",
    )(ys, ym, yr, proj, proj, proj, gb, x, w0, w1, w2, wo)


def _rope_angles(positions, dim):
    inv = 1.0 / (ROPE_THETA ** (jnp.arange(0, dim, 2, dtype=F32) / dim))
    return positions.astype(F32).reshape(-1, 1) * inv


def _pad_lanes(v, n):
    return jnp.concatenate([v, jnp.zeros((n - v.shape[0],), v.dtype)]).reshape(1, n)


def kernel(x, positions, ffn1_norm, ffn1_w_gate, ffn1_w_up, ffn1_w_down, mix_norm, w_in, gate_b, conv_w, conv_b, dt_bias, a_log, d_skip, ssm_norm, q_a_norm, w_q_b, kv_a_norm, w_kv_b, q_norm, k_norm, ret_norm, w_br_ssm, w_br_mla, w_br_ret, w_out, ffn2_norm, ffn2_w_gate, ffn2_w_up, ffn2_w_down):
    batch, seq, d = x.shape
    depth = w_in.shape[0]
    x = x.reshape(batch * seq, d)

    ang = _rope_angles(positions, MLA_ROPE)
    cm, sm = jnp.cos(ang), jnp.sin(ang)
    zq = jnp.zeros_like(cm)
    rope_c = jnp.concatenate([cm, cm, zq, zq], axis=1)
    rope_s1 = jnp.concatenate([-sm, zq, zq, zq], axis=1)
    rope_s2 = jnp.concatenate([zq, sm, zq, zq], axis=1)
    ang = _rope_angles(positions, RET_HEAD)
    cos_ret, sin_ret = jnp.cos(ang), jnp.sin(ang)

    w_pad = jnp.pad(w_in.astype(BF16), ((0, 0), (0, 0), (0, -w_in.shape[2] % LANES)))
    ffn1_f32 = (ffn1_w_gate, ffn1_w_up, ffn1_w_down)
    ffn2_f32 = (ffn2_w_gate, ffn2_w_up, ffn2_w_down)
    merge_f32 = (w_br_ssm, w_br_mla, w_br_ret, w_out)
    ffn1 = tuple(w[0].astype(BF16) for w in ffn1_f32)
    row = lambda v: v.reshape(1, -1)
    for l in range(depth):
        x, ffn2, (w16,) = _ffn(x, row(ffn1_norm[l]), *ffn1, cast=ffn2_f32, relayout=(w_pad,),
                               side_layer=l)

        proj, dt_raw, w_merge = _inproj(x, row(mix_norm[l]), w16, cast=merge_f32, side_layer=l)

        wq = w_q_b[l].reshape(MLA_Q_LORA, MLA_HEADS, MLA_QK_DIM)
        wq = jnp.pad(wq, ((0, 0), (0, 0), (0, MLA_QK_PAD - MLA_QK_DIM)))
        wq = wq.transpose(1, 0, 2).astype(BF16)
        wkv = w_kv_b[l].reshape(MLA_KV_LORA, MLA_HEADS, MLA_NOPE + MLA_V)
        wkv = wkv.transpose(1, 0, 2).astype(BF16)
        y_ssm, y_ret, q, k, v = _scans(
            proj, dt_raw,
            conv_w[l][:, :SSM_D_INNER], row(conv_b[l][:SSM_D_INNER]),
            conv_w[l][:, SSM_D_INNER:], row(conv_b[l][SSM_D_INNER:]),
            _pad_lanes(dt_bias[l], LANES), _pad_lanes(a_log[l], LANES),
            row(jnp.repeat(d_skip[l], SSM_HEAD_DIM)), row(ssm_norm[l]),
            cos_ret, sin_ret, row(ret_norm[l]),
            rope_c, rope_s1, rope_s2, row(q_a_norm[l]), row(kv_a_norm[l]),
            _pad_lanes(q_norm[l], MLA_QK_PAD), _pad_lanes(k_norm[l], MLA_QK_PAD), wq, wkv,
            batch=batch, seq=seq)
        y_mla = _attention(q, k, v, batch=batch, seq=seq)

        x = _merge(y_ssm, y_mla, y_ret, proj, row(gate_b[l]), x, *w_merge)

        if l + 1 < depth:
            x, ffn1, _ = _ffn(x, row(ffn2_norm[l]), *ffn2, cast=ffn1_f32, side_layer=l + 1)
        else:
            x = _ffn(x, row(ffn2_norm[l]), *ffn2)[0]
    return x.reshape(batch, seq, d)
```

```python
import functools
import math

import jax
import jax.numpy as jnp
import numpy as np
from jax import lax
from jax.experimental import pallas as pl
from jax.experimental.pallas import tpu as pltpu

F32 = jnp.float32
BF16 = jnp.bfloat16

NORM_EPS = 1e-6
ROPE_THETA = 10000.0

SSM_HEADS = 16
SSM_HEAD_DIM = 64
SSM_D_INNER = SSM_HEADS * SSM_HEAD_DIM
SSM_GROUPS = 2
SSM_STATE = 128
SSM_CONV = 4
SSD_CHUNK = 256
MLA_HEADS = 8
MLA_Q_LORA = 768
MLA_KV_LORA = 512
MLA_NOPE = 128
MLA_ROPE = 64
MLA_QK_DIM = MLA_NOPE + MLA_ROPE
MLA_V = 128
RET_HEADS = 4
RET_HEAD = 256
RET_CHUNK = 256
N_BRANCH = 3

LANES = 128
MLA_QK_PAD = MLA_NOPE + LANES
VMEM_LIMIT = 60 * 1024 * 1024


def _cparams(sem):
    return pltpu.CompilerParams(dimension_semantics=sem, vmem_limit_bytes=VMEM_LIMIT)


def _resident(shape):
    nd = len(shape)
    return pl.BlockSpec(shape, lambda *_: (0,) * nd, pipeline_mode=pl.Buffered(1))


BF16_SUBLANES = 16


def _row_blocks(rows, grid):
    steps = grid[0] * grid[1]
    br = next(b for b in range(BF16_SUBLANES, rows + 1, BF16_SUBLANES)
              if rows % b == 0 and rows // b <= steps)
    return br, lambda i, j: jnp.minimum(i * grid[1] + j, rows // br - 1)


def _side_specs(arrays, layer, grid, out_cols=None):
    in_specs, out_shapes, out_specs = [], [], []
    for a in arrays:
        _, rows, cols = a.shape
        br, block = _row_blocks(rows, grid)
        oc = out_cols or cols
        in_specs.append(pl.BlockSpec((None, br, cols),
                                     lambda i, j, block=block: (layer, block(i, j), 0)))
        out_shapes.append(jax.ShapeDtypeStruct((rows, oc), BF16))
        out_specs.append(pl.BlockSpec((br, oc), lambda i, j, block=block: (block(i, j), 0)))
    return in_specs, out_shapes, out_specs


def _cast_blocks(src_refs, dst_refs):
    for src, dst in zip(src_refs, dst_refs):
        dst[...] = src[...].astype(BF16)


def _rms(x, n):
    return lax.rsqrt(jnp.sum(x * x, axis=-1, keepdims=True) * (1.0 / n) + NORM_EPS)


def _sigmoid(x):
    return 0.5 * jnp.tanh(0.5 * x) + 0.5


def _silu(x):
    h = 0.5 * x
    return h * jnp.tanh(h) + h


def _dot(a, b):
    return jnp.dot(a, b, preferred_element_type=F32)


def _dot_nt(a, b):
    return lax.dot_general(a, b, (((1,), (1,)), ((), ())), preferred_element_type=F32)


def _ffn_kernel(x_ref, nw_ref, wg_ref, wu_ref, wd_ref, *refs, n_cast, n_relayout):
    n_side = n_cast + n_relayout
    side_in, o_ref, side_out, xn_ref = (refs[:n_side], refs[n_side],
                                        refs[n_side + 1:2 * n_side + 1], refs[-1])
    d = x_ref.shape[-1]
    j = pl.program_id(1)

    def half_ffn(xn):
        _cast_blocks(side_in[:n_cast], side_out[:n_cast])
        for src, dst in zip(side_in[n_cast:], side_out[n_cast:]):
            _w_in_relayout(src, dst)
        g = _dot(xn, wg_ref[...])
        u = _dot(xn, wu_ref[...])
        return _dot((_silu(g) * (0.5 * u)).astype(BF16), wd_ref[...])

    @pl.when(j == 0)
    def _():
        x = x_ref[...]
        xn = (x * _rms(x, d) * nw_ref[...]).astype(BF16)
        xn_ref[...] = xn
        o_ref[...] = x + half_ffn(xn)

    @pl.when(j > 0)
    def _():
        o_ref[...] += half_ffn(xn_ref[...])


def _ffn(x, nw, wg, wu, wd, *, cast=(), relayout=(), side_layer=0, tm=1024, tf=512):
    m, d = x.shape
    f = wg.shape[1]
    tm = min(tm, m)
    grid = (m // tm, f // tf)
    specs = [_side_specs(cast, side_layer, grid), _side_specs(relayout, side_layer, grid, W_IN_OUT)]
    side_in, side_shapes, side_out = (sum((sp[k] for sp in specs), []) for k in range(3))
    out = pl.pallas_call(
        functools.partial(_ffn_kernel, n_cast=len(cast), n_relayout=len(relayout)),
        out_shape=[jax.ShapeDtypeStruct((m, d), F32)] + side_shapes,
        grid=grid,
        in_specs=[
            pl.BlockSpec((tm, d), lambda i, j: (i, 0)),
            pl.BlockSpec((1, d), lambda i, j: (0, 0)),
            pl.BlockSpec((d, tf), lambda i, j: (0, j)),
            pl.BlockSpec((d, tf), lambda i, j: (0, j)),
            pl.BlockSpec((tf, d), lambda i, j: (j, 0)),
        ] + side_in,
        out_specs=[pl.BlockSpec((tm, d), lambda i, j: (i, 0))] + side_out,
        scratch_shapes=[pltpu.VMEM((tm, d), BF16)],
        compiler_params=_cparams(("parallel", "arbitrary")),
        name="ffn",
    )(x, nw, wg, wu, wd, *cast, *relayout)
    return out[0], tuple(out[1:1 + len(cast)]), tuple(out[1 + len(cast):])


P_GATES = 0
P_RQ = 6144
P_RK = 7168
P_RV = 8192
P_RG = 9216
P_Z = 10240
P_XS = 11264
P_BC = 12288
P_KPE = 12800
P_QLAT = 13056
P_CKV = 13824
P_TOTAL = 14336


def _inproj_kernel(x_ref, nw_ref, w_ref, wdt_ref, *refs):
    n_cast = (len(refs) - 3) // 2
    cast_in, (o_ref, dt_ref), cast_out, xn_ref = (refs[:n_cast], refs[n_cast:n_cast + 2],
                                                  refs[n_cast + 2:2 * n_cast + 2], refs[-1])
    j = pl.program_id(1)
    d = x_ref.shape[-1]

    @pl.when(j == 0)
    def _():
        _cast_blocks(cast_in, cast_out)
        x = x_ref[...]
        xn = (x * _rms(x, d) * nw_ref[...]).astype(BF16)
        xn_ref[...] = xn
        dt_ref[...] = _dot(xn, wdt_ref[...])
        o_ref[...] = _dot(xn, w_ref[...]).astype(BF16)

    @pl.when(j > 0)
    def _():
        _cast_blocks(cast_in, cast_out)
        o_ref[...] = _dot(xn_ref[...], w_ref[...]).astype(BF16)


def _inproj(x, nw, w, *, cast=(), side_layer=0, tm=1024, tn=2048):
    m, d = x.shape
    n = P_TOTAL
    tm = min(tm, m)
    grid = (m // tm, n // tn)
    cast_in, cast_shapes, cast_out = _side_specs(cast, side_layer, grid)
    out = pl.pallas_call(
        _inproj_kernel,
        out_shape=[jax.ShapeDtypeStruct((m, n), BF16),
                   jax.ShapeDtypeStruct((m, LANES), F32)] + cast_shapes,
        grid=grid,
        in_specs=[
            pl.BlockSpec((tm, d), lambda i, j: (i, 0)),
            pl.BlockSpec((1, d), lambda i, j: (0, 0)),
            pl.BlockSpec((d, tn), lambda i, j: (0, j)),
            pl.BlockSpec((d, LANES), lambda i, j: (0, P_TOTAL // LANES)),
        ] + cast_in,
        out_specs=[
            pl.BlockSpec((tm, tn), lambda i, j: (i, j)),
            pl.BlockSpec((tm, LANES), lambda i, j: (i, 0)),
        ] + cast_out,
        scratch_shapes=[pltpu.VMEM((tm, d), BF16)],
        compiler_params=_cparams(("parallel", "arbitrary")),
        name="inproj",
    )(x, nw, w, w, *cast)
    return out[0], out[1], tuple(out[2:])


_W_IN_SEGMENTS = (
    (P_GATES, 8016, 6144), (P_RQ, 3920, 1024), (P_RK, 4944, 1024), (P_RV, 5968, 1024),
    (P_RG, 6992, 1024), (P_Z, 0, 1024), (P_XS, 1024, 1536), (P_KPE, 3856, MLA_ROPE),
    (P_QLAT, 2576, MLA_Q_LORA), (P_CKV, 3344, MLA_KV_LORA), (P_TOTAL, 2560, SSM_HEADS))
W_IN_OUT = P_TOTAL + LANES


def _w_in_relayout(x_ref, o_ref):
    n_src = x_ref.shape[1]
    segs = sorted(_W_IN_SEGMENTS)
    for k, (dest, _, width) in enumerate(segs):
        full = dest + width // LANES * LANES
        nxt = segs[k + 1][0] if k + 1 < len(segs) else o_ref.shape[1]
        if nxt > full:
            o_ref[:, full:nxt] = jnp.zeros((o_ref.shape[0], nxt - full), BF16)
    for dest, source, width in segs:
        lo = source // LANES * LANES
        hi = min(-(-(source + width) // LANES) * LANES, n_src)
        o_ref[:, dest:dest + width] = x_ref[:, lo:hi][:, source - lo:source - lo + width]


def _split3(x):
    hi = x.astype(BF16)
    r1 = x - hi.astype(F32)
    mid = r1.astype(BF16)
    lo = (r1 - mid.astype(F32)).astype(BF16)
    return hi, mid, lo


def _ssd_constants(L):
    t = np.arange(L)
    shifts = np.stack([(t[:, None] - t[None, :] == j) for j in range(1, SSM_CONV)])
    tril = t[:, None] >= t[None, :]
    expand = np.zeros((LANES, SSM_D_INNER), bool)
    for h in range(SSM_HEADS):
        expand[h, h * SSM_HEAD_DIM:(h + 1) * SSM_HEAD_DIM] = True
    expand2 = np.concatenate([expand, expand])
    expand4 = np.concatenate([expand, expand, expand, np.zeros_like(expand)])
    return tuple(jnp.asarray(m, BF16) for m in (shifts, tril, expand2, expand4))


SSD_TAIL = 8


def _ssd_reset(xtail_ref, btail_ref, state_ref):
    xtail_ref[0:SSD_TAIL, :] = jnp.zeros((SSD_TAIL, xtail_ref.shape[1]), F32)
    btail_ref[0:SSD_TAIL, :] = jnp.zeros((SSD_TAIL, btail_ref.shape[1]), F32)
    state_ref[...] = jnp.zeros(state_ref.shape, F32)


def _ssd_chunk(xs_ref, bc_ref, z_ref, dt_ref, cwx_ref, cbx_ref, cwb_ref, cbb_ref,
               dtb_ref, alog_ref, dskip_ref, nw_ref, shift_ref, tril_ref, e2_ref, e4_ref,
               o_ref, xtail_ref, btail_ref, state_ref):
    L = xs_ref.shape[0]
    P = SSM_HEAD_DIM
    N = SSM_STATE
    HG = SSM_HEADS // SSM_GROUPS
    GW = HG * P
    T = SSD_TAIL

    def conv_silu(tail_ref, in_ref, w_ref, b_ref):
        x16 = in_ref[...]
        xf = x16.astype(F32)
        acc = b_ref[...] + w_ref[SSM_CONV - 1:SSM_CONV, :] * xf
        for j in range(1, SSM_CONV):
            acc = acc + w_ref[SSM_CONV - 1 - j:SSM_CONV - j, :] * _dot(shift_ref[j - 1], x16)
        tail_ref[T:2 * T, :] = xf[0:T, :]
        head = b_ref[...] + w_ref[0:1, :] * tail_ref[T - 3:2 * T - 3, :]
        for j in range(1, SSM_CONV):
            head = head + w_ref[j:j + 1, :] * tail_ref[T - 3 + j:2 * T - 3 + j, :]
        tail_ref[0:T, :] = xf[L - T:L, :]
        return _silu(jnp.concatenate([head, acc[T:, :]], axis=0))

    xc = conv_silu(xtail_ref, xs_ref, cwx_ref, cbx_ref)
    bcc = conv_silu(btail_ref, bc_ref, cwb_ref, cbb_ref)

    dtr = dt_ref[...] + dtb_ref[...]
    dt = jnp.maximum(dtr, 0.0) + jnp.log1p(jnp.exp(-jnp.abs(dtr)))
    adt = dt * (-jnp.exp(alog_ref[...]) * math.log2(math.e))
    tril = tril_ref[...]
    hi, mid, lo = _split3(adt)
    acs = _dot(tril, hi) + _dot(tril, mid) + _dot(tril, lo)
    acs_t = acs.T

    hi, mid, _ = _split3(dt)
    dtx = _dot(jnp.concatenate([hi, mid], axis=1), e2_ref[...])
    hi, mid, lo = _split3(acs)
    ax = _dot(jnp.concatenate([hi, mid, lo, jnp.zeros_like(lo)], axis=1), e4_ref[...])
    a_last = ax[L - 1:L, :]
    xd = xc * dtx
    xd16 = xd.astype(BF16)
    xdd16 = (xd * jnp.exp2(a_last - ax)).astype(BF16)
    e_ax = jnp.exp2(ax)
    e_last = jnp.exp2(a_last)

    causal = (lax.broadcasted_iota(jnp.int32, (L, L), 0)
              >= lax.broadcasted_iota(jnp.int32, (L, L), 1))
    first_half = lax.broadcasted_iota(jnp.int32, (L, LANES), 1) < P
    tiles = []
    for g in range(SSM_GROUPS):
        bm = bcc[:, g * N:(g + 1) * N]
        cm16 = bcc[:, (SSM_GROUPS + g) * N:(SSM_GROUPS + g + 1) * N].astype(BF16)
        cb = _dot_nt(cm16, bm.astype(BF16))
        gs = slice(g * GW, (g + 1) * GW)
        prev = state_ref[g]
        y_off = _dot(cm16, prev.astype(BF16)) * e_ax[:, gs]
        state_ref[g] = prev * e_last[:, gs] + _dot(bm.T.astype(BF16), xdd16[:, gs])
        for t in range(GW // LANES):
            ts = slice(g * GW + t * LANES, g * GW + (t + 1) * LANES)
            pair = []
            for k in range(LANES // P):
                h = (g * GW + t * LANES) // P + k
                seg = acs[:, h:h + 1] - acs_t[h:h + 1, :]
                decay = jnp.exp2(jnp.where(causal, seg, -jnp.inf))
                pair.append(_dot((cb * decay).astype(BF16), xd16[:, ts]))
            tiles.append(jnp.where(first_half, pair[0], pair[1])
                         + y_off[:, t * LANES:(t + 1) * LANES])

    y = jnp.concatenate(tiles, axis=1)
    y = (y + dskip_ref[...] * xc) * _silu(z_ref[...].astype(F32))
    for g in range(SSM_GROUPS):
        gs = slice(g * GW, (g + 1) * GW)
        yg = y[:, gs]
        o_ref[:, gs] = (yg * _rms(yg, GW) * nw_ref[:, gs]).astype(BF16)


def _rope_tile(x, c, s1, s2):
    q = MLA_ROPE // 2
    return x * c + pltpu.roll(x, LANES - q, 1) * s1 + pltpu.roll(x, q, 1) * s2


def _mla_prep_kernel(ql_ref, ckv_ref, kpe_ref, c_ref, s1_ref, s2_ref, qan_ref, kvan_ref,
                     qn_ref, kn_ref, wq_ref, wkv_ref, q_out, k_out, v_out, *, q_scale):
    ql = ql_ref[...].astype(F32)
    qa = (ql * _rms(ql, MLA_Q_LORA) * qan_ref[...]).astype(BF16)
    ckv = ckv_ref[...].astype(F32)
    kva = (ckv * _rms(ckv, MLA_KV_LORA) * kvan_ref[...]).astype(BF16)
    kpe = kpe_ref[...].astype(F32)
    kpe_sq = kpe * kpe
    c, s1, s2 = c_ref[...], s1_ref[...], s2_ref[...]
    qn, kn = qn_ref[...] * q_scale, kn_ref[...]
    kpe_rot = _rope_tile(kpe * kn[:, MLA_NOPE:], c, s1, s2)
    ones = jnp.ones((kpe.shape[0], MLA_V), BF16)
    tile_sum = jnp.where(
        (lax.broadcasted_iota(jnp.int32, (2 * LANES, 2 * LANES), 0) < LANES)
        == (lax.broadcasted_iota(jnp.int32, (2 * LANES, 2 * LANES), 1) < LANES),
        1.0, 0.0).astype(BF16)

    def inv_rms_pair(sq_a, sq_b):
        ss = _dot(jnp.concatenate([sq_a, sq_b], axis=1).astype(BF16), tile_sum)
        r = lax.rsqrt(ss * (1.0 / MLA_QK_DIM) + NORM_EPS)
        return r[:, :LANES], r[:, LANES:]

    def project(h):
        return [(_dot(qa, wq_ref[h + i]), _dot(kva, wkv_ref[h + i])) for i in range(2)]

    nxt = project(0)
    for h0 in range(0, MLA_HEADS, 2):
        cur = nxt
        if h0 + 2 < MLA_HEADS:
            nxt = project(h0 + 2)
        q_sq = [qh * qh for qh, _ in cur]
        rq = inv_rms_pair(*(sq[:, :LANES] + sq[:, LANES:] for sq in q_sq))
        rk = inv_rms_pair(*(kvh[:, :MLA_NOPE] * kvh[:, :MLA_NOPE] + kpe_sq for _, kvh in cur))
        for i, (qh, kvh) in enumerate(cur):
            h = h0 + i
            qh = qh * jnp.concatenate([rq[i], rq[i]], axis=1) * qn
            q_out[h, :, 0:MLA_NOPE] = qh[:, 0:MLA_NOPE].astype(BF16)
            q_out[h, :, MLA_NOPE:] = _rope_tile(qh[:, MLA_NOPE:], c, s1, s2).astype(BF16)
            k_out[h, :, 0:MLA_NOPE] = (kvh[:, 0:MLA_NOPE] * rk[i] * kn[:, 0:MLA_NOPE]).astype(BF16)
            k_out[h, :, MLA_NOPE:] = (kpe_rot * rk[i]).astype(BF16)
            v_out[h, :, 0:MLA_V] = kvh[:, MLA_NOPE:].astype(BF16)
            v_out[h, :, MLA_V:] = ones


def _attn_kernel(q_ref, k_ref, v_ref, o_ref, s_ref, acc_ref, *, blk, heads):
    i = pl.program_id(2)

    def scores(j, slot):
        start = pl.multiple_of(j * blk, blk)
        for h in range(heads):
            s_ref[slot, h] = _dot_nt(q_ref[h], k_ref[h, pl.ds(start, blk), :])

    def consume(j, slot, m_old, masked):
        start = pl.multiple_of(j * blk, blk)
        probs = []
        for h, m in enumerate(m_old):
            s = s_ref[slot, h]
            if masked:
                row = lax.broadcasted_iota(jnp.int32, (blk, blk), 0)
                col = lax.broadcasted_iota(jnp.int32, (blk, blk), 1)
                s = jnp.where(row >= col, s, -jnp.inf)
            m_new = jnp.maximum(m, jnp.max(s, axis=-1, keepdims=True))
            probs.append((m_new, jnp.exp2(m - m_new), jnp.exp2(s - m_new).astype(BF16)))
        for h, (_, alpha, p) in enumerate(probs):
            acc_ref[h] = alpha * acc_ref[h] + _dot(p, v_ref[h, pl.ds(start, blk), :])
        return tuple(m_new for m_new, _, _ in probs)

    def finish():
        for h in range(heads):
            acc = acc_ref[h]
            o_ref[:, h * MLA_V:(h + 1) * MLA_V] = (acc[:, :MLA_V] / acc[:, MLA_V:]).astype(BF16)

    def pair(t, m):
        j = 2 * t
        scores(j + 1, 1)
        m = consume(j, 0, m, False)
        scores(j + 2, 0)
        return consume(j + 1, 1, m, False)

    scores(0, 0)
    acc_ref[...] = jnp.zeros(acc_ref.shape, F32)
    m = lax.fori_loop(0, i // 2, pair,
                      tuple(jnp.full((blk, 1), -jnp.inf, F32) for _ in range(heads)))
    last_even = 2 * (i // 2)

    @pl.when(i % 2 == 0)
    def _():
        consume(last_even, 0, m, True)
        finish()

    @pl.when(i % 2 == 1)
    def _():
        scores(last_even + 1, 1)
        consume(last_even + 1, 1, consume(last_even, 0, m, False), True)
        finish()


def _attention(q, k, v, *, batch, seq, blk=512, heads=4):
    blk = min(blk, seq)
    nq = seq // blk
    return pl.pallas_call(
        functools.partial(_attn_kernel, blk=blk, heads=heads),
        out_shape=jax.ShapeDtypeStruct((batch * seq, MLA_HEADS * MLA_V), BF16),
        grid=(batch, MLA_HEADS // heads, nq),
        in_specs=[
            pl.BlockSpec((heads, blk, MLA_QK_PAD), lambda b, h, i: (h, b * nq + i, 0)),
            pl.BlockSpec((heads, seq, MLA_QK_PAD), lambda b, h, i: (h, b, 0)),
            pl.BlockSpec((heads, seq, 2 * MLA_V), lambda b, h, i: (h, b, 0)),
        ],
        out_specs=pl.BlockSpec((blk, heads * MLA_V), lambda b, h, i: (b * nq + i, h)),
        scratch_shapes=[pltpu.VMEM((2, heads, blk, blk), F32),
                        pltpu.VMEM((heads, blk, 2 * MLA_V), F32)],
        compiler_params=_cparams(("parallel", "parallel", "arbitrary")),
        name="mla_attention",
    )(q, k, v)


def _ret_constants(L):
    expo = 5.0 + 7.0 * np.arange(RET_HEADS, dtype=np.float32) / np.float32(RET_HEADS - 1)
    log_gamma = np.log1p(-np.exp2(-expo)).astype(np.float32)[:, None, None]
    pos = np.arange(L, dtype=np.float32)
    rel = pos[:, None] - pos[None, :]
    dmask = np.where(rel >= 0, np.exp(rel * log_gamma), 0.0).astype(np.float32)
    lanes = np.ones((1, 1, RET_HEAD), np.float32)
    q_dec = np.exp((pos + 1.0)[None, :, None] * log_gamma).astype(np.float32) * lanes
    k_dec = np.exp((L - 1.0 - pos)[None, :, None] * log_gamma).astype(np.float32) * lanes
    chunk_dec = [float(v) for v in np.exp(L * log_gamma[:, 0, 0])]
    return jnp.asarray(dmask), jnp.asarray(q_dec), jnp.asarray(k_dec), chunk_dec


def _ret_chunk(rq_ref, rk_ref, rv_ref, rg_ref, cos_ref, sin_ref, nw_ref, dmask_ref, qdec_ref,
               kdec_ref, o_ref, state_ref, *, chunk_dec):
    D = RET_HEAD
    half = D // 2
    cos, sin = cos_ref[...], sin_ref[...]

    def rope(ref, hs):
        x1 = ref[:, hs.start:hs.start + half].astype(F32)
        x2 = ref[:, hs.start + half:hs.stop].astype(F32)
        return jnp.concatenate([x1 * cos - x2 * sin, x1 * sin + x2 * cos], axis=-1)

    for h in range(RET_HEADS):
        hs = slice(h * D, (h + 1) * D)
        q = rope(rq_ref, hs)
        k = rope(rk_ref, hs) * (RET_HEAD ** -0.5)
        v16 = rv_ref[:, hs]
        q16 = q.astype(BF16)
        scores = _dot_nt(q16, k.astype(BF16)) * dmask_ref[h]
        y = _dot(scores.astype(BF16), v16)
        prev = state_ref[h]
        y = y + _dot(q16, prev.astype(BF16)) * qdec_ref[h]
        kd = k * kdec_ref[h]
        state_ref[h] = prev * chunk_dec[h] + _dot(kd.T.astype(BF16), v16)
        y = y * _rms(y, D) * nw_ref[:, hs]
        o_ref[:, hs] = (_silu(rg_ref[:, hs].astype(F32)) * y).astype(BF16)


N_SSD_IN, N_RET_IN, N_MLA_IN, N_SSD_SCRATCH = 16, 10, 12, 3


def _scans_kernel(*refs, chunk_dec, q_scale):
    n_in = N_SSD_IN + N_RET_IN + N_MLA_IN
    ssd_in, ret_in, mla_in = (refs[:N_SSD_IN], refs[N_SSD_IN:N_SSD_IN + N_RET_IN],
                              refs[N_SSD_IN + N_RET_IN:n_in])
    y_ssm_ref, y_ret_ref, q_ref, k_ref, v_ref = refs[n_in:n_in + 5]
    scratch = refs[n_in + 5:]
    ssd_scratch, ret_state = scratch[:N_SSD_SCRATCH], scratch[N_SSD_SCRATCH]

    @pl.when(pl.program_id(1) == 0)
    def _():
        _ssd_reset(*ssd_scratch)
        ret_state[...] = jnp.zeros(ret_state.shape, F32)

    _mla_prep_kernel(*mla_in, q_ref, k_ref, v_ref, q_scale=q_scale)
    _ssd_chunk(*ssd_in, y_ssm_ref, *ssd_scratch)
    _ret_chunk(*ret_in, y_ret_ref, ret_state, chunk_dec=chunk_dec)


def _scans(proj, dt_raw, cwx, cbx, cwb, cbb, dtb, alog, dskip, ssm_nw, cos, sin, ret_nw,
           rope_c, rope_s1, rope_s2, qan, kvan, qn, kn, wq, wkv, *, batch, seq):
    L = math.gcd(seq, SSD_CHUNK)
    assert L == math.gcd(seq, RET_CHUNK)
    nc = seq // L
    di = SSM_D_INNER
    bcw = 2 * SSM_GROUPS * SSM_STATE
    w = RET_HEADS * RET_HEAD
    tok = lambda b, c: b * nc + c
    blk = lambda width, col: pl.BlockSpec((L, width), lambda b, c: (tok(b, c), col // width))
    ssd_params = (cwx, cbx, cwb, cbb, dtb, alog, dskip, ssm_nw) + _ssd_constants(L)
    dmask, q_dec, k_dec, chunk_dec = _ret_constants(L)
    ret_params = (ret_nw, dmask, q_dec, k_dec)
    mla_params = (qan, kvan, qn, kn, wq, wkv)
    assert (4 + len(ssd_params), 6 + len(ret_params), 6 + len(mla_params)) == (
        N_SSD_IN, N_RET_IN, N_MLA_IN)
    m = batch * seq
    heads_spec = lambda width: pl.BlockSpec((MLA_HEADS, L, width), lambda b, c: (0, tok(b, c), 0))
    return pl.pallas_call(
        functools.partial(_scans_kernel, chunk_dec=chunk_dec,
                          q_scale=MLA_QK_DIM ** -0.5 * math.log2(math.e)),
        out_shape=(jax.ShapeDtypeStruct((m, di), BF16), jax.ShapeDtypeStruct((m, w), BF16),
                   jax.ShapeDtypeStruct((MLA_HEADS, m, MLA_QK_PAD), BF16),
                   jax.ShapeDtypeStruct((MLA_HEADS, m, MLA_QK_PAD), BF16),
                   jax.ShapeDtypeStruct((MLA_HEADS, m, 2 * MLA_V), BF16)),
        grid=(batch, nc),
        in_specs=[blk(di, P_XS), blk(bcw, P_BC), blk(di, P_Z), blk(LANES, 0)]
        + [_resident(p.shape) for p in ssd_params]
        + [blk(w, P_RQ), blk(w, P_RK), blk(w, P_RV), blk(w, P_RG),
           blk(RET_HEAD // 2, 0), blk(RET_HEAD // 2, 0)]
        + [_resident(p.shape) for p in ret_params]
        + [blk(MLA_Q_LORA, P_QLAT), blk(MLA_KV_LORA, P_CKV), blk(LANES, P_KPE),
           blk(LANES, 0), blk(LANES, 0), blk(LANES, 0)]
        + [_resident(p.shape) for p in mla_params],
        out_specs=(blk(di, 0), blk(w, 0), heads_spec(MLA_QK_PAD), heads_spec(MLA_QK_PAD),
                   heads_spec(2 * MLA_V)),
        scratch_shapes=[
            pltpu.VMEM((2 * SSD_TAIL, di), F32),
            pltpu.VMEM((2 * SSD_TAIL, bcw), F32),
            pltpu.VMEM((SSM_GROUPS, SSM_STATE, di // SSM_GROUPS), F32),
            pltpu.VMEM((RET_HEADS, RET_HEAD, RET_HEAD), F32),
        ],
        compiler_params=_cparams(("parallel", "arbitrary")),
        name="scans",
    )(proj, proj, proj, dt_raw, *ssd_params, proj, proj, proj, proj, cos, sin, *ret_params,
      proj, proj, proj, rope_c, rope_s1, rope_s2, *mla_params)


def _merge_kernel(ys_ref, ym_ref, yr_ref, g0_ref, g1_ref, g2_ref, gb_ref, x_ref,
                  w0_ref, w1_ref, w2_ref, wo_ref, o_ref):
    d = x_ref.shape[-1]

    def branch(k, y_ref, g_ref, w_ref):
        gate = _sigmoid(g_ref[...].astype(F32) + gb_ref[:, k * d:(k + 1) * d])
        return gate * _dot(y_ref[...], w_ref[...])

    merged = (branch(0, ys_ref, g0_ref, w0_ref) + branch(1, ym_ref, g1_ref, w1_ref)
              + branch(2, yr_ref, g2_ref, w2_ref))
    o_ref[...] = x_ref[...] + _dot(merged.astype(BF16), wo_ref[...])


def _merge(ys, ym, yr, proj, gb, x, w0, w1, w2, wo, *, tm=256):
    m, d = x.shape
    tm = min(tm, m)
    bw = ys.shape[1]
    row = lambda i: (i, 0)
    return pl.pallas_call(
        _merge_kernel,
        out_shape=jax.ShapeDtypeStruct((m, d), F32),
        grid=(m // tm,),
        in_specs=[
            pl.BlockSpec((tm, bw), row), pl.BlockSpec((tm, bw), row), pl.BlockSpec((tm, bw), row),
            pl.BlockSpec((tm, d), lambda i: (i, P_GATES // d)),
            pl.BlockSpec((tm, d), lambda i: (i, P_GATES // d + 1)),
            pl.BlockSpec((tm, d), lambda i: (i, P_GATES // d + 2)),
            _resident(gb.shape),
            pl.BlockSpec((tm, d), row),
            _resident(w0.shape), _resident(w1.shape), _resident(w2.shape), _resident(wo.shape),
        ],
        out_specs=pl.BlockSpec((tm, d), row),
        compiler_params=_cparams(("parallel",)),
        name="merge",
    )(ys, ym, yr, proj, proj, proj, gb, x, w0, w1, w2, wo)


def _rope_angles(positions, dim):
    inv = 1.0 / (ROPE_THETA ** (jnp.arange(0, dim, 2, dtype=F32) / dim))
    return positions.astype(F32).reshape(-1, 1) * inv


def _pad_lanes(v, n):
    return jnp.concatenate([v, jnp.zeros((n - v.shape[0],), v.dtype)]).reshape(1, n)


def kernel(x, positions, ffn1_norm, ffn1_w_gate, ffn1_w_up, ffn1_w_down, mix_norm, w_in, gate_b, conv_w, conv_b, dt_bias, a_log, d_skip, ssm_norm, q_a_norm, w_q_b, kv_a_norm, w_kv_b, q_norm, k_norm, ret_norm, w_br_ssm, w_br_mla, w_br_ret, w_out, ffn2_norm, ffn2_w_gate, ffn2_w_up, ffn2_w_down):
    batch, seq, d = x.shape
    depth = w_in.shape[0]
    x = x.reshape(batch * seq, d)

    ang = _rope_angles(positions, MLA_ROPE)
    cm, sm = jnp.cos(ang), jnp.sin(ang)
    zq = jnp.zeros_like(cm)
    rope_c = jnp.concatenate([cm, cm, zq, zq], axis=1)
    rope_s1 = jnp.concatenate([-sm, zq, zq, zq], axis=1)
    rope_s2 = jnp.concatenate([zq, sm, zq, zq], axis=1)
    ang = _rope_angles(positions, RET_HEAD)
    cos_ret, sin_ret = jnp.cos(ang), jnp.sin(ang)

    w_pad = jnp.pad(w_in.astype(BF16), ((0, 0), (0, 0), (0, -w_in.shape[2] % LANES)))
    ffn1_f32 = (ffn1_w_gate, ffn1_w_up, ffn1_w_down)
    ffn2_f32 = (ffn2_w_gate, ffn2_w_up, ffn2_w_down)
    merge_f32 = (w_br_ssm, w_br_mla, w_br_ret, w_out)
    ffn1 = tuple(w[0].astype(BF16) for w in ffn1_f32)
    row = lambda v: v.reshape(1, -1)
    for l in range(depth):
        x, _, (w16,) = _ffn(x, row(ffn1_norm[l]), *ffn1, relayout=(w_pad,), side_layer=l)

        proj, dt_raw, converted = _inproj(x, row(mix_norm[l]), w16, cast=merge_f32 + ffn2_f32,
                                          side_layer=l)
        w_merge, ffn2 = converted[:len(merge_f32)], converted[len(merge_f32):]

        wq = w_q_b[l].reshape(MLA_Q_LORA, MLA_HEADS, MLA_QK_DIM)
        wq = jnp.pad(wq, ((0, 0), (0, 0), (0, MLA_QK_PAD - MLA_QK_DIM)))
        wq = wq.transpose(1, 0, 2).astype(BF16)
        wkv = w_kv_b[l].reshape(MLA_KV_LORA, MLA_HEADS, MLA_NOPE + MLA_V)
        wkv = wkv.transpose(1, 0, 2).astype(BF16)
        y_ssm, y_ret, q, k, v = _scans(
            proj, dt_raw,
            conv_w[l][:, :SSM_D_INNER], row(conv_b[l][:SSM_D_INNER]),
            conv_w[l][:, SSM_D_INNER:], row(conv_b[l][SSM_D_INNER:]),
            _pad_lanes(dt_bias[l], LANES), _pad_lanes(a_log[l], LANES),
            row(jnp.repeat(d_skip[l], SSM_HEAD_DIM)), row(ssm_norm[l]),
            cos_ret, sin_ret, row(ret_norm[l]),
            rope_c, rope_s1, rope_s2, row(q_a_norm[l]), row(kv_a_norm[l]),
            _pad_lanes(q_norm[l], MLA_QK_PAD), _pad_lanes(k_norm[l], MLA_QK_PAD), wq, wkv,
            batch=batch, seq=seq)
        y_mla = _attention(q, k, v, batch=batch, seq=seq)

        x = _merge(y_ssm, y_mla, y_ret, proj, row(gate_b[l]), x, *w_merge)

        if l + 1 < depth:
            x, ffn1, _ = _ffn(x, row(ffn2_norm[l]), *ffn2, cast=ffn1_f32, side_layer=l + 1)
        else:
            x = _ffn(x, row(ffn2_norm[l]), *ffn2)[0]
    return x.reshape(batch, seq, d)
```
